```python
import jax, jax.numpy as jnp
from jax import lax
import numpy as np

D_MODEL = 2048
BATCH = 4
SEQ = 2048
DEPTH = 1
DEC_BATCH = 128
DEC_SEQ = 1
PAST_LEN = 16384
PAGE_SIZE = 128

N_META = 16
RWKV_HEAD = 64
RWKV_WIDTH = D_MODEL // 2
RWKV_HEADS = RWKV_WIDTH // RWKV_HEAD
DECAY_LORA = 96
AAA_LORA = 96
GATE_LORA = 256
LNX_EPS = 64e-5
SSM_WIDTH = D_MODEL // 2
SSM_GROUP = 16
SSM_GROUPS = SSM_WIDTH // SSM_GROUP
SSM_STATE = 64
DT_MIN = 0.001
DT_MAX = 0.1
N_EXPERT_GROUPS = 4
EXPERTS_PER_GROUP = 8
N_EXPERTS = N_EXPERT_GROUPS * EXPERTS_PER_GROUP
TOP_K_INNER = 2
D_EXPERT = D_MODEL // 4
NORM_EPS = 1e-6
SHIFT_WIDTH = 3 * RWKV_WIDTH + DECAY_LORA + AAA_LORA + GATE_LORA
SHIFT_SPLITS = (RWKV_WIDTH, 2 * RWKV_WIDTH, 3 * RWKV_WIDTH,
                3 * RWKV_WIDTH + DECAY_LORA, 3 * RWKV_WIDTH + DECAY_LORA + AAA_LORA)
IN_WIDTH = SHIFT_WIDTH + SSM_WIDTH + 2 * D_MODEL

kernel_name = 'hybrid_rwkv7_s5_hmoe_step'


def rms_norm(x, g):
    xf = x.astype(jnp.float32)
    y = xf * lax.rsqrt(jnp.mean(xf * xf, axis=-1, keepdims=True) + NORM_EPS)
    return (y * g.astype(jnp.float32)).astype(x.dtype)


def wkv7_recurrence(r, wdec, k, v, kk, b, s0):
    def step(s, inp):
        r_t, w_t, k_t, v_t, kk_t, b_t = inp
        sa = jnp.einsum('bhij,bhj->bhi', s, -kk_t)
        s = s * w_t[:, :, None, :] + sa[..., None] * b_t[:, :, None, :] + v_t[..., None] * k_t[:, :, None, :]
        return s, jnp.einsum('bhij,bhj->bhi', s, r_t)
    xs = tuple(jnp.moveaxis(a, 1, 0) for a in (r, wdec, k, v, kk, b))
    s_fin, ys = lax.scan(step, s0, xs)
    return jnp.moveaxis(ys, 0, 1), s_fin


def rwkv7_mixer(p, shift_prev, s0, w, l):
    f32 = jnp.float32
    bsz, t = p.shape[:2]
    prev = jnp.concatenate([shift_prev[:, None].astype(p.dtype), p[:, :-1]], axis=1)
    q = (p + (prev - p) * w['shift_mu'][l]).astype(f32)
    r, k, v, xw, xa, xg = jnp.split(q, SHIFT_SPLITS, axis=-1)
    wlog = -jax.nn.softplus(-(w['w0'][l] + jnp.tanh(xw) @ w['w_decay_up'][l].astype(f32))) - 0.5
    wdec = jnp.exp(-jnp.exp(wlog))
    a = jax.nn.sigmoid(w['a0'][l] + xa @ w['w_aaa_up'][l].astype(f32))
    g = jax.nn.sigmoid(xg) @ w['w_gate_up'][l].astype(f32)
    kk = k * w['k_k'][l]
    k = k * (1.0 + (a - 1.0) * w['k_a'][l])
    heads = lambda z: z.reshape(bsz, t, RWKV_HEADS, RWKV_HEAD)
    kk = heads(kk)
    kk = kk / jnp.maximum(jnp.sqrt(jnp.sum(kk * kk, axis=-1, keepdims=True)), 1e-12)
    rh, kh, vh, ah, wh = heads(r), heads(k), heads(v), heads(a), heads(wdec)
    y, s_fin = wkv7_recurrence(rh, wh, kh, vh, kk, kk * ah, s0.astype(f32))
    mu = jnp.mean(y, axis=-1, keepdims=True)
    var = jnp.mean(jnp.square(y - mu), axis=-1, keepdims=True)
    lnx_w = w['lnx_w'][l].astype(f32).reshape(RWKV_HEADS, RWKV_HEAD)
    lnx_b = w['lnx_b'][l].astype(f32).reshape(RWKV_HEADS, RWKV_HEAD)
    y = (y - mu) * lax.rsqrt(var + LNX_EPS) * lnx_w + lnx_b
    y = y + jnp.sum(rh * kh * w['r_k'][l], axis=-1, keepdims=True) * vh
    y = y.reshape(bsz, t, RWKV_WIDTH) * g
    return y.astype(p.dtype), s_fin.astype(s0.dtype), p[:, -1].astype(shift_prev.dtype)


def s5_combine(e1, e2):
    a1r, a1i, b1r, b1i = e1
    a2r, a2i, b2r, b2i = e2
    return (a2r * a1r - a2i * a1i, a2r * a1i + a2i * a1r,
            a2r * b1r - a2i * b1i + b2r, a2r * b1i + a2i * b1r + b2i)


def s5_mixer(u, h0_re, h0_im, w, l):
    f32 = jnp.float32
    bsz, t = u.shape[:2]
    uf = u.astype(f32).reshape(bsz, t, SSM_GROUPS, SSM_GROUP)
    lam_re = w['ssm_lam_re'][l].astype(f32)
    lam_im = w['ssm_lam_im'][l].astype(f32)
    dt = jnp.exp(w['ssm_log_dt'][l].astype(f32))[:, None]
    mag = jnp.exp(lam_re * dt)
    abar_re = mag * jnp.cos(lam_im * dt)
    abar_im = mag * jnp.sin(lam_im * dt)
    den = lam_re * lam_re + lam_im * lam_im
    nr = abar_re - 1.0
    coef_re = (nr * lam_re + abar_im * lam_im) / den
    coef_im = (abar_im * lam_re - nr * lam_im) / den
    b_re = w['ssm_b_re'][l].astype(f32)
    b_im = w['ssm_b_im'][l].astype(f32)
    bbar_re = coef_re[..., None] * b_re - coef_im[..., None] * b_im
    bbar_im = coef_re[..., None] * b_im + coef_im[..., None] * b_re
    bu_re = jnp.einsum('gpc,btgc->btgp', bbar_re, uf)
    bu_im = jnp.einsum('gpc,btgc->btgp', bbar_im, uf)
    h0r = h0_re.astype(f32)
    h0i = h0_im.astype(f32)
    bu_re = bu_re.at[:, 0].add(abar_re * h0r - abar_im * h0i)
    bu_im = bu_im.at[:, 0].add(abar_re * h0i + abar_im * h0r)
    a_re = jnp.broadcast_to(abar_re, bu_re.shape)
    a_im = jnp.broadcast_to(abar_im, bu_im.shape)
    _, _, h_re, h_im = lax.associative_scan(s5_combine, (a_re, a_im, bu_re, bu_im), axis=1)
    y = (jnp.einsum('gcp,btgp->btgc', w['ssm_c_re'][l].astype(f32), h_re)
         - jnp.einsum('gcp,btgp->btgc', w['ssm_c_im'][l].astype(f32), h_im)
         + w['ssm_d'][l].astype(f32) * uf)
    y = jax.nn.gelu(y.reshape(bsz, t, SSM_WIDTH))
    y = y * jax.nn.sigmoid(y @ w['w_glu'][l].astype(f32) + w['b_glu'][l].astype(f32))
    return y.astype(u.dtype), h_re[:, -1].astype(h0_re.dtype), h_im[:, -1].astype(h0_im.dtype)


def hier_moe(x, w, l):
    f32 = jnp.float32
    bsz, t, d = x.shape
    xt = x.reshape(-1, d)
    grp_prob = jax.nn.softmax((xt @ w['w_router_grp'][l]).astype(f32) + w['b_router_grp'][l], axis=-1)
    g_idx = jnp.argmax(grp_prob, axis=-1)
    g_p = jnp.max(grp_prob, axis=-1)
    exp_logits = ((xt @ w['w_router_exp'][l]).astype(f32) + w['b_router_exp'][l]).reshape(-1, N_EXPERT_GROUPS, EXPERTS_PER_GROUP)
    sel = jnp.take_along_axis(exp_logits, g_idx[:, None, None], axis=1)[:, 0]
    top_v, top_i = lax.top_k(sel, TOP_K_INNER)
    top_w = jax.nn.softmax(top_v, axis=-1) * g_p[:, None]
    eid = g_idx[:, None] * EXPERTS_PER_GROUP + top_i
    combine = jnp.sum(jax.nn.one_hot(eid, N_EXPERTS, dtype=f32) * top_w[..., None], axis=1)
    out = jnp.zeros(xt.shape, f32)
    for e in range(N_EXPERTS):
        hid = jax.nn.silu(xt @ w['w_exp_gate'][l, e]) * (xt @ w['w_exp_up'][l, e])
        out = out + combine[:, e:e + 1] * (hid @ w['w_exp_down'][l, e]).astype(f32)
    return out.astype(x.dtype).reshape(bsz, t, d)


def run_trunk(h, shift0, wkv0, re0, im0, w):
    new_shift, new_wkv, new_re, new_im = [], [], [], []
    for l in range(DEPTH):
        xn = rms_norm(h, w['g_mix'][l])
        proj = xn @ w['w_in'][l]
        p_shift = proj[..., :SHIFT_WIDTH]
        u = proj[..., SHIFT_WIDTH:SHIFT_WIDTH + SSM_WIDTH]
        gates = jax.nn.sigmoid(proj[..., SHIFT_WIDTH + SSM_WIDTH:].astype(jnp.float32))
        gate_a = gates[..., :D_MODEL]
        gate_b = gates[..., D_MODEL:]
        ya, wkv_n, shift_n = rwkv7_mixer(p_shift, shift0[l], wkv0[l], w, l)
        yb, re_n, im_n = s5_mixer(u, re0[l], im0[l], w, l)
        merged = (gate_a * (ya @ w['w_br_a'][l]).astype(jnp.float32)
                  + gate_b * (yb @ w['w_br_b'][l]).astype(jnp.float32))
        h = h + merged.astype(h.dtype) @ w['w_out'][l]
        h = h + hier_moe(rms_norm(h, w['g_ffn'][l]), w, l)
        new_shift.append(shift_n)
        new_wkv.append(wkv_n)
        new_re.append(re_n)
        new_im.append(im_n)
    y = rms_norm(h, w['g_final'])
    return y, jnp.stack(new_shift), jnp.stack(new_wkv), jnp.stack(new_re), jnp.stack(new_im)


def setup_inputs(seed: int = 0) -> dict:
    key = jax.random.key(seed)
    keys = iter(jax.random.split(key, 64))
    f32 = jnp.float32
    L = DEPTH

    def nrm(shape, scale=1.0):
        return jax.random.normal(next(keys), shape, f32) * scale

    def unif(shape, lo, hi):
        return jax.random.uniform(next(keys), shape, f32, lo, hi)

    lam_im = jnp.broadcast_to(jnp.pi * jnp.arange(SSM_STATE, dtype=f32), (L, SSM_GROUPS, SSM_STATE))
    w0 = jnp.broadcast_to(jnp.linspace(-6.0, -1.0, RWKV_WIDTH, dtype=f32), (L, RWKV_WIDTH))
    return {
        'x_prompt': nrm((BATCH, SEQ, D_MODEL)),
        'x_sample': nrm((DEC_BATCH, DEC_SEQ, D_MODEL)),
        'state_shift': nrm((L, DEC_BATCH, SHIFT_WIDTH)),
        'state_wkv': nrm((L, DEC_BATCH, RWKV_HEADS, RWKV_HEAD, RWKV_HEAD), 0.5),
        'state_ssm_re': nrm((L, DEC_BATCH, SSM_GROUPS, SSM_STATE), 0.5),
        'state_ssm_im': nrm((L, DEC_BATCH, SSM_GROUPS, SSM_STATE), 0.5),
        'meta_tokens': nrm((N_META, D_MODEL)),
        'g_mix': 1.0 + nrm((L, D_MODEL), 0.01),
        'w_in': nrm((L, D_MODEL, IN_WIDTH), D_MODEL ** -0.5),
        'shift_mu': unif((L, SHIFT_WIDTH), 0.0, 1.0),
        'w0': w0 + nrm((L, RWKV_WIDTH), 0.1),
        'w_decay_up': nrm((L, DECAY_LORA, RWKV_WIDTH), 0.1 * DECAY_LORA ** -0.5),
        'a0': nrm((L, RWKV_WIDTH), 0.1),
        'w_aaa_up': nrm((L, AAA_LORA, RWKV_WIDTH), 0.1 * AAA_LORA ** -0.5),
        'w_gate_up': nrm((L, GATE_LORA, RWKV_WIDTH), GATE_LORA ** -0.5),
        'k_k': 0.85 + nrm((L, RWKV_WIDTH), 0.05),
        'k_a': 1.0 + nrm((L, RWKV_WIDTH), 0.05),
        'r_k': nrm((L, RWKV_HEADS, RWKV_HEAD), 0.1),
        'lnx_w': 1.0 + nrm((L, RWKV_WIDTH), 0.05),
        'lnx_b': nrm((L, RWKV_WIDTH), 0.01),
        'ssm_lam_re': -0.5 + nrm((L, SSM_GROUPS, SSM_STATE), 0.01),
        'ssm_lam_im': lam_im + nrm((L, SSM_GROUPS, SSM_STATE), 0.01),
        'ssm_log_dt': unif((L, SSM_GROUPS), float(np.log(DT_MIN)), float(np.log(DT_MAX))),
        'ssm_b_re': nrm((L, SSM_GROUPS, SSM_STATE, SSM_GROUP), (2 * SSM_GROUP) ** -0.5),
        'ssm_b_im': nrm((L, SSM_GROUPS, SSM_STATE, SSM_GROUP), (2 * SSM_GROUP) ** -0.5),
        'ssm_c_re': nrm((L, SSM_GROUPS, SSM_GROUP, SSM_STATE), SSM_STATE ** -0.5),
        'ssm_c_im': nrm((L, SSM_GROUPS, SSM_GROUP, SSM_STATE), SSM_STATE ** -0.5),
        'ssm_d': nrm((L, SSM_GROUPS, SSM_GROUP)),
        'w_glu': nrm((L, SSM_WIDTH, SSM_WIDTH), SSM_WIDTH ** -0.5),
        'b_glu': nrm((L, SSM_WIDTH), 0.01),
        'w_br_a': nrm((L, RWKV_WIDTH, D_MODEL), RWKV_WIDTH ** -0.5),
        'w_br_b': nrm((L, SSM_WIDTH, D_MODEL), SSM_WIDTH ** -0.5),
        'w_out': nrm((L, D_MODEL, D_MODEL), D_MODEL ** -0.5),
        'g_ffn': 1.0 + nrm((L, D_MODEL), 0.01),
        'w_router_grp': nrm((L, D_MODEL, N_EXPERT_GROUPS), D_MODEL ** -0.5),
        'b_router_grp': nrm((L, N_EXPERT_GROUPS), 0.01),
        'w_router_exp': nrm((L, D_MODEL, N_EXPERTS), D_MODEL ** -0.5),
        'b_router_exp': nrm((L, N_EXPERTS), 0.01),
        'w_exp_gate': nrm((L, N_EXPERTS, D_MODEL, D_EXPERT), D_MODEL ** -0.5),
        'w_exp_up': nrm((L, N_EXPERTS, D_MODEL, D_EXPERT), D_MODEL ** -0.5),
        'w_exp_down': nrm((L, N_EXPERTS, D_EXPERT, D_MODEL), D_EXPERT ** -0.5),
        'g_final': 1.0 + nrm((D_MODEL,), 0.01),
    }


def reference(x_prompt, x_sample, state_shift, state_wkv, state_ssm_re, state_ssm_im,
              meta_tokens, g_mix, w_in, shift_mu, w0, w_decay_up, a0, w_aaa_up, w_gate_up,
              k_k, k_a, r_k, lnx_w, lnx_b, ssm_lam_re, ssm_lam_im, ssm_log_dt, ssm_b_re, ssm_b_im,
              ssm_c_re, ssm_c_im, ssm_d, w_glu, b_glu, w_br_a, w_br_b, w_out, g_ffn,
              w_router_grp, b_router_grp, w_router_exp, b_router_exp,
              w_exp_gate, w_exp_up, w_exp_down, g_final):
    w = {
        'g_mix': g_mix, 'w_in': w_in, 'shift_mu': shift_mu, 'w0': w0, 'w_decay_up': w_decay_up,
        'a0': a0, 'w_aaa_up': w_aaa_up, 'w_gate_up': w_gate_up, 'k_k': k_k, 'k_a': k_a,
        'r_k': r_k, 'lnx_w': lnx_w, 'lnx_b': lnx_b, 'ssm_lam_re': ssm_lam_re,
        'ssm_lam_im': ssm_lam_im, 'ssm_log_dt': ssm_log_dt, 'ssm_b_re': ssm_b_re,
        'ssm_b_im': ssm_b_im, 'ssm_c_re': ssm_c_re, 'ssm_c_im': ssm_c_im, 'ssm_d': ssm_d,
        'w_glu': w_glu, 'b_glu': b_glu, 'w_br_a': w_br_a, 'w_br_b': w_br_b, 'w_out': w_out,
        'g_ffn': g_ffn, 'w_router_grp': w_router_grp, 'b_router_grp': b_router_grp,
        'w_router_exp': w_router_exp, 'b_router_exp': b_router_exp,
        'w_exp_gate': w_exp_gate, 'w_exp_up': w_exp_up, 'w_exp_down': w_exp_down,
        'g_final': g_final,
    }
    bsz = x_prompt.shape[0]
    meta = jnp.broadcast_to(meta_tokens[None].astype(x_prompt.dtype), (bsz, N_META, D_MODEL))
    h_p = jnp.concatenate([meta, x_prompt], axis=1)
    shift0_p = jnp.zeros((DEPTH, bsz, SHIFT_WIDTH), state_shift.dtype)
    wkv0_p = jnp.zeros((DEPTH, bsz, RWKV_HEADS, RWKV_HEAD, RWKV_HEAD), state_wkv.dtype)
    re0_p = jnp.zeros((DEPTH, bsz, SSM_GROUPS, SSM_STATE), state_ssm_re.dtype)
    im0_p = jnp.zeros((DEPTH, bsz, SSM_GROUPS, SSM_STATE), state_ssm_im.dtype)
    y_p, shift_p, wkv_p, re_p, im_p = run_trunk(h_p, shift0_p, wkv0_p, re0_p, im0_p, w)
    y_prompt = y_p[:, N_META:]
    y_sample, shift_s, wkv_s, re_s, im_s = run_trunk(x_sample, state_shift, state_wkv,
                                                     state_ssm_re, state_ssm_im, w)
    return (y_prompt, y_sample, shift_p, wkv_p, re_p, im_p, shift_s, wkv_s, re_s, im_s)
```

```python
import functools

import jax
import jax.numpy as jnp
from jax import lax
from jax.experimental import pallas as pl
from jax.experimental.pallas import tpu as pltpu

F32 = jnp.float32
BF16 = jnp.bfloat16

NORM_EPS = 1e-6
LNX_EPS = 64e-5
N_META = 16
HEAD = 64
SSM_GROUP = 16
SSM_STATE = 64
EXPERTS_PER_GROUP = 8
LANES = 128
LORA_PAD = 128
VMEM_LIMIT = 56 * 1024 * 1024


def _cparams(sem):
    return pltpu.CompilerParams(dimension_semantics=sem, vmem_limit_bytes=VMEM_LIMIT)


def _const_spec(shape):
    nd = len(shape)
    return pl.BlockSpec(shape, lambda *_: (0,) * nd)


def _inproj_kernel(x_ref, g_ref, w_ref, o_ref, xn_ref):
    @pl.when(pl.program_id(1) == 0)
    def _():
        x = x_ref[...]
        ms = jnp.mean(x * x, axis=-1, keepdims=True)
        xn_ref[...] = (x * lax.rsqrt(ms + NORM_EPS) * g_ref[...]).astype(BF16)

    o_ref[...] = jnp.dot(xn_ref[...], w_ref[...], preferred_element_type=F32)


def _inproj(x_all, g_mix, w_in_p, tm, tn):
    nt, d = x_all.shape
    n_out = w_in_p.shape[1]
    return pl.pallas_call(
        _inproj_kernel,
        grid=(nt // tm, n_out // tn),
        in_specs=[pl.BlockSpec((tm, d), lambda i, j: (i, 0)),
                  pl.BlockSpec((1, d), lambda i, j: (0, 0)),
                  pl.BlockSpec((d, tn), lambda i, j: (0, j))],
        out_specs=pl.BlockSpec((tm, tn), lambda i, j: (i, j)),
        out_shape=jax.ShapeDtypeStruct((nt, n_out), F32),
        scratch_shapes=[pltpu.VMEM((tm, d), BF16)],
        compiler_params=_cparams(("parallel", "arbitrary")),
        name="inproj",
    )(x_all, g_mix, w_in_p)


def _softplus(z):
    return jnp.maximum(z, 0.0) + jnp.log1p(jnp.exp(-jnp.abs(z)))


def _rwkv_prep_kernel(*refs, width, carry_prev):
    if carry_prev:
        (rkv_ref, lo_ref, mu_rkv_ref, mu_lo_ref, w0_ref, a0_ref, kk_ref, ka_ref,
         wd_ref, wa_ref, wg_ref,
         r_out, w_out, k_out, v_out, kk_out, a_out, g_out, c_rkv, c_lo) = refs

        @pl.when(pl.program_id(1) == 0)
        def _():
            c_rkv[...] = jnp.zeros_like(c_rkv)
            c_lo[...] = jnp.zeros_like(c_lo)
    else:
        (rkv_ref, lo_ref, prev_rkv_ref, prev_lo_ref, mu_rkv_ref, mu_lo_ref, w0_ref, a0_ref,
         kk_ref, ka_ref, wd_ref, wa_ref, wg_ref,
         r_out, w_out, k_out, v_out, kk_out, a_out, g_out) = refs

    tm = rkv_ref.shape[0]
    first_row = lax.broadcasted_iota(jnp.int32, (tm, 1), 0) == 0

    def shifted(p, prev_ref, carry_ref, cols):
        if carry_prev:
            prev = jnp.where(first_row, carry_ref[:, cols], pltpu.roll(p, 1, 0))
        else:
            prev = prev_ref[:, cols]
        return prev

    def lerp(p, prev, mu):
        return p + (prev - p) * mu

    lo_cols = slice(0, lo_ref.shape[1])
    p_lo = lo_ref[...]
    q_lo = lerp(p_lo, shifted(p_lo, None if carry_prev else prev_lo_ref,
                              c_lo if carry_prev else None, lo_cols), mu_lo_ref[...])
    xw = q_lo[:, 0:LORA_PAD]
    xa = q_lo[:, LORA_PAD:2 * LORA_PAD]
    xg = q_lo[:, 2 * LORA_PAD:]
    dw = jnp.dot(jnp.tanh(xw).astype(BF16), wd_ref[...], preferred_element_type=F32)
    wlog = -_softplus(-(w0_ref[...] + dw)) - 0.5
    w_out[...] = jnp.exp(-jnp.exp(wlog))
    a = jax.nn.sigmoid(a0_ref[...] + jnp.dot(xa.astype(BF16), wa_ref[...],
                                             preferred_element_type=F32))
    a_out[...] = a
    g_out[...] = jnp.dot(jax.nn.sigmoid(xg).astype(BF16), wg_ref[...],
                         preferred_element_type=F32)

    def q_of(idx):
        cols = slice(idx * width, (idx + 1) * width)
        p = rkv_ref[:, cols]
        prev = shifted(p, None if carry_prev else prev_rkv_ref,
                       c_rkv if carry_prev else None, cols)
        return lerp(p, prev, mu_rkv_ref[:, cols])

    r_out[...] = q_of(0)
    k = q_of(1)
    kk_out[...] = k * kk_ref[...]
    k_out[...] = k * (1.0 + (a - 1.0) * ka_ref[...])
    v_out[...] = q_of(2)

    if carry_prev:
        c_rkv[...] = rkv_ref[tm - 1:tm, :]
        c_lo[...] = lo_ref[tm - 1:tm, :]


def _rwkv_prep(proj, row_block0, n_rows, tm, seqs, width, lo_col_block, lo_width,
               mu_rkv, mu_lo, w0, a0, k_k, k_a, wd, wa, wg, prev=None):
    carry_prev = prev is None
    per_seq = n_rows // seqs // tm
    row_map = lambda b, c: (row_block0 + b * per_seq + c, 0)
    lo_map = lambda b, c: (row_block0 + b * per_seq + c, lo_col_block)
    out_map = lambda b, c: (b * per_seq + c, 0)
    in_specs = [pl.BlockSpec((tm, 3 * width), row_map), pl.BlockSpec((tm, lo_width), lo_map)]
    args = [proj, proj]
    if not carry_prev:
        in_specs += [pl.BlockSpec((tm, 3 * width), out_map), pl.BlockSpec((tm, lo_width), out_map)]
        args += list(prev)
    consts = [mu_rkv, mu_lo, w0, a0, k_k, k_a, wd, wa, wg]
    in_specs += [_const_spec(c.shape) for c in consts]
    args += consts
    scratch = []
    if carry_prev:
        scratch = [pltpu.VMEM((1, 3 * width), F32), pltpu.VMEM((1, lo_width), F32)]
    out_sd = jax.ShapeDtypeStruct((n_rows, width), F32)
    return pl.pallas_call(
        functools.partial(_rwkv_prep_kernel, width=width, carry_prev=carry_prev),
        grid=(seqs, per_seq),
        in_specs=in_specs,
        out_specs=[pl.BlockSpec((tm, width), out_map)] * 7,
        out_shape=[out_sd] * 7,
        scratch_shapes=scratch,
        compiler_params=_cparams(("parallel", "arbitrary")),
        name="rwkv_prep_seq" if carry_prev else "rwkv_prep_step",
    )(*args)


def _wkv_kernel(r_ref, w_ref, k_ref, kk_ref, al_ref, v_ref, s0_ref, rk_ref, lw_ref, lb_ref,
                y_ref, sf_ref, s_scr, a_scr, b_scr, *, ni, tc, isplit, ri, unroll):
    @pl.when(pl.program_id(1) == 0)
    def _():
        s_scr[...] = s0_ref[...]

    def prep(t, carry):
        kk = kk_ref[t]
        ss = jnp.sum(kk * kk, axis=0, keepdims=True)
        kkn = kk / jnp.maximum(jnp.sqrt(ss), 1e-12)
        a_scr[t] = -kkn
        b_scr[t] = kkn * al_ref[t]
        return carry

    lax.fori_loop(0, tc, prep, 0)

    for g in range(ni // ri):
        rows = tuple(s_scr[g * ri + q] for q in range(ri))

        def step(t, rows, g=g):
            a = a_scr[t]
            w = w_ref[t]
            b = b_scr[t]
            k = k_ref[t]
            r = r_ref[t]
            new = []
            for q in range(ri):
                s = rows[q]
                sa = jnp.sum(s * a, axis=0, keepdims=True)
                vq = v_ref[t, g * ri + q:g * ri + q + 1, :]
                s = s * w + sa * b + vq * k
                y_ref[t, g * ri + q:g * ri + q + 1, :] = jnp.sum(s * r, axis=0, keepdims=True)
                new.append(s)
            return tuple(new)

        rows = lax.fori_loop(0, tc, step, rows, unroll=unroll)
        for q in range(ri):
            s_scr[g * ri + q] = rows[q]

    def isum(x):
        s = jnp.broadcast_to(jnp.sum(x, axis=0, keepdims=True), (8, LANES))
        if isplit:
            s = s + pltpu.roll(s, LANES // 2, 1)
        return s[0:1]

    def post(t, carry):
        y = y_ref[t]
        v = v_ref[t]
        mu = isum(y) * (1.0 / HEAD)
        d = y - mu
        var = isum(d * d) * (1.0 / HEAD)
        yn = d * lax.rsqrt(var + LNX_EPS) * lw_ref[...] + lb_ref[...]
        bonus = jnp.sum(r_ref[t] * k_ref[t] * rk_ref[...], axis=0, keepdims=True)
        y_ref[t] = yn + bonus * v
        return carry

    lax.fori_loop(0, tc, post, 0)

    @pl.when(pl.program_id(1) == pl.num_programs(1) - 1)
    def _():
        sf_ref[...] = s_scr[...]


def _wkv(r, w, k, kk, al, v, s0, rk, lw, lb, tc, isplit, ri, unroll):
    t, nj, lanes = r.shape
    ni = v.shape[1]
    jspec = pl.BlockSpec((tc, nj, LANES), lambda l, c: (c, 0, l))
    ispec = pl.BlockSpec((tc, ni, LANES), lambda l, c: (c, 0, l))
    sspec = pl.BlockSpec((ni, nj, LANES), lambda l, c: (0, 0, l))
    return pl.pallas_call(
        functools.partial(_wkv_kernel, ni=ni, tc=tc, isplit=isplit, ri=ri, unroll=unroll),
        grid=(lanes // LANES, t // tc),
        in_specs=[jspec, jspec, jspec, jspec, jspec, ispec, sspec,
                  _const_spec(rk.shape), _const_spec(lw.shape), _const_spec(lb.shape)],
        out_specs=[ispec, sspec],
        out_shape=[jax.ShapeDtypeStruct((t, ni, lanes), F32),
                   jax.ShapeDtypeStruct((ni, nj, lanes), F32)],
        scratch_shapes=[pltpu.VMEM((ni, nj, LANES), F32),
                        pltpu.VMEM((tc, nj, LANES), F32),
                        pltpu.VMEM((tc, nj, LANES), F32)],
        compiler_params=_cparams(("parallel", "arbitrary")),
        name="wkv_seq" if isplit else "wkv_step",
    )(r, w, k, kk, al, v, s0, rk, lw, lb)


def _s5_kernel(*refs, sequential, n_blk, scan_lanes):
    if sequential:
        (u_ref, bre_ref, bim_ref, cre_ref, cim_ref, d_ref, are_ref, aim_ref, wglu_ref, bglu_ref,
         yb_ref, hre_out, him_out, bure, buim, c_re, c_im) = refs
    else:
        (u_ref, h0re_ref, h0im_ref, bre_ref, bim_ref, cre_ref, cim_ref, d_ref, are_ref, aim_ref,
         wglu_ref, bglu_ref, yb_ref, hre_out, him_out, bure, buim) = refs

    tm = u_ref.shape[0]
    kin = bre_ref.shape[1]
    kst = bre_ref.shape[2]
    u = u_ref[...]
    ub = u.astype(BF16)
    for kb in range(n_blk):
        ukb = ub[:, kb * kin:(kb + 1) * kin]
        bure[:, kb * kst:(kb + 1) * kst] = jnp.dot(ukb, bre_ref[kb], preferred_element_type=F32)
        buim[:, kb * kst:(kb + 1) * kst] = jnp.dot(ukb, bim_ref[kb], preferred_element_type=F32)

    if sequential:
        @pl.when(pl.program_id(1) == 0)
        def _():
            c_re[...] = jnp.zeros_like(c_re)
            c_im[...] = jnp.zeros_like(c_im)

        for lb in range(bure.shape[1] // scan_lanes):
            sl = pl.ds(lb * scan_lanes, scan_lanes)
            ar = are_ref[:, sl]
            ai = aim_ref[:, sl]

            def step(t, h, sl=sl, ar=ar, ai=ai):
                hr, hi = h
                row = pl.ds(t, 1)
                nr = ar * hr - ai * hi + bure[row, sl]
                ni = ar * hi + ai * hr + buim[row, sl]
                bure[row, sl] = nr
                buim[row, sl] = ni
                return nr, ni

            hr, hi = lax.fori_loop(0, tm, step, (c_re[:, sl], c_im[:, sl]))
            c_re[:, sl] = hr
            c_im[:, sl] = hi
        hre_out[0] = c_re[...]
        him_out[0] = c_im[...]
    else:
        h0r = h0re_ref[...]
        h0i = h0im_ref[...]
        ar = are_ref[...]
        ai = aim_ref[...]
        nr = bure[...] + (ar * h0r - ai * h0i)
        ni = buim[...] + (ar * h0i + ai * h0r)
        bure[...] = nr
        buim[...] = ni
        hre_out[...] = nr
        him_out[...] = ni

    ys = []
    for kb in range(n_blk):
        cols = slice(kb * kst, (kb + 1) * kst)
        yre = jnp.dot(bure[:, cols].astype(BF16), cre_ref[kb], preferred_element_type=F32)
        yim = jnp.dot(buim[:, cols].astype(BF16), cim_ref[kb], preferred_element_type=F32)
        ys.append(yre - yim)
    y = jnp.concatenate(ys, axis=1) + d_ref[...] * u
    y = jax.nn.gelu(y)
    gate = jnp.dot(y.astype(BF16), wglu_ref[...], preferred_element_type=F32) + bglu_ref[...]
    yb_ref[...] = (y * jax.nn.sigmoid(gate)).astype(BF16)


def _s5(proj, row_block0, n_rows, tm, seqs, u_col_block, width, consts, h0=None):
    sequential = h0 is None
    bre = consts[0]
    n_blk, _, kst = bre.shape
    n_state = n_blk * kst
    per_seq = n_rows // seqs // tm
    u_map = lambda b, c: (row_block0 + b * per_seq + c, u_col_block)
    out_map = lambda b, c: (b * per_seq + c, 0)
    in_specs = [pl.BlockSpec((tm, width), u_map)]
    args = [proj]
    if not sequential:
        in_specs += [pl.BlockSpec((tm, n_state), out_map)] * 2
        args += list(h0)
    in_specs += [_const_spec(c.shape) for c in consts]
    args += list(consts)
    scratch = [pltpu.VMEM((tm, n_state), F32), pltpu.VMEM((tm, n_state), F32)]
    if sequential:
        scratch += [pltpu.VMEM((1, n_state), F32), pltpu.VMEM((1, n_state), F32)]
        st_spec = pl.BlockSpec((1, 1, n_state), lambda b, c: (b, 0, 0))
        st_shape = jax.ShapeDtypeStruct((seqs, 1, n_state), F32)
    else:
        st_spec = pl.BlockSpec((tm, n_state), out_map)
        st_shape = jax.ShapeDtypeStruct((n_rows, n_state), F32)
    return pl.pallas_call(
        functools.partial(_s5_kernel, sequential=sequential, n_blk=n_blk, scan_lanes=1024),
        grid=(seqs, per_seq),
        in_specs=in_specs,
        out_specs=[pl.BlockSpec((tm, width), out_map), st_spec, st_spec],
        out_shape=[jax.ShapeDtypeStruct((n_rows, width), BF16), st_shape, st_shape],
        scratch_shapes=scratch,
        compiler_params=_cparams(("parallel", "arbitrary")),
        name="s5_seq" if sequential else "s5_step",
    )(*args)


def _route(logits, n_grp):
    lane = lax.broadcasted_iota(jnp.int32, logits.shape, 1).astype(F32)
    neg = jnp.float32(-1e30)
    big = jnp.float32(1e9)
    is_grp = lane < n_grp
    gl = jnp.where(is_grp, logits, neg)
    gmax = jnp.max(gl, axis=1, keepdims=True)
    gsum = jnp.sum(jnp.where(is_grp, jnp.exp(gl - gmax), 0.0), axis=1, keepdims=True)
    g_p = 1.0 / gsum
    g_idx = jnp.min(jnp.where(is_grp & (gl == gmax), lane, big), axis=1, keepdims=True)
    lo = n_grp + g_idx * EXPERTS_PER_GROUP
    in_grp = (lane >= lo) & (lane < lo + EXPERTS_PER_GROUP)
    el = jnp.where(in_grp, logits, neg)
    v1 = jnp.max(el, axis=1, keepdims=True)
    i1 = jnp.min(jnp.where(in_grp & (el == v1), lane, big), axis=1, keepdims=True)
    rest = in_grp & (lane != i1)
    el2 = jnp.where(rest, logits, neg)
    v2 = jnp.max(el2, axis=1, keepdims=True)
    i2 = jnp.min(jnp.where(rest & (el2 == v2), lane, big), axis=1, keepdims=True)
    e2 = jnp.exp(v2 - v1)
    w1 = g_p / (1.0 + e2)
    w2 = g_p * e2 / (1.0 + e2)
    return jnp.where(lane == i1, w1, 0.0) + jnp.where(lane == i2, w2, 0.0)


def _merge_kernel(ya_ref, g_ref, yb_ref, ga_ref, gb_ref, x_ref, wa_ref, wb_ref, wo_ref,
                  gf_ref, wr_ref, br_ref, h_out, xn_out, comb_out, *, n_grp):
    ya = (ya_ref[...] * g_ref[...]).astype(BF16)
    ma = jnp.dot(ya, wa_ref[...], preferred_element_type=F32)
    mb = jnp.dot(yb_ref[...], wb_ref[...], preferred_element_type=F32)
    merged = jax.nn.sigmoid(ga_ref[...]) * ma + jax.nn.sigmoid(gb_ref[...]) * mb
    h = x_ref[...] + jnp.dot(merged.astype(BF16), wo_ref[...], preferred_element_type=F32)
    h_out[...] = h
    ms = jnp.mean(h * h, axis=-1, keepdims=True)
    xn = h * lax.rsqrt(ms + NORM_EPS) * gf_ref[...]
    xn_out[...] = xn.astype(BF16)
    logits = jnp.dot(xn, wr_ref[...], preferred_element_type=F32,
                     precision=lax.Precision.HIGHEST) + br_ref[...]
    comb_out[...] = _route(logits, n_grp)


def _merge(ya, g, yb, proj, x_all, w_br_a, w_br_b, w_out, g_ffn, w_router, b_router,
           tm, ga_col_block, n_grp):
    nt, d = x_all.shape
    wdt = ya.shape[1]
    row = lambda i: (i, 0)
    single = dict(pipeline_mode=pl.Buffered(1))
    in_specs = [pl.BlockSpec((tm, wdt), row), pl.BlockSpec((tm, wdt), row),
                pl.BlockSpec((tm, wdt), row),
                pl.BlockSpec((tm, d), lambda i: (i, ga_col_block)),
                pl.BlockSpec((tm, d), lambda i: (i, ga_col_block + 1)),
                pl.BlockSpec((tm, d), row),
                pl.BlockSpec(w_br_a.shape, lambda i: (0, 0), **single),
                pl.BlockSpec(w_br_b.shape, lambda i: (0, 0), **single),
                pl.BlockSpec(w_out.shape, lambda i: (0, 0), **single),
                _const_spec(g_ffn.shape), _const_spec(w_router.shape), _const_spec(b_router.shape)]
    return pl.pallas_call(
        functools.partial(_merge_kernel, n_grp=n_grp),
        grid=(nt // tm,),
        in_specs=in_specs,
        out_specs=[pl.BlockSpec((tm, d), row), pl.BlockSpec((tm, d), row),
                   pl.BlockSpec((tm, LANES), row)],
        out_shape=[jax.ShapeDtypeStruct((nt, d), F32), jax.ShapeDtypeStruct((nt, d), BF16),
                   jax.ShapeDtypeStruct((nt, LANES), F32)],
        compiler_params=_cparams(("parallel",)),
        name="merge_route",
    )(ya, g, yb, proj, proj, x_all, w_br_a, w_br_b, w_out, g_ffn, w_router, b_router)


def _moe_kernel(x_ref, comb_ref, h_ref, wg_ref, wu_ref, wd_ref, gfin_ref, o_ref, *, n_grp):
    e = pl.program_id(1)

    @pl.when(e == 0)
    def _():
        o_ref[...] = jnp.zeros_like(o_ref)

    x = x_ref[...]
    xg = jnp.dot(x, wg_ref[0], preferred_element_type=F32)
    xu = jnp.dot(x, wu_ref[0], preferred_element_type=F32)
    hid = (jax.nn.silu(xg) * xu).astype(BF16)
    ye = jnp.dot(hid, wd_ref[0], preferred_element_type=F32)
    lane = lax.broadcasted_iota(jnp.int32, comb_ref.shape, 1)
    ce = jnp.sum(jnp.where(lane == n_grp + e, comb_ref[...], 0.0), axis=1, keepdims=True)
    o_ref[...] += ce * ye

    @pl.when(e == pl.num_programs(1) - 1)
    def _():
        h = h_ref[...] + o_ref[...]
        ms = jnp.mean(h * h, axis=-1, keepdims=True)
        o_ref[...] = h * lax.rsqrt(ms + NORM_EPS) * gfin_ref[...]


def _moe(xn, comb, h, wg, wu, wd, g_final, tm, n_grp):
    nt, d = xn.shape
    n_exp, _, de = wg.shape
    row = lambda i, e: (i, 0)
    return pl.pallas_call(
        functools.partial(_moe_kernel, n_grp=n_grp),
        grid=(nt // tm, n_exp),
        in_specs=[pl.BlockSpec((tm, d), row), pl.BlockSpec((tm, LANES), row),
                  pl.BlockSpec((tm, d), row),
                  pl.BlockSpec((1, d, de), lambda i, e: (e, 0, 0)),
                  pl.BlockSpec((1, d, de), lambda i, e: (e, 0, 0)),
                  pl.BlockSpec((1, de, d), lambda i, e: (e, 0, 0)),
                  pl.BlockSpec((1, d), lambda i, e: (0, 0))],
        out_specs=pl.BlockSpec((tm, d), row),
        out_shape=jax.ShapeDtypeStruct((nt, d), F32),
        compiler_params=_cparams(("parallel", "arbitrary")),
        name="moe_dense",
    )(xn, comb, h, wg, wu, wd, g_final)


def _pad_cols(w, to):
    return jnp.pad(w, ((0, 0), (0, to - w.shape[1])))


def _pad_rows(w, to):
    return jnp.pad(w, ((0, to - w.shape[0]), (0, 0)))


def _s5_consts(lam_re, lam_im, log_dt, b_re, b_im, c_re, c_im, d, w_glu, b_glu):
    dt = jnp.exp(log_dt)[:, None]
    mag = jnp.exp(lam_re * dt)
    abar_re = mag * jnp.cos(lam_im * dt)
    abar_im = mag * jnp.sin(lam_im * dt)
    den = lam_re * lam_re + lam_im * lam_im
    nr = abar_re - 1.0
    coef_re = (nr * lam_re + abar_im * lam_im) / den
    coef_im = (abar_im * lam_re - nr * lam_im) / den
    bbar_re = coef_re[..., None] * b_re - coef_im[..., None] * b_im
    bbar_im = coef_re[..., None] * b_im + coef_im[..., None] * b_re
    n_g, n_p, n_c = b_re.shape
    gpb = LANES // n_c
    eye = jnp.eye(gpb, dtype=F32)

    def in_blk(bb):
        bb = bb.reshape(n_g // gpb, gpb, n_p, n_c)
        return jnp.einsum('kgpc,gh->kgchp', bb, eye).reshape(
            n_g // gpb, gpb * n_c, gpb * n_p).astype(BF16)

    def out_blk(cc):
        cc = cc.reshape(n_g // gpb, gpb, n_c, n_p)
        return jnp.einsum('kgcp,gh->khpgc', cc, eye).reshape(
            n_g // gpb, gpb * n_p, gpb * n_c).astype(BF16)

    return (in_blk(bbar_re), in_blk(bbar_im), out_blk(c_re), out_blk(c_im),
            d.reshape(1, -1), abar_re.reshape(1, -1), abar_im.reshape(1, -1),
            w_glu.astype(BF16), b_glu.reshape(1, -1))


def kernel(x_prompt, x_sample, state_shift, state_wkv, state_ssm_re, state_ssm_im, meta_tokens, g_mix, w_in, shift_mu, w0, w_decay_up, a0, w_aaa_up, w_gate_up, k_k, k_a, r_k, lnx_w, lnx_b, ssm_lam_re, ssm_lam_im, ssm_log_dt, ssm_b_re, ssm_b_im, ssm_c_re, ssm_c_im, ssm_d, w_glu, b_glu, w_br_a, w_br_b, w_out, g_ffn, w_router_grp, b_router_grp, w_router_exp, b_router_exp, w_exp_gate, w_exp_up, w_exp_down, g_final):
    depth = g_mix.shape[0]
    assert depth == 1, "single-layer trunk"
    bsz, seq, d = x_prompt.shape
    nb = x_sample.shape[0]
    assert x_sample.shape[1] == 1
    t_p = seq + N_META
    n_p = bsz * t_p
    width = k_k.shape[1]
    heads = width // HEAD
    n_dl, n_al, n_gl = w_decay_up.shape[1], w_aaa_up.shape[1], w_gate_up.shape[1]
    n_grp = w_router_grp.shape[2]
    n_exp = w_router_exp.shape[2]
    assert bsz * heads * 2 == LANES and (nb * heads) % LANES == 0

    s_row0 = -(-n_p // nb) * nb
    n_t = s_row0 + nb
    meta = jnp.broadcast_to(meta_tokens[None], (bsz, N_META, d))
    x_all = jnp.concatenate(
        [jnp.concatenate([meta, x_prompt], axis=1).reshape(n_p, d),
         jnp.zeros((s_row0 - n_p, d), F32), x_sample.reshape(nb, d)], axis=0)

    c_rkv = 3 * width
    c_xw, c_xa, c_xg = c_rkv, c_rkv + n_dl, c_rkv + n_dl + n_al
    c_u = c_xg + n_gl
    c_ga = c_u + width
    lo_width = 2 * LORA_PAD + n_gl

    def regroup(m):
        return jnp.concatenate(
            [m[:, :c_rkv], m[:, c_u:], _pad_cols(m[:, c_xw:c_xa], LORA_PAD),
             _pad_cols(m[:, c_xa:c_xg], LORA_PAD), m[:, c_xg:c_u]], axis=1)

    def ungroup(m):
        lo = c_rkv + width + 2 * d
        return jnp.concatenate(
            [m[:, :c_rkv], m[:, lo:lo + n_dl], m[:, lo + LORA_PAD:lo + LORA_PAD + n_al],
             m[:, lo + 2 * LORA_PAD:]], axis=1)

    w_in_p = regroup(w_in[0]).astype(BF16)
    n_proj = w_in_p.shape[1]
    proj = _inproj(x_all, g_mix, w_in_p, tm=768, tn=512)

    mu_all = regroup(_pad_cols(shift_mu, c_u + width + 2 * d))
    mu_rkv, mu_lo = mu_all[:, :c_rkv], mu_all[:, n_proj - lo_width:]
    st_all = regroup(_pad_cols(state_shift[0], c_u + width + 2 * d))
    prev_s = (st_all[:, :c_rkv], st_all[:, n_proj - lo_width:])
    lo_col_block = (n_proj - lo_width) // lo_width
    prep_w = (mu_rkv, mu_lo, w0, a0, k_k, k_a,
              _pad_rows(w_decay_up[0], LORA_PAD).astype(BF16),
              _pad_rows(w_aaa_up[0], LORA_PAD).astype(BF16), w_gate_up[0].astype(BF16))

    tm_seq = t_p // 6
    rp, wp, kp, vp, kkp, ap, gp = _rwkv_prep(proj, 0, n_p, tm_seq, bsz, width, lo_col_block,
                                             lo_width, *prep_w)
    rs, ws, ks, vs, kks, as_, gs = _rwkv_prep(proj, s_row0 // nb, nb, nb, 1, width, lo_col_block,
                                              lo_width, *prep_w, prev=prev_s)

    half = HEAD // 2

    def to_lanes_p(z):
        z = z.reshape(bsz, t_p, heads, HEAD).transpose(1, 3, 0, 2).reshape(t_p, HEAD, bsz * heads)
        return jnp.concatenate([z, z], axis=-1)

    def to_lanes_pi(z):
        z = z.reshape(bsz, t_p, heads, 2, half).transpose(1, 4, 3, 0, 2)
        return z.reshape(t_p, half, LANES)

    def to_lanes_s(z):
        return z.reshape(nb, heads, HEAD).transpose(2, 0, 1).reshape(1, HEAD, nb * heads)

    rk_l = jnp.tile(r_k[0].T, (1, LANES // heads))
    lw_p = jnp.broadcast_to(lnx_w[0].reshape(heads, 2, half).transpose(2, 1, 0)[:, :, None, :],
                            (half, 2, bsz, heads)).reshape(half, LANES)
    lb_p = jnp.broadcast_to(lnx_b[0].reshape(heads, 2, half).transpose(2, 1, 0)[:, :, None, :],
                            (half, 2, bsz, heads)).reshape(half, LANES)
    lw_s = jnp.tile(lnx_w[0].reshape(heads, HEAD).T, (1, LANES // heads))
    lb_s = jnp.tile(lnx_b[0].reshape(heads, HEAD).T, (1, LANES // heads))

    s0_p = jnp.zeros((half, HEAD, LANES), F32)
    y_p, sf_p = _wkv(to_lanes_p(rp), to_lanes_p(wp), to_lanes_p(kp), to_lanes_p(kkp),
                     to_lanes_p(ap), to_lanes_pi(vp), s0_p, rk_l, lw_p, lb_p,
                     tc=48, isplit=True, ri=2, unroll=2)
    s0_s = state_wkv[0].transpose(2, 3, 0, 1).reshape(HEAD, HEAD, nb * heads)
    y_s, sf_s = _wkv(to_lanes_s(rs), to_lanes_s(ws), to_lanes_s(ks), to_lanes_s(kks),
                     to_lanes_s(as_), to_lanes_s(vs), s0_s, rk_l, lw_s, lb_s,
                     tc=1, isplit=False, ri=2, unroll=1)

    ya_p = y_p.reshape(t_p, half, 2, bsz, heads).transpose(3, 0, 4, 2, 1).reshape(n_p, width)
    ya_s = y_s.reshape(HEAD, nb, heads).transpose(1, 2, 0).reshape(nb, width)
    wkv_p = sf_p.reshape(half, HEAD, 2, bsz, heads).transpose(3, 4, 2, 0, 1).reshape(
        1, bsz, heads, HEAD, HEAD)
    wkv_s = sf_s.reshape(HEAD, HEAD, nb, heads).transpose(2, 3, 0, 1)[None]

    s5c = _s5_consts(ssm_lam_re[0], ssm_lam_im[0], ssm_log_dt[0], ssm_b_re[0], ssm_b_im[0],
                     ssm_c_re[0], ssm_c_im[0], ssm_d[0], w_glu[0], b_glu)
    n_g, n_st = ssm_lam_re.shape[1], ssm_lam_re.shape[2]
    u_col_block = c_rkv // width
    yb_p, re_p, im_p = _s5(proj, 0, n_p, tm_seq, bsz, u_col_block, width, s5c)
    h0 = (state_ssm_re[0].reshape(nb, n_g * n_st), state_ssm_im[0].reshape(nb, n_g * n_st))
    yb_s, re_s, im_s = _s5(proj, s_row0 // nb, nb, nb, 1, u_col_block, width, s5c, h0=h0)

    def unify(a, b):
        return jnp.concatenate([a, jnp.zeros((s_row0 - n_p, a.shape[1]), a.dtype), b], axis=0)

    w_router = _pad_cols(jnp.concatenate([w_router_grp[0], w_router_exp[0]], axis=1), LANES)
    b_router = _pad_cols(jnp.concatenate([b_router_grp, b_router_exp], axis=1), LANES)
    assert n_grp + n_exp <= LANES and n_exp == n_grp * EXPERTS_PER_GROUP
    h1, xn2, comb = _merge(unify(ya_p, ya_s), unify(gp, gs), unify(yb_p, yb_s), proj, x_all,
                           w_br_a[0].astype(BF16), w_br_b[0].astype(BF16), w_out[0].astype(BF16),
                           g_ffn, w_router, b_router, tm=256, ga_col_block=(c_rkv + width) // d,
                           n_grp=n_grp)
    y_all = _moe(xn2, comb, h1, w_exp_gate[0].astype(BF16), w_exp_up[0].astype(BF16),
                 w_exp_down[0].astype(BF16), g_final.reshape(1, d), tm=384, n_grp=n_grp)

    y_prompt = y_all[:n_p].reshape(bsz, t_p, d)[:, N_META:]
    y_sample = y_all[s_row0:].reshape(nb, 1, d)
    last_p = proj[t_p - 1:n_p:t_p]
    shift_p = ungroup(last_p)[None]
    shift_s = ungroup(proj[s_row0:])[None]
    return (y_prompt, y_sample, shift_p, wkv_p,
            re_p.reshape(1, bsz, n_g, n_st), im_p.reshape(1, bsz, n_g, n_st),
            shift_s, wkv_s,
            re_s.reshape(1, nb, n_g, n_st), im_s.reshape(1, nb, n_g, n_st))
```

```python
import functools

import jax
import jax.numpy as jnp
from jax import lax
from jax.experimental import pallas as pl
from jax.experimental.pallas import tpu as pltpu

F32 = jnp.float32
BF16 = jnp.bfloat16

NORM_EPS = 1e-6
LNX_EPS = 64e-5
N_META = 16
HEAD = 64
SSM_GROUP = 16
SSM_STATE = 64
EXPERTS_PER_GROUP = 8
LANES = 128
LORA_PAD = 128
ROUTE_E1, ROUTE_E2, ROUTE_W1, ROUTE_W2 = 0, 1, 2, 3
VMEM_LIMIT = 56 * 1024 * 1024


def _cparams(sem):
    return pltpu.CompilerParams(dimension_semantics=sem, vmem_limit_bytes=VMEM_LIMIT)


def _const_spec(shape):
    nd = len(shape)
    return pl.BlockSpec(shape, lambda *_: (0,) * nd)


def _inproj_kernel(x_ref, g_ref, w_ref, o_ref, xn_ref):
    @pl.when(pl.program_id(1) == 0)
    def _():
        x = x_ref[...]
        ms = jnp.mean(x * x, axis=-1, keepdims=True)
        xn_ref[...] = (x * lax.rsqrt(ms + NORM_EPS) * g_ref[...]).astype(BF16)

    o_ref[...] = jnp.dot(xn_ref[...], w_ref[...], preferred_element_type=F32)


def _inproj(x_all, g_mix, w_in_p, tm, tn):
    nt, d = x_all.shape
    n_out = w_in_p.shape[1]
    return pl.pallas_call(
        _inproj_kernel,
        grid=(nt // tm, n_out // tn),
        in_specs=[pl.BlockSpec((tm, d), lambda i, j: (i, 0)),
                  pl.BlockSpec((1, d), lambda i, j: (0, 0)),
                  pl.BlockSpec((d, tn), lambda i, j: (0, j))],
        out_specs=pl.BlockSpec((tm, tn), lambda i, j: (i, j)),
        out_shape=jax.ShapeDtypeStruct((nt, n_out), F32),
        scratch_shapes=[pltpu.VMEM((tm, d), BF16)],
        compiler_params=_cparams(("parallel", "arbitrary")),
        name="inproj",
    )(x_all, g_mix, w_in_p)


def _softplus(z):
    return jnp.maximum(z, 0.0) + jnp.log1p(jnp.exp(-jnp.abs(z)))


def _rwkv_prep_kernel(*refs, width, carry_prev):
    if carry_prev:
        (rkv_ref, lo_ref, mu_rkv_ref, mu_lo_ref, w0_ref, a0_ref, kk_ref, ka_ref,
         wd_ref, wa_ref, wg_ref,
         r_out, w_out, k_out, v_out, kk_out, a_out, g_out, c_rkv, c_lo) = refs

        @pl.when(pl.program_id(1) == 0)
        def _():
            c_rkv[...] = jnp.zeros_like(c_rkv)
            c_lo[...] = jnp.zeros_like(c_lo)
    else:
        (rkv_ref, lo_ref, prev_rkv_ref, prev_lo_ref, mu_rkv_ref, mu_lo_ref, w0_ref, a0_ref,
         kk_ref, ka_ref, wd_ref, wa_ref, wg_ref,
         r_out, w_out, k_out, v_out, kk_out, a_out, g_out) = refs

    tm = rkv_ref.shape[0]
    first_row = lax.broadcasted_iota(jnp.int32, (tm, 1), 0) == 0

    def shifted(p, prev_ref, carry_ref, cols):
        if carry_prev:
            prev = jnp.where(first_row, carry_ref[:, cols], pltpu.roll(p, 1, 0))
        else:
            prev = prev_ref[:, cols]
        return prev

    def lerp(p, prev, mu):
        return p + (prev - p) * mu

    lo_cols = slice(0, lo_ref.shape[1])
    p_lo = lo_ref[...]
    q_lo = lerp(p_lo, shifted(p_lo, None if carry_prev else prev_lo_ref,
                              c_lo if carry_prev else None, lo_cols), mu_lo_ref[...])
    xw = q_lo[:, 0:LORA_PAD]
    xa = q_lo[:, LORA_PAD:2 * LORA_PAD]
    xg = q_lo[:, 2 * LORA_PAD:]
    dw = jnp.dot(jnp.tanh(xw).astype(BF16), wd_ref[...], preferred_element_type=F32)
    wlog = -_softplus(-(w0_ref[...] + dw)) - 0.5
    w_out[...] = jnp.exp(-jnp.exp(wlog))
    a = jax.nn.sigmoid(a0_ref[...] + jnp.dot(xa.astype(BF16), wa_ref[...],
                                             preferred_element_type=F32))
    a_out[...] = a
    g_out[...] = jnp.dot(jax.nn.sigmoid(xg).astype(BF16), wg_ref[...],
                         preferred_element_type=F32)

    def q_of(idx):
        cols = slice(idx * width, (idx + 1) * width)
        p = rkv_ref[:, cols]
        prev = shifted(p, None if carry_prev else prev_rkv_ref,
                       c_rkv if carry_prev else None, cols)
        return lerp(p, prev, mu_rkv_ref[:, cols])

    r_out[...] = q_of(0)
    k = q_of(1)
    kk_out[...] = k * kk_ref[...]
    k_out[...] = k * (1.0 + (a - 1.0) * ka_ref[...])
    v_out[...] = q_of(2)

    if carry_prev:
        c_rkv[...] = rkv_ref[tm - 1:tm, :]
        c_lo[...] = lo_ref[tm - 1:tm, :]


def _rwkv_prep(proj, row_block0, n_rows, tm, seqs, width, lo_col_block, lo_width,
               mu_rkv, mu_lo, w0, a0, k_k, k_a, wd, wa, wg, prev=None):
    carry_prev = prev is None
    per_seq = n_rows // seqs // tm
    row_map = lambda b, c: (row_block0 + b * per_seq + c, 0)
    lo_map = lambda b, c: (row_block0 + b * per_seq + c, lo_col_block)
    out_map = lambda b, c: (b * per_seq + c, 0)
    in_specs = [pl.BlockSpec((tm, 3 * width), row_map), pl.BlockSpec((tm, lo_width), lo_map)]
    args = [proj, proj]
    if not carry_prev:
        in_specs += [pl.BlockSpec((tm, 3 * width), out_map), pl.BlockSpec((tm, lo_width), out_map)]
        args += list(prev)
    consts = [mu_rkv, mu_lo, w0, a0, k_k, k_a, wd, wa, wg]
    in_specs += [_const_spec(c.shape) for c in consts]
    args += consts
    scratch = []
    if carry_prev:
        scratch = [pltpu.VMEM((1, 3 * width), F32), pltpu.VMEM((1, lo_width), F32)]
    out_sd = jax.ShapeDtypeStruct((n_rows, width), F32)
    return pl.pallas_call(
        functools.partial(_rwkv_prep_kernel, width=width, carry_prev=carry_prev),
        grid=(seqs, per_seq),
        in_specs=in_specs,
        out_specs=[pl.BlockSpec((tm, width), out_map)] * 7,
        out_shape=[out_sd] * 7,
        scratch_shapes=scratch,
        compiler_params=_cparams(("parallel", "arbitrary")),
        name="rwkv_prep_seq" if carry_prev else "rwkv_prep_step",
    )(*args)


def _wkv_kernel(r_ref, w_ref, k_ref, kk_ref, al_ref, v_ref, s0_ref, rk_ref, lw_ref, lb_ref,
                y_ref, sf_ref, s_scr, a_scr, b_scr, *, ni, tc, isplit, ri, unroll):
    @pl.when(pl.program_id(1) == 0)
    def _():
        s_scr[...] = s0_ref[...]

    def prep(t, carry):
        kk = kk_ref[t]
        ss = jnp.sum(kk * kk, axis=0, keepdims=True)
        kkn = kk / jnp.maximum(jnp.sqrt(ss), 1e-12)
        a_scr[t] = -kkn
        b_scr[t] = kkn * al_ref[t]
        return carry

    lax.fori_loop(0, tc, prep, 0)

    for g in range(ni // ri):
        rows = tuple(s_scr[g * ri + q] for q in range(ri))

        def step(t, rows, g=g):
            a = a_scr[t]
            w = w_ref[t]
            b = b_scr[t]
            k = k_ref[t]
            r = r_ref[t]
            new = []
            for q in range(ri):
                s = rows[q]
                sa = jnp.sum(s * a, axis=0, keepdims=True)
                vq = v_ref[t, g * ri + q:g * ri + q + 1, :]
                s = s * w + sa * b + vq * k
                y_ref[t, g * ri + q:g * ri + q + 1, :] = jnp.sum(s * r, axis=0, keepdims=True)
                new.append(s)
            return tuple(new)

        rows = lax.fori_loop(0, tc, step, rows, unroll=unroll)
        for q in range(ri):
            s_scr[g * ri + q] = rows[q]

    def isum(x):
        s = jnp.broadcast_to(jnp.sum(x, axis=0, keepdims=True), (8, LANES))
        if isplit:
            s = s + pltpu.roll(s, LANES // 2, 1)
        return s[0:1]

    def post(t, carry):
        y = y_ref[t]
        v = v_ref[t]
        mu = isum(y) * (1.0 / HEAD)
        d = y - mu
        var = isum(d * d) * (1.0 / HEAD)
        yn = d * lax.rsqrt(var + LNX_EPS) * lw_ref[...] + lb_ref[...]
        bonus = jnp.sum(r_ref[t] * k_ref[t] * rk_ref[...], axis=0, keepdims=True)
        y_ref[t] = yn + bonus * v
        return carry

    lax.fori_loop(0, tc, post, 0)

    @pl.when(pl.program_id(1) == pl.num_programs(1) - 1)
    def _():
        sf_ref[...] = s_scr[...]


def _wkv(r, w, k, kk, al, v, s0, rk, lw, lb, tc, isplit, ri, unroll):
    t, nj, lanes = r.shape
    ni = v.shape[1]
    jspec = pl.BlockSpec((tc, nj, LANES), lambda l, c: (c, 0, l))
    ispec = pl.BlockSpec((tc, ni, LANES), lambda l, c: (c, 0, l))
    sspec = pl.BlockSpec((ni, nj, LANES), lambda l, c: (0, 0, l))
    return pl.pallas_call(
        functools.partial(_wkv_kernel, ni=ni, tc=tc, isplit=isplit, ri=ri, unroll=unroll),
        grid=(lanes // LANES, t // tc),
        in_specs=[jspec, jspec, jspec, jspec, jspec, ispec, sspec,
                  _const_spec(rk.shape), _const_spec(lw.shape), _const_spec(lb.shape)],
        out_specs=[ispec, sspec],
        out_shape=[jax.ShapeDtypeStruct((t, ni, lanes), F32),
                   jax.ShapeDtypeStruct((ni, nj, lanes), F32)],
        scratch_shapes=[pltpu.VMEM((ni, nj, LANES), F32),
                        pltpu.VMEM((tc, nj, LANES), F32),
                        pltpu.VMEM((tc, nj, LANES), F32)],
        compiler_params=_cparams(("parallel", "arbitrary")),
        name="wkv_seq" if isplit else "wkv_step",
    )(r, w, k, kk, al, v, s0, rk, lw, lb)


def _s5_kernel(*refs, sequential, n_blk, scan_lanes):
    if sequential:
        (u_ref, bre_ref, bim_ref, cre_ref, cim_ref, d_ref, are_ref, aim_ref, wglu_ref, bglu_ref,
         yb_ref, hre_out, him_out, bure, buim, c_re, c_im) = refs
    else:
        (u_ref, h0re_ref, h0im_ref, bre_ref, bim_ref, cre_ref, cim_ref, d_ref, are_ref, aim_ref,
         wglu_ref, bglu_ref, yb_ref, hre_out, him_out, bure, buim) = refs

    tm = u_ref.shape[0]
    kin = bre_ref.shape[1]
    kst = bre_ref.shape[2]
    u = u_ref[...]
    ub = u.astype(BF16)
    for kb in range(n_blk):
        ukb = ub[:, kb * kin:(kb + 1) * kin]
        bure[:, kb * kst:(kb + 1) * kst] = jnp.dot(ukb, bre_ref[kb], preferred_element_type=F32)
        buim[:, kb * kst:(kb + 1) * kst] = jnp.dot(ukb, bim_ref[kb], preferred_element_type=F32)

    if sequential:
        @pl.when(pl.program_id(1) == 0)
        def _():
            c_re[...] = jnp.zeros_like(c_re)
            c_im[...] = jnp.zeros_like(c_im)

        for lb in range(bure.shape[1] // scan_lanes):
            sl = pl.ds(lb * scan_lanes, scan_lanes)
            ar = are_ref[:, sl]
            ai = aim_ref[:, sl]

            def step(t, h, sl=sl, ar=ar, ai=ai):
                hr, hi = h
                row = pl.ds(t, 1)
                nr = ar * hr - ai * hi + bure[row, sl]
                ni = ar * hi + ai * hr + buim[row, sl]
                bure[row, sl] = nr
                buim[row, sl] = ni
                return nr, ni

            hr, hi = lax.fori_loop(0, tm, step, (c_re[:, sl], c_im[:, sl]))
            c_re[:, sl] = hr
            c_im[:, sl] = hi
        hre_out[0] = c_re[...]
        him_out[0] = c_im[...]
    else:
        h0r = h0re_ref[...]
        h0i = h0im_ref[...]
        ar = are_ref[...]
        ai = aim_ref[...]
        nr = bure[...] + (ar * h0r - ai * h0i)
        ni = buim[...] + (ar * h0i + ai * h0r)
        bure[...] = nr
        buim[...] = ni
        hre_out[...] = nr
        him_out[...] = ni

    ys = []
    for kb in range(n_blk):
        cols = slice(kb * kst, (kb + 1) * kst)
        yre = jnp.dot(bure[:, cols].astype(BF16), cre_ref[kb], preferred_element_type=F32)
        yim = jnp.dot(buim[:, cols].astype(BF16), cim_ref[kb], preferred_element_type=F32)
        ys.append(yre - yim)
    y = jnp.concatenate(ys, axis=1) + d_ref[...] * u
    y = jax.nn.gelu(y)
    gate = jnp.dot(y.astype(BF16), wglu_ref[...], preferred_element_type=F32) + bglu_ref[...]
    yb_ref[...] = (y * jax.nn.sigmoid(gate)).astype(BF16)


def _s5(proj, row_block0, n_rows, tm, seqs, u_col_block, width, consts, h0=None):
    sequential = h0 is None
    bre = consts[0]
    n_blk, _, kst = bre.shape
    n_state = n_blk * kst
    per_seq = n_rows // seqs // tm
    u_map = lambda b, c: (row_block0 + b * per_seq + c, u_col_block)
    out_map = lambda b, c: (b * per_seq + c, 0)
    in_specs = [pl.BlockSpec((tm, width), u_map)]
    args = [proj]
    if not sequential:
        in_specs += [pl.BlockSpec((tm, n_state), out_map)] * 2
        args += list(h0)
    in_specs += [_const_spec(c.shape) for c in consts]
    args += list(consts)
    scratch = [pltpu.VMEM((tm, n_state), F32), pltpu.VMEM((tm, n_state), F32)]
    if sequential:
        scratch += [pltpu.VMEM((1, n_state), F32), pltpu.VMEM((1, n_state), F32)]
        st_spec = pl.BlockSpec((1, 1, n_state), lambda b, c: (b, 0, 0))
        st_shape = jax.ShapeDtypeStruct((seqs, 1, n_state), F32)
    else:
        st_spec = pl.BlockSpec((tm, n_state), out_map)
        st_shape = jax.ShapeDtypeStruct((n_rows, n_state), F32)
    return pl.pallas_call(
        functools.partial(_s5_kernel, sequential=sequential, n_blk=n_blk, scan_lanes=1024),
        grid=(seqs, per_seq),
        in_specs=in_specs,
        out_specs=[pl.BlockSpec((tm, width), out_map), st_spec, st_spec],
        out_shape=[jax.ShapeDtypeStruct((n_rows, width), BF16), st_shape, st_shape],
        scratch_shapes=scratch,
        compiler_params=_cparams(("parallel", "arbitrary")),
        name="s5_seq" if sequential else "s5_step",
    )(*args)


def _route(logits, n_grp):
    lane = lax.broadcasted_iota(jnp.int32, logits.shape, 1).astype(F32)
    neg = jnp.float32(-1e30)
    big = jnp.float32(1e9)
    is_grp = lane < n_grp
    gl = jnp.where(is_grp, logits, neg)
    gmax = jnp.max(gl, axis=1, keepdims=True)
    gsum = jnp.sum(jnp.where(is_grp, jnp.exp(gl - gmax), 0.0), axis=1, keepdims=True)
    g_p = 1.0 / gsum
    g_idx = jnp.min(jnp.where(is_grp & (gl == gmax), lane, big), axis=1, keepdims=True)
    lo = n_grp + g_idx * EXPERTS_PER_GROUP
    in_grp = (lane >= lo) & (lane < lo + EXPERTS_PER_GROUP)
    el = jnp.where(in_grp, logits, neg)
    v1 = jnp.max(el, axis=1, keepdims=True)
    i1 = jnp.min(jnp.where(in_grp & (el == v1), lane, big), axis=1, keepdims=True)
    rest = in_grp & (lane != i1)
    el2 = jnp.where(rest, logits, neg)
    v2 = jnp.max(el2, axis=1, keepdims=True)
    i2 = jnp.min(jnp.where(rest & (el2 == v2), lane, big), axis=1, keepdims=True)
    e2 = jnp.exp(v2 - v1)
    w1 = g_p / (1.0 + e2)
    w2 = g_p * e2 / (1.0 + e2)
    return (jnp.where(lane == ROUTE_E1, i1 - n_grp, 0.0) + jnp.where(lane == ROUTE_E2, i2 - n_grp, 0.0)
            + jnp.where(lane == ROUTE_W1, w1, 0.0) + jnp.where(lane == ROUTE_W2, w2, 0.0))


def _merge_kernel(ya_ref, g_ref, yb_ref, ga_ref, gb_ref, x_ref, wa_ref, wb_ref, wo_ref,
                  gf_ref, wr_ref, br_ref, h_out, xn_out, comb_out, *, n_grp):
    ya = (ya_ref[...] * g_ref[...]).astype(BF16)
    ma = jnp.dot(ya, wa_ref[...], preferred_element_type=F32)
    mb = jnp.dot(yb_ref[...], wb_ref[...], preferred_element_type=F32)
    merged = jax.nn.sigmoid(ga_ref[...]) * ma + jax.nn.sigmoid(gb_ref[...]) * mb
    h = x_ref[...] + jnp.dot(merged.astype(BF16), wo_ref[...], preferred_element_type=F32)
    h_out[...] = h
    ms = jnp.mean(h * h, axis=-1, keepdims=True)
    xn = h * lax.rsqrt(ms + NORM_EPS) * gf_ref[...]
    xn_out[...] = xn
    logits = jnp.dot(xn, wr_ref[...], preferred_element_type=F32,
                     precision=lax.Precision.HIGHEST) + br_ref[...]
    comb_out[...] = _route(logits, n_grp)


def _merge(ya, g, yb, proj, x_all, w_br_a, w_br_b, w_out, g_ffn, w_router, b_router,
           tm, ga_col_block, n_grp):
    nt, d = x_all.shape
    wdt = ya.shape[1]
    row = lambda i: (i, 0)
    single = dict(pipeline_mode=pl.Buffered(1))
    in_specs = [pl.BlockSpec((tm, wdt), row), pl.BlockSpec((tm, wdt), row),
                pl.BlockSpec((tm, wdt), row),
                pl.BlockSpec((tm, d), lambda i: (i, ga_col_block)),
                pl.BlockSpec((tm, d), lambda i: (i, ga_col_block + 1)),
                pl.BlockSpec((tm, d), row),
                pl.BlockSpec(w_br_a.shape, lambda i: (0, 0), **single),
                pl.BlockSpec(w_br_b.shape, lambda i: (0, 0), **single),
                pl.BlockSpec(w_out.shape, lambda i: (0, 0), **single),
                _const_spec(g_ffn.shape), _const_spec(w_router.shape), _const_spec(b_router.shape)]
    return pl.pallas_call(
        functools.partial(_merge_kernel, n_grp=n_grp),
        grid=(nt // tm,),
        in_specs=in_specs,
        out_specs=[pl.BlockSpec((tm, d), row), pl.BlockSpec((tm, d), row),
                   pl.BlockSpec((tm, LANES), row)],
        out_shape=[jax.ShapeDtypeStruct((nt, d), F32), jax.ShapeDtypeStruct((nt, d), F32),
                   jax.ShapeDtypeStruct((nt, LANES), F32)],
        compiler_params=_cparams(("parallel",)),
        name="merge_route",
    )(ya, g, yb, proj, proj, x_all, w_br_a, w_br_b, w_out, g_ffn, w_router, b_router)


def _moe_kernel(te_ref, first_ref, nvalid_ref, nused_ref, src_cur, src_nxt, dst_cur,
                xn_hbm, wg_ref, wu_ref, wd_ref, y_hbm,
                xbuf, obuf, wg_bf, wu_bf, wd_bf, gsem, ssem, *, tm):
    i = pl.program_id(0)
    last = pl.num_programs(0) - 1
    n_used = nused_ref[0]
    slot = lax.rem(i, 2)

    def gather_start(src, s):
        def body(r, c):
            pltpu.make_async_copy(xn_hbm.at[pl.ds(src[0, 0, r], 1)], xbuf.at[s, pl.ds(r, 1)],
                                  gsem.at[s]).start()
            return c
        lax.fori_loop(0, tm, body, 0, unroll=8)

    def gather_wait(s):
        pltpu.make_async_copy(xn_hbm.at[pl.ds(0, tm)], xbuf.at[s], gsem.at[s]).wait()

    def scatter_start(s, n_rows):
        def body(r, c):
            pltpu.make_async_copy(obuf.at[s, pl.ds(r, 1)], y_hbm.at[pl.ds(dst_cur[0, 0, r], 1)],
                                  ssem.at[s]).start()
            return c
        lax.fori_loop(0, n_rows, body, 0)

    def scatter_wait(s, n_rows):
        n_full = pl.multiple_of(lax.shift_left(lax.shift_right_logical(n_rows, 3), 3), 8)

        @pl.when(n_full > 0)
        def _():
            pltpu.make_async_copy(obuf.at[s, pl.ds(0, n_full)], y_hbm.at[pl.ds(0, n_full)],
                                  ssem.at[s]).wait()

        def body(r, c):
            pltpu.make_async_copy(obuf.at[s, pl.ds(r, 1)], y_hbm.at[pl.ds(r, 1)],
                                  ssem.at[s]).wait()
            return c
        lax.fori_loop(n_full, n_rows, body, 0)

    @pl.when(i == 0)
    def _():
        gather_start(src_cur, 0)

    @pl.when(i < n_used)
    def _():
        gather_wait(slot)

    @pl.when(i + 1 < n_used)
    def _():
        gather_start(src_nxt, 1 - slot)

    @pl.when((i >= 2) & (i - 2 < n_used))
    def _():
        scatter_wait(slot, nvalid_ref[i - 2])

    @pl.when(i < n_used)
    def _():
        @pl.when(first_ref[i] == 1)
        def _():
            wg_bf[...] = wg_ref[0].astype(BF16)
            wu_bf[...] = wu_ref[0].astype(BF16)
            wd_bf[...] = wd_ref[0].astype(BF16)

        x = xbuf[slot].astype(BF16)
        xg = jnp.dot(x, wg_bf[...], preferred_element_type=F32)
        xu = jnp.dot(x, wu_bf[...], preferred_element_type=F32)
        hid = (jax.nn.silu(xg) * xu).astype(BF16)
        obuf[slot] = jnp.dot(hid, wd_bf[...], preferred_element_type=F32)
        scatter_start(slot, nvalid_ref[i])

    @pl.when(i == last)
    def _():
        @pl.when((i >= 1) & (i - 1 < n_used))
        def _():
            scatter_wait(1 - slot, nvalid_ref[i - 1])

        @pl.when(i < n_used)
        def _():
            scatter_wait(slot, nvalid_ref[i])


def _moe(xn, tile_expert, tile_first, tile_valid, n_used, src_rows, dst_rows, wg, wu, wd,
         n_out_rows, tm):
    nt, d = xn.shape
    n_exp, _, de = wg.shape
    n_tiles = src_rows.shape[0]
    smem_cur = pl.BlockSpec((1, 1, tm), lambda i, *_: (i, 0, 0), memory_space=pltpu.SMEM)
    smem_nxt = pl.BlockSpec((1, 1, tm), lambda i, *_: (jnp.minimum(i + 1, n_tiles - 1), 0, 0),
                            memory_space=pltpu.SMEM)
    grid_spec = pltpu.PrefetchScalarGridSpec(
        num_scalar_prefetch=4,
        grid=(n_tiles,),
        in_specs=[smem_cur, smem_nxt, smem_cur,
                  pl.BlockSpec(memory_space=pl.ANY),
                  pl.BlockSpec((1, d, de), lambda i, te, *_: (te[i], 0, 0)),
                  pl.BlockSpec((1, d, de), lambda i, te, *_: (te[i], 0, 0)),
                  pl.BlockSpec((1, de, d), lambda i, te, *_: (te[i], 0, 0))],
        out_specs=pl.BlockSpec(memory_space=pl.ANY),
        scratch_shapes=[pltpu.VMEM((2, tm, d), F32), pltpu.VMEM((2, tm, d), F32),
                        pltpu.VMEM((d, de), BF16), pltpu.VMEM((d, de), BF16),
                        pltpu.VMEM((de, d), BF16),
                        pltpu.SemaphoreType.DMA((2,)), pltpu.SemaphoreType.DMA((2,))])
    return pl.pallas_call(
        functools.partial(_moe_kernel, tm=tm),
        grid_spec=grid_spec,
        out_shape=jax.ShapeDtypeStruct((n_out_rows, d), F32),
        compiler_params=_cparams(("arbitrary",)),
        name="moe_grouped",
    )(tile_expert, tile_first, tile_valid, n_used, src_rows, src_rows, dst_rows, xn, wg, wu, wd)


def _moe_plan(route, n_exp, tm):
    nt = route.shape[0]
    n_pairs = 2 * nt
    n_tiles = n_pairs // tm + n_exp
    eid = jnp.concatenate([route[:, ROUTE_E1], route[:, ROUTE_E2]]).astype(jnp.int32)
    onehot = (eid[:, None] == jnp.arange(n_exp, dtype=jnp.int32)[None, :]).astype(jnp.int32)
    csum = jnp.cumsum(onehot, axis=0)
    rank = jnp.take_along_axis(csum, eid[:, None], axis=1)[:, 0] - 1
    cnt = csum[-1]
    tiles = (cnt + tm - 1) // tm
    tile_end = jnp.cumsum(tiles)
    n_used = tile_end[-1]
    pos = (tile_end - tiles)[eid] * tm + rank
    pair = jnp.arange(n_pairs, dtype=jnp.int32)
    src_rows = jnp.zeros((n_tiles * tm,), jnp.int32).at[pos].set(pair % nt)
    dst_rows = jnp.zeros((n_tiles * tm,), jnp.int32).at[pos].set(pair)
    tile_id = jnp.minimum(jnp.arange(n_tiles, dtype=jnp.int32), n_used - 1)
    tile_expert = jnp.searchsorted(tile_end, tile_id, side='right').astype(jnp.int32)
    tile_first = jnp.concatenate(
        [jnp.ones((1,), jnp.int32), (tile_expert[1:] != tile_expert[:-1]).astype(jnp.int32)])
    tile_valid = jnp.clip(cnt[tile_expert] - (tile_id - (tile_end - tiles)[tile_expert]) * tm, 0, tm)
    return (tile_expert, tile_first, tile_valid.astype(jnp.int32),
            n_used.reshape(1).astype(jnp.int32),
            src_rows.reshape(n_tiles, 1, tm), dst_rows.reshape(n_tiles, 1, tm), n_pairs)


def _final_kernel(h_ref, y1_ref, y2_ref, route_ref, g_ref, o_ref):
    lane = lax.broadcasted_iota(jnp.int32, route_ref.shape, 1)
    route = route_ref[...]
    w1 = jnp.sum(jnp.where(lane == ROUTE_W1, route, 0.0), axis=1, keepdims=True)
    w2 = jnp.sum(jnp.where(lane == ROUTE_W2, route, 0.0), axis=1, keepdims=True)
    h = h_ref[...] + (w1 * y1_ref[...] + w2 * y2_ref[...])
    ms = jnp.mean(h * h, axis=-1, keepdims=True)
    o_ref[...] = h * lax.rsqrt(ms + NORM_EPS) * g_ref[...]


def _final(h, y_moe, route, g_final, tm):
    nt, d = h.shape
    row = lambda i: (i, 0)
    return pl.pallas_call(
        _final_kernel,
        grid=(nt // tm,),
        in_specs=[pl.BlockSpec((tm, d), row), pl.BlockSpec((tm, d), row),
                  pl.BlockSpec((tm, d), lambda i: (i + nt // tm, 0)),
                  pl.BlockSpec((tm, LANES), row), pl.BlockSpec((1, d), lambda i: (0, 0))],
        out_specs=pl.BlockSpec((tm, d), row),
        out_shape=jax.ShapeDtypeStruct((nt, d), F32),
        compiler_params=_cparams(("parallel",)),
        name="final_norm",
    )(h, y_moe, y_moe, route, g_final)


def _pad_cols(w, to):
    return jnp.pad(w, ((0, 0), (0, to - w.shape[1])))


def _pad_rows(w, to):
    return jnp.pad(w, ((0, to - w.shape[0]), (0, 0)))


def _s5_consts(lam_re, lam_im, log_dt, b_re, b_im, c_re, c_im, d, w_glu, b_glu):
    dt = jnp.exp(log_dt)[:, None]
    mag = jnp.exp(lam_re * dt)
    abar_re = mag * jnp.cos(lam_im * dt)
    abar_im = mag * jnp.sin(lam_im * dt)
    den = lam_re * lam_re + lam_im * lam_im
    nr = abar_re - 1.0
    coef_re = (nr * lam_re + abar_im * lam_im) / den
    coef_im = (abar_im * lam_re - nr * lam_im) / den
    bbar_re = coef_re[..., None] * b_re - coef_im[..., None] * b_im
    bbar_im = coef_re[..., None] * b_im + coef_im[..., None] * b_re
    n_g, n_p, n_c = b_re.shape
    gpb = LANES // n_c
    eye = jnp.eye(gpb, dtype=F32)

    def in_blk(bb):
        bb = bb.reshape(n_g // gpb, gpb, n_p, n_c)
        return jnp.einsum('kgpc,gh->kgchp', bb, eye).reshape(
            n_g // gpb, gpb * n_c, gpb * n_p).astype(BF16)

    def out_blk(cc):
        cc = cc.reshape(n_g // gpb, gpb, n_c, n_p)
        return jnp.einsum('kgcp,gh->khpgc', cc, eye).reshape(
            n_g // gpb, gpb * n_p, gpb * n_c).astype(BF16)

    return (in_blk(bbar_re), in_blk(bbar_im), out_blk(c_re), out_blk(c_im),
            d.reshape(1, -1), abar_re.reshape(1, -1), abar_im.reshape(1, -1),
            w_glu.astype(BF16), b_glu.reshape(1, -1))


def kernel(x_prompt, x_sample, state_shift, state_wkv, state_ssm_re, state_ssm_im, meta_tokens, g_mix, w_in, shift_mu, w0, w_decay_up, a0, w_aaa_up, w_gate_up, k_k, k_a, r_k, lnx_w, lnx_b, ssm_lam_re, ssm_lam_im, ssm_log_dt, ssm_b_re, ssm_b_im, ssm_c_re, ssm_c_im, ssm_d, w_glu, b_glu, w_br_a, w_br_b, w_out, g_ffn, w_router_grp, b_router_grp, w_router_exp, b_router_exp, w_exp_gate, w_exp_up, w_exp_down, g_final):
    depth = g_mix.shape[0]
    assert depth == 1, "single-layer trunk"
    bsz, seq, d = x_prompt.shape
    nb = x_sample.shape[0]
    assert x_sample.shape[1] == 1
    t_p = seq + N_META
    n_p = bsz * t_p
    width = k_k.shape[1]
    heads = width // HEAD
    n_dl, n_al, n_gl = w_decay_up.shape[1], w_aaa_up.shape[1], w_gate_up.shape[1]
    n_grp = w_router_grp.shape[2]
    n_exp = w_router_exp.shape[2]
    assert bsz * heads * 2 == LANES and (nb * heads) % LANES == 0

    s_row0 = -(-n_p // nb) * nb
    n_t = s_row0 + nb
    meta = jnp.broadcast_to(meta_tokens[None], (bsz, N_META, d))
    x_all = jnp.concatenate(
        [jnp.concatenate([meta, x_prompt], axis=1).reshape(n_p, d),
         jnp.zeros((s_row0 - n_p, d), F32), x_sample.reshape(nb, d)], axis=0)

    c_rkv = 3 * width
    c_xw, c_xa, c_xg = c_rkv, c_rkv + n_dl, c_rkv + n_dl + n_al
    c_u = c_xg + n_gl
    c_ga = c_u + width
    lo_width = 2 * LORA_PAD + n_gl

    def regroup(m):
        return jnp.concatenate(
            [m[:, :c_rkv], m[:, c_u:], _pad_cols(m[:, c_xw:c_xa], LORA_PAD),
             _pad_cols(m[:, c_xa:c_xg], LORA_PAD), m[:, c_xg:c_u]], axis=1)

    def ungroup(m):
        lo = c_rkv + width + 2 * d
        return jnp.concatenate(
            [m[:, :c_rkv], m[:, lo:lo + n_dl], m[:, lo + LORA_PAD:lo + LORA_PAD + n_al],
             m[:, lo + 2 * LORA_PAD:]], axis=1)

    w_in_p = regroup(w_in[0]).astype(BF16)
    n_proj = w_in_p.shape[1]
    proj = _inproj(x_all, g_mix, w_in_p, tm=768, tn=512)

    mu_all = regroup(_pad_cols(shift_mu, c_u + width + 2 * d))
    mu_rkv, mu_lo = mu_all[:, :c_rkv], mu_all[:, n_proj - lo_width:]
    st_all = regroup(_pad_cols(state_shift[0], c_u + width + 2 * d))
    prev_s = (st_all[:, :c_rkv], st_all[:, n_proj - lo_width:])
    lo_col_block = (n_proj - lo_width) // lo_width
    prep_w = (mu_rkv, mu_lo, w0, a0, k_k, k_a,
              _pad_rows(w_decay_up[0], LORA_PAD).astype(BF16),
              _pad_rows(w_aaa_up[0], LORA_PAD).astype(BF16), w_gate_up[0].astype(BF16))

    tm_seq = t_p // 6
    rp, wp, kp, vp, kkp, ap, gp = _rwkv_prep(proj, 0, n_p, tm_seq, bsz, width, lo_col_block,
                                             lo_width, *prep_w)
    rs, ws, ks, vs, kks, as_, gs = _rwkv_prep(proj, s_row0 // nb, nb, nb, 1, width, lo_col_block,
                                              lo_width, *prep_w, prev=prev_s)

    half = HEAD // 2

    def to_lanes_p(z):
        z = z.reshape(bsz, t_p, heads, HEAD).transpose(1, 3, 0, 2).reshape(t_p, HEAD, bsz * heads)
        return jnp.concatenate([z, z], axis=-1)

    def to_lanes_pi(z):
        z = z.reshape(bsz, t_p, heads, 2, half).transpose(1, 4, 3, 0, 2)
        return z.reshape(t_p, half, LANES)

    def to_lanes_s(z):
        return z.reshape(nb, heads, HEAD).transpose(2, 0, 1).reshape(1, HEAD, nb * heads)

    rk_l = jnp.tile(r_k[0].T, (1, LANES // heads))
    lw_p = jnp.broadcast_to(lnx_w[0].reshape(heads, 2, half).transpose(2, 1, 0)[:, :, None, :],
                            (half, 2, bsz, heads)).reshape(half, LANES)
    lb_p = jnp.broadcast_to(lnx_b[0].reshape(heads, 2, half).transpose(2, 1, 0)[:, :, None, :],
                            (half, 2, bsz, heads)).reshape(half, LANES)
    lw_s = jnp.tile(lnx_w[0].reshape(heads, HEAD).T, (1, LANES // heads))
    lb_s = jnp.tile(lnx_b[0].reshape(heads, HEAD).T, (1, LANES // heads))

    s0_p = jnp.zeros((half, HEAD, LANES), F32)
    y_p, sf_p = _wkv(to_lanes_p(rp), to_lanes_p(wp), to_lanes_p(kp), to_lanes_p(kkp),
                     to_lanes_p(ap), to_lanes_pi(vp), s0_p, rk_l, lw_p, lb_p,
                     tc=48, isplit=True, ri=2, unroll=2)
    s0_s = state_wkv[0].transpose(2, 3, 0, 1).reshape(HEAD, HEAD, nb * heads)
    y_s, sf_s = _wkv(to_lanes_s(rs), to_lanes_s(ws), to_lanes_s(ks), to_lanes_s(kks),
                     to_lanes_s(as_), to_lanes_s(vs), s0_s, rk_l, lw_s, lb_s,
                     tc=1, isplit=False, ri=2, unroll=1)

    ya_p = y_p.reshape(t_p, half, 2, bsz, heads).transpose(3, 0, 4, 2, 1).reshape(n_p, width)
    ya_s = y_s.reshape(HEAD, nb, heads).transpose(1, 2, 0).reshape(nb, width)
    wkv_p = sf_p.reshape(half, HEAD, 2, bsz, heads).transpose(3, 4, 2, 0, 1).reshape(
        1, bsz, heads, HEAD, HEAD)
    wkv_s = sf_s.reshape(HEAD, HEAD, nb, heads).transpose(2, 3, 0, 1)[None]

    s5c = _s5_consts(ssm_lam_re[0], ssm_lam_im[0], ssm_log_dt[0], ssm_b_re[0], ssm_b_im[0],
                     ssm_c_re[0], ssm_c_im[0], ssm_d[0], w_glu[0], b_glu)
    n_g, n_st = ssm_lam_re.shape[1], ssm_lam_re.shape[2]
    u_col_block = c_rkv // width
    yb_p, re_p, im_p = _s5(proj, 0, n_p, tm_seq, bsz, u_col_block, width, s5c)
    h0 = (state_ssm_re[0].reshape(nb, n_g * n_st), state_ssm_im[0].reshape(nb, n_g * n_st))
    yb_s, re_s, im_s = _s5(proj, s_row0 // nb, nb, nb, 1, u_col_block, width, s5c, h0=h0)

    def unify(a, b):
        return jnp.concatenate([a, jnp.zeros((s_row0 - n_p, a.shape[1]), a.dtype), b], axis=0)

    w_router = _pad_cols(jnp.concatenate([w_router_grp[0], w_router_exp[0]], axis=1), LANES)
    b_router = _pad_cols(jnp.concatenate([b_router_grp, b_router_exp], axis=1), LANES)
    assert n_grp + n_exp <= LANES and n_exp == n_grp * EXPERTS_PER_GROUP
    h1, xn2, route = _merge(unify(ya_p, ya_s), unify(gp, gs), unify(yb_p, yb_s), proj, x_all,
                           w_br_a[0].astype(BF16), w_br_b[0].astype(BF16), w_out[0].astype(BF16),
                           g_ffn, w_router, b_router, tm=256, ga_col_block=(c_rkv + width) // d,
                           n_grp=n_grp)
    tm_moe = 256
    te, tf, tv, nu, src_rows, dst_rows, n_moe_rows = _moe_plan(route, n_exp, tm_moe)
    y_moe = _moe(xn2, te, tf, tv, nu, src_rows, dst_rows, w_exp_gate[0], w_exp_up[0], w_exp_down[0],
                 n_moe_rows, tm_moe)
    y_all = _final(h1, y_moe, route, g_final.reshape(1, d), tm=256)

    y_prompt = y_all[:n_p].reshape(bsz, t_p, d)[:, N_META:]
    y_sample = y_all[s_row0:].reshape(nb, 1, d)
    last_p = proj[t_p - 1:n_p:t_p]
    shift_p = ungroup(last_p)[None]
    shift_s = ungroup(proj[s_row0:])[None]
    return (y_prompt, y_sample, shift_p, wkv_p,
            re_p.reshape(1, bsz, n_g, n_st), im_p.reshape(1, bsz, n_g, n_st),
            shift_s, wkv_s,
            re_s.reshape(1, nb, n_g, n_st), im_s.reshape(1, nb, n_g, n_st))
```

```python
import functools

import jax
import jax.numpy as jnp
from jax import lax
from jax.experimental import pallas as pl
from jax.experimental.pallas import tpu as pltpu

F32 = jnp.float32
BF16 = jnp.bfloat16

NORM_EPS = 1e-6
LNX_EPS = 64e-5
N_META = 16
HEAD = 64
SSM_GROUP = 16
SSM_STATE = 64
EXPERTS_PER_GROUP = 8
LANES = 128
LORA_PAD = 128
ROUTE_E1, ROUTE_E2, ROUTE_W1, ROUTE_W2 = 0, 1, 2, 3
VMEM_LIMIT = 56 * 1024 * 1024


def _cparams(sem):
    return pltpu.CompilerParams(dimension_semantics=sem, vmem_limit_bytes=VMEM_LIMIT)


def _const_spec(shape):
    nd = len(shape)
    return pl.BlockSpec(shape, lambda *_: (0,) * nd)


def _inproj_kernel(x_ref, g_ref, w_ref, o_ref, xn_ref):
    @pl.when(pl.program_id(1) == 0)
    def _():
        x = x_ref[...]
        ms = jnp.mean(x * x, axis=-1, keepdims=True)
        xn_ref[...] = (x * lax.rsqrt(ms + NORM_EPS) * g_ref[...]).astype(BF16)

    o_ref[...] = jnp.dot(xn_ref[...], w_ref[...], preferred_element_type=F32)


def _inproj(x_all, g_mix, w_in_p, tm, tn):
    nt, d = x_all.shape
    n_out = w_in_p.shape[1]
    return pl.pallas_call(
        _inproj_kernel,
        grid=(nt // tm, n_out // tn),
        in_specs=[pl.BlockSpec((tm, d), lambda i, j: (i, 0)),
                  pl.BlockSpec((1, d), lambda i, j: (0, 0)),
                  pl.BlockSpec((d, tn), lambda i, j: (0, j))],
        out_specs=pl.BlockSpec((tm, tn), lambda i, j: (i, j)),
        out_shape=jax.ShapeDtypeStruct((nt, n_out), F32),
        scratch_shapes=[pltpu.VMEM((tm, d), BF16)],
        compiler_params=_cparams(("parallel", "arbitrary")),
        name="inproj",
    )(x_all, g_mix, w_in_p)


def _softplus(z):
    return jnp.maximum(z, 0.0) + jnp.log1p(jnp.exp(-jnp.abs(z)))


def _rwkv_prep_kernel(*refs, width, carry_prev):
    if carry_prev:
        (rkv_ref, lo_ref, mu_rkv_ref, mu_lo_ref, w0_ref, a0_ref, kk_ref, ka_ref,
         wd_ref, wa_ref, wg_ref,
         r_out, w_out, k_out, v_out, kk_out, a_out, g_out, c_rkv, c_lo) = refs

        @pl.when(pl.program_id(1) == 0)
        def _():
            c_rkv[...] = jnp.zeros_like(c_rkv)
            c_lo[...] = jnp.zeros_like(c_lo)
    else:
        (rkv_ref, lo_ref, prev_rkv_ref, prev_lo_ref, mu_rkv_ref, mu_lo_ref, w0_ref, a0_ref,
         kk_ref, ka_ref, wd_ref, wa_ref, wg_ref,
         r_out, w_out, k_out, v_out, kk_out, a_out, g_out) = refs

    tm = rkv_ref.shape[0]
    first_row = lax.broadcasted_iota(jnp.int32, (tm, 1), 0) == 0

    def shifted(p, prev_ref, carry_ref, cols):
        if carry_prev:
            prev = jnp.where(first_row, carry_ref[:, cols], pltpu.roll(p, 1, 0))
        else:
            prev = prev_ref[:, cols]
        return prev

    def lerp(p, prev, mu):
        return p + (prev - p) * mu

    lo_cols = slice(0, lo_ref.shape[1])
    p_lo = lo_ref[...]
    q_lo = lerp(p_lo, shifted(p_lo, None if carry_prev else prev_lo_ref,
                              c_lo if carry_prev else None, lo_cols), mu_lo_ref[...])
    xw = q_lo[:, 0:LORA_PAD]
    xa = q_lo[:, LORA_PAD:2 * LORA_PAD]
    xg = q_lo[:, 2 * LORA_PAD:]
    dw = jnp.dot(jnp.tanh(xw).astype(BF16), wd_ref[...], preferred_element_type=F32)
    wlog = -_softplus(-(w0_ref[...] + dw)) - 0.5
    w_out[...] = jnp.exp(-jnp.exp(wlog))
    a = jax.nn.sigmoid(a0_ref[...] + jnp.dot(xa.astype(BF16), wa_ref[...],
                                             preferred_element_type=F32))
    a_out[...] = a
    g_out[...] = jnp.dot(jax.nn.sigmoid(xg).astype(BF16), wg_ref[...],
                         preferred_element_type=F32)

    def q_of(idx):
        cols = slice(idx * width, (idx + 1) * width)
        p = rkv_ref[:, cols]
        prev = shifted(p, None if carry_prev else prev_rkv_ref,
                       c_rkv if carry_prev else None, cols)
        return lerp(p, prev, mu_rkv_ref[:, cols])

    r_out[...] = q_of(0)
    k = q_of(1)
    kk_out[...] = k * kk_ref[...]
    k_out[...] = k * (1.0 + (a - 1.0) * ka_ref[...])
    v_out[...] = q_of(2)

    if carry_prev:
        c_rkv[...] = rkv_ref[tm - 1:tm, :]
        c_lo[...] = lo_ref[tm - 1:tm, :]


def _rwkv_prep(proj, row_block0, n_rows, tm, seqs, width, lo_col_block, lo_width,
               mu_rkv, mu_lo, w0, a0, k_k, k_a, wd, wa, wg, prev=None):
    carry_prev = prev is None
    per_seq = n_rows // seqs // tm
    row_map = lambda b, c: (row_block0 + b * per_seq + c, 0)
    lo_map = lambda b, c: (row_block0 + b * per_seq + c, lo_col_block)
    out_map = lambda b, c: (b * per_seq + c, 0)
    in_specs = [pl.BlockSpec((tm, 3 * width), row_map), pl.BlockSpec((tm, lo_width), lo_map)]
    args = [proj, proj]
    if not carry_prev:
        in_specs += [pl.BlockSpec((tm, 3 * width), out_map), pl.BlockSpec((tm, lo_width), out_map)]
        args += list(prev)
    consts = [mu_rkv, mu_lo, w0, a0, k_k, k_a, wd, wa, wg]
    in_specs += [_const_spec(c.shape) for c in consts]
    args += consts
    scratch = []
    if carry_prev:
        scratch = [pltpu.VMEM((1, 3 * width), F32), pltpu.VMEM((1, lo_width), F32)]
    out_sd = jax.ShapeDtypeStruct((n_rows, width), F32)
    return pl.pallas_call(
        functools.partial(_rwkv_prep_kernel, width=width, carry_prev=carry_prev),
        grid=(seqs, per_seq),
        in_specs=in_specs,
        out_specs=[pl.BlockSpec((tm, width), out_map)] * 7,
        out_shape=[out_sd] * 7,
        scratch_shapes=scratch,
        compiler_params=_cparams(("parallel", "arbitrary")),
        name="rwkv_prep_seq" if carry_prev else "rwkv_prep_step",
    )(*args)


def _wkv_kernel(r_ref, w_ref, k_ref, kk_ref, al_ref, v_ref, s0_ref, rk_ref, lw_ref, lb_ref,
                y_ref, sf_ref, s_scr, a_scr, b_scr, *, ni, nj, tc, isplit, unroll):
    @pl.when(pl.program_id(1) == 0)
    def _():
        s_scr[...] = s0_ref[...]

    def prep(t, carry):
        kk = kk_ref[t]
        ss = jnp.sum(kk * kk, axis=0, keepdims=True)
        kkn = kk / jnp.maximum(jnp.sqrt(ss), 1e-12)
        a_scr[t] = -kkn
        b_scr[t] = kkn * al_ref[t]
        return carry

    lax.fori_loop(0, tc, prep, 0, unroll=unroll)

    def row(ref, t, j):
        return ref[t, j:j + 1, :]

    def tree(parts):
        while len(parts) > 1:
            parts = [parts[i] + parts[i + 1] for i in range(0, len(parts), 2)]
        return parts[0]

    n_acc = 4
    sa0 = tree([sum(s_scr[j] * row(a_scr, 0, j) for j in range(q, nj, n_acc))
                for q in range(n_acc)])

    def step(t, sa):
        v = v_ref[t]
        tn = jnp.minimum(t + 1, tc - 1)
        y = [None] * n_acc
        san = [None] * n_acc
        for j in range(nj):
            s = s_scr[j] * row(w_ref, t, j) + sa * row(b_scr, t, j) + v * row(k_ref, t, j)
            s_scr[j] = s
            yj = s * row(r_ref, t, j)
            sj = s * row(a_scr, tn, j)
            q = j % n_acc
            y[q] = yj if y[q] is None else y[q] + yj
            san[q] = sj if san[q] is None else san[q] + sj
        y_ref[t] = tree(y)
        return tree(san)

    lax.fori_loop(0, tc, step, sa0)

    def isum(x):
        s = jnp.broadcast_to(jnp.sum(x, axis=0, keepdims=True), (8, LANES))
        if isplit:
            s = s + pltpu.roll(s, LANES // 2, 1)
        return s[0:1]

    def post(t, carry):
        y = y_ref[t]
        v = v_ref[t]
        mu = isum(y) * (1.0 / HEAD)
        d = y - mu
        var = isum(d * d) * (1.0 / HEAD)
        yn = d * lax.rsqrt(var + LNX_EPS) * lw_ref[...] + lb_ref[...]
        bonus = jnp.sum(r_ref[t] * k_ref[t] * rk_ref[...], axis=0, keepdims=True)
        y_ref[t] = yn + bonus * v
        return carry

    lax.fori_loop(0, tc, post, 0, unroll=unroll)

    @pl.when(pl.program_id(1) == pl.num_programs(1) - 1)
    def _():
        sf_ref[...] = s_scr[...]


def _wkv(r, w, k, kk, al, v, s0, rk, lw, lb, tc, isplit, unroll):
    t, nj, lanes = r.shape
    ni = v.shape[1]
    jspec = pl.BlockSpec((tc, nj, LANES), lambda l, c: (c, 0, l))
    ispec = pl.BlockSpec((tc, ni, LANES), lambda l, c: (c, 0, l))
    sspec = pl.BlockSpec((nj, ni, LANES), lambda l, c: (0, 0, l))
    return pl.pallas_call(
        functools.partial(_wkv_kernel, ni=ni, nj=nj, tc=tc, isplit=isplit, unroll=unroll),
        grid=(lanes // LANES, t // tc),
        in_specs=[jspec, jspec, jspec, jspec, jspec, ispec, sspec,
                  _const_spec(rk.shape), _const_spec(lw.shape), _const_spec(lb.shape)],
        out_specs=[ispec, sspec],
        out_shape=[jax.ShapeDtypeStruct((t, ni, lanes), F32),
                   jax.ShapeDtypeStruct((nj, ni, lanes), F32)],
        scratch_shapes=[pltpu.VMEM((nj, ni, LANES), F32),
                        pltpu.VMEM((tc, nj, LANES), F32),
                        pltpu.VMEM((tc, nj, LANES), F32)],
        compiler_params=_cparams(("parallel", "arbitrary")),
        name="wkv_seq" if isplit else "wkv_step",
    )(r, w, k, kk, al, v, s0, rk, lw, lb)


def _s5_kernel(*refs, sequential, n_blk, scan_lanes):
    if sequential:
        (u_ref, bre_ref, bim_ref, cre_ref, cim_ref, d_ref, are_ref, aim_ref, wglu_ref, bglu_ref,
         yb_ref, hre_out, him_out, bure, buim, c_re, c_im) = refs
    else:
        (u_ref, h0re_ref, h0im_ref, bre_ref, bim_ref, cre_ref, cim_ref, d_ref, are_ref, aim_ref,
         wglu_ref, bglu_ref, yb_ref, hre_out, him_out, bure, buim) = refs

    tm = u_ref.shape[0]
    kin = bre_ref.shape[1]
    kst = bre_ref.shape[2]
    u = u_ref[...]
    ub = u.astype(BF16)
    for kb in range(n_blk):
        ukb = ub[:, kb * kin:(kb + 1) * kin]
        bure[:, kb * kst:(kb + 1) * kst] = jnp.dot(ukb, bre_ref[kb], preferred_element_type=F32)
        buim[:, kb * kst:(kb + 1) * kst] = jnp.dot(ukb, bim_ref[kb], preferred_element_type=F32)

    if sequential:
        @pl.when(pl.program_id(1) == 0)
        def _():
            c_re[...] = jnp.zeros_like(c_re)
            c_im[...] = jnp.zeros_like(c_im)

        for lb in range(bure.shape[1] // scan_lanes):
            sl = pl.ds(lb * scan_lanes, scan_lanes)
            ar = are_ref[:, sl]
            ai = aim_ref[:, sl]

            def step(t, h, sl=sl, ar=ar, ai=ai):
                hr, hi = h
                row = pl.ds(t, 1)
                nr = ar * hr - ai * hi + bure[row, sl]
                ni = ar * hi + ai * hr + buim[row, sl]
                bure[row, sl] = nr
                buim[row, sl] = ni
                return nr, ni

            hr, hi = lax.fori_loop(0, tm, step, (c_re[:, sl], c_im[:, sl]))
            c_re[:, sl] = hr
            c_im[:, sl] = hi
        hre_out[0] = c_re[...]
        him_out[0] = c_im[...]
    else:
        h0r = h0re_ref[...]
        h0i = h0im_ref[...]
        ar = are_ref[...]
        ai = aim_ref[...]
        nr = bure[...] + (ar * h0r - ai * h0i)
        ni = buim[...] + (ar * h0i + ai * h0r)
        bure[...] = nr
        buim[...] = ni
        hre_out[...] = nr
        him_out[...] = ni

    ys = []
    for kb in range(n_blk):
        cols = slice(kb * kst, (kb + 1) * kst)
        yre = jnp.dot(bure[:, cols].astype(BF16), cre_ref[kb], preferred_element_type=F32)
        yim = jnp.dot(buim[:, cols].astype(BF16), cim_ref[kb], preferred_element_type=F32)
        ys.append(yre - yim)
    y = jnp.concatenate(ys, axis=1) + d_ref[...] * u
    y = jax.nn.gelu(y)
    gate = jnp.dot(y.astype(BF16), wglu_ref[...], preferred_element_type=F32) + bglu_ref[...]
    yb_ref[...] = (y * jax.nn.sigmoid(gate)).astype(BF16)


def _s5(proj, row_block0, n_rows, tm, seqs, u_col_block, width, consts, h0=None):
    sequential = h0 is None
    bre = consts[0]
    n_blk, _, kst = bre.shape
    n_state = n_blk * kst
    per_seq = n_rows // seqs // tm
    u_map = lambda b, c: (row_block0 + b * per_seq + c, u_col_block)
    out_map = lambda b, c: (b * per_seq + c, 0)
    in_specs = [pl.BlockSpec((tm, width), u_map)]
    args = [proj]
    if not sequential:
        in_specs += [pl.BlockSpec((tm, n_state), out_map)] * 2
        args += list(h0)
    in_specs += [_const_spec(c.shape) for c in consts]
    args += list(consts)
    scratch = [pltpu.VMEM((tm, n_state), F32), pltpu.VMEM((tm, n_state), F32)]
    if sequential:
        scratch += [pltpu.VMEM((1, n_state), F32), pltpu.VMEM((1, n_state), F32)]
        st_spec = pl.BlockSpec((1, 1, n_state), lambda b, c: (b, 0, 0))
        st_shape = jax.ShapeDtypeStruct((seqs, 1, n_state), F32)
    else:
        st_spec = pl.BlockSpec((tm, n_state), out_map)
        st_shape = jax.ShapeDtypeStruct((n_rows, n_state), F32)
    return pl.pallas_call(
        functools.partial(_s5_kernel, sequential=sequential, n_blk=n_blk, scan_lanes=1024),
        grid=(seqs, per_seq),
        in_specs=in_specs,
        out_specs=[pl.BlockSpec((tm, width), out_map), st_spec, st_spec],
        out_shape=[jax.ShapeDtypeStruct((n_rows, width), BF16), st_shape, st_shape],
        scratch_shapes=scratch,
        compiler_params=_cparams(("parallel", "arbitrary")),
        name="s5_seq" if sequential else "s5_step",
    )(*args)


def _route(logits, n_grp):
    lane = lax.broadcasted_iota(jnp.int32, logits.shape, 1).astype(F32)
    neg = jnp.float32(-1e30)
    big = jnp.float32(1e9)
    is_grp = lane < n_grp
    gl = jnp.where(is_grp, logits, neg)
    gmax = jnp.max(gl, axis=1, keepdims=True)
    gsum = jnp.sum(jnp.where(is_grp, jnp.exp(gl - gmax), 0.0), axis=1, keepdims=True)
    g_p = 1.0 / gsum
    g_idx = jnp.min(jnp.where(is_grp & (gl == gmax), lane, big), axis=1, keepdims=True)
    lo = n_grp + g_idx * EXPERTS_PER_GROUP
    in_grp = (lane >= lo) & (lane < lo + EXPERTS_PER_GROUP)
    el = jnp.where(in_grp, logits, neg)
    v1 = jnp.max(el, axis=1, keepdims=True)
    i1 = jnp.min(jnp.where(in_grp & (el == v1), lane, big), axis=1, keepdims=True)
    rest = in_grp & (lane != i1)
    el2 = jnp.where(rest, logits, neg)
    v2 = jnp.max(el2, axis=1, keepdims=True)
    i2 = jnp.min(jnp.where(rest & (el2 == v2), lane, big), axis=1, keepdims=True)
    e2 = jnp.exp(v2 - v1)
    w1 = g_p / (1.0 + e2)
    w2 = g_p * e2 / (1.0 + e2)
    return (jnp.where(lane == ROUTE_E1, i1 - n_grp, 0.0) + jnp.where(lane == ROUTE_E2, i2 - n_grp, 0.0)
            + jnp.where(lane == ROUTE_W1, w1, 0.0) + jnp.where(lane == ROUTE_W2, w2, 0.0))


def _merge_kernel(ya_ref, g_ref, yb_ref, ga_ref, gb_ref, x_ref, wa_ref, wb_ref, wo_ref,
                  gf_ref, wr_ref, br_ref, h_out, xn_out, comb_out, *, n_grp):
    ya = (ya_ref[...] * g_ref[...]).astype(BF16)
    ma = jnp.dot(ya, wa_ref[...], preferred_element_type=F32)
    mb = jnp.dot(yb_ref[...], wb_ref[...], preferred_element_type=F32)
    merged = jax.nn.sigmoid(ga_ref[...]) * ma + jax.nn.sigmoid(gb_ref[...]) * mb
    h = x_ref[...] + jnp.dot(merged.astype(BF16), wo_ref[...], preferred_element_type=F32)
    h_out[...] = h
    ms = jnp.mean(h * h, axis=-1, keepdims=True)
    xn = h * lax.rsqrt(ms + NORM_EPS) * gf_ref[...]
    xn_out[...] = xn
    logits = jnp.dot(xn, wr_ref[...], preferred_element_type=F32,
                     precision=lax.Precision.HIGHEST) + br_ref[...]
    comb_out[...] = _route(logits, n_grp)


def _merge(ya, g, yb, proj, x_all, w_br_a, w_br_b, w_out, g_ffn, w_router, b_router,
           tm, ga_col_block, n_grp):
    nt, d = x_all.shape
    wdt = ya.shape[1]
    row = lambda i: (i, 0)
    single = dict(pipeline_mode=pl.Buffered(1))
    in_specs = [pl.BlockSpec((tm, wdt), row), pl.BlockSpec((tm, wdt), row),
                pl.BlockSpec((tm, wdt), row),
                pl.BlockSpec((tm, d), lambda i: (i, ga_col_block)),
                pl.BlockSpec((tm, d), lambda i: (i, ga_col_block + 1)),
                pl.BlockSpec((tm, d), row),
                pl.BlockSpec(w_br_a.shape, lambda i: (0, 0), **single),
                pl.BlockSpec(w_br_b.shape, lambda i: (0, 0), **single),
                pl.BlockSpec(w_out.shape, lambda i: (0, 0), **single),
                _const_spec(g_ffn.shape), _const_spec(w_router.shape), _const_spec(b_router.shape)]
    return pl.pallas_call(
        functools.partial(_merge_kernel, n_grp=n_grp),
        grid=(nt // tm,),
        in_specs=in_specs,
        out_specs=[pl.BlockSpec((tm, d), row), pl.BlockSpec((tm, d), row),
                   pl.BlockSpec((tm, LANES), row)],
        out_shape=[jax.ShapeDtypeStruct((nt, d), F32), jax.ShapeDtypeStruct((nt, d), F32),
                   jax.ShapeDtypeStruct((nt, LANES), F32)],
        compiler_params=_cparams(("parallel",)),
        name="merge_route",
    )(ya, g, yb, proj, proj, x_all, w_br_a, w_br_b, w_out, g_ffn, w_router, b_router)


def _moe_kernel(te_ref, first_ref, nvalid_ref, nused_ref, src_cur, src_nxt, dst_cur,
                xn_hbm, wg_ref, wu_ref, wd_ref, y_hbm,
                xbuf, obuf, wg_bf, wu_bf, wd_bf, gsem, ssem, *, tm):
    i = pl.program_id(0)
    last = pl.num_programs(0) - 1
    n_used = nused_ref[0]
    slot = lax.rem(i, 2)

    def gather_start(src, s):
        def body(r, c):
            pltpu.make_async_copy(xn_hbm.at[pl.ds(src[0, 0, r], 1)], xbuf.at[s, pl.ds(r, 1)],
                                  gsem.at[s]).start()
            return c
        lax.fori_loop(0, tm, body, 0, unroll=8)

    def gather_wait(s):
        pltpu.make_async_copy(xn_hbm.at[pl.ds(0, tm)], xbuf.at[s], gsem.at[s]).wait()

    def scatter_start(s, n_rows):
        def body(r, c):
            pltpu.make_async_copy(obuf.at[s, pl.ds(r, 1)], y_hbm.at[pl.ds(dst_cur[0, 0, r], 1)],
                                  ssem.at[s]).start()
            return c
        lax.fori_loop(0, n_rows, body, 0)

    def scatter_wait(s, n_rows):
        n_full = pl.multiple_of(lax.shift_left(lax.shift_right_logical(n_rows, 3), 3), 8)

        @pl.when(n_full > 0)
        def _():
            pltpu.make_async_copy(obuf.at[s, pl.ds(0, n_full)], y_hbm.at[pl.ds(0, n_full)],
                                  ssem.at[s]).wait()

        def body(r, c):
            pltpu.make_async_copy(obuf.at[s, pl.ds(r, 1)], y_hbm.at[pl.ds(r, 1)],
                                  ssem.at[s]).wait()
            return c
        lax.fori_loop(n_full, n_rows, body, 0)

    @pl.when(i == 0)
    def _():
        gather_start(src_cur, 0)

    @pl.when(i < n_used)
    def _():
        gather_wait(slot)

    @pl.when(i + 1 < n_used)
    def _():
        gather_start(src_nxt, 1 - slot)

    @pl.when((i >= 2) & (i - 2 < n_used))
    def _():
        scatter_wait(slot, nvalid_ref[i - 2])

    @pl.when(i < n_used)
    def _():
        @pl.when(first_ref[i] == 1)
        def _():
            wg_bf[...] = wg_ref[0].astype(BF16)
            wu_bf[...] = wu_ref[0].astype(BF16)
            wd_bf[...] = wd_ref[0].astype(BF16)

        x = xbuf[slot].astype(BF16)
        xg = jnp.dot(x, wg_bf[...], preferred_element_type=F32)
        xu = jnp.dot(x, wu_bf[...], preferred_element_type=F32)
        hid = (jax.nn.silu(xg) * xu).astype(BF16)
        obuf[slot] = jnp.dot(hid, wd_bf[...], preferred_element_type=F32)
        scatter_start(slot, nvalid_ref[i])

    @pl.when(i == last)
    def _():
        @pl.when((i >= 1) & (i - 1 < n_used))
        def _():
            scatter_wait(1 - slot, nvalid_ref[i - 1])

        @pl.when(i < n_used)
        def _():
            scatter_wait(slot, nvalid_ref[i])


def _moe(xn, tile_expert, tile_first, tile_valid, n_used, src_rows, dst_rows, wg, wu, wd,
         n_out_rows, tm):
    nt, d = xn.shape
    n_exp, _, de = wg.shape
    n_tiles = src_rows.shape[0]
    smem_cur = pl.BlockSpec((1, 1, tm), lambda i, *_: (i, 0, 0), memory_space=pltpu.SMEM)
    smem_nxt = pl.BlockSpec((1, 1, tm), lambda i, *_: (jnp.minimum(i + 1, n_tiles - 1), 0, 0),
                            memory_space=pltpu.SMEM)
    grid_spec = pltpu.PrefetchScalarGridSpec(
        num_scalar_prefetch=4,
        grid=(n_tiles,),
        in_specs=[smem_cur, smem_nxt, smem_cur,
                  pl.BlockSpec(memory_space=pl.ANY),
                  pl.BlockSpec((1, d, de), lambda i, te, *_: (te[i], 0, 0)),
                  pl.BlockSpec((1, d, de), lambda i, te, *_: (te[i], 0, 0)),
                  pl.BlockSpec((1, de, d), lambda i, te, *_: (te[i], 0, 0))],
        out_specs=pl.BlockSpec(memory_space=pl.ANY),
        scratch_shapes=[pltpu.VMEM((2, tm, d), F32), pltpu.VMEM((2, tm, d), F32),
                        pltpu.VMEM((d, de), BF16), pltpu.VMEM((d, de), BF16),
                        pltpu.VMEM((de, d), BF16),
                        pltpu.SemaphoreType.DMA((2,)), pltpu.SemaphoreType.DMA((2,))])
    return pl.pallas_call(
        functools.partial(_moe_kernel, tm=tm),
        grid_spec=grid_spec,
        out_shape=jax.ShapeDtypeStruct((n_out_rows, d), F32),
        compiler_params=_cparams(("arbitrary",)),
        name="moe_grouped",
    )(tile_expert, tile_first, tile_valid, n_used, src_rows, src_rows, dst_rows, xn, wg, wu, wd)


def _moe_plan(route, n_exp, tm):
    nt = route.shape[0]
    n_pairs = 2 * nt
    n_tiles = n_pairs // tm + n_exp
    eid = jnp.concatenate([route[:, ROUTE_E1], route[:, ROUTE_E2]]).astype(jnp.int32)
    onehot = (eid[:, None] == jnp.arange(n_exp, dtype=jnp.int32)[None, :]).astype(jnp.int32)
    csum = jnp.cumsum(onehot, axis=0)
    rank = jnp.take_along_axis(csum, eid[:, None], axis=1)[:, 0] - 1
    cnt = csum[-1]
    tiles = (cnt + tm - 1) // tm
    tile_end = jnp.cumsum(tiles)
    n_used = tile_end[-1]
    pos = (tile_end - tiles)[eid] * tm + rank
    pair = jnp.arange(n_pairs, dtype=jnp.int32)
    dst_rows = jnp.zeros((n_tiles * tm,), jnp.int32).at[pos].set(pair)
    src_rows = dst_rows % nt
    tile_id = jnp.minimum(jnp.arange(n_tiles, dtype=jnp.int32), n_used - 1)
    tile_expert = jnp.sum((tile_end[None, :] <= tile_id[:, None]).astype(jnp.int32), axis=1)
    tile_first = jnp.concatenate(
        [jnp.ones((1,), jnp.int32), (tile_expert[1:] != tile_expert[:-1]).astype(jnp.int32)])
    tile_valid = jnp.clip(cnt[tile_expert] - (tile_id - (tile_end - tiles)[tile_expert]) * tm, 0, tm)
    return (tile_expert, tile_first, tile_valid.astype(jnp.int32),
            n_used.reshape(1).astype(jnp.int32),
            src_rows.reshape(n_tiles, 1, tm), dst_rows.reshape(n_tiles, 1, tm), n_pairs)


def _final_kernel(h_ref, y1_ref, y2_ref, route_ref, g_ref, o_ref):
    lane = lax.broadcasted_iota(jnp.int32, route_ref.shape, 1)
    route = route_ref[...]
    w1 = jnp.sum(jnp.where(lane == ROUTE_W1, route, 0.0), axis=1, keepdims=True)
    w2 = jnp.sum(jnp.where(lane == ROUTE_W2, route, 0.0), axis=1, keepdims=True)
    h = h_ref[...] + (w1 * y1_ref[...] + w2 * y2_ref[...])
    ms = jnp.mean(h * h, axis=-1, keepdims=True)
    o_ref[...] = h * lax.rsqrt(ms + NORM_EPS) * g_ref[...]


def _final(h, y_moe, route, g_final, tm):
    nt, d = h.shape
    row = lambda i: (i, 0)
    return pl.pallas_call(
        _final_kernel,
        grid=(nt // tm,),
        in_specs=[pl.BlockSpec((tm, d), row), pl.BlockSpec((tm, d), row),
                  pl.BlockSpec((tm, d), lambda i: (i + nt // tm, 0)),
                  pl.BlockSpec((tm, LANES), row), pl.BlockSpec((1, d), lambda i: (0, 0))],
        out_specs=pl.BlockSpec((tm, d), row),
        out_shape=jax.ShapeDtypeStruct((nt, d), F32),
        compiler_params=_cparams(("parallel",)),
        name="final_norm",
    )(h, y_moe, y_moe, route, g_final)


def _pad_cols(w, to):
    return jnp.pad(w, ((0, 0), (0, to - w.shape[1])))


def _pad_rows(w, to):
    return jnp.pad(w, ((0, to - w.shape[0]), (0, 0)))


def _s5_consts(lam_re, lam_im, log_dt, b_re, b_im, c_re, c_im, d, w_glu, b_glu):
    dt = jnp.exp(log_dt)[:, None]
    mag = jnp.exp(lam_re * dt)
    abar_re = mag * jnp.cos(lam_im * dt)
    abar_im = mag * jnp.sin(lam_im * dt)
    den = lam_re * lam_re + lam_im * lam_im
    nr = abar_re - 1.0
    coef_re = (nr * lam_re + abar_im * lam_im) / den
    coef_im = (abar_im * lam_re - nr * lam_im) / den
    bbar_re = coef_re[..., None] * b_re - coef_im[..., None] * b_im
    bbar_im = coef_re[..., None] * b_im + coef_im[..., None] * b_re
    n_g, n_p, n_c = b_re.shape
    gpb = LANES // n_c
    eye = jnp.eye(gpb, dtype=F32)

    def in_blk(bb):
        bb = bb.reshape(n_g // gpb, gpb, n_p, n_c)
        return jnp.einsum('kgpc,gh->kgchp', bb, eye).reshape(
            n_g // gpb, gpb * n_c, gpb * n_p).astype(BF16)

    def out_blk(cc):
        cc = cc.reshape(n_g // gpb, gpb, n_c, n_p)
        return jnp.einsum('kgcp,gh->khpgc', cc, eye).reshape(
            n_g // gpb, gpb * n_p, gpb * n_c).astype(BF16)

    return (in_blk(bbar_re), in_blk(bbar_im), out_blk(c_re), out_blk(c_im),
            d.reshape(1, -1), abar_re.reshape(1, -1), abar_im.reshape(1, -1),
            w_glu.astype(BF16), b_glu.reshape(1, -1))


def kernel(x_prompt, x_sample, state_shift, state_wkv, state_ssm_re, state_ssm_im, meta_tokens, g_mix, w_in, shift_mu, w0, w_decay_up, a0, w_aaa_up, w_gate_up, k_k, k_a, r_k, lnx_w, lnx_b, ssm_lam_re, ssm_lam_im, ssm_log_dt, ssm_b_re, ssm_b_im, ssm_c_re, ssm_c_im, ssm_d, w_glu, b_glu, w_br_a, w_br_b, w_out, g_ffn, w_router_grp, b_router_grp, w_router_exp, b_router_exp, w_exp_gate, w_exp_up, w_exp_down, g_final):
    depth = g_mix.shape[0]
    assert depth == 1, "single-layer trunk"
    bsz, seq, d = x_prompt.shape
    nb = x_sample.shape[0]
    assert x_sample.shape[1] == 1
    t_p = seq + N_META
    n_p = bsz * t_p
    width = k_k.shape[1]
    heads = width // HEAD
    n_dl, n_al, n_gl = w_decay_up.shape[1], w_aaa_up.shape[1], w_gate_up.shape[1]
    n_grp = w_router_grp.shape[2]
    n_exp = w_router_exp.shape[2]
    assert bsz * heads * 2 == LANES and (nb * heads) % LANES == 0

    s_row0 = -(-n_p // nb) * nb
    n_t = s_row0 + nb
    meta = jnp.broadcast_to(meta_tokens[None], (bsz, N_META, d))
    x_all = jnp.concatenate(
        [jnp.concatenate([meta, x_prompt], axis=1).reshape(n_p, d),
         jnp.zeros((s_row0 - n_p, d), F32), x_sample.reshape(nb, d)], axis=0)

    c_rkv = 3 * width
    c_xw, c_xa, c_xg = c_rkv, c_rkv + n_dl, c_rkv + n_dl + n_al
    c_u = c_xg + n_gl
    c_ga = c_u + width
    lo_width = 2 * LORA_PAD + n_gl

    def regroup(m):
        return jnp.concatenate(
            [m[:, :c_rkv], m[:, c_u:], _pad_cols(m[:, c_xw:c_xa], LORA_PAD),
             _pad_cols(m[:, c_xa:c_xg], LORA_PAD), m[:, c_xg:c_u]], axis=1)

    def ungroup(m):
        lo = c_rkv + width + 2 * d
        return jnp.concatenate(
            [m[:, :c_rkv], m[:, lo:lo + n_dl], m[:, lo + LORA_PAD:lo + LORA_PAD + n_al],
             m[:, lo + 2 * LORA_PAD:]], axis=1)

    w_in_p = regroup(w_in[0]).astype(BF16)
    n_proj = w_in_p.shape[1]
    proj = _inproj(x_all, g_mix, w_in_p, tm=768, tn=512)

    mu_all = regroup(_pad_cols(shift_mu, c_u + width + 2 * d))
    mu_rkv, mu_lo = mu_all[:, :c_rkv], mu_all[:, n_proj - lo_width:]
    st_all = regroup(_pad_cols(state_shift[0], c_u + width + 2 * d))
    prev_s = (st_all[:, :c_rkv], st_all[:, n_proj - lo_width:])
    lo_col_block = (n_proj - lo_width) // lo_width
    prep_w = (mu_rkv, mu_lo, w0, a0, k_k, k_a,
              _pad_rows(w_decay_up[0], LORA_PAD).astype(BF16),
              _pad_rows(w_aaa_up[0], LORA_PAD).astype(BF16), w_gate_up[0].astype(BF16))

    tm_seq = t_p // 6
    rp, wp, kp, vp, kkp, ap, gp = _rwkv_prep(proj, 0, n_p, tm_seq, bsz, width, lo_col_block,
                                             lo_width, *prep_w)
    rs, ws, ks, vs, kks, as_, gs = _rwkv_prep(proj, s_row0 // nb, nb, nb, 1, width, lo_col_block,
                                              lo_width, *prep_w, prev=prev_s)

    half = HEAD // 2

    def to_lanes_p(z):
        z = jnp.broadcast_to(z.reshape(1, bsz, t_p, heads, HEAD), (2, bsz, t_p, heads, HEAD))
        return z.transpose(2, 4, 0, 1, 3).reshape(t_p, HEAD, LANES)

    def to_lanes_pi(z):
        z = z.reshape(bsz, t_p, heads, 2, half).transpose(1, 4, 3, 0, 2)
        return z.reshape(t_p, half, LANES)

    def to_lanes_s(z):
        return z.reshape(nb, heads, HEAD).transpose(2, 0, 1).reshape(1, HEAD, nb * heads)

    rk_l = jnp.tile(r_k[0].T, (1, LANES // heads))
    lw_p = jnp.broadcast_to(lnx_w[0].reshape(heads, 2, half).transpose(2, 1, 0)[:, :, None, :],
                            (half, 2, bsz, heads)).reshape(half, LANES)
    lb_p = jnp.broadcast_to(lnx_b[0].reshape(heads, 2, half).transpose(2, 1, 0)[:, :, None, :],
                            (half, 2, bsz, heads)).reshape(half, LANES)
    lw_s = jnp.tile(lnx_w[0].reshape(heads, HEAD).T, (1, LANES // heads))
    lb_s = jnp.tile(lnx_b[0].reshape(heads, HEAD).T, (1, LANES // heads))

    s0_p = jnp.zeros((HEAD, half, LANES), F32)
    y_p, sf_p = _wkv(to_lanes_p(rp), to_lanes_p(wp), to_lanes_p(kp), to_lanes_p(kkp),
                     to_lanes_p(ap), to_lanes_pi(vp), s0_p, rk_l, lw_p, lb_p,
                     tc=48, isplit=True, unroll=8)
    s0_s = state_wkv[0].transpose(3, 2, 0, 1).reshape(HEAD, HEAD, nb * heads)
    y_s, sf_s = _wkv(to_lanes_s(rs), to_lanes_s(ws), to_lanes_s(ks), to_lanes_s(kks),
                     to_lanes_s(as_), to_lanes_s(vs), s0_s, rk_l, lw_s, lb_s,
                     tc=1, isplit=False, unroll=1)

    ya_p = y_p.reshape(t_p, half, 2, bsz, heads).transpose(3, 0, 4, 2, 1).reshape(n_p, width)
    ya_s = y_s.reshape(HEAD, nb, heads).transpose(1, 2, 0).reshape(nb, width)
    wkv_p = sf_p.reshape(HEAD, half, 2, bsz, heads).transpose(3, 4, 2, 1, 0).reshape(
        1, bsz, heads, HEAD, HEAD)
    wkv_s = sf_s.reshape(HEAD, HEAD, nb, heads).transpose(2, 3, 1, 0)[None]

    s5c = _s5_consts(ssm_lam_re[0], ssm_lam_im[0], ssm_log_dt[0], ssm_b_re[0], ssm_b_im[0],
                     ssm_c_re[0], ssm_c_im[0], ssm_d[0], w_glu[0], b_glu)
    n_g, n_st = ssm_lam_re.shape[1], ssm_lam_re.shape[2]
    u_col_block = c_rkv // width
    yb_p, re_p, im_p = _s5(proj, 0, n_p, tm_seq, bsz, u_col_block, width, s5c)
    h0 = (state_ssm_re[0].reshape(nb, n_g * n_st), state_ssm_im[0].reshape(nb, n_g * n_st))
    yb_s, re_s, im_s = _s5(proj, s_row0 // nb, nb, nb, 1, u_col_block, width, s5c, h0=h0)

    def unify(a, b):
        return jnp.concatenate([a, jnp.zeros((s_row0 - n_p, a.shape[1]), a.dtype), b], axis=0)

    w_router = _pad_cols(jnp.concatenate([w_router_grp[0], w_router_exp[0]], axis=1), LANES)
    b_router = _pad_cols(jnp.concatenate([b_router_grp, b_router_exp], axis=1), LANES)
    assert n_grp + n_exp <= LANES and n_exp == n_grp * EXPERTS_PER_GROUP
    h1, xn2, route = _merge(unify(ya_p, ya_s), unify(gp, gs), unify(yb_p, yb_s), proj, x_all,
                           w_br_a[0].astype(BF16), w_br_b[0].astype(BF16), w_out[0].astype(BF16),
                           g_ffn, w_router, b_router, tm=256, ga_col_block=(c_rkv + width) // d,
                           n_grp=n_grp)
    tm_moe = 256
    te, tf, tv, nu, src_rows, dst_rows, n_moe_rows = _moe_plan(route, n_exp, tm_moe)
    y_moe = _moe(xn2, te, tf, tv, nu, src_rows, dst_rows, w_exp_gate[0], w_exp_up[0], w_exp_down[0],
                 n_moe_rows, tm_moe)
    y_all = _final(h1, y_moe, route, g_final.reshape(1, d), tm=256)

    y_prompt = y_all[:n_p].reshape(bsz, t_p, d)[:, N_META:]
    y_sample = y_all[s_row0:].reshape(nb, 1, d)
    last_p = proj[t_p - 1:n_p:t_p]
    shift_p = ungroup(last_p)[None]
    shift_s = ungroup(proj[s_row0:])[None]
    return (y_prompt, y_sample, shift_p, wkv_p,
            re_p.reshape(1, bsz, n_g, n_st), im_p.reshape(1, bsz, n_g, n_st),
            shift_s, wkv_s,
            re_s.reshape(1, nb, n_g, n_st), im_s.reshape(1, nb, n_g, n_st))
```

```python
import functools

import jax
import jax.numpy as jnp
from jax import lax
from jax.experimental import pallas as pl
from jax.experimental.pallas import tpu as pltpu

F32 = jnp.float32
BF16 = jnp.bfloat16

NORM_EPS = 1e-6
LNX_EPS = 64e-5
N_META = 16
HEAD = 64
SSM_GROUP = 16
SSM_STATE = 64
EXPERTS_PER_GROUP = 8
LANES = 128
SUBLANES = 8
LORA_PAD = 128
ROUTE_E1, ROUTE_E2, ROUTE_W1, ROUTE_W2 = 0, 1, 2, 3
VMEM_LIMIT = 56 * 1024 * 1024


def _cparams(sem):
    return pltpu.CompilerParams(dimension_semantics=sem, vmem_limit_bytes=VMEM_LIMIT)


def _const_spec(shape):
    nd = len(shape)
    return pl.BlockSpec(shape, lambda *_: (0,) * nd)


def _inproj_kernel(x_ref, g_ref, w_ref, o_ref, xn_ref):
    @pl.when(pl.program_id(1) == 0)
    def _():
        x = x_ref[...]
        ms = jnp.mean(x * x, axis=-1, keepdims=True)
        xn_ref[...] = (x * lax.rsqrt(ms + NORM_EPS) * g_ref[...]).astype(BF16)

    o_ref[...] = jnp.dot(xn_ref[...], w_ref[...], preferred_element_type=F32)


def _inproj(x_all, g_mix, w_in_p, tm, tn):
    nt, d = x_all.shape
    n_out = w_in_p.shape[1]
    return pl.pallas_call(
        _inproj_kernel,
        grid=(nt // tm, n_out // tn),
        in_specs=[pl.BlockSpec((tm, d), lambda i, j: (i, 0)),
                  pl.BlockSpec((1, d), lambda i, j: (0, 0)),
                  pl.BlockSpec((d, tn), lambda i, j: (0, j))],
        out_specs=pl.BlockSpec((tm, tn), lambda i, j: (i, j)),
        out_shape=jax.ShapeDtypeStruct((nt, n_out), F32),
        scratch_shapes=[pltpu.VMEM((tm, d), BF16)],
        compiler_params=_cparams(("parallel", "arbitrary")),
        name="inproj",
    )(x_all, g_mix, w_in_p)


def _softplus(z):
    return jnp.maximum(z, 0.0) + jnp.log1p(jnp.exp(-jnp.abs(z)))


def _rwkv_prep_kernel(*refs, width, carry_prev):
    if carry_prev:
        (rkv_ref, lo_ref, mu_rkv_ref, mu_lo_ref, w0_ref, a0_ref, kk_ref, ka_ref,
         wd_ref, wa_ref, wg_ref, _g_all,
         r_out, w_out, k_out, v_out, kk_out, a_out, g_out, c_rkv, c_lo) = refs

        @pl.when(pl.program_id(1) == 0)
        def _():
            c_rkv[...] = jnp.zeros_like(c_rkv)
            c_lo[...] = jnp.zeros_like(c_lo)
    else:
        (rkv_ref, lo_ref, prev_rkv_ref, prev_lo_ref, mu_rkv_ref, mu_lo_ref, w0_ref, a0_ref,
         kk_ref, ka_ref, wd_ref, wa_ref, wg_ref, _g_all,
         r_out, w_out, k_out, v_out, kk_out, a_out, g_out) = refs

    tm = rkv_ref.shape[0]
    first_row = lax.broadcasted_iota(jnp.int32, (tm, 1), 0) == 0

    def shifted(p, prev_ref, carry_ref, cols):
        if carry_prev:
            prev = jnp.where(first_row, carry_ref[:, cols], pltpu.roll(p, 1, 0))
        else:
            prev = prev_ref[:, cols]
        return prev

    def lerp(p, prev, mu):
        return p + (prev - p) * mu

    lo_cols = slice(0, lo_ref.shape[1])
    p_lo = lo_ref[...]
    q_lo = lerp(p_lo, shifted(p_lo, None if carry_prev else prev_lo_ref,
                              c_lo if carry_prev else None, lo_cols), mu_lo_ref[...])
    xw = q_lo[:, 0:LORA_PAD]
    xa = q_lo[:, LORA_PAD:2 * LORA_PAD]
    xg = q_lo[:, 2 * LORA_PAD:]
    dw = jnp.dot(jnp.tanh(xw).astype(BF16), wd_ref[...], preferred_element_type=F32)
    wlog = -_softplus(-(w0_ref[...] + dw)) - 0.5
    w_out[...] = jnp.exp(-jnp.exp(wlog))
    a = jax.nn.sigmoid(a0_ref[...] + jnp.dot(xa.astype(BF16), wa_ref[...],
                                             preferred_element_type=F32))
    a_out[...] = a
    g_out[...] = jnp.dot(jax.nn.sigmoid(xg).astype(BF16), wg_ref[...],
                         preferred_element_type=F32)

    def q_of(idx):
        cols = slice(idx * width, (idx + 1) * width)
        p = rkv_ref[:, cols]
        prev = shifted(p, None if carry_prev else prev_rkv_ref,
                       c_rkv if carry_prev else None, cols)
        return lerp(p, prev, mu_rkv_ref[:, cols])

    r_out[...] = q_of(0)
    k = q_of(1)
    kk_out[...] = k * kk_ref[...]
    k_out[...] = k * (1.0 + (a - 1.0) * ka_ref[...])
    v_out[...] = q_of(2)

    if carry_prev:
        c_rkv[...] = rkv_ref[tm - 1:tm, :]
        c_lo[...] = lo_ref[tm - 1:tm, :]


def _rwkv_prep(proj, row_block0, n_rows, tm, seqs, width, lo_col_block, lo_width,
               mu_rkv, mu_lo, w0, a0, k_k, k_a, wd, wa, wg, g_all, prev=None):
    carry_prev = prev is None
    per_seq = n_rows // seqs // tm
    row_map = lambda b, c: (row_block0 + b * per_seq + c, 0)
    lo_map = lambda b, c: (row_block0 + b * per_seq + c, lo_col_block)
    out_map = lambda b, c: (b * per_seq + c, 0)
    in_specs = [pl.BlockSpec((tm, 3 * width), row_map), pl.BlockSpec((tm, lo_width), lo_map)]
    args = [proj, proj]
    if not carry_prev:
        in_specs += [pl.BlockSpec((tm, 3 * width), out_map), pl.BlockSpec((tm, lo_width), out_map)]
        args += list(prev)
    consts = [mu_rkv, mu_lo, w0, a0, k_k, k_a, wd, wa, wg]
    in_specs += [_const_spec(c.shape) for c in consts] + [pl.BlockSpec(memory_space=pl.ANY)]
    args += consts + [g_all]
    scratch = []
    if carry_prev:
        scratch = [pltpu.VMEM((1, 3 * width), F32), pltpu.VMEM((1, lo_width), F32)]
    out_sd = jax.ShapeDtypeStruct((n_rows, width), F32)
    return pl.pallas_call(
        functools.partial(_rwkv_prep_kernel, width=width, carry_prev=carry_prev),
        grid=(seqs, per_seq),
        in_specs=in_specs,
        out_specs=[pl.BlockSpec((tm, width), out_map)] * 6 + [pl.BlockSpec((tm, width), row_map)],
        out_shape=[out_sd] * 6 + [jax.ShapeDtypeStruct(g_all.shape, g_all.dtype)],
        input_output_aliases={len(args) - 1: 6},
        scratch_shapes=scratch,
        compiler_params=_cparams(("parallel", "arbitrary")),
        name="rwkv_prep_seq" if carry_prev else "rwkv_prep_step",
    )(*args)


def _wkv_kernel(r_ref, w_ref, k_ref, kk_ref, al_ref, v_ref, s0_ref, rk_ref, lw_ref, lb_ref,
                y_ref, sf_ref, s_scr, a_scr, b_scr, *dup_scr, ni, nj, tc, isplit, unroll):
    @pl.when(pl.program_id(1) == 0)
    def _():
        s_scr[...] = s0_ref[...]

    def dup(x):
        return jnp.concatenate([x, x], axis=-1) if isplit else x

    if isplit:
        w_src, k_src, r_src = dup_scr
    else:
        w_src, k_src, r_src = w_ref, k_ref, r_ref

    def prep(t, carry):
        kk = dup(kk_ref[t])
        ss = jnp.sum(kk * kk, axis=0, keepdims=True)
        kkn = kk / jnp.maximum(jnp.sqrt(ss), 1e-12)
        a_scr[t] = -kkn
        b_scr[t] = kkn * dup(al_ref[t])
        if isplit:
            w_src[t] = dup(w_ref[t])
            k_src[t] = dup(k_ref[t])
            r_src[t] = dup(r_ref[t])
        return carry

    lax.fori_loop(0, tc, prep, 0, unroll=unroll)

    def row(ref, t, j):
        return ref[t, j:j + 1, :]

    def tree(parts):
        while len(parts) > 1:
            parts = [parts[i] + parts[i + 1] for i in range(0, len(parts), 2)]
        return parts[0]

    n_acc = 4
    sa0 = tree([sum(s_scr[j] * row(a_scr, 0, j) for j in range(q, nj, n_acc))
                for q in range(n_acc)])

    def step(t, sa):
        v = v_ref[t]
        tn = jnp.minimum(t + 1, tc - 1)
        y = [None] * n_acc
        san = [None] * n_acc
        for j in range(nj):
            s = s_scr[j] * row(w_src, t, j) + sa * row(b_scr, t, j) + v * row(k_src, t, j)
            s_scr[j] = s
            yj = s * row(r_src, t, j)
            sj = s * row(a_scr, tn, j)
            q = j % n_acc
            y[q] = yj if y[q] is None else y[q] + yj
            san[q] = sj if san[q] is None else san[q] + sj
        y_ref[t] = tree(y)
        return tree(san)

    lax.fori_loop(0, tc, step, sa0)

    def isum(x):
        s = jnp.broadcast_to(jnp.sum(x, axis=0, keepdims=True), (8, LANES))
        if isplit:
            s = s + pltpu.roll(s, LANES // 2, 1)
        return s[0:1]

    def post(t, carry):
        y = y_ref[t]
        v = v_ref[t]
        mu = isum(y) * (1.0 / HEAD)
        d = y - mu
        var = isum(d * d) * (1.0 / HEAD)
        yn = d * lax.rsqrt(var + LNX_EPS) * lw_ref[...] + lb_ref[...]
        bonus = jnp.sum(r_src[t] * k_src[t] * rk_ref[...], axis=0, keepdims=True)
        y_ref[t] = yn + bonus * v
        return carry

    lax.fori_loop(0, tc, post, 0, unroll=unroll)

    @pl.when(pl.program_id(1) == pl.num_programs(1) - 1)
    def _():
        sf_ref[...] = s_scr[...]


def _wkv(r, w, k, kk, al, v, s0, rk, lw, lb, tc, isplit, unroll):
    t, nj, jl = r.shape
    ni, lanes = v.shape[1:]
    assert jl == (LANES // 2 if isplit else lanes)
    jspec = pl.BlockSpec((tc, nj, min(jl, LANES)), lambda l, c: (c, 0, l))
    dup_scr = [pltpu.VMEM((tc, nj, LANES), F32)] * 3 if isplit else []
    ispec = pl.BlockSpec((tc, ni, LANES), lambda l, c: (c, 0, l))
    sspec = pl.BlockSpec((nj, ni, LANES), lambda l, c: (0, 0, l))
    return pl.pallas_call(
        functools.partial(_wkv_kernel, ni=ni, nj=nj, tc=tc, isplit=isplit, unroll=unroll),
        grid=(lanes // LANES, t // tc),
        in_specs=[jspec, jspec, jspec, jspec, jspec, ispec, sspec,
                  _const_spec(rk.shape), _const_spec(lw.shape), _const_spec(lb.shape)],
        out_specs=[ispec, sspec],
        out_shape=[jax.ShapeDtypeStruct((t, ni, lanes), F32),
                   jax.ShapeDtypeStruct((nj, ni, lanes), F32)],
        scratch_shapes=[pltpu.VMEM((nj, ni, LANES), F32),
                        pltpu.VMEM((tc, nj, LANES), F32),
                        pltpu.VMEM((tc, nj, LANES), F32)] + dup_scr,
        compiler_params=_cparams(("parallel", "arbitrary")),
        name="wkv_seq" if isplit else "wkv_step",
    )(r, w, k, kk, al, v, s0, rk, lw, lb)


def _s5_kernel(*refs, sequential, n_blk, scan_lanes):
    if sequential:
        (u_ref, bre_ref, bim_ref, cre_ref, cim_ref, d_ref, are_ref, aim_ref, wglu_ref, bglu_ref,
         _yb_all, yb_ref, hre_out, him_out, bure, buim, c_re, c_im) = refs
    else:
        (u_ref, h0re_ref, h0im_ref, bre_ref, bim_ref, cre_ref, cim_ref, d_ref, are_ref, aim_ref,
         wglu_ref, bglu_ref, _yb_all, yb_ref, hre_out, him_out, bure, buim) = refs

    tm = u_ref.shape[0]
    kin = bre_ref.shape[1]
    kst = bre_ref.shape[2]
    u = u_ref[...]
    ub = u.astype(BF16)
    for kb in range(n_blk):
        ukb = ub[:, kb * kin:(kb + 1) * kin]
        bure[:, kb * kst:(kb + 1) * kst] = jnp.dot(ukb, bre_ref[kb], preferred_element_type=F32)
        buim[:, kb * kst:(kb + 1) * kst] = jnp.dot(ukb, bim_ref[kb], preferred_element_type=F32)

    if sequential:
        @pl.when(pl.program_id(1) == 0)
        def _():
            c_re[...] = jnp.zeros_like(c_re)
            c_im[...] = jnp.zeros_like(c_im)

        for lb in range(bure.shape[1] // scan_lanes):
            sl = pl.ds(lb * scan_lanes, scan_lanes)
            ar = are_ref[:, sl]
            ai = aim_ref[:, sl]

            def step(t, h, sl=sl, ar=ar, ai=ai):
                hr, hi = h
                row = pl.ds(t, 1)
                nr = ar * hr - ai * hi + bure[row, sl]
                ni = ar * hi + ai * hr + buim[row, sl]
                bure[row, sl] = nr
                buim[row, sl] = ni
                return nr, ni

            hr, hi = lax.fori_loop(0, tm, step, (c_re[:, sl], c_im[:, sl]))
            c_re[:, sl] = hr
            c_im[:, sl] = hi
        hre_out[0] = c_re[...]
        him_out[0] = c_im[...]
    else:
        h0r = h0re_ref[...]
        h0i = h0im_ref[...]
        ar = are_ref[...]
        ai = aim_ref[...]
        nr = bure[...] + (ar * h0r - ai * h0i)
        ni = buim[...] + (ar * h0i + ai * h0r)
        bure[...] = nr
        buim[...] = ni
        hre_out[...] = nr
        him_out[...] = ni

    ys = []
    for kb in range(n_blk):
        cols = slice(kb * kst, (kb + 1) * kst)
        yre = jnp.dot(bure[:, cols].astype(BF16), cre_ref[kb], preferred_element_type=F32)
        yim = jnp.dot(buim[:, cols].astype(BF16), cim_ref[kb], preferred_element_type=F32)
        ys.append(yre - yim)
    y = jnp.concatenate(ys, axis=1) + d_ref[...] * u
    y = jax.nn.gelu(y)
    gate = jnp.dot(y.astype(BF16), wglu_ref[...], preferred_element_type=F32) + bglu_ref[...]
    yb_ref[...] = (y * jax.nn.sigmoid(gate)).astype(BF16)


def _s5(proj, row_block0, n_rows, tm, seqs, u_col_block, width, consts, yb_all, h0=None):
    sequential = h0 is None
    bre = consts[0]
    n_blk, _, kst = bre.shape
    n_state = n_blk * kst
    per_seq = n_rows // seqs // tm
    u_map = lambda b, c: (row_block0 + b * per_seq + c, u_col_block)
    out_map = lambda b, c: (b * per_seq + c, 0)
    in_specs = [pl.BlockSpec((tm, width), u_map)]
    args = [proj]
    if not sequential:
        in_specs += [pl.BlockSpec((tm, n_state), out_map)] * 2
        args += list(h0)
    in_specs += [_const_spec(c.shape) for c in consts] + [pl.BlockSpec(memory_space=pl.ANY)]
    args += list(consts) + [yb_all]
    row_map = lambda b, c: (row_block0 + b * per_seq + c, 0)
    scratch = [pltpu.VMEM((tm, n_state), F32), pltpu.VMEM((tm, n_state), F32)]
    if sequential:
        scratch += [pltpu.VMEM((1, n_state), F32), pltpu.VMEM((1, n_state), F32)]
        st_spec = pl.BlockSpec((1, 1, n_state), lambda b, c: (b, 0, 0))
        st_shape = jax.ShapeDtypeStruct((seqs, 1, n_state), F32)
    else:
        st_spec = pl.BlockSpec((tm, n_state), out_map)
        st_shape = jax.ShapeDtypeStruct((n_rows, n_state), F32)
    return pl.pallas_call(
        functools.partial(_s5_kernel, sequential=sequential, n_blk=n_blk, scan_lanes=1024),
        grid=(seqs, per_seq),
        in_specs=in_specs,
        out_specs=[pl.BlockSpec((tm, width), row_map), st_spec, st_spec],
        out_shape=[jax.ShapeDtypeStruct(yb_all.shape, yb_all.dtype), st_shape, st_shape],
        input_output_aliases={len(args) - 1: 0},
        scratch_shapes=scratch,
        compiler_params=_cparams(("parallel", "arbitrary")),
        name="s5_seq" if sequential else "s5_step",
    )(*args)


def _route(logits, n_grp):
    lane = lax.broadcasted_iota(jnp.int32, logits.shape, 1).astype(F32)
    neg = jnp.float32(-1e30)
    big = jnp.float32(1e9)
    is_grp = lane < n_grp
    gl = jnp.where(is_grp, logits, neg)
    gmax = jnp.max(gl, axis=1, keepdims=True)
    gsum = jnp.sum(jnp.where(is_grp, jnp.exp(gl - gmax), 0.0), axis=1, keepdims=True)
    g_p = 1.0 / gsum
    g_idx = jnp.min(jnp.where(is_grp & (gl == gmax), lane, big), axis=1, keepdims=True)
    lo = n_grp + g_idx * EXPERTS_PER_GROUP
    in_grp = (lane >= lo) & (lane < lo + EXPERTS_PER_GROUP)
    el = jnp.where(in_grp, logits, neg)
    v1 = jnp.max(el, axis=1, keepdims=True)
    i1 = jnp.min(jnp.where(in_grp & (el == v1), lane, big), axis=1, keepdims=True)
    rest = in_grp & (lane != i1)
    el2 = jnp.where(rest, logits, neg)
    v2 = jnp.max(el2, axis=1, keepdims=True)
    i2 = jnp.min(jnp.where(rest & (el2 == v2), lane, big), axis=1, keepdims=True)
    e2 = jnp.exp(v2 - v1)
    w1 = g_p / (1.0 + e2)
    w2 = g_p * e2 / (1.0 + e2)
    return (jnp.where(lane == ROUTE_E1, i1 - n_grp, 0.0) + jnp.where(lane == ROUTE_E2, i2 - n_grp, 0.0)
            + jnp.where(lane == ROUTE_W1, w1, 0.0) + jnp.where(lane == ROUTE_W2, w2, 0.0))


def _merge_kernel(ya_ref, g_ref, yb_ref, ga_ref, gb_ref, x_ref, wa_ref, wb_ref, wo_ref,
                  gf_ref, wr_ref, br_ref, h_out, xn_out, comb_out, *, n_grp):
    ya = (ya_ref[...] * g_ref[...]).astype(BF16)
    ma = jnp.dot(ya, wa_ref[...], preferred_element_type=F32)
    mb = jnp.dot(yb_ref[...], wb_ref[...], preferred_element_type=F32)
    merged = jax.nn.sigmoid(ga_ref[...]) * ma + jax.nn.sigmoid(gb_ref[...]) * mb
    h = x_ref[...] + jnp.dot(merged.astype(BF16), wo_ref[...], preferred_element_type=F32)
    h_out[...] = h
    ms = jnp.mean(h * h, axis=-1, keepdims=True)
    xn = h * lax.rsqrt(ms + NORM_EPS) * gf_ref[...]
    xn_out[...] = xn
    logits = jnp.dot(xn, wr_ref[...], preferred_element_type=F32,
                     precision=lax.Precision.HIGHEST) + br_ref[...]
    comb_out[...] = _route(logits, n_grp)


def _merge(ya, g, yb, proj, x_all, w_br_a, w_br_b, w_out, g_ffn, w_router, b_router,
           tm, ga_col_block, n_grp):
    nt, d = x_all.shape
    wdt = ya.shape[1]
    row = lambda i: (i, 0)
    single = dict(pipeline_mode=pl.Buffered(1))
    in_specs = [pl.BlockSpec((tm, wdt), row), pl.BlockSpec((tm, wdt), row),
                pl.BlockSpec((tm, wdt), row),
                pl.BlockSpec((tm, d), lambda i: (i, ga_col_block)),
                pl.BlockSpec((tm, d), lambda i: (i, ga_col_block + 1)),
                pl.BlockSpec((tm, d), row),
                pl.BlockSpec(w_br_a.shape, lambda i: (0, 0), **single),
                pl.BlockSpec(w_br_b.shape, lambda i: (0, 0), **single),
                pl.BlockSpec(w_out.shape, lambda i: (0, 0), **single),
                _const_spec(g_ffn.shape), _const_spec(w_router.shape), _const_spec(b_router.shape)]
    return pl.pallas_call(
        functools.partial(_merge_kernel, n_grp=n_grp),
        grid=(nt // tm,),
        in_specs=in_specs,
        out_specs=[pl.BlockSpec((tm, d), row), pl.BlockSpec((tm, d), row),
                   pl.BlockSpec((tm, LANES), row)],
        out_shape=[jax.ShapeDtypeStruct((nt, d), F32), jax.ShapeDtypeStruct((nt, d), F32),
                   jax.ShapeDtypeStruct((nt, LANES), F32)],
        compiler_params=_cparams(("parallel",)),
        name="merge_route",
    )(ya, g, yb, proj, proj, x_all, w_br_a, w_br_b, w_out, g_ffn, w_router, b_router)


def _moe_kernel(te_ref, first_ref, nvalid_ref, nused_ref, src_cur, src_nxt, dst_cur,
                xn_hbm, wg_ref, wu_ref, wd_ref, y_hbm,
                xbuf, obuf, wg_bf, wu_bf, wd_bf, gsem, ssem, *, tm):
    i = pl.program_id(0)
    last = pl.num_programs(0) - 1
    n_used = nused_ref[0]
    slot = lax.rem(i, 2)

    def gather_start(src, s):
        def body(r, c):
            pltpu.make_async_copy(xn_hbm.at[pl.ds(src[0, 0, r], 1)], xbuf.at[s, pl.ds(r, 1)],
                                  gsem.at[s]).start()
            return c
        lax.fori_loop(0, tm, body, 0, unroll=8)

    def gather_wait(s):
        pltpu.make_async_copy(xn_hbm.at[pl.ds(0, tm)], xbuf.at[s], gsem.at[s]).wait()

    def scatter_start(s, n_rows):
        def body(r, c):
            pltpu.make_async_copy(obuf.at[s, pl.ds(r, 1)], y_hbm.at[pl.ds(dst_cur[0, 0, r], 1)],
                                  ssem.at[s]).start()
            return c
        lax.fori_loop(0, n_rows, body, 0)

    def scatter_wait(s, n_rows):
        n_full = pl.multiple_of(lax.shift_left(lax.shift_right_logical(n_rows, 3), 3), 8)

        @pl.when(n_full > 0)
        def _():
            pltpu.make_async_copy(obuf.at[s, pl.ds(0, n_full)], y_hbm.at[pl.ds(0, n_full)],
                                  ssem.at[s]).wait()

        def body(r, c):
            pltpu.make_async_copy(obuf.at[s, pl.ds(r, 1)], y_hbm.at[pl.ds(r, 1)],
                                  ssem.at[s]).wait()
            return c
        lax.fori_loop(n_full, n_rows, body, 0)

    @pl.when(i == 0)
    def _():
        gather_start(src_cur, 0)

    @pl.when(i < n_used)
    def _():
        gather_wait(slot)

    @pl.when(i + 1 < n_used)
    def _():
        gather_start(src_nxt, 1 - slot)

    @pl.when((i >= 2) & (i - 2 < n_used))
    def _():
        scatter_wait(slot, nvalid_ref[i - 2])

    @pl.when(i < n_used)
    def _():
        @pl.when(first_ref[i] == 1)
        def _():
            wg_bf[...] = wg_ref[0].astype(BF16)
            wu_bf[...] = wu_ref[0].astype(BF16)
            wd_bf[...] = wd_ref[0].astype(BF16)

        x = xbuf[slot].astype(BF16)
        xg = jnp.dot(x, wg_bf[...], preferred_element_type=F32)
        xu = jnp.dot(x, wu_bf[...], preferred_element_type=F32)
        hid = (jax.nn.silu(xg) * xu).astype(BF16)
        obuf[slot] = jnp.dot(hid, wd_bf[...], preferred_element_type=F32)
        scatter_start(slot, nvalid_ref[i])

    @pl.when(i == last)
    def _():
        @pl.when((i >= 1) & (i - 1 < n_used))
        def _():
            scatter_wait(1 - slot, nvalid_ref[i - 1])

        @pl.when(i < n_used)
        def _():
            scatter_wait(slot, nvalid_ref[i])


def _moe(xn, tile_expert, tile_first, tile_valid, n_used, src_rows, dst_rows, wg, wu, wd,
         n_out_rows, tm):
    nt, d = xn.shape
    n_exp, _, de = wg.shape
    n_tiles = src_rows.shape[0]
    smem_cur = pl.BlockSpec((1, 1, tm), lambda i, *_: (i, 0, 0), memory_space=pltpu.SMEM)
    smem_nxt = pl.BlockSpec((1, 1, tm), lambda i, *_: (jnp.minimum(i + 1, n_tiles - 1), 0, 0),
                            memory_space=pltpu.SMEM)
    grid_spec = pltpu.PrefetchScalarGridSpec(
        num_scalar_prefetch=4,
        grid=(n_tiles,),
        in_specs=[smem_cur, smem_nxt, smem_cur,
                  pl.BlockSpec(memory_space=pl.ANY),
                  pl.BlockSpec((1, d, de), lambda i, te, *_: (te[i], 0, 0)),
                  pl.BlockSpec((1, d, de), lambda i, te, *_: (te[i], 0, 0)),
                  pl.BlockSpec((1, de, d), lambda i, te, *_: (te[i], 0, 0))],
        out_specs=pl.BlockSpec(memory_space=pl.ANY),
        scratch_shapes=[pltpu.VMEM((2, tm, d), F32), pltpu.VMEM((2, tm, d), F32),
                        pltpu.VMEM((d, de), BF16), pltpu.VMEM((d, de), BF16),
                        pltpu.VMEM((de, d), BF16),
                        pltpu.SemaphoreType.DMA((2,)), pltpu.SemaphoreType.DMA((2,))])
    return pl.pallas_call(
        functools.partial(_moe_kernel, tm=tm),
        grid_spec=grid_spec,
        out_shape=jax.ShapeDtypeStruct((n_out_rows, d), F32),
        compiler_params=_cparams(("arbitrary",)),
        name="moe_grouped",
    )(tile_expert, tile_first, tile_valid, n_used, src_rows, src_rows, dst_rows, xn, wg, wu, wd)


def _moe_plan(route, n_exp, tm):
    nt = route.shape[0]
    n_pairs = 2 * nt
    n_tiles = n_pairs // tm + n_exp
    eid = jnp.concatenate([route[:, ROUTE_E1], route[:, ROUTE_E2]]).astype(jnp.int32)
    onehot = (eid[:, None] == jnp.arange(n_exp, dtype=jnp.int32)[None, :]).astype(jnp.int32)
    csum = jnp.cumsum(onehot, axis=0)
    rank = jnp.take_along_axis(csum, eid[:, None], axis=1)[:, 0] - 1
    cnt = csum[-1]
    tiles = (cnt + tm - 1) // tm
    tile_end = jnp.cumsum(tiles)
    n_used = tile_end[-1]
    pos = (tile_end - tiles)[eid] * tm + rank
    pair = jnp.arange(n_pairs, dtype=jnp.int32)
    dst_rows = jnp.zeros((n_tiles * tm,), jnp.int32).at[pos].set(pair)
    src_rows = dst_rows % nt
    tile_id = jnp.minimum(jnp.arange(n_tiles, dtype=jnp.int32), n_used - 1)
    tile_expert = jnp.sum((tile_end[None, :] <= tile_id[:, None]).astype(jnp.int32), axis=1)
    tile_first = jnp.concatenate(
        [jnp.ones((1,), jnp.int32), (tile_expert[1:] != tile_expert[:-1]).astype(jnp.int32)])
    tile_valid = jnp.clip(cnt[tile_expert] - (tile_id - (tile_end - tiles)[tile_expert]) * tm, 0, tm)
    return (tile_expert, tile_first, tile_valid.astype(jnp.int32),
            n_used.reshape(1).astype(jnp.int32),
            src_rows.reshape(n_tiles, 1, tm), dst_rows.reshape(n_tiles, 1, tm), n_pairs)


def _final_kernel(h_ref, y1_ref, y2_ref, route_ref, g_ref, o_ref):
    lane = lax.broadcasted_iota(jnp.int32, route_ref.shape, 1)
    route = route_ref[...]
    w1 = jnp.sum(jnp.where(lane == ROUTE_W1, route, 0.0), axis=1, keepdims=True)
    w2 = jnp.sum(jnp.where(lane == ROUTE_W2, route, 0.0), axis=1, keepdims=True)
    h = h_ref[...] + (w1 * y1_ref[...] + w2 * y2_ref[...])
    ms = jnp.mean(h * h, axis=-1, keepdims=True)
    o_ref[...] = h * lax.rsqrt(ms + NORM_EPS) * g_ref[...]


def _final(h, y_moe, route, g_final, row0, seg_stride, seg_rows, n_seg, tm):
    nt, d = h.shape
    per_seg = seg_rows // tm
    assert all(x % SUBLANES == 0 for x in (row0, seg_stride, tm, nt))
    off = lambda s, c: row0 + s * seg_stride + c * tm
    rows = lambda width, base: pl.BlockSpec((pl.Element(tm), pl.Element(width)),
                                            lambda s, c: (pl.multiple_of(base + off(s, c), SUBLANES), 0))
    return pl.pallas_call(
        _final_kernel,
        grid=(n_seg, per_seg),
        in_specs=[rows(d, 0), rows(d, 0), rows(d, nt), rows(LANES, 0),
                  pl.BlockSpec((1, d), lambda s, c: (0, 0))],
        out_specs=pl.BlockSpec((tm, d), lambda s, c: (s * per_seg + c, 0)),
        out_shape=jax.ShapeDtypeStruct((n_seg * seg_rows, d), F32),
        compiler_params=_cparams(("parallel", "parallel")),
        name="final_norm",
    )(h, y_moe, y_moe, route, g_final)


def _pad_cols(w, to):
    return jnp.pad(w, ((0, 0), (0, to - w.shape[1])))


def _pad_rows(w, to):
    return jnp.pad(w, ((0, to - w.shape[0]), (0, 0)))


def _s5_consts(lam_re, lam_im, log_dt, b_re, b_im, c_re, c_im, d, w_glu, b_glu):
    dt = jnp.exp(log_dt)[:, None]
    mag = jnp.exp(lam_re * dt)
    abar_re = mag * jnp.cos(lam_im * dt)
    abar_im = mag * jnp.sin(lam_im * dt)
    den = lam_re * lam_re + lam_im * lam_im
    nr = abar_re - 1.0
    coef_re = (nr * lam_re + abar_im * lam_im) / den
    coef_im = (abar_im * lam_re - nr * lam_im) / den
    bbar_re = coef_re[..., None] * b_re - coef_im[..., None] * b_im
    bbar_im = coef_re[..., None] * b_im + coef_im[..., None] * b_re
    n_g, n_p, n_c = b_re.shape
    gpb = LANES // n_c
    eye = jnp.eye(gpb, dtype=F32)

    def in_blk(bb):
        bb = bb.reshape(n_g // gpb, gpb, n_p, n_c)
        return jnp.einsum('kgpc,gh->kgchp', bb, eye).reshape(
            n_g // gpb, gpb * n_c, gpb * n_p).astype(BF16)

    def out_blk(cc):
        cc = cc.reshape(n_g // gpb, gpb, n_c, n_p)
        return jnp.einsum('kgcp,gh->khpgc', cc, eye).reshape(
            n_g // gpb, gpb * n_p, gpb * n_c).astype(BF16)

    return (in_blk(bbar_re), in_blk(bbar_im), out_blk(c_re), out_blk(c_im),
            d.reshape(1, -1), abar_re.reshape(1, -1), abar_im.reshape(1, -1),
            w_glu.astype(BF16), b_glu.reshape(1, -1))


def kernel(x_prompt, x_sample, state_shift, state_wkv, state_ssm_re, state_ssm_im, meta_tokens, g_mix, w_in, shift_mu, w0, w_decay_up, a0, w_aaa_up, w_gate_up, k_k, k_a, r_k, lnx_w, lnx_b, ssm_lam_re, ssm_lam_im, ssm_log_dt, ssm_b_re, ssm_b_im, ssm_c_re, ssm_c_im, ssm_d, w_glu, b_glu, w_br_a, w_br_b, w_out, g_ffn, w_router_grp, b_router_grp, w_router_exp, b_router_exp, w_exp_gate, w_exp_up, w_exp_down, g_final):
    depth = g_mix.shape[0]
    assert depth == 1, "single-layer trunk"
    bsz, seq, d = x_prompt.shape
    nb = x_sample.shape[0]
    assert x_sample.shape[1] == 1
    t_p = seq + N_META
    n_p = bsz * t_p
    width = k_k.shape[1]
    heads = width // HEAD
    n_dl, n_al, n_gl = w_decay_up.shape[1], w_aaa_up.shape[1], w_gate_up.shape[1]
    n_grp = w_router_grp.shape[2]
    n_exp = w_router_exp.shape[2]
    assert bsz * heads * 2 == LANES and (nb * heads) % LANES == 0

    s_row0 = -(-n_p // nb) * nb
    n_t = s_row0 + nb
    meta = jnp.broadcast_to(meta_tokens[None], (bsz, N_META, d))
    x_all = jnp.concatenate(
        [jnp.concatenate([meta, x_prompt], axis=1).reshape(n_p, d),
         jnp.zeros((s_row0 - n_p, d), F32), x_sample.reshape(nb, d)], axis=0)

    c_rkv = 3 * width
    c_xw, c_xa, c_xg = c_rkv, c_rkv + n_dl, c_rkv + n_dl + n_al
    c_u = c_xg + n_gl
    c_ga = c_u + width
    lo_width = 2 * LORA_PAD + n_gl

    def regroup(m):
        return jnp.concatenate(
            [m[:, :c_rkv], m[:, c_u:], _pad_cols(m[:, c_xw:c_xa], LORA_PAD),
             _pad_cols(m[:, c_xa:c_xg], LORA_PAD), m[:, c_xg:c_u]], axis=1)

    def ungroup(m):
        lo = c_rkv + width + 2 * d
        return jnp.concatenate(
            [m[:, :c_rkv], m[:, lo:lo + n_dl], m[:, lo + LORA_PAD:lo + LORA_PAD + n_al],
             m[:, lo + 2 * LORA_PAD:]], axis=1)

    w_in_p = regroup(w_in[0].astype(BF16))
    n_proj = w_in_p.shape[1]
    proj = _inproj(x_all, g_mix, w_in_p, tm=768, tn=512)

    mu_all = regroup(_pad_cols(shift_mu, c_u + width + 2 * d))
    mu_rkv, mu_lo = mu_all[:, :c_rkv], mu_all[:, n_proj - lo_width:]
    st_all = regroup(_pad_cols(state_shift[0], c_u + width + 2 * d))
    prev_s = (st_all[:, :c_rkv], st_all[:, n_proj - lo_width:])
    lo_col_block = (n_proj - lo_width) // lo_width
    prep_w = (mu_rkv, mu_lo, w0, a0, k_k, k_a,
              _pad_rows(w_decay_up[0], LORA_PAD).astype(BF16),
              _pad_rows(w_aaa_up[0], LORA_PAD).astype(BF16), w_gate_up[0].astype(BF16))

    tm_seq = t_p // 6
    g_all = jnp.zeros((n_t, width), F32)
    rp, wp, kp, vp, kkp, ap, g_all = _rwkv_prep(proj, 0, n_p, tm_seq, bsz, width, lo_col_block,
                                                lo_width, *prep_w, g_all)
    rs, ws, ks, vs, kks, as_, g_all = _rwkv_prep(proj, s_row0 // nb, nb, nb, 1, width,
                                                 lo_col_block, lo_width, *prep_w, g_all,
                                                 prev=prev_s)

    half = HEAD // 2

    def to_lanes_p(z):
        return z.reshape(bsz, t_p, heads, HEAD).transpose(1, 3, 0, 2).reshape(
            t_p, HEAD, bsz * heads)

    def to_lanes_pi(z):
        z = z.reshape(bsz, t_p, heads, 2, half).transpose(1, 4, 3, 0, 2)
        return z.reshape(t_p, half, LANES)

    def to_lanes_s(z):
        return z.reshape(nb, heads, HEAD).transpose(2, 0, 1).reshape(1, HEAD, nb * heads)

    rk_l = jnp.tile(r_k[0].T, (1, LANES // heads))
    lw_p = jnp.broadcast_to(lnx_w[0].reshape(heads, 2, half).transpose(2, 1, 0)[:, :, None, :],
                            (half, 2, bsz, heads)).reshape(half, LANES)
    lb_p = jnp.broadcast_to(lnx_b[0].reshape(heads, 2, half).transpose(2, 1, 0)[:, :, None, :],
                            (half, 2, bsz, heads)).reshape(half, LANES)
    lw_s = jnp.tile(lnx_w[0].reshape(heads, HEAD).T, (1, LANES // heads))
    lb_s = jnp.tile(lnx_b[0].reshape(heads, HEAD).T, (1, LANES // heads))

    s0_p = jnp.zeros((HEAD, half, LANES), F32)
    y_p, sf_p = _wkv(to_lanes_p(rp), to_lanes_p(wp), to_lanes_p(kp), to_lanes_p(kkp),
                     to_lanes_p(ap), to_lanes_pi(vp), s0_p, rk_l, lw_p, lb_p,
                     tc=48, isplit=True, unroll=8)
    s0_s = state_wkv[0].transpose(3, 2, 0, 1).reshape(HEAD, HEAD, nb * heads)
    y_s, sf_s = _wkv(to_lanes_s(rs), to_lanes_s(ws), to_lanes_s(ks), to_lanes_s(kks),
                     to_lanes_s(as_), to_lanes_s(vs), s0_s, rk_l, lw_s, lb_s,
                     tc=1, isplit=False, unroll=1)

    ya_p = y_p.reshape(t_p, half, 2, bsz, heads).transpose(3, 0, 4, 2, 1).reshape(n_p, width)
    ya_s = y_s.reshape(HEAD, nb, heads).transpose(1, 2, 0).reshape(nb, width)
    wkv_p = sf_p.reshape(HEAD, half, 2, bsz, heads).transpose(3, 4, 2, 1, 0).reshape(
        1, bsz, heads, HEAD, HEAD)
    wkv_s = sf_s.reshape(HEAD, HEAD, nb, heads).transpose(2, 3, 1, 0)[None]

    s5c = _s5_consts(ssm_lam_re[0], ssm_lam_im[0], ssm_log_dt[0], ssm_b_re[0], ssm_b_im[0],
                     ssm_c_re[0], ssm_c_im[0], ssm_d[0], w_glu[0], b_glu)
    n_g, n_st = ssm_lam_re.shape[1], ssm_lam_re.shape[2]
    u_col_block = c_rkv // width
    yb_all = jnp.zeros((n_t, width), BF16)
    yb_all, re_p, im_p = _s5(proj, 0, n_p, tm_seq, bsz, u_col_block, width, s5c, yb_all)
    h0 = (state_ssm_re[0].reshape(nb, n_g * n_st), state_ssm_im[0].reshape(nb, n_g * n_st))
    yb_all, re_s, im_s = _s5(proj, s_row0 // nb, nb, nb, 1, u_col_block, width, s5c, yb_all,
                             h0=h0)

    def unify(a, b):
        return jnp.concatenate([a, jnp.zeros((s_row0 - n_p, a.shape[1]), a.dtype), b], axis=0)

    w_router = _pad_cols(jnp.concatenate([w_router_grp[0], w_router_exp[0]], axis=1), LANES)
    b_router = _pad_cols(jnp.concatenate([b_router_grp, b_router_exp], axis=1), LANES)
    assert n_grp + n_exp <= LANES and n_exp == n_grp * EXPERTS_PER_GROUP
    h1, xn2, route = _merge(unify(ya_p, ya_s), g_all, yb_all, proj, x_all,
                           w_br_a[0].astype(BF16), w_br_b[0].astype(BF16), w_out[0].astype(BF16),
                           g_ffn, w_router, b_router, tm=256, ga_col_block=(c_rkv + width) // d,
                           n_grp=n_grp)
    tm_moe = 256
    te, tf, tv, nu, src_rows, dst_rows, n_moe_rows = _moe_plan(route, n_exp, tm_moe)
    y_moe = _moe(xn2, te, tf, tv, nu, src_rows, dst_rows, w_exp_gate[0], w_exp_up[0], w_exp_down[0],
                 n_moe_rows, tm_moe)
    gfin = g_final.reshape(1, d)
    y_prompt = _final(h1, y_moe, route, gfin, N_META, t_p, seq, bsz, tm=512).reshape(bsz, seq, d)
    y_sample = _final(h1, y_moe, route, gfin, s_row0, 0, nb, 1, tm=nb).reshape(nb, 1, d)
    last_p = proj[t_p - 1:n_p:t_p]
    shift_p = ungroup(last_p)[None]
    shift_s = ungroup(proj[s_row0:])[None]
    return (y_prompt, y_sample, shift_p, wkv_p,
            re_p.reshape(1, bsz, n_g, n_st), im_p.reshape(1, bsz, n_g, n_st),
            shift_s, wkv_s,
            re_s.reshape(1, nb, n_g, n_st), im_s.reshape(1, nb, n_g, n_st))
```

```python
import functools

import jax
import jax.numpy as jnp
from jax import lax
from jax.experimental import pallas as pl
from jax.experimental.pallas import tpu as pltpu

F32 = jnp.float32
BF16 = jnp.bfloat16

NORM_EPS = 1e-6
LNX_EPS = 64e-5
N_META = 16
HEAD = 64
SSM_GROUP = 16
SSM_STATE = 64
EXPERTS_PER_GROUP = 8
LANES = 128
SUBLANES = 8
LORA_PAD = 128
ROUTE_E1, ROUTE_E2, ROUTE_W1, ROUTE_W2 = 0, 1, 2, 3
VMEM_LIMIT = 56 * 1024 * 1024


def _cparams(sem):
    return pltpu.CompilerParams(dimension_semantics=sem, vmem_limit_bytes=VMEM_LIMIT)


def _const_spec(shape):
    nd = len(shape)
    return pl.BlockSpec(shape, lambda *_: (0,) * nd)


def _inproj_kernel(x_ref, g_ref, w_ref, o_ref, xn_ref):
    @pl.when(pl.program_id(1) == 0)
    def _():
        x = x_ref[...]
        ms = jnp.mean(x * x, axis=-1, keepdims=True)
        xn_ref[...] = (x * lax.rsqrt(ms + NORM_EPS) * g_ref[...]).astype(BF16)

    o_ref[...] = jnp.dot(xn_ref[...], w_ref[...], preferred_element_type=F32)


def _inproj(x_all, g_mix, w_in_p, tm, tn):
    nt, d = x_all.shape
    n_out = w_in_p.shape[1]
    return pl.pallas_call(
        _inproj_kernel,
        grid=(nt // tm, n_out // tn),
        in_specs=[pl.BlockSpec((tm, d), lambda i, j: (i, 0)),
                  pl.BlockSpec((1, d), lambda i, j: (0, 0)),
                  pl.BlockSpec((d, tn), lambda i, j: (0, j))],
        out_specs=pl.BlockSpec((tm, tn), lambda i, j: (i, j)),
        out_shape=jax.ShapeDtypeStruct((nt, n_out), F32),
        scratch_shapes=[pltpu.VMEM((tm, d), BF16)],
        compiler_params=_cparams(("parallel", "arbitrary")),
        name="inproj",
    )(x_all, g_mix, w_in_p)


def _softplus(z):
    return jnp.maximum(z, 0.0) + jnp.log1p(jnp.exp(-jnp.abs(z)))


def _rwkv_prep_kernel(*refs, width, carry_prev):
    if carry_prev:
        (rkv_ref, lo_ref, mu_rkv_ref, mu_lo_ref, w0_ref, a0_ref, kk_ref, ka_ref,
         wd_ref, wa_ref, wg_ref, _g_all,
         r_out, w_out, k_out, v_out, kk_out, a_out, g_out, c_rkv, c_lo) = refs

        @pl.when(pl.program_id(1) == 0)
        def _():
            c_rkv[...] = jnp.zeros_like(c_rkv)
            c_lo[...] = jnp.zeros_like(c_lo)
    else:
        (rkv_ref, lo_ref, prev_rkv_ref, prev_lo_ref, mu_rkv_ref, mu_lo_ref, w0_ref, a0_ref,
         kk_ref, ka_ref, wd_ref, wa_ref, wg_ref, _g_all,
         r_out, w_out, k_out, v_out, kk_out, a_out, g_out) = refs

    tm = rkv_ref.shape[0]
    first_row = lax.broadcasted_iota(jnp.int32, (tm, 1), 0) == 0

    def shifted(p, prev_ref, carry_ref, cols):
        if carry_prev:
            prev = jnp.where(first_row, carry_ref[:, cols], pltpu.roll(p, 1, 0))
        else:
            prev = prev_ref[:, cols]
        return prev

    def lerp(p, prev, mu):
        return p + (prev - p) * mu

    lo_cols = slice(0, lo_ref.shape[1])
    p_lo = lo_ref[...]
    q_lo = lerp(p_lo, shifted(p_lo, None if carry_prev else prev_lo_ref,
                              c_lo if carry_prev else None, lo_cols), mu_lo_ref[...])
    xw = q_lo[:, 0:LORA_PAD]
    xa = q_lo[:, LORA_PAD:2 * LORA_PAD]
    xg = q_lo[:, 2 * LORA_PAD:]
    dw = jnp.dot(jnp.tanh(xw).astype(BF16), wd_ref[...], preferred_element_type=F32)
    wlog = -_softplus(-(w0_ref[...] + dw)) - 0.5
    w_out[...] = jnp.exp(-jnp.exp(wlog))
    a = jax.nn.sigmoid(a0_ref[...] + jnp.dot(xa.astype(BF16), wa_ref[...],
                                             preferred_element_type=F32))
    a_out[...] = a
    g_out[...] = jnp.dot(jax.nn.sigmoid(xg).astype(BF16), wg_ref[...],
                         preferred_element_type=F32)

    def q_of(idx):
        cols = slice(idx * width, (idx + 1) * width)
        p = rkv_ref[:, cols]
        prev = shifted(p, None if carry_prev else prev_rkv_ref,
                       c_rkv if carry_prev else None, cols)
        return lerp(p, prev, mu_rkv_ref[:, cols])

    r_out[...] = q_of(0)
    k = q_of(1)
    kk_out[...] = k * kk_ref[...]
    k_out[...] = k * (1.0 + (a - 1.0) * ka_ref[...])
    v_out[...] = q_of(2)

    if carry_prev:
        c_rkv[...] = rkv_ref[tm - 1:tm, :]
        c_lo[...] = lo_ref[tm - 1:tm, :]


def _rwkv_prep(proj, row_block0, n_rows, tm, seqs, width, lo_col_block, lo_width,
               mu_rkv, mu_lo, w0, a0, k_k, k_a, wd, wa, wg, g_all, prev=None):
    carry_prev = prev is None
    per_seq = n_rows // seqs // tm
    row_map = lambda b, c: (row_block0 + b * per_seq + c, 0)
    lo_map = lambda b, c: (row_block0 + b * per_seq + c, lo_col_block)
    out_map = lambda b, c: (b * per_seq + c, 0)
    in_specs = [pl.BlockSpec((tm, 3 * width), row_map), pl.BlockSpec((tm, lo_width), lo_map)]
    args = [proj, proj]
    if not carry_prev:
        in_specs += [pl.BlockSpec((tm, 3 * width), out_map), pl.BlockSpec((tm, lo_width), out_map)]
        args += list(prev)
    consts = [mu_rkv, mu_lo, w0, a0, k_k, k_a, wd, wa, wg]
    in_specs += [_const_spec(c.shape) for c in consts] + [pl.BlockSpec(memory_space=pl.ANY)]
    args += consts + [g_all]
    scratch = []
    if carry_prev:
        scratch = [pltpu.VMEM((1, 3 * width), F32), pltpu.VMEM((1, lo_width), F32)]
    out_sd = jax.ShapeDtypeStruct((n_rows // seqs, seqs * width), F32)
    tmaj_map = lambda b, c: (c, b)
    return pl.pallas_call(
        functools.partial(_rwkv_prep_kernel, width=width, carry_prev=carry_prev),
        grid=(seqs, per_seq),
        in_specs=in_specs,
        out_specs=[pl.BlockSpec((tm, width), tmaj_map)] * 6 + [pl.BlockSpec((tm, width), row_map)],
        out_shape=[out_sd] * 6 + [jax.ShapeDtypeStruct(g_all.shape, g_all.dtype)],
        input_output_aliases={len(args) - 1: 6},
        scratch_shapes=scratch,
        compiler_params=_cparams(("parallel", "arbitrary")),
        name="rwkv_prep_seq" if carry_prev else "rwkv_prep_step",
    )(*args)


def _wkv_kernel(r_ref, w_ref, k_ref, kk_ref, al_ref, v_ref, s0_ref, rk_ref, lw_ref, lb_ref,
                y_ref, sf_ref, s_scr, a_scr, b_scr, *dup_scr, ni, nj, tc, isplit, unroll):
    @pl.when(pl.program_id(1) == 0)
    def _():
        s_scr[...] = s0_ref[...]

    def dup(x):
        return jnp.concatenate([x, x], axis=-1) if isplit else x

    if isplit:
        w_src, k_src, r_src = dup_scr
    else:
        w_src, k_src, r_src = w_ref, k_ref, r_ref

    def prep(t, carry):
        kk = dup(kk_ref[t])
        ss = jnp.sum(kk * kk, axis=0, keepdims=True)
        kkn = kk / jnp.maximum(jnp.sqrt(ss), 1e-12)
        a_scr[t] = -kkn
        b_scr[t] = kkn * dup(al_ref[t])
        if isplit:
            w_src[t] = dup(w_ref[t])
            k_src[t] = dup(k_ref[t])
            r_src[t] = dup(r_ref[t])
        return carry

    lax.fori_loop(0, tc, prep, 0, unroll=unroll)

    def row(ref, t, j):
        return ref[t, j:j + 1, :]

    def tree(parts):
        while len(parts) > 1:
            parts = [parts[i] + parts[i + 1] for i in range(0, len(parts), 2)]
        return parts[0]

    n_acc = 4
    sa0 = tree([sum(s_scr[j] * row(a_scr, 0, j) for j in range(q, nj, n_acc))
                for q in range(n_acc)])

    def step(t, sa):
        v = v_ref[t]
        tn = jnp.minimum(t + 1, tc - 1)
        y = [None] * n_acc
        san = [None] * n_acc
        for j in range(nj):
            s = s_scr[j] * row(w_src, t, j) + sa * row(b_scr, t, j) + v * row(k_src, t, j)
            s_scr[j] = s
            yj = s * row(r_src, t, j)
            sj = s * row(a_scr, tn, j)
            q = j % n_acc
            y[q] = yj if y[q] is None else y[q] + yj
            san[q] = sj if san[q] is None else san[q] + sj
        y_ref[t] = tree(y)
        return tree(san)

    lax.fori_loop(0, tc, step, sa0)

    def isum(x):
        s = jnp.broadcast_to(jnp.sum(x, axis=0, keepdims=True), (8, LANES))
        if isplit:
            s = s + pltpu.roll(s, LANES // 2, 1)
        return s[0:1]

    def post(t, carry):
        y = y_ref[t]
        v = v_ref[t]
        mu = isum(y) * (1.0 / HEAD)
        d = y - mu
        var = isum(d * d) * (1.0 / HEAD)
        yn = d * lax.rsqrt(var + LNX_EPS) * lw_ref[...] + lb_ref[...]
        bonus = jnp.sum(r_src[t] * k_src[t] * rk_ref[...], axis=0, keepdims=True)
        y_ref[t] = yn + bonus * v
        return carry

    lax.fori_loop(0, tc, post, 0, unroll=unroll)

    @pl.when(pl.program_id(1) == pl.num_programs(1) - 1)
    def _():
        sf_ref[...] = s_scr[...]


def _wkv(r, w, k, kk, al, v, s0, rk, lw, lb, tc, isplit, unroll):
    t, nj, jl = r.shape
    ni, lanes = v.shape[1:]
    assert jl == (LANES // 2 if isplit else lanes)
    jspec = pl.BlockSpec((tc, nj, min(jl, LANES)), lambda l, c: (c, 0, l))
    dup_scr = [pltpu.VMEM((tc, nj, LANES), F32)] * 3 if isplit else []
    ispec = pl.BlockSpec((tc, ni, LANES), lambda l, c: (c, 0, l))
    sspec = pl.BlockSpec((nj, ni, LANES), lambda l, c: (0, 0, l))
    return pl.pallas_call(
        functools.partial(_wkv_kernel, ni=ni, nj=nj, tc=tc, isplit=isplit, unroll=unroll),
        grid=(lanes // LANES, t // tc),
        in_specs=[jspec, jspec, jspec, jspec, jspec, ispec, sspec,
                  _const_spec(rk.shape), _const_spec(lw.shape), _const_spec(lb.shape)],
        out_specs=[ispec, sspec],
        out_shape=[jax.ShapeDtypeStruct((t, ni, lanes), F32),
                   jax.ShapeDtypeStruct((nj, ni, lanes), F32)],
        scratch_shapes=[pltpu.VMEM((nj, ni, LANES), F32),
                        pltpu.VMEM((tc, nj, LANES), F32),
                        pltpu.VMEM((tc, nj, LANES), F32)] + dup_scr,
        compiler_params=_cparams(("parallel", "arbitrary")),
        name="wkv_seq" if isplit else "wkv_step",
    )(r, w, k, kk, al, v, s0, rk, lw, lb)


def _s5_kernel(*refs, sequential, n_blk, pitch, unroll):
    if sequential:
        (u_ref, bre_ref, bim_ref, cre_ref, cim_ref, d_ref, are_ref, aim_ref, wglu_ref, bglu_ref,
         _yb_all, yb_ref, hre_out, him_out, st_re, st_im, c_re, c_im) = refs
    else:
        (u_ref, h0re_ref, h0im_ref, bre_ref, bim_ref, cre_ref, cim_ref, d_ref, are_ref, aim_ref,
         wglu_ref, bglu_ref, _yb_all, yb_ref, hre_out, him_out, st_re, st_im) = refs

    tm = u_ref.shape[0]
    kin = bre_ref.shape[1]
    kst = bre_ref.shape[2]
    tiles_per_blk = kst // LANES
    n_tiles = n_blk * tiles_per_blk
    u = u_ref[...]
    ub = u.astype(BF16)

    def tile_rows(k):
        return slice(k * pitch, k * pitch + tm)

    for kb in range(n_blk):
        ukb = ub[:, kb * kin:(kb + 1) * kin]
        bu_re = jnp.dot(ukb, bre_ref[kb], preferred_element_type=F32)
        bu_im = jnp.dot(ukb, bim_ref[kb], preferred_element_type=F32)
        for n in range(tiles_per_blk):
            k = kb * tiles_per_blk + n
            cols = slice(n * LANES, (n + 1) * LANES)
            if sequential:
                st_re[tile_rows(k), :] = bu_re[:, cols]
                st_im[tile_rows(k), :] = bu_im[:, cols]
            else:
                kc = slice(k * LANES, (k + 1) * LANES)
                ar, ai = are_ref[k:k + 1, :], aim_ref[k:k + 1, :]
                h0r, h0i = h0re_ref[:, kc], h0im_ref[:, kc]
                nr = bu_re[:, cols] + (ar * h0r - ai * h0i)
                ni = bu_im[:, cols] + (ar * h0i + ai * h0r)
                st_re[tile_rows(k), :] = nr
                st_im[tile_rows(k), :] = ni
                hre_out[:, kc] = nr
                him_out[:, kc] = ni

    if sequential:
        @pl.when(pl.program_id(1) == 0)
        def _():
            c_re[...] = jnp.zeros_like(c_re)
            c_im[...] = jnp.zeros_like(c_im)

        ar = are_ref[...]
        ai = aim_ref[...]
        groups = range(n_tiles // SUBLANES)

        def token_rows(t, m):
            return pl.ds(t + m * SUBLANES * pitch, SUBLANES, stride=pitch)

        def step(t, h):
            hr, hi = h
            bur = jnp.concatenate([st_re[token_rows(t, m), :] for m in groups], axis=0)
            bui = jnp.concatenate([st_im[token_rows(t, m), :] for m in groups], axis=0)
            nr = ar * hr - ai * hi + bur
            ni = ar * hi + ai * hr + bui
            for m in groups:
                st_re[token_rows(t, m), :] = nr[m * SUBLANES:(m + 1) * SUBLANES]
                st_im[token_rows(t, m), :] = ni[m * SUBLANES:(m + 1) * SUBLANES]
            return nr, ni

        hr, hi = lax.fori_loop(0, tm, step, (c_re[...], c_im[...]), unroll=unroll)
        c_re[...] = hr
        c_im[...] = hi
        hre_out[0] = hr
        him_out[0] = hi

    ys = []
    for kb in range(n_blk):
        tiles = range(kb * tiles_per_blk, (kb + 1) * tiles_per_blk)
        h_re = jnp.concatenate([st_re[tile_rows(k), :] for k in tiles], axis=1)
        h_im = jnp.concatenate([st_im[tile_rows(k), :] for k in tiles], axis=1)
        yre = jnp.dot(h_re.astype(BF16), cre_ref[kb], preferred_element_type=F32)
        yim = jnp.dot(h_im.astype(BF16), cim_ref[kb], preferred_element_type=F32)
        ys.append(yre - yim)
    y = jnp.concatenate(ys, axis=1) + d_ref[...] * u
    y = jax.nn.gelu(y)
    gate = jnp.dot(y.astype(BF16), wglu_ref[...], preferred_element_type=F32) + bglu_ref[...]
    yb_ref[...] = (y * jax.nn.sigmoid(gate)).astype(BF16)


def _s5(proj, row_block0, n_rows, tm, seqs, u_col_block, width, consts, yb_all, h0=None):
    sequential = h0 is None
    bre = consts[0]
    n_blk, _, kst = bre.shape
    n_state = n_blk * kst
    n_tiles = n_state // LANES
    assert tm % SUBLANES == 0 and n_tiles % SUBLANES == 0
    pitch = tm
    per_seq = n_rows // seqs // tm
    u_map = lambda b, c: (row_block0 + b * per_seq + c, u_col_block)
    out_map = lambda b, c: (b * per_seq + c, 0)
    row_map = lambda b, c: (row_block0 + b * per_seq + c, 0)
    in_specs = [pl.BlockSpec((tm, width), u_map)]
    args = [proj]
    if not sequential:
        in_specs += [pl.BlockSpec((tm, n_state), out_map)] * 2
        args += list(h0)
    in_specs += [_const_spec(c.shape) for c in consts] + [pl.BlockSpec(memory_space=pl.ANY)]
    args += list(consts) + [yb_all]
    scratch = [pltpu.VMEM((n_tiles * pitch, LANES), F32), pltpu.VMEM((n_tiles * pitch, LANES), F32)]
    if sequential:
        scratch += [pltpu.VMEM((n_tiles, LANES), F32), pltpu.VMEM((n_tiles, LANES), F32)]
        st_spec = pl.BlockSpec((1, n_tiles, LANES), lambda b, c: (b, 0, 0))
        st_shape = jax.ShapeDtypeStruct((seqs, n_tiles, LANES), F32)
    else:
        st_spec = pl.BlockSpec((tm, n_state), out_map)
        st_shape = jax.ShapeDtypeStruct((n_rows, n_state), F32)
    return pl.pallas_call(
        functools.partial(_s5_kernel, sequential=sequential, n_blk=n_blk, pitch=pitch, unroll=2),
        grid=(seqs, per_seq),
        in_specs=in_specs,
        out_specs=[pl.BlockSpec((tm, width), row_map), st_spec, st_spec],
        out_shape=[jax.ShapeDtypeStruct(yb_all.shape, yb_all.dtype), st_shape, st_shape],
        input_output_aliases={len(args) - 1: 0},
        scratch_shapes=scratch,
        compiler_params=_cparams(("parallel", "arbitrary")),
        name="s5_seq" if sequential else "s5_step",
    )(*args)


def _route(logits, n_grp):
    lane = lax.broadcasted_iota(jnp.int32, logits.shape, 1).astype(F32)
    neg = jnp.float32(-1e30)
    big = jnp.float32(1e9)
    is_grp = lane < n_grp
    gl = jnp.where(is_grp, logits, neg)
    gmax = jnp.max(gl, axis=1, keepdims=True)
    gsum = jnp.sum(jnp.where(is_grp, jnp.exp(gl - gmax), 0.0), axis=1, keepdims=True)
    g_p = 1.0 / gsum
    g_idx = jnp.min(jnp.where(is_grp & (gl == gmax), lane, big), axis=1, keepdims=True)
    lo = n_grp + g_idx * EXPERTS_PER_GROUP
    in_grp = (lane >= lo) & (lane < lo + EXPERTS_PER_GROUP)
    el = jnp.where(in_grp, logits, neg)
    v1 = jnp.max(el, axis=1, keepdims=True)
    i1 = jnp.min(jnp.where(in_grp & (el == v1), lane, big), axis=1, keepdims=True)
    rest = in_grp & (lane != i1)
    el2 = jnp.where(rest, logits, neg)
    v2 = jnp.max(el2, axis=1, keepdims=True)
    i2 = jnp.min(jnp.where(rest & (el2 == v2), lane, big), axis=1, keepdims=True)
    e2 = jnp.exp(v2 - v1)
    w1 = g_p / (1.0 + e2)
    w2 = g_p * e2 / (1.0 + e2)
    return (jnp.where(lane == ROUTE_E1, i1 - n_grp, 0.0) + jnp.where(lane == ROUTE_E2, i2 - n_grp, 0.0)
            + jnp.where(lane == ROUTE_W1, w1, 0.0) + jnp.where(lane == ROUTE_W2, w2, 0.0))


def _merge_kernel(ya_ref, g_ref, yb_ref, ga_ref, gb_ref, x_ref, wa_ref, wb_ref, wo_ref,
                  gf_ref, wr_ref, br_ref, h_out, xn_out, comb_out, *, n_grp):
    ya = (ya_ref[...] * g_ref[...]).astype(BF16)
    ma = jnp.dot(ya, wa_ref[...], preferred_element_type=F32)
    mb = jnp.dot(yb_ref[...], wb_ref[...], preferred_element_type=F32)
    merged = jax.nn.sigmoid(ga_ref[...]) * ma + jax.nn.sigmoid(gb_ref[...]) * mb
    h = x_ref[...] + jnp.dot(merged.astype(BF16), wo_ref[...], preferred_element_type=F32)
    h_out[...] = h
    ms = jnp.mean(h * h, axis=-1, keepdims=True)
    xn = h * lax.rsqrt(ms + NORM_EPS) * gf_ref[...]
    xn_out[...] = xn
    logits = jnp.dot(xn, wr_ref[...], preferred_element_type=F32,
                     precision=lax.Precision.HIGHEST) + br_ref[...]
    comb_out[...] = _route(logits, n_grp)


def _merge(ya, g, yb, proj, x_all, w_br_a, w_br_b, w_out, g_ffn, w_router, b_router,
           tm, ga_col_block, n_grp):
    nt, d = x_all.shape
    wdt = ya.shape[1]
    row = lambda i: (i, 0)
    single = dict(pipeline_mode=pl.Buffered(1))
    in_specs = [pl.BlockSpec((tm, wdt), row), pl.BlockSpec((tm, wdt), row),
                pl.BlockSpec((tm, wdt), row),
                pl.BlockSpec((tm, d), lambda i: (i, ga_col_block)),
                pl.BlockSpec((tm, d), lambda i: (i, ga_col_block + 1)),
                pl.BlockSpec((tm, d), row),
                pl.BlockSpec(w_br_a.shape, lambda i: (0, 0), **single),
                pl.BlockSpec(w_br_b.shape, lambda i: (0, 0), **single),
                pl.BlockSpec(w_out.shape, lambda i: (0, 0), **single),
                _const_spec(g_ffn.shape), _const_spec(w_router.shape), _const_spec(b_router.shape)]
    return pl.pallas_call(
        functools.partial(_merge_kernel, n_grp=n_grp),
        grid=(nt // tm,),
        in_specs=in_specs,
        out_specs=[pl.BlockSpec((tm, d), row), pl.BlockSpec((tm, d), row),
                   pl.BlockSpec((tm, LANES), row)],
        out_shape=[jax.ShapeDtypeStruct((nt, d), F32), jax.ShapeDtypeStruct((nt, d), F32),
                   jax.ShapeDtypeStruct((nt, LANES), F32)],
        compiler_params=_cparams(("parallel",)),
        name="merge_route",
    )(ya, g, yb, proj, proj, x_all, w_br_a, w_br_b, w_out, g_ffn, w_router, b_router)


def _moe_kernel(te_ref, first_ref, nvalid_ref, nused_ref, src_cur, src_nxt, dst_cur,
                xn_hbm, wg_ref, wu_ref, wd_ref, y_hbm,
                xbuf, obuf, wg_bf, wu_bf, wd_bf, gsem, ssem, *, tm):
    i = pl.program_id(0)
    last = pl.num_programs(0) - 1
    n_used = nused_ref[0]
    slot = lax.rem(i, 2)

    def gather_start(src, s):
        def body(r, c):
            pltpu.make_async_copy(xn_hbm.at[pl.ds(src[0, 0, r], 1)], xbuf.at[s, pl.ds(r, 1)],
                                  gsem.at[s]).start()
            return c
        lax.fori_loop(0, tm, body, 0, unroll=8)

    def gather_wait(s):
        pltpu.make_async_copy(xn_hbm.at[pl.ds(0, tm)], xbuf.at[s], gsem.at[s]).wait()

    def scatter_start(s, n_rows):
        def body(r, c):
            pltpu.make_async_copy(obuf.at[s, pl.ds(r, 1)], y_hbm.at[pl.ds(dst_cur[0, 0, r], 1)],
                                  ssem.at[s]).start()
            return c
        lax.fori_loop(0, n_rows, body, 0)

    def scatter_wait(s, n_rows):
        n_full = pl.multiple_of(lax.shift_left(lax.shift_right_logical(n_rows, 3), 3), 8)

        @pl.when(n_full > 0)
        def _():
            pltpu.make_async_copy(obuf.at[s, pl.ds(0, n_full)], y_hbm.at[pl.ds(0, n_full)],
                                  ssem.at[s]).wait()

        def body(r, c):
            pltpu.make_async_copy(obuf.at[s, pl.ds(r, 1)], y_hbm.at[pl.ds(r, 1)],
                                  ssem.at[s]).wait()
            return c
        lax.fori_loop(n_full, n_rows, body, 0)

    @pl.when(i == 0)
    def _():
        gather_start(src_cur, 0)

    @pl.when(i < n_used)
    def _():
        gather_wait(slot)

    @pl.when(i + 1 < n_used)
    def _():
        gather_start(src_nxt, 1 - slot)

    @pl.when((i >= 2) & (i - 2 < n_used))
    def _():
        scatter_wait(slot, nvalid_ref[i - 2])

    @pl.when(i < n_used)
    def _():
        @pl.when(first_ref[i] == 1)
        def _():
            wg_bf[...] = wg_ref[0].astype(BF16)
            wu_bf[...] = wu_ref[0].astype(BF16)
            wd_bf[...] = wd_ref[0].astype(BF16)

        x = xbuf[slot].astype(BF16)
        xg = jnp.dot(x, wg_bf[...], preferred_element_type=F32)
        xu = jnp.dot(x, wu_bf[...], preferred_element_type=F32)
        hid = (jax.nn.silu(xg) * xu).astype(BF16)
        obuf[slot] = jnp.dot(hid, wd_bf[...], preferred_element_type=F32)
        scatter_start(slot, nvalid_ref[i])

    @pl.when(i == last)
    def _():
        @pl.when((i >= 1) & (i - 1 < n_used))
        def _():
            scatter_wait(1 - slot, nvalid_ref[i - 1])

        @pl.when(i < n_used)
        def _():
            scatter_wait(slot, nvalid_ref[i])


def _moe(xn, tile_expert, tile_first, tile_valid, n_used, src_rows, dst_rows, wg, wu, wd,
         n_out_rows, tm):
    nt, d = xn.shape
    n_exp, _, de = wg.shape
    n_tiles = src_rows.shape[0]
    smem_cur = pl.BlockSpec((1, 1, tm), lambda i, *_: (i, 0, 0), memory_space=pltpu.SMEM)
    smem_nxt = pl.BlockSpec((1, 1, tm), lambda i, *_: (jnp.minimum(i + 1, n_tiles - 1), 0, 0),
                            memory_space=pltpu.SMEM)
    grid_spec = pltpu.PrefetchScalarGridSpec(
        num_scalar_prefetch=4,
        grid=(n_tiles,),
        in_specs=[smem_cur, smem_nxt, smem_cur,
                  pl.BlockSpec(memory_space=pl.ANY),
                  pl.BlockSpec((1, d, de), lambda i, te, *_: (te[i], 0, 0)),
                  pl.BlockSpec((1, d, de), lambda i, te, *_: (te[i], 0, 0)),
                  pl.BlockSpec((1, de, d), lambda i, te, *_: (te[i], 0, 0))],
        out_specs=pl.BlockSpec(memory_space=pl.ANY),
        scratch_shapes=[pltpu.VMEM((2, tm, d), F32), pltpu.VMEM((2, tm, d), F32),
                        pltpu.VMEM((d, de), BF16), pltpu.VMEM((d, de), BF16),
                        pltpu.VMEM((de, d), BF16),
                        pltpu.SemaphoreType.DMA((2,)), pltpu.SemaphoreType.DMA((2,))])
    return pl.pallas_call(
        functools.partial(_moe_kernel, tm=tm),
        grid_spec=grid_spec,
        out_shape=jax.ShapeDtypeStruct((n_out_rows, d), F32),
        compiler_params=_cparams(("arbitrary",)),
        name="moe_grouped",
    )(tile_expert, tile_first, tile_valid, n_used, src_rows, src_rows, dst_rows, xn, wg, wu, wd)


def _moe_plan(route, n_exp, tm):
    nt = route.shape[0]
    n_pairs = 2 * nt
    n_tiles = n_pairs // tm + n_exp
    eid = jnp.concatenate([route[:, ROUTE_E1], route[:, ROUTE_E2]]).astype(jnp.int32)
    onehot = (eid[:, None] == jnp.arange(n_exp, dtype=jnp.int32)[None, :]).astype(jnp.int32)
    csum = jnp.cumsum(onehot, axis=0)
    rank = jnp.take_along_axis(csum, eid[:, None], axis=1)[:, 0] - 1
    cnt = csum[-1]
    tiles = (cnt + tm - 1) // tm
    tile_end = jnp.cumsum(tiles)
    n_used = tile_end[-1]
    pos = (tile_end - tiles)[eid] * tm + rank
    pair = jnp.arange(n_pairs, dtype=jnp.int32)
    dst_rows = jnp.zeros((n_tiles * tm,), jnp.int32).at[pos].set(pair)
    src_rows = dst_rows % nt
    tile_id = jnp.minimum(jnp.arange(n_tiles, dtype=jnp.int32), n_used - 1)
    tile_expert = jnp.sum((tile_end[None, :] <= tile_id[:, None]).astype(jnp.int32), axis=1)
    tile_first = jnp.concatenate(
        [jnp.ones((1,), jnp.int32), (tile_expert[1:] != tile_expert[:-1]).astype(jnp.int32)])
    tile_valid = jnp.clip(cnt[tile_expert] - (tile_id - (tile_end - tiles)[tile_expert]) * tm, 0, tm)
    return (tile_expert, tile_first, tile_valid.astype(jnp.int32),
            n_used.reshape(1).astype(jnp.int32),
            src_rows.reshape(n_tiles, 1, tm), dst_rows.reshape(n_tiles, 1, tm), n_pairs)


def _final_kernel(h_ref, y1_ref, y2_ref, route_ref, g_ref, o_ref):
    lane = lax.broadcasted_iota(jnp.int32, route_ref.shape, 1)
    route = route_ref[...]
    w1 = jnp.sum(jnp.where(lane == ROUTE_W1, route, 0.0), axis=1, keepdims=True)
    w2 = jnp.sum(jnp.where(lane == ROUTE_W2, route, 0.0), axis=1, keepdims=True)
    h = h_ref[...] + (w1 * y1_ref[...] + w2 * y2_ref[...])
    ms = jnp.mean(h * h, axis=-1, keepdims=True)
    o_ref[...] = h * lax.rsqrt(ms + NORM_EPS) * g_ref[...]


def _final(h, y_moe, route, g_final, row0, seg_stride, seg_rows, n_seg, tm):
    nt, d = h.shape
    per_seg = seg_rows // tm
    assert all(x % SUBLANES == 0 for x in (row0, seg_stride, tm, nt))
    off = lambda s, c: row0 + s * seg_stride + c * tm
    rows = lambda width, base: pl.BlockSpec((pl.Element(tm), pl.Element(width)),
                                            lambda s, c: (pl.multiple_of(base + off(s, c), SUBLANES), 0))
    return pl.pallas_call(
        _final_kernel,
        grid=(n_seg, per_seg),
        in_specs=[rows(d, 0), rows(d, 0), rows(d, nt), rows(LANES, 0),
                  pl.BlockSpec((1, d), lambda s, c: (0, 0))],
        out_specs=pl.BlockSpec((tm, d), lambda s, c: (s * per_seg + c, 0)),
        out_shape=jax.ShapeDtypeStruct((n_seg * seg_rows, d), F32),
        compiler_params=_cparams(("parallel", "parallel")),
        name="final_norm",
    )(h, y_moe, y_moe, route, g_final)


def _pad_cols(w, to):
    return jnp.pad(w, ((0, 0), (0, to - w.shape[1])))


def _pad_rows(w, to):
    return jnp.pad(w, ((0, to - w.shape[0]), (0, 0)))


def _s5_consts(lam_re, lam_im, log_dt, b_re, b_im, c_re, c_im, d, w_glu, b_glu):
    dt = jnp.exp(log_dt)[:, None]
    mag = jnp.exp(lam_re * dt)
    abar_re = mag * jnp.cos(lam_im * dt)
    abar_im = mag * jnp.sin(lam_im * dt)
    den = lam_re * lam_re + lam_im * lam_im
    nr = abar_re - 1.0
    coef_re = (nr * lam_re + abar_im * lam_im) / den
    coef_im = (abar_im * lam_re - nr * lam_im) / den
    bbar_re = coef_re[..., None] * b_re - coef_im[..., None] * b_im
    bbar_im = coef_re[..., None] * b_im + coef_im[..., None] * b_re
    n_g, n_p, n_c = b_re.shape
    gpb = LANES // n_c
    eye = jnp.eye(gpb, dtype=F32)

    def in_blk(bb):
        bb = bb.reshape(n_g // gpb, gpb, n_p, n_c)
        return jnp.einsum('kgpc,gh->kgchp', bb, eye).reshape(
            n_g // gpb, gpb * n_c, gpb * n_p).astype(BF16)

    def out_blk(cc):
        cc = cc.reshape(n_g // gpb, gpb, n_c, n_p)
        return jnp.einsum('kgcp,gh->khpgc', cc, eye).reshape(
            n_g // gpb, gpb * n_p, gpb * n_c).astype(BF16)

    return (in_blk(bbar_re), in_blk(bbar_im), out_blk(c_re), out_blk(c_im),
            d.reshape(1, -1), abar_re.reshape(-1, LANES), abar_im.reshape(-1, LANES),
            w_glu.astype(BF16), b_glu.reshape(1, -1))


def kernel(x_prompt, x_sample, state_shift, state_wkv, state_ssm_re, state_ssm_im, meta_tokens, g_mix, w_in, shift_mu, w0, w_decay_up, a0, w_aaa_up, w_gate_up, k_k, k_a, r_k, lnx_w, lnx_b, ssm_lam_re, ssm_lam_im, ssm_log_dt, ssm_b_re, ssm_b_im, ssm_c_re, ssm_c_im, ssm_d, w_glu, b_glu, w_br_a, w_br_b, w_out, g_ffn, w_router_grp, b_router_grp, w_router_exp, b_router_exp, w_exp_gate, w_exp_up, w_exp_down, g_final):
    depth = g_mix.shape[0]
    assert depth == 1, "single-layer trunk"
    bsz, seq, d = x_prompt.shape
    nb = x_sample.shape[0]
    assert x_sample.shape[1] == 1
    t_p = seq + N_META
    n_p = bsz * t_p
    width = k_k.shape[1]
    heads = width // HEAD
    n_dl, n_al, n_gl = w_decay_up.shape[1], w_aaa_up.shape[1], w_gate_up.shape[1]
    n_grp = w_router_grp.shape[2]
    n_exp = w_router_exp.shape[2]
    assert bsz * heads * 2 == LANES and (nb * heads) % LANES == 0

    s_row0 = -(-n_p // nb) * nb
    n_t = s_row0 + nb
    meta = jnp.broadcast_to(meta_tokens[None], (bsz, N_META, d))
    x_all = jnp.concatenate(
        [jnp.concatenate([meta, x_prompt], axis=1).reshape(n_p, d),
         jnp.zeros((s_row0 - n_p, d), F32), x_sample.reshape(nb, d)], axis=0)

    c_rkv = 3 * width
    c_xw, c_xa, c_xg = c_rkv, c_rkv + n_dl, c_rkv + n_dl + n_al
    c_u = c_xg + n_gl
    c_ga = c_u + width
    lo_width = 2 * LORA_PAD + n_gl

    def regroup(m):
        return jnp.concatenate(
            [m[:, :c_rkv], m[:, c_u:], _pad_cols(m[:, c_xw:c_xa], LORA_PAD),
             _pad_cols(m[:, c_xa:c_xg], LORA_PAD), m[:, c_xg:c_u]], axis=1)

    def ungroup(m):
        lo = c_rkv + width + 2 * d
        return jnp.concatenate(
            [m[:, :c_rkv], m[:, lo:lo + n_dl], m[:, lo + LORA_PAD:lo + LORA_PAD + n_al],
             m[:, lo + 2 * LORA_PAD:]], axis=1)

    w_in_p = regroup(w_in[0].astype(BF16))
    n_proj = w_in_p.shape[1]
    proj = _inproj(x_all, g_mix, w_in_p, tm=1408, tn=512)

    mu_all = regroup(_pad_cols(shift_mu, c_u + width + 2 * d))
    mu_rkv, mu_lo = mu_all[:, :c_rkv], mu_all[:, n_proj - lo_width:]
    st_all = regroup(_pad_cols(state_shift[0], c_u + width + 2 * d))
    prev_s = (st_all[:, :c_rkv], st_all[:, n_proj - lo_width:])
    lo_col_block = (n_proj - lo_width) // lo_width
    prep_w = (mu_rkv, mu_lo, w0, a0, k_k, k_a,
              _pad_rows(w_decay_up[0], LORA_PAD).astype(BF16),
              _pad_rows(w_aaa_up[0], LORA_PAD).astype(BF16), w_gate_up[0].astype(BF16))

    tm_seq = t_p // 6
    g_all = jnp.zeros((n_t, width), F32)
    rp, wp, kp, vp, kkp, ap, g_all = _rwkv_prep(proj, 0, n_p, tm_seq, bsz, width, lo_col_block,
                                                lo_width, *prep_w, g_all)
    rs, ws, ks, vs, kks, as_, g_all = _rwkv_prep(proj, s_row0 // nb, nb, nb, 1, width,
                                                 lo_col_block, lo_width, *prep_w, g_all,
                                                 prev=prev_s)

    half = HEAD // 2

    def to_lanes_p(z):
        return z.reshape(t_p, bsz * heads, HEAD).transpose(0, 2, 1)

    def to_lanes_pi(z):
        z = z.reshape(t_p, bsz * heads, 2, half).transpose(0, 3, 2, 1)
        return z.reshape(t_p, half, LANES)

    def to_lanes_s(z):
        return z.reshape(nb, heads, HEAD).transpose(2, 0, 1).reshape(1, HEAD, nb * heads)

    rk_l = jnp.tile(r_k[0].T, (1, LANES // heads))
    lw_p = jnp.broadcast_to(lnx_w[0].reshape(heads, 2, half).transpose(2, 1, 0)[:, :, None, :],
                            (half, 2, bsz, heads)).reshape(half, LANES)
    lb_p = jnp.broadcast_to(lnx_b[0].reshape(heads, 2, half).transpose(2, 1, 0)[:, :, None, :],
                            (half, 2, bsz, heads)).reshape(half, LANES)
    lw_s = jnp.tile(lnx_w[0].reshape(heads, HEAD).T, (1, LANES // heads))
    lb_s = jnp.tile(lnx_b[0].reshape(heads, HEAD).T, (1, LANES // heads))

    s0_p = jnp.zeros((HEAD, half, LANES), F32)
    y_p, sf_p = _wkv(to_lanes_p(rp), to_lanes_p(wp), to_lanes_p(kp), to_lanes_p(kkp),
                     to_lanes_p(ap), to_lanes_pi(vp), s0_p, rk_l, lw_p, lb_p,
                     tc=48, isplit=True, unroll=8)
    s0_s = state_wkv[0].transpose(3, 2, 0, 1).reshape(HEAD, HEAD, nb * heads)
    y_s, sf_s = _wkv(to_lanes_s(rs), to_lanes_s(ws), to_lanes_s(ks), to_lanes_s(kks),
                     to_lanes_s(as_), to_lanes_s(vs), s0_s, rk_l, lw_s, lb_s,
                     tc=1, isplit=False, unroll=1)

    ya_p = y_p.reshape(t_p, half, 2, bsz, heads).transpose(3, 0, 4, 2, 1).reshape(n_p, width)
    ya_s = y_s.reshape(HEAD, nb, heads).transpose(1, 2, 0).reshape(nb, width)
    wkv_p = sf_p.reshape(HEAD, half, 2, bsz, heads).transpose(3, 4, 2, 1, 0).reshape(
        1, bsz, heads, HEAD, HEAD)
    wkv_s = sf_s.reshape(HEAD, HEAD, nb, heads).transpose(2, 3, 1, 0)[None]

    s5c = _s5_consts(ssm_lam_re[0], ssm_lam_im[0], ssm_log_dt[0], ssm_b_re[0], ssm_b_im[0],
                     ssm_c_re[0], ssm_c_im[0], ssm_d[0], w_glu[0], b_glu)
    n_g, n_st = ssm_lam_re.shape[1], ssm_lam_re.shape[2]
    u_col_block = c_rkv // width
    yb_all = jnp.zeros((n_t, width), BF16)
    yb_all, re_p, im_p = _s5(proj, 0, n_p, tm_seq, bsz, u_col_block, width, s5c, yb_all)
    h0 = (state_ssm_re[0].reshape(nb, n_g * n_st), state_ssm_im[0].reshape(nb, n_g * n_st))
    yb_all, re_s, im_s = _s5(proj, s_row0 // nb, nb, nb, 1, u_col_block, width, s5c, yb_all,
                             h0=h0)

    def unify(a, b):
        return jnp.concatenate([a, jnp.zeros((s_row0 - n_p, a.shape[1]), a.dtype), b], axis=0)

    w_router = _pad_cols(jnp.concatenate([w_router_grp[0], w_router_exp[0]], axis=1), LANES)
    b_router = _pad_cols(jnp.concatenate([b_router_grp, b_router_exp], axis=1), LANES)
    assert n_grp + n_exp <= LANES and n_exp == n_grp * EXPERTS_PER_GROUP
    h1, xn2, route = _merge(unify(ya_p, ya_s), g_all, yb_all, proj, x_all,
                           w_br_a[0].astype(BF16), w_br_b[0].astype(BF16), w_out[0].astype(BF16),
                           g_ffn, w_router, b_router, tm=256, ga_col_block=(c_rkv + width) // d,
                           n_grp=n_grp)
    tm_moe = 256
    te, tf, tv, nu, src_rows, dst_rows, n_moe_rows = _moe_plan(route, n_exp, tm_moe)
    y_moe = _moe(xn2, te, tf, tv, nu, src_rows, dst_rows, w_exp_gate[0], w_exp_up[0], w_exp_down[0],
                 n_moe_rows, tm_moe)
    gfin = g_final.reshape(1, d)
    y_prompt = _final(h1, y_moe, route, gfin, N_META, t_p, seq, bsz, tm=512).reshape(bsz, seq, d)
    y_sample = _final(h1, y_moe, route, gfin, s_row0, 0, nb, 1, tm=nb).reshape(nb, 1, d)
    last_p = proj[t_p - 1:n_p:t_p]
    shift_p = ungroup(last_p)[None]
    shift_s = ungroup(proj[s_row0:])[None]
    return (y_prompt, y_sample, shift_p, wkv_p,
            re_p.reshape(1, bsz, n_g, n_st), im_p.reshape(1, bsz, n_g, n_st),
            shift_s, wkv_s,
            re_s.reshape(1, nb, n_g, n_st), im_s.reshape(1, nb, n_g, n_st))
```

```python
import functools

import jax
import jax.numpy as jnp
from jax import lax
from jax.experimental import pallas as pl
from jax.experimental.pallas import tpu as pltpu

F32 = jnp.float32
BF16 = jnp.bfloat16

NORM_EPS = 1e-6
LNX_EPS = 64e-5
N_META = 16
HEAD = 64
SSM_GROUP = 16
SSM_STATE = 64
EXPERTS_PER_GROUP = 8
LANES = 128
SUBLANES = 8
LORA_PAD = 128
ROUTE_E1, ROUTE_E2, ROUTE_W1, ROUTE_W2 = 0, 1, 2, 3
VMEM_LIMIT = 56 * 1024 * 1024


def _cparams(sem):
    return pltpu.CompilerParams(dimension_semantics=sem, vmem_limit_bytes=VMEM_LIMIT)


def _const_spec(shape):
    nd = len(shape)
    return pl.BlockSpec(shape, lambda *_: (0,) * nd)


def _inproj_kernel(x_ref, g_ref, w_ref, o_ref, xn_ref):
    @pl.when(pl.program_id(1) == 0)
    def _():
        x = x_ref[...]
        ms = jnp.mean(x * x, axis=-1, keepdims=True)
        xn_ref[...] = (x * lax.rsqrt(ms + NORM_EPS) * g_ref[...]).astype(BF16)

    o_ref[...] = jnp.dot(xn_ref[...], w_ref[...], preferred_element_type=F32)


def _inproj(x_all, g_mix, w_in_p, tm, tn):
    nt, d = x_all.shape
    n_out = w_in_p.shape[1]
    return pl.pallas_call(
        _inproj_kernel,
        grid=(nt // tm, n_out // tn),
        in_specs=[pl.BlockSpec((tm, d), lambda i, j: (i, 0)),
                  pl.BlockSpec((1, d), lambda i, j: (0, 0)),
                  pl.BlockSpec((d, tn), lambda i, j: (0, j))],
        out_specs=pl.BlockSpec((tm, tn), lambda i, j: (i, j)),
        out_shape=jax.ShapeDtypeStruct((nt, n_out), F32),
        scratch_shapes=[pltpu.VMEM((tm, d), BF16)],
        compiler_params=_cparams(("parallel", "arbitrary")),
        name="inproj",
    )(x_all, g_mix, w_in_p)


def _softplus(z):
    return jnp.maximum(z, 0.0) + jnp.log1p(jnp.exp(-jnp.abs(z)))


def _rwkv_prep_kernel(*refs, width, carry_prev):
    if carry_prev:
        (rkv_ref, lo_ref, mu_rkv_ref, mu_lo_ref, w0_ref, a0_ref, kk_ref, ka_ref,
         wd_ref, wa_ref, wg_ref, _g_all,
         r_out, w_out, k_out, v_out, kk_out, a_out, g_out, c_rkv, c_lo) = refs

        @pl.when(pl.program_id(1) == 0)
        def _():
            c_rkv[...] = jnp.zeros_like(c_rkv)
            c_lo[...] = jnp.zeros_like(c_lo)
    else:
        (rkv_ref, lo_ref, prev_rkv_ref, prev_lo_ref, mu_rkv_ref, mu_lo_ref, w0_ref, a0_ref,
         kk_ref, ka_ref, wd_ref, wa_ref, wg_ref, _g_all,
         r_out, w_out, k_out, v_out, kk_out, a_out, g_out) = refs

    tm = rkv_ref.shape[0]
    first_row = lax.broadcasted_iota(jnp.int32, (tm, 1), 0) == 0

    def shifted(p, prev_ref, carry_ref, cols):
        if carry_prev:
            prev = jnp.where(first_row, carry_ref[:, cols], pltpu.roll(p, 1, 0))
        else:
            prev = prev_ref[:, cols]
        return prev

    def lerp(p, prev, mu):
        return p + (prev - p) * mu

    lo_cols = slice(0, lo_ref.shape[1])
    p_lo = lo_ref[...]
    q_lo = lerp(p_lo, shifted(p_lo, None if carry_prev else prev_lo_ref,
                              c_lo if carry_prev else None, lo_cols), mu_lo_ref[...])
    xw = q_lo[:, 0:LORA_PAD]
    xa = q_lo[:, LORA_PAD:2 * LORA_PAD]
    xg = q_lo[:, 2 * LORA_PAD:]
    dw = jnp.dot(jnp.tanh(xw).astype(BF16), wd_ref[...], preferred_element_type=F32)
    wlog = -_softplus(-(w0_ref[...] + dw)) - 0.5
    w_out[...] = jnp.exp(-jnp.exp(wlog))
    a = jax.nn.sigmoid(a0_ref[...] + jnp.dot(xa.astype(BF16), wa_ref[...],
                                             preferred_element_type=F32))
    a_out[...] = a
    g_out[...] = jnp.dot(jax.nn.sigmoid(xg).astype(BF16), wg_ref[...],
                         preferred_element_type=F32)

    def q_of(idx):
        cols = slice(idx * width, (idx + 1) * width)
        p = rkv_ref[:, cols]
        prev = shifted(p, None if carry_prev else prev_rkv_ref,
                       c_rkv if carry_prev else None, cols)
        return lerp(p, prev, mu_rkv_ref[:, cols])

    r_out[...] = q_of(0)
    k = q_of(1)
    kk_out[...] = k * kk_ref[...]
    k_out[...] = k * (1.0 + (a - 1.0) * ka_ref[...])
    v_out[...] = q_of(2)

    if carry_prev:
        c_rkv[...] = rkv_ref[tm - 1:tm, :]
        c_lo[...] = lo_ref[tm - 1:tm, :]


def _rwkv_prep(proj, row_block0, n_rows, tm, seqs, width, lo_col_block, lo_width,
               mu_rkv, mu_lo, w0, a0, k_k, k_a, wd, wa, wg, g_all, prev=None):
    carry_prev = prev is None
    per_seq = n_rows // seqs // tm
    row_map = lambda b, c: (row_block0 + b * per_seq + c, 0)
    lo_map = lambda b, c: (row_block0 + b * per_seq + c, lo_col_block)
    out_map = lambda b, c: (b * per_seq + c, 0)
    in_specs = [pl.BlockSpec((tm, 3 * width), row_map), pl.BlockSpec((tm, lo_width), lo_map)]
    args = [proj, proj]
    if not carry_prev:
        in_specs += [pl.BlockSpec((tm, 3 * width), out_map), pl.BlockSpec((tm, lo_width), out_map)]
        args += list(prev)
    consts = [mu_rkv, mu_lo, w0, a0, k_k, k_a, wd, wa, wg]
    in_specs += [_const_spec(c.shape) for c in consts] + [pl.BlockSpec(memory_space=pl.ANY)]
    args += consts + [g_all]
    scratch = []
    if carry_prev:
        scratch = [pltpu.VMEM((1, 3 * width), F32), pltpu.VMEM((1, lo_width), F32)]
    out_sd = jax.ShapeDtypeStruct((n_rows // seqs, seqs * width), F32)
    tmaj_map = lambda b, c: (c, b)
    return pl.pallas_call(
        functools.partial(_rwkv_prep_kernel, width=width, carry_prev=carry_prev),
        grid=(seqs, per_seq),
        in_specs=in_specs,
        out_specs=[pl.BlockSpec((tm, width), tmaj_map)] * 6 + [pl.BlockSpec((tm, width), row_map)],
        out_shape=[out_sd] * 6 + [jax.ShapeDtypeStruct(g_all.shape, g_all.dtype)],
        input_output_aliases={len(args) - 1: 6},
        scratch_shapes=scratch,
        compiler_params=_cparams(("parallel", "arbitrary")),
        name="rwkv_prep_seq" if carry_prev else "rwkv_prep_step",
    )(*args)


def _wkv_kernel(r_ref, w_ref, k_ref, kk_ref, al_ref, v_ref, s0_ref, rk_ref, lw_ref, lb_ref,
                y_ref, sf_ref, s_scr, a_scr, b_scr, *dup_scr, ni, nj, tc, isplit, unroll):
    @pl.when(pl.program_id(1) == 0)
    def _():
        s_scr[...] = s0_ref[...]

    def dup(x):
        return jnp.concatenate([x, x], axis=-1) if isplit else x

    if isplit:
        w_src, k_src, r_src = dup_scr
    else:
        w_src, k_src, r_src = w_ref, k_ref, r_ref

    def prep(t, carry):
        kk = dup(kk_ref[t])
        ss = jnp.sum(kk * kk, axis=0, keepdims=True)
        kkn = kk / jnp.maximum(jnp.sqrt(ss), 1e-12)
        a_scr[t] = -kkn
        b_scr[t] = kkn * dup(al_ref[t])
        if isplit:
            w_src[t] = dup(w_ref[t])
            k_src[t] = dup(k_ref[t])
            r_src[t] = dup(r_ref[t])
        return carry

    lax.fori_loop(0, tc, prep, 0, unroll=unroll)

    def row(ref, t, j):
        return ref[t, j:j + 1, :]

    def tree(parts):
        while len(parts) > 1:
            parts = [parts[i] + parts[i + 1] for i in range(0, len(parts), 2)]
        return parts[0]

    n_acc = 4
    sa0 = tree([sum(s_scr[j] * row(a_scr, 0, j) for j in range(q, nj, n_acc))
                for q in range(n_acc)])

    def step(t, sa):
        v = v_ref[t]
        tn = jnp.minimum(t + 1, tc - 1)
        y = [None] * n_acc
        san = [None] * n_acc
        for j in range(nj):
            s = s_scr[j] * row(w_src, t, j) + sa * row(b_scr, t, j) + v * row(k_src, t, j)
            s_scr[j] = s
            yj = s * row(r_src, t, j)
            sj = s * row(a_scr, tn, j)
            q = j % n_acc
            y[q] = yj if y[q] is None else y[q] + yj
            san[q] = sj if san[q] is None else san[q] + sj
        y_ref[t] = tree(y)
        return tree(san)

    lax.fori_loop(0, tc, step, sa0)

    def isum(x):
        s = jnp.broadcast_to(jnp.sum(x, axis=0, keepdims=True), (8, LANES))
        if isplit:
            s = s + pltpu.roll(s, LANES // 2, 1)
        return s[0:1]

    def post(t, carry):
        y = y_ref[t]
        v = v_ref[t]
        mu = isum(y) * (1.0 / HEAD)
        d = y - mu
        var = isum(d * d) * (1.0 / HEAD)
        yn = d * lax.rsqrt(var + LNX_EPS) * lw_ref[...] + lb_ref[...]
        bonus = jnp.sum(r_src[t] * k_src[t] * rk_ref[...], axis=0, keepdims=True)
        y_ref[t] = yn + bonus * v
        return carry

    lax.fori_loop(0, tc, post, 0, unroll=unroll)

    @pl.when(pl.program_id(1) == pl.num_programs(1) - 1)
    def _():
        sf_ref[...] = s_scr[...]


def _wkv(r, w, k, kk, al, v, s0, rk, lw, lb, tc, isplit, unroll):
    t, nj, jl = r.shape
    ni, lanes = v.shape[1:]
    assert jl == (LANES // 2 if isplit else lanes)
    jspec = pl.BlockSpec((tc, nj, min(jl, LANES)), lambda l, c: (c, 0, l))
    dup_scr = [pltpu.VMEM((tc, nj, LANES), F32)] * 3 if isplit else []
    ispec = pl.BlockSpec((tc, ni, LANES), lambda l, c: (c, 0, l))
    sspec = pl.BlockSpec((nj, ni, LANES), lambda l, c: (0, 0, l))
    return pl.pallas_call(
        functools.partial(_wkv_kernel, ni=ni, nj=nj, tc=tc, isplit=isplit, unroll=unroll),
        grid=(lanes // LANES, t // tc),
        in_specs=[jspec, jspec, jspec, jspec, jspec, ispec, sspec,
                  _const_spec(rk.shape), _const_spec(lw.shape), _const_spec(lb.shape)],
        out_specs=[ispec, sspec],
        out_shape=[jax.ShapeDtypeStruct((t, ni, lanes), F32),
                   jax.ShapeDtypeStruct((nj, ni, lanes), F32)],
        scratch_shapes=[pltpu.VMEM((nj, ni, LANES), F32),
                        pltpu.VMEM((tc, nj, LANES), F32),
                        pltpu.VMEM((tc, nj, LANES), F32)] + dup_scr,
        compiler_params=_cparams(("parallel", "arbitrary")),
        name="wkv_seq" if isplit else "wkv_step",
    )(r, w, k, kk, al, v, s0, rk, lw, lb)


def _s5_kernel(*refs, sequential, n_blk, pitch, unroll):
    if sequential:
        (u_ref, bre_ref, bim_ref, cre_ref, cim_ref, d_ref, are_ref, aim_ref, wglu_ref, bglu_ref,
         _yb_all, yb_ref, hre_out, him_out, st_re, st_im, c_re, c_im) = refs
    else:
        (u_ref, h0re_ref, h0im_ref, bre_ref, bim_ref, cre_ref, cim_ref, d_ref, are_ref, aim_ref,
         wglu_ref, bglu_ref, _yb_all, yb_ref, hre_out, him_out, st_re, st_im) = refs

    tm = u_ref.shape[0]
    kin = bre_ref.shape[1]
    kst = bre_ref.shape[2]
    tiles_per_blk = kst // LANES
    n_tiles = n_blk * tiles_per_blk
    u = u_ref[...]
    ub = u.astype(BF16)

    def tile_rows(k):
        return slice(k * pitch, k * pitch + tm)

    for kb in range(n_blk):
        ukb = ub[:, kb * kin:(kb + 1) * kin]
        bu_re = jnp.dot(ukb, bre_ref[kb], preferred_element_type=F32)
        bu_im = jnp.dot(ukb, bim_ref[kb], preferred_element_type=F32)
        for n in range(tiles_per_blk):
            k = kb * tiles_per_blk + n
            cols = slice(n * LANES, (n + 1) * LANES)
            if sequential:
                st_re[tile_rows(k), :] = bu_re[:, cols]
                st_im[tile_rows(k), :] = bu_im[:, cols]
            else:
                kc = slice(k * LANES, (k + 1) * LANES)
                ar, ai = are_ref[k:k + 1, :], aim_ref[k:k + 1, :]
                h0r, h0i = h0re_ref[:, kc], h0im_ref[:, kc]
                nr = bu_re[:, cols] + (ar * h0r - ai * h0i)
                ni = bu_im[:, cols] + (ar * h0i + ai * h0r)
                st_re[tile_rows(k), :] = nr
                st_im[tile_rows(k), :] = ni
                hre_out[:, kc] = nr
                him_out[:, kc] = ni

    if sequential:
        @pl.when(pl.program_id(1) == 0)
        def _():
            c_re[...] = jnp.zeros_like(c_re)
            c_im[...] = jnp.zeros_like(c_im)

        ar = are_ref[...]
        ai = aim_ref[...]
        groups = range(n_tiles // SUBLANES)

        def token_rows(t, m):
            return pl.ds(t + m * SUBLANES * pitch, SUBLANES, stride=pitch)

        def step(t, h):
            hr, hi = h
            bur = jnp.concatenate([st_re[token_rows(t, m), :] for m in groups], axis=0)
            bui = jnp.concatenate([st_im[token_rows(t, m), :] for m in groups], axis=0)
            nr = ar * hr - ai * hi + bur
            ni = ar * hi + ai * hr + bui
            for m in groups:
                st_re[token_rows(t, m), :] = nr[m * SUBLANES:(m + 1) * SUBLANES]
                st_im[token_rows(t, m), :] = ni[m * SUBLANES:(m + 1) * SUBLANES]
            return nr, ni

        hr, hi = lax.fori_loop(0, tm, step, (c_re[...], c_im[...]), unroll=unroll)
        c_re[...] = hr
        c_im[...] = hi
        hre_out[0] = hr
        him_out[0] = hi

    ys = []
    for kb in range(n_blk):
        tiles = range(kb * tiles_per_blk, (kb + 1) * tiles_per_blk)
        h_re = jnp.concatenate([st_re[tile_rows(k), :] for k in tiles], axis=1)
        h_im = jnp.concatenate([st_im[tile_rows(k), :] for k in tiles], axis=1)
        yre = jnp.dot(h_re.astype(BF16), cre_ref[kb], preferred_element_type=F32)
        yim = jnp.dot(h_im.astype(BF16), cim_ref[kb], preferred_element_type=F32)
        ys.append(yre - yim)
    y = jnp.concatenate(ys, axis=1) + d_ref[...] * u
    y = jax.nn.gelu(y)
    gate = jnp.dot(y.astype(BF16), wglu_ref[...], preferred_element_type=F32) + bglu_ref[...]
    yb_ref[...] = (y * jax.nn.sigmoid(gate)).astype(BF16)


def _s5(proj, row_block0, n_rows, tm, seqs, u_col_block, width, consts, yb_all, h0=None):
    sequential = h0 is None
    bre = consts[0]
    n_blk, _, kst = bre.shape
    n_state = n_blk * kst
    n_tiles = n_state // LANES
    assert tm % SUBLANES == 0 and n_tiles % SUBLANES == 0
    pitch = tm
    per_seq = n_rows // seqs // tm
    u_map = lambda b, c: (row_block0 + b * per_seq + c, u_col_block)
    out_map = lambda b, c: (b * per_seq + c, 0)
    row_map = lambda b, c: (row_block0 + b * per_seq + c, 0)
    in_specs = [pl.BlockSpec((tm, width), u_map)]
    args = [proj]
    if not sequential:
        in_specs += [pl.BlockSpec((tm, n_state), out_map)] * 2
        args += list(h0)
    in_specs += [_const_spec(c.shape) for c in consts] + [pl.BlockSpec(memory_space=pl.ANY)]
    args += list(consts) + [yb_all]
    scratch = [pltpu.VMEM((n_tiles * pitch, LANES), F32), pltpu.VMEM((n_tiles * pitch, LANES), F32)]
    if sequential:
        scratch += [pltpu.VMEM((n_tiles, LANES), F32), pltpu.VMEM((n_tiles, LANES), F32)]
        st_spec = pl.BlockSpec((1, n_tiles, LANES), lambda b, c: (b, 0, 0))
        st_shape = jax.ShapeDtypeStruct((seqs, n_tiles, LANES), F32)
    else:
        st_spec = pl.BlockSpec((tm, n_state), out_map)
        st_shape = jax.ShapeDtypeStruct((n_rows, n_state), F32)
    return pl.pallas_call(
        functools.partial(_s5_kernel, sequential=sequential, n_blk=n_blk, pitch=pitch, unroll=2),
        grid=(seqs, per_seq),
        in_specs=in_specs,
        out_specs=[pl.BlockSpec((tm, width), row_map), st_spec, st_spec],
        out_shape=[jax.ShapeDtypeStruct(yb_all.shape, yb_all.dtype), st_shape, st_shape],
        input_output_aliases={len(args) - 1: 0},
        scratch_shapes=scratch,
        compiler_params=_cparams(("parallel", "arbitrary")),
        name="s5_seq" if sequential else "s5_step",
    )(*args)


def _route(logits, n_grp):
    lane = lax.broadcasted_iota(jnp.int32, logits.shape, 1).astype(F32)
    neg = jnp.float32(-1e30)
    big = jnp.float32(1e9)
    is_grp = lane < n_grp
    gl = jnp.where(is_grp, logits, neg)
    gmax = jnp.max(gl, axis=1, keepdims=True)
    gsum = jnp.sum(jnp.where(is_grp, jnp.exp(gl - gmax), 0.0), axis=1, keepdims=True)
    g_p = 1.0 / gsum
    g_idx = jnp.min(jnp.where(is_grp & (gl == gmax), lane, big), axis=1, keepdims=True)
    lo = n_grp + g_idx * EXPERTS_PER_GROUP
    in_grp = (lane >= lo) & (lane < lo + EXPERTS_PER_GROUP)
    el = jnp.where(in_grp, logits, neg)
    v1 = jnp.max(el, axis=1, keepdims=True)
    i1 = jnp.min(jnp.where(in_grp & (el == v1), lane, big), axis=1, keepdims=True)
    rest = in_grp & (lane != i1)
    el2 = jnp.where(rest, logits, neg)
    v2 = jnp.max(el2, axis=1, keepdims=True)
    i2 = jnp.min(jnp.where(rest & (el2 == v2), lane, big), axis=1, keepdims=True)
    e2 = jnp.exp(v2 - v1)
    w1 = g_p / (1.0 + e2)
    w2 = g_p * e2 / (1.0 + e2)
    return (jnp.where(lane == ROUTE_E1, i1 - n_grp, 0.0) + jnp.where(lane == ROUTE_E2, i2 - n_grp, 0.0)
            + jnp.where(lane == ROUTE_W1, w1, 0.0) + jnp.where(lane == ROUTE_W2, w2, 0.0))


def _merge_kernel(ya_ref, g_ref, yb_ref, ga_ref, gb_ref, x_ref, wa_ref, wb_ref, wo_ref,
                  gf_ref, wr_ref, br_ref, h_out, xn_out, comb_out, *, n_grp):
    ya = (ya_ref[...] * g_ref[...]).astype(BF16)
    ma = jnp.dot(ya, wa_ref[...], preferred_element_type=F32)
    mb = jnp.dot(yb_ref[...], wb_ref[...], preferred_element_type=F32)
    merged = jax.nn.sigmoid(ga_ref[...]) * ma + jax.nn.sigmoid(gb_ref[...]) * mb
    h = x_ref[...] + jnp.dot(merged.astype(BF16), wo_ref[...], preferred_element_type=F32)
    h_out[...] = h
    ms = jnp.mean(h * h, axis=-1, keepdims=True)
    xn = h * lax.rsqrt(ms + NORM_EPS) * gf_ref[...]
    xn_out[...] = xn
    x_hi = xn.astype(BF16)
    x_lo = (xn - x_hi.astype(F32)).astype(BF16)
    w_hi, w_lo = wr_ref[0], wr_ref[1]
    logits = (jnp.dot(x_hi, w_hi, preferred_element_type=F32)
              + (jnp.dot(x_hi, w_lo, preferred_element_type=F32)
                 + jnp.dot(x_lo, w_hi, preferred_element_type=F32))) + br_ref[...]
    comb_out[...] = _route(logits, n_grp)


def _merge(ya, g, yb, proj, x_all, w_br_a, w_br_b, w_out, g_ffn, w_router, b_router,
           tm, ga_col_block, n_grp):
    nt, d = x_all.shape
    wdt = ya.shape[1]
    row = lambda i: (i, 0)
    single = dict(pipeline_mode=pl.Buffered(1))
    in_specs = [pl.BlockSpec((tm, wdt), row), pl.BlockSpec((tm, wdt), row),
                pl.BlockSpec((tm, wdt), row),
                pl.BlockSpec((tm, d), lambda i: (i, ga_col_block)),
                pl.BlockSpec((tm, d), lambda i: (i, ga_col_block + 1)),
                pl.BlockSpec((tm, d), row),
                pl.BlockSpec(w_br_a.shape, lambda i: (0, 0), **single),
                pl.BlockSpec(w_br_b.shape, lambda i: (0, 0), **single),
                pl.BlockSpec(w_out.shape, lambda i: (0, 0), **single),
                _const_spec(g_ffn.shape), _const_spec(w_router.shape), _const_spec(b_router.shape)]
    return pl.pallas_call(
        functools.partial(_merge_kernel, n_grp=n_grp),
        grid=(nt // tm,),
        in_specs=in_specs,
        out_specs=[pl.BlockSpec((tm, d), row), pl.BlockSpec((tm, d), row),
                   pl.BlockSpec((tm, LANES), row)],
        out_shape=[jax.ShapeDtypeStruct((nt, d), F32), jax.ShapeDtypeStruct((nt, d), F32),
                   jax.ShapeDtypeStruct((nt, LANES), F32)],
        compiler_params=_cparams(("parallel",)),
        name="merge_route",
    )(ya, g, yb, proj, proj, x_all, w_br_a, w_br_b, w_out, g_ffn, w_router, b_router)


def _moe_kernel(te_ref, first_ref, nvalid_ref, nused_ref, src_cur, src_nxt, dst_cur,
                xn_hbm, wg_ref, wu_ref, wd_ref, y_hbm,
                xbuf, obuf, wg_bf, wu_bf, wd_bf, gsem, ssem, *, tm):
    i = pl.program_id(0)
    last = pl.num_programs(0) - 1
    n_used = nused_ref[0]
    slot = lax.rem(i, 2)

    def gather_start(src, s):
        def body(r, c):
            pltpu.make_async_copy(xn_hbm.at[pl.ds(src[0, 0, r], 1)], xbuf.at[s, pl.ds(r, 1)],
                                  gsem.at[s]).start()
            return c
        lax.fori_loop(0, tm, body, 0, unroll=8)

    def gather_wait(s):
        pltpu.make_async_copy(xn_hbm.at[pl.ds(0, tm)], xbuf.at[s], gsem.at[s]).wait()

    def scatter_start(s, n_rows):
        def body(r, c):
            pltpu.make_async_copy(obuf.at[s, pl.ds(r, 1)], y_hbm.at[pl.ds(dst_cur[0, 0, r], 1)],
                                  ssem.at[s]).start()
            return c
        lax.fori_loop(0, n_rows, body, 0)

    def scatter_wait(s, n_rows):
        n_full = pl.multiple_of(lax.shift_left(lax.shift_right_logical(n_rows, 3), 3), 8)

        @pl.when(n_full > 0)
        def _():
            pltpu.make_async_copy(obuf.at[s, pl.ds(0, n_full)], y_hbm.at[pl.ds(0, n_full)],
                                  ssem.at[s]).wait()

        def body(r, c):
            pltpu.make_async_copy(obuf.at[s, pl.ds(r, 1)], y_hbm.at[pl.ds(r, 1)],
                                  ssem.at[s]).wait()
            return c
        lax.fori_loop(n_full, n_rows, body, 0)

    @pl.when(i == 0)
    def _():
        gather_start(src_cur, 0)

    @pl.when(i < n_used)
    def _():
        gather_wait(slot)

    @pl.when(i + 1 < n_used)
    def _():
        gather_start(src_nxt, 1 - slot)

    @pl.when((i >= 2) & (i - 2 < n_used))
    def _():
        scatter_wait(slot, nvalid_ref[i - 2])

    @pl.when(i < n_used)
    def _():
        @pl.when(first_ref[i] == 1)
        def _():
            wg_bf[...] = wg_ref[0].astype(BF16)
            wu_bf[...] = wu_ref[0].astype(BF16)
            wd_bf[...] = wd_ref[0].astype(BF16)

        x = xbuf[slot].astype(BF16)
        xg = jnp.dot(x, wg_bf[...], preferred_element_type=F32)
        xu = jnp.dot(x, wu_bf[...], preferred_element_type=F32)
        hid = (jax.nn.silu(xg) * xu).astype(BF16)
        obuf[slot] = jnp.dot(hid, wd_bf[...], preferred_element_type=F32)
        scatter_start(slot, nvalid_ref[i])

    @pl.when(i == last)
    def _():
        @pl.when((i >= 1) & (i - 1 < n_used))
        def _():
            scatter_wait(1 - slot, nvalid_ref[i - 1])

        @pl.when(i < n_used)
        def _():
            scatter_wait(slot, nvalid_ref[i])


def _moe(xn, tile_expert, tile_first, tile_valid, n_used, src_rows, dst_rows, wg, wu, wd,
         n_out_rows, tm):
    nt, d = xn.shape
    n_exp, _, de = wg.shape
    n_tiles = src_rows.shape[0]
    smem_cur = pl.BlockSpec((1, 1, tm), lambda i, *_: (i, 0, 0), memory_space=pltpu.SMEM)
    smem_nxt = pl.BlockSpec((1, 1, tm), lambda i, *_: (jnp.minimum(i + 1, n_tiles - 1), 0, 0),
                            memory_space=pltpu.SMEM)
    grid_spec = pltpu.PrefetchScalarGridSpec(
        num_scalar_prefetch=4,
        grid=(n_tiles,),
        in_specs=[smem_cur, smem_nxt, smem_cur,
                  pl.BlockSpec(memory_space=pl.ANY),
                  pl.BlockSpec((1, d, de), lambda i, te, *_: (te[i], 0, 0)),
                  pl.BlockSpec((1, d, de), lambda i, te, *_: (te[i], 0, 0)),
                  pl.BlockSpec((1, de, d), lambda i, te, *_: (te[i], 0, 0))],
        out_specs=pl.BlockSpec(memory_space=pl.ANY),
        scratch_shapes=[pltpu.VMEM((2, tm, d), F32), pltpu.VMEM((2, tm, d), F32),
                        pltpu.VMEM((d, de), BF16), pltpu.VMEM((d, de), BF16),
                        pltpu.VMEM((de, d), BF16),
                        pltpu.SemaphoreType.DMA((2,)), pltpu.SemaphoreType.DMA((2,))])
    return pl.pallas_call(
        functools.partial(_moe_kernel, tm=tm),
        grid_spec=grid_spec,
        out_shape=jax.ShapeDtypeStruct((n_out_rows, d), F32),
        compiler_params=_cparams(("arbitrary",)),
        name="moe_grouped",
    )(tile_expert, tile_first, tile_valid, n_used, src_rows, src_rows, dst_rows, xn, wg, wu, wd)


def _moe_plan(route, n_exp, tm):
    nt = route.shape[0]
    n_pairs = 2 * nt
    n_tiles = n_pairs // tm + n_exp
    eid = jnp.concatenate([route[:, ROUTE_E1], route[:, ROUTE_E2]]).astype(jnp.int32)
    onehot = (eid[:, None] == jnp.arange(n_exp, dtype=jnp.int32)[None, :]).astype(jnp.int32)
    csum = jnp.cumsum(onehot, axis=0)
    rank = jnp.take_along_axis(csum, eid[:, None], axis=1)[:, 0] - 1
    cnt = csum[-1]
    tiles = (cnt + tm - 1) // tm
    tile_end = jnp.cumsum(tiles)
    n_used = tile_end[-1]
    pos = (tile_end - tiles)[eid] * tm + rank
    pair = jnp.arange(n_pairs, dtype=jnp.int32)
    dst_rows = jnp.zeros((n_tiles * tm,), jnp.int32).at[pos].set(pair)
    src_rows = dst_rows % nt
    tile_id = jnp.minimum(jnp.arange(n_tiles, dtype=jnp.int32), n_used - 1)
    tile_expert = jnp.sum((tile_end[None, :] <= tile_id[:, None]).astype(jnp.int32), axis=1)
    tile_first = jnp.concatenate(
        [jnp.ones((1,), jnp.int32), (tile_expert[1:] != tile_expert[:-1]).astype(jnp.int32)])
    tile_valid = jnp.clip(cnt[tile_expert] - (tile_id - (tile_end - tiles)[tile_expert]) * tm, 0, tm)
    return (tile_expert, tile_first, tile_valid.astype(jnp.int32),
            n_used.reshape(1).astype(jnp.int32),
            src_rows.reshape(n_tiles, 1, tm), dst_rows.reshape(n_tiles, 1, tm), n_pairs)


def _final_kernel(h_ref, y1_ref, y2_ref, route_ref, g_ref, o_ref):
    lane = lax.broadcasted_iota(jnp.int32, route_ref.shape, 1)
    route = route_ref[...]
    w1 = jnp.sum(jnp.where(lane == ROUTE_W1, route, 0.0), axis=1, keepdims=True)
    w2 = jnp.sum(jnp.where(lane == ROUTE_W2, route, 0.0), axis=1, keepdims=True)
    h = h_ref[...] + (w1 * y1_ref[...] + w2 * y2_ref[...])
    ms = jnp.mean(h * h, axis=-1, keepdims=True)
    o_ref[...] = h * lax.rsqrt(ms + NORM_EPS) * g_ref[...]


def _final(h, y_moe, route, g_final, row0, seg_stride, seg_rows, n_seg, tm):
    nt, d = h.shape
    per_seg = seg_rows // tm
    assert all(x % SUBLANES == 0 for x in (row0, seg_stride, tm, nt))
    off = lambda s, c: row0 + s * seg_stride + c * tm
    rows = lambda width, base: pl.BlockSpec((pl.Element(tm), pl.Element(width)),
                                            lambda s, c: (pl.multiple_of(base + off(s, c), SUBLANES), 0))
    return pl.pallas_call(
        _final_kernel,
        grid=(n_seg, per_seg),
        in_specs=[rows(d, 0), rows(d, 0), rows(d, nt), rows(LANES, 0),
                  pl.BlockSpec((1, d), lambda s, c: (0, 0))],
        out_specs=pl.BlockSpec((tm, d), lambda s, c: (s * per_seg + c, 0)),
        out_shape=jax.ShapeDtypeStruct((n_seg * seg_rows, d), F32),
        compiler_params=_cparams(("parallel", "parallel")),
        name="final_norm",
    )(h, y_moe, y_moe, route, g_final)


def _pad_cols(w, to):
    return jnp.pad(w, ((0, 0), (0, to - w.shape[1])))


def _pad_rows(w, to):
    return jnp.pad(w, ((0, to - w.shape[0]), (0, 0)))


def _s5_consts(lam_re, lam_im, log_dt, b_re, b_im, c_re, c_im, d, w_glu, b_glu):
    dt = jnp.exp(log_dt)[:, None]
    mag = jnp.exp(lam_re * dt)
    abar_re = mag * jnp.cos(lam_im * dt)
    abar_im = mag * jnp.sin(lam_im * dt)
    den = lam_re * lam_re + lam_im * lam_im
    nr = abar_re - 1.0
    coef_re = (nr * lam_re + abar_im * lam_im) / den
    coef_im = (abar_im * lam_re - nr * lam_im) / den
    bbar_re = coef_re[..., None] * b_re - coef_im[..., None] * b_im
    bbar_im = coef_re[..., None] * b_im + coef_im[..., None] * b_re
    n_g, n_p, n_c = b_re.shape
    gpb = LANES // n_c
    eye = jnp.eye(gpb, dtype=F32)

    def in_blk(bb):
        bb = bb.reshape(n_g // gpb, gpb, n_p, n_c)
        return jnp.einsum('kgpc,gh->kgchp', bb, eye).reshape(
            n_g // gpb, gpb * n_c, gpb * n_p).astype(BF16)

    def out_blk(cc):
        cc = cc.reshape(n_g // gpb, gpb, n_c, n_p)
        return jnp.einsum('kgcp,gh->khpgc', cc, eye).reshape(
            n_g // gpb, gpb * n_p, gpb * n_c).astype(BF16)

    return (in_blk(bbar_re), in_blk(bbar_im), out_blk(c_re), out_blk(c_im),
            d.reshape(1, -1), abar_re.reshape(-1, LANES), abar_im.reshape(-1, LANES),
            w_glu.astype(BF16), b_glu.reshape(1, -1))


def kernel(x_prompt, x_sample, state_shift, state_wkv, state_ssm_re, state_ssm_im, meta_tokens, g_mix, w_in, shift_mu, w0, w_decay_up, a0, w_aaa_up, w_gate_up, k_k, k_a, r_k, lnx_w, lnx_b, ssm_lam_re, ssm_lam_im, ssm_log_dt, ssm_b_re, ssm_b_im, ssm_c_re, ssm_c_im, ssm_d, w_glu, b_glu, w_br_a, w_br_b, w_out, g_ffn, w_router_grp, b_router_grp, w_router_exp, b_router_exp, w_exp_gate, w_exp_up, w_exp_down, g_final):
    depth = g_mix.shape[0]
    assert depth == 1, "single-layer trunk"
    bsz, seq, d = x_prompt.shape
    nb = x_sample.shape[0]
    assert x_sample.shape[1] == 1
    t_p = seq + N_META
    n_p = bsz * t_p
    width = k_k.shape[1]
    heads = width // HEAD
    n_dl, n_al, n_gl = w_decay_up.shape[1], w_aaa_up.shape[1], w_gate_up.shape[1]
    n_grp = w_router_grp.shape[2]
    n_exp = w_router_exp.shape[2]
    assert bsz * heads * 2 == LANES and (nb * heads) % LANES == 0

    s_row0 = -(-n_p // nb) * nb
    n_t = s_row0 + nb
    pieces = []
    for b in range(bsz):
        pieces += [meta_tokens, x_prompt[b]]
    x_all = jnp.concatenate(
        pieces + [jnp.zeros((s_row0 - n_p, d), F32), x_sample.reshape(nb, d)], axis=0)

    c_rkv = 3 * width
    c_xw, c_xa, c_xg = c_rkv, c_rkv + n_dl, c_rkv + n_dl + n_al
    c_u = c_xg + n_gl
    c_ga = c_u + width
    lo_width = 2 * LORA_PAD + n_gl

    def regroup(m):
        return jnp.concatenate(
            [m[:, :c_rkv], m[:, c_u:], _pad_cols(m[:, c_xw:c_xa], LORA_PAD),
             _pad_cols(m[:, c_xa:c_xg], LORA_PAD), m[:, c_xg:c_u]], axis=1)

    def ungroup(m):
        lo = c_rkv + width + 2 * d
        return jnp.concatenate(
            [m[:, :c_rkv], m[:, lo:lo + n_dl], m[:, lo + LORA_PAD:lo + LORA_PAD + n_al],
             m[:, lo + 2 * LORA_PAD:]], axis=1)

    w_in_p = regroup(w_in[0].astype(BF16))
    n_proj = w_in_p.shape[1]
    proj = _inproj(x_all, g_mix, w_in_p, tm=1408, tn=512)

    mu_all = regroup(_pad_cols(shift_mu, c_u + width + 2 * d))
    mu_rkv, mu_lo = mu_all[:, :c_rkv], mu_all[:, n_proj - lo_width:]
    st_all = regroup(_pad_cols(state_shift[0], c_u + width + 2 * d))
    prev_s = (st_all[:, :c_rkv], st_all[:, n_proj - lo_width:])
    lo_col_block = (n_proj - lo_width) // lo_width
    prep_w = (mu_rkv, mu_lo, w0, a0, k_k, k_a,
              _pad_rows(w_decay_up[0], LORA_PAD).astype(BF16),
              _pad_rows(w_aaa_up[0], LORA_PAD).astype(BF16), w_gate_up[0].astype(BF16))

    tm_seq = t_p // 6
    g_all = jnp.zeros((n_t, width), F32)
    rp, wp, kp, vp, kkp, ap, g_all = _rwkv_prep(proj, 0, n_p, tm_seq, bsz, width, lo_col_block,
                                                lo_width, *prep_w, g_all)
    rs, ws, ks, vs, kks, as_, g_all = _rwkv_prep(proj, s_row0 // nb, nb, nb, 1, width,
                                                 lo_col_block, lo_width, *prep_w, g_all,
                                                 prev=prev_s)

    s5c = _s5_consts(ssm_lam_re[0], ssm_lam_im[0], ssm_log_dt[0], ssm_b_re[0], ssm_b_im[0],
                     ssm_c_re[0], ssm_c_im[0], ssm_d[0], w_glu[0], b_glu)
    n_g, n_st = ssm_lam_re.shape[1], ssm_lam_re.shape[2]
    u_col_block = c_rkv // width
    yb_all = jnp.zeros((n_t, width), BF16)
    yb_all, re_p, im_p = _s5(proj, 0, n_p, tm_seq, bsz, u_col_block, width, s5c, yb_all)
    h0 = (state_ssm_re[0].reshape(nb, n_g * n_st), state_ssm_im[0].reshape(nb, n_g * n_st))
    yb_all, re_s, im_s = _s5(proj, s_row0 // nb, nb, nb, 1, u_col_block, width, s5c, yb_all,
                             h0=h0)

    half = HEAD // 2

    def to_lanes_p(z):
        return z.reshape(t_p, bsz * heads, HEAD).transpose(0, 2, 1)

    def to_lanes_pi(z):
        z = z.reshape(t_p, bsz * heads, 2, half).transpose(0, 3, 2, 1)
        return z.reshape(t_p, half, LANES)

    def to_lanes_s(z):
        return z.reshape(nb, heads, HEAD).transpose(2, 0, 1).reshape(1, HEAD, nb * heads)

    rk_l = jnp.tile(r_k[0].T, (1, LANES // heads))
    lw_p = jnp.broadcast_to(lnx_w[0].reshape(heads, 2, half).transpose(2, 1, 0)[:, :, None, :],
                            (half, 2, bsz, heads)).reshape(half, LANES)
    lb_p = jnp.broadcast_to(lnx_b[0].reshape(heads, 2, half).transpose(2, 1, 0)[:, :, None, :],
                            (half, 2, bsz, heads)).reshape(half, LANES)
    lw_s = jnp.tile(lnx_w[0].reshape(heads, HEAD).T, (1, LANES // heads))
    lb_s = jnp.tile(lnx_b[0].reshape(heads, HEAD).T, (1, LANES // heads))

    s0_p = jnp.zeros((HEAD, half, LANES), F32)
    y_p, sf_p = _wkv(to_lanes_p(rp), to_lanes_p(wp), to_lanes_p(kp), to_lanes_p(kkp),
                     to_lanes_p(ap), to_lanes_pi(vp), s0_p, rk_l, lw_p, lb_p,
                     tc=48, isplit=True, unroll=8)
    s0_s = state_wkv[0].transpose(3, 2, 0, 1).reshape(HEAD, HEAD, nb * heads)
    y_s, sf_s = _wkv(to_lanes_s(rs), to_lanes_s(ws), to_lanes_s(ks), to_lanes_s(kks),
                     to_lanes_s(as_), to_lanes_s(vs), s0_s, rk_l, lw_s, lb_s,
                     tc=1, isplit=False, unroll=1)

    ya_p = y_p.reshape(t_p, half, 2, bsz, heads).transpose(3, 0, 4, 2, 1).reshape(n_p, width)
    ya_s = y_s.reshape(HEAD, nb, heads).transpose(1, 2, 0).reshape(nb, width)
    wkv_p = sf_p.reshape(HEAD, half, 2, bsz, heads).transpose(3, 4, 2, 1, 0).reshape(
        1, bsz, heads, HEAD, HEAD)
    wkv_s = sf_s.reshape(HEAD, HEAD, nb, heads).transpose(2, 3, 1, 0)[None]

    def unify(a, b):
        return jnp.concatenate([a, jnp.zeros((s_row0 - n_p, a.shape[1]), a.dtype), b], axis=0)

    w_router = _pad_cols(jnp.concatenate([w_router_grp[0], w_router_exp[0]], axis=1), LANES)
    w_router_hi = w_router.astype(BF16)
    w_router = jnp.stack([w_router_hi, (w_router - w_router_hi.astype(F32)).astype(BF16)])
    b_router = _pad_cols(jnp.concatenate([b_router_grp, b_router_exp], axis=1), LANES)
    assert n_grp + n_exp <= LANES and n_exp == n_grp * EXPERTS_PER_GROUP
    h1, xn2, route = _merge(unify(ya_p, ya_s), g_all, yb_all, proj, x_all,
                           w_br_a[0].astype(BF16), w_br_b[0].astype(BF16), w_out[0].astype(BF16),
                           g_ffn, w_router, b_router, tm=256, ga_col_block=(c_rkv + width) // d,
                           n_grp=n_grp)
    tm_moe = 256
    te, tf, tv, nu, src_rows, dst_rows, n_moe_rows = _moe_plan(route, n_exp, tm_moe)
    y_moe = _moe(xn2, te, tf, tv, nu, src_rows, dst_rows, w_exp_gate[0], w_exp_up[0], w_exp_down[0],
                 n_moe_rows, tm_moe)
    gfin = g_final.reshape(1, d)
    y_prompt = _final(h1, y_moe, route, gfin, N_META, t_p, seq, bsz, tm=512).reshape(bsz, seq, d)
    y_sample = _final(h1, y_moe, route, gfin, s_row0, 0, nb, 1, tm=nb).reshape(nb, 1, d)
    last_p = jnp.concatenate([proj[(b + 1) * t_p - 1:(b + 1) * t_p] for b in range(bsz)], axis=0)
    shift_p = ungroup(last_p)[None]
    shift_s = ungroup(proj[s_row0:])[None]
    return (y_prompt, y_sample, shift_p, wkv_p,
            re_p.reshape(1, bsz, n_g, n_st), im_p.reshape(1, bsz, n_g, n_st),
            shift_s, wkv_s,
            re_s.reshape(1, nb, n_g, n_st), im_s.reshape(1, nb, n_g, n_st))
```

```python
import functools

import jax
import jax.numpy as jnp
from jax import lax
from jax.experimental import pallas as pl
from jax.experimental.pallas import tpu as pltpu

F32 = jnp.float32
BF16 = jnp.bfloat16

NORM_EPS = 1e-6
LNX_EPS = 64e-5
N_META = 16
HEAD = 64
SSM_GROUP = 16
SSM_STATE = 64
EXPERTS_PER_GROUP = 8
LANES = 128
SUBLANES = 8
LORA_PAD = 128
ROUTE_E1, ROUTE_E2, ROUTE_W1, ROUTE_W2 = 0, 1, 2, 3
VMEM_LIMIT = 56 * 1024 * 1024


def _cparams(sem):
    return pltpu.CompilerParams(dimension_semantics=sem, vmem_limit_bytes=VMEM_LIMIT)


def _const_spec(shape):
    nd = len(shape)
    return pl.BlockSpec(shape, lambda *_: (0,) * nd)


def _inproj_kernel(x_ref, g_ref, w_ref, o_ref, xn_ref):
    @pl.when(pl.program_id(1) == 0)
    def _():
        x = x_ref[...]
        ms = jnp.mean(x * x, axis=-1, keepdims=True)
        xn_ref[...] = (x * lax.rsqrt(ms + NORM_EPS) * g_ref[...]).astype(BF16)

    o_ref[...] = jnp.dot(xn_ref[...], w_ref[...], preferred_element_type=F32)


def _inproj(x_all, g_mix, w_in_p, tm, tn):
    nt, d = x_all.shape
    n_out = w_in_p.shape[1]
    return pl.pallas_call(
        _inproj_kernel,
        grid=(nt // tm, n_out // tn),
        in_specs=[pl.BlockSpec((tm, d), lambda i, j: (i, 0)),
                  pl.BlockSpec((1, d), lambda i, j: (0, 0)),
                  pl.BlockSpec((d, tn), lambda i, j: (0, j))],
        out_specs=pl.BlockSpec((tm, tn), lambda i, j: (i, j)),
        out_shape=jax.ShapeDtypeStruct((nt, n_out), F32),
        scratch_shapes=[pltpu.VMEM((tm, d), BF16)],
        compiler_params=_cparams(("parallel", "arbitrary")),
        name="inproj",
    )(x_all, g_mix, w_in_p)


def _softplus(z):
    return jnp.maximum(z, 0.0) + jnp.log1p(jnp.exp(-jnp.abs(z)))


def _rwkv_prep_kernel(*refs, width, carry_prev):
    if carry_prev:
        (rkv_ref, lo_ref, mu_rkv_ref, mu_lo_ref, w0_ref, a0_ref,
         wd_ref, wa_ref, wg_ref, _g_all,
         r_out, w_out, k_out, v_out, a_out, g_out, c_rkv, c_lo) = refs

        @pl.when(pl.program_id(1) == 0)
        def _():
            c_rkv[...] = jnp.zeros_like(c_rkv)
            c_lo[...] = jnp.zeros_like(c_lo)
    else:
        (rkv_ref, lo_ref, prev_rkv_ref, prev_lo_ref, mu_rkv_ref, mu_lo_ref, w0_ref, a0_ref,
         wd_ref, wa_ref, wg_ref, _g_all,
         r_out, w_out, k_out, v_out, a_out, g_out) = refs

    tm = rkv_ref.shape[0]
    first_row = lax.broadcasted_iota(jnp.int32, (tm, 1), 0) == 0

    def shifted(p, prev_ref, carry_ref, cols):
        if carry_prev:
            prev = jnp.where(first_row, carry_ref[:, cols], pltpu.roll(p, 1, 0))
        else:
            prev = prev_ref[:, cols]
        return prev

    def lerp(p, prev, mu):
        return p + (prev - p) * mu

    lo_cols = slice(0, lo_ref.shape[1])
    p_lo = lo_ref[...]
    q_lo = lerp(p_lo, shifted(p_lo, None if carry_prev else prev_lo_ref,
                              c_lo if carry_prev else None, lo_cols), mu_lo_ref[...])
    xw = q_lo[:, 0:LORA_PAD]
    xa = q_lo[:, LORA_PAD:2 * LORA_PAD]
    xg = q_lo[:, 2 * LORA_PAD:]
    dw = jnp.dot(jnp.tanh(xw).astype(BF16), wd_ref[...], preferred_element_type=F32)
    wlog = -_softplus(-(w0_ref[...] + dw)) - 0.5
    w_out[...] = jnp.exp(-jnp.exp(wlog))
    a = jax.nn.sigmoid(a0_ref[...] + jnp.dot(xa.astype(BF16), wa_ref[...],
                                             preferred_element_type=F32))
    a_out[...] = a
    g_out[...] = jnp.dot(jax.nn.sigmoid(xg).astype(BF16), wg_ref[...],
                         preferred_element_type=F32)

    def q_of(idx):
        cols = slice(idx * width, (idx + 1) * width)
        p = rkv_ref[:, cols]
        prev = shifted(p, None if carry_prev else prev_rkv_ref,
                       c_rkv if carry_prev else None, cols)
        return lerp(p, prev, mu_rkv_ref[:, cols])

    r_out[...] = q_of(0)
    k_out[...] = q_of(1)
    v_out[...] = q_of(2)

    if carry_prev:
        c_rkv[...] = rkv_ref[tm - 1:tm, :]
        c_lo[...] = lo_ref[tm - 1:tm, :]


def _rwkv_prep(proj, row_block0, n_rows, tm, seqs, width, lo_col_block, lo_width,
               mu_rkv, mu_lo, w0, a0, wd, wa, wg, g_all, prev=None):
    carry_prev = prev is None
    per_seq = n_rows // seqs // tm
    row_map = lambda b, c: (row_block0 + b * per_seq + c, 0)
    lo_map = lambda b, c: (row_block0 + b * per_seq + c, lo_col_block)
    out_map = lambda b, c: (b * per_seq + c, 0)
    in_specs = [pl.BlockSpec((tm, 3 * width), row_map), pl.BlockSpec((tm, lo_width), lo_map)]
    args = [proj, proj]
    if not carry_prev:
        in_specs += [pl.BlockSpec((tm, 3 * width), out_map), pl.BlockSpec((tm, lo_width), out_map)]
        args += list(prev)
    consts = [mu_rkv, mu_lo, w0, a0, wd, wa, wg]
    in_specs += [_const_spec(c.shape) for c in consts] + [pl.BlockSpec(memory_space=pl.ANY)]
    args += consts + [g_all]
    scratch = []
    if carry_prev:
        scratch = [pltpu.VMEM((1, 3 * width), F32), pltpu.VMEM((1, lo_width), F32)]
    out_sd = jax.ShapeDtypeStruct((n_rows // seqs, seqs * width), F32)
    tmaj_map = lambda b, c: (c, b)
    return pl.pallas_call(
        functools.partial(_rwkv_prep_kernel, width=width, carry_prev=carry_prev),
        grid=(seqs, per_seq),
        in_specs=in_specs,
        out_specs=[pl.BlockSpec((tm, width), tmaj_map)] * 5 + [pl.BlockSpec((tm, width), row_map)],
        out_shape=[out_sd] * 5 + [jax.ShapeDtypeStruct(g_all.shape, g_all.dtype)],
        input_output_aliases={len(args) - 1: 5},
        scratch_shapes=scratch,
        compiler_params=_cparams(("parallel", "arbitrary")),
        name="rwkv_prep_seq" if carry_prev else "rwkv_prep_step",
    )(*args)


def _wkv_kernel(r_ref, w_ref, kraw_ref, al_ref, v_ref, s0_ref, kkc_ref, kac_ref, rk_ref, lw_ref,
                lb_ref, y_ref, sf_ref, s_scr, a_scr, b_scr, k_src, *dup_scr,
                ni, nj, tc, isplit, unroll):
    @pl.when(pl.program_id(1) == 0)
    def _():
        s_scr[...] = s0_ref[...]

    def dup(x):
        return jnp.concatenate([x, x], axis=-1) if isplit else x

    if isplit:
        w_src, r_src, v_src = dup_scr
    else:
        w_src, r_src, v_src = w_ref, r_ref, v_ref

    def prep(t, carry):
        kraw = dup(kraw_ref[t])
        al = dup(al_ref[t])
        kk = kraw * kkc_ref[...]
        k_src[t] = kraw * (1.0 + (al - 1.0) * kac_ref[...])
        ss = jnp.sum(kk * kk, axis=0, keepdims=True)
        kkn = kk / jnp.maximum(jnp.sqrt(ss), 1e-12)
        a_scr[t] = -kkn
        b_scr[t] = kkn * al
        if isplit:
            w_src[t] = dup(w_ref[t])
            r_src[t] = dup(r_ref[t])
            v = v_ref[t]
            v_src[t] = jnp.concatenate([v[:ni], v[ni:]], axis=-1)
        return carry

    lax.fori_loop(0, tc, prep, 0, unroll=unroll)

    def row(ref, t, j):
        return ref[t, j:j + 1, :]

    def tree(parts):
        while len(parts) > 1:
            parts = [parts[i] + parts[i + 1] for i in range(0, len(parts), 2)]
        return parts[0]

    n_acc = 4
    sa0 = tree([sum(s_scr[j] * row(a_scr, 0, j) for j in range(q, nj, n_acc))
                for q in range(n_acc)])

    def step(t, sa):
        v = v_src[t]
        tn = jnp.minimum(t + 1, tc - 1)
        y = [None] * n_acc
        san = [None] * n_acc
        for j in range(nj):
            s = s_scr[j] * row(w_src, t, j) + sa * row(b_scr, t, j) + v * row(k_src, t, j)
            s_scr[j] = s
            yj = s * row(r_src, t, j)
            sj = s * row(a_scr, tn, j)
            q = j % n_acc
            y[q] = yj if y[q] is None else y[q] + yj
            san[q] = sj if san[q] is None else san[q] + sj
        y_ref[t] = tree(y)
        return tree(san)

    lax.fori_loop(0, tc, step, sa0)

    def isum(x):
        s = jnp.broadcast_to(jnp.sum(x, axis=0, keepdims=True), (8, LANES))
        if isplit:
            s = s + pltpu.roll(s, LANES // 2, 1)
        return s[0:1]

    def post(t, carry):
        y = y_ref[t]
        v = v_src[t]
        mu = isum(y) * (1.0 / HEAD)
        d = y - mu
        var = isum(d * d) * (1.0 / HEAD)
        yn = d * lax.rsqrt(var + LNX_EPS) * lw_ref[...] + lb_ref[...]
        bonus = jnp.sum(r_src[t] * k_src[t] * rk_ref[...], axis=0, keepdims=True)
        y_ref[t] = yn + bonus * v
        return carry

    lax.fori_loop(0, tc, post, 0, unroll=unroll)

    @pl.when(pl.program_id(1) == pl.num_programs(1) - 1)
    def _():
        sf_ref[...] = s_scr[...]


def _wkv(r, w, kraw, al, v, s0, kkc, kac, rk, lw, lb, tc, isplit, unroll):
    t, nj, jl = r.shape
    ni, lanes = s0.shape[1:]
    assert jl == (LANES // 2 if isplit else lanes) and v.shape == r.shape
    jspec = pl.BlockSpec((tc, nj, min(jl, LANES)), lambda l, c: (c, 0, l))
    dup_scr = []
    if isplit:
        dup_scr = [pltpu.VMEM((tc, nj, LANES), F32)] * 2 + [pltpu.VMEM((tc, ni, LANES), F32)]
    ispec = pl.BlockSpec((tc, ni, LANES), lambda l, c: (c, 0, l))
    sspec = pl.BlockSpec((nj, ni, LANES), lambda l, c: (0, 0, l))
    consts = [kkc, kac, rk, lw, lb]
    return pl.pallas_call(
        functools.partial(_wkv_kernel, ni=ni, nj=nj, tc=tc, isplit=isplit, unroll=unroll),
        grid=(lanes // LANES, t // tc),
        in_specs=[jspec, jspec, jspec, jspec, jspec, sspec] + [_const_spec(c.shape) for c in consts],
        out_specs=[ispec, sspec],
        out_shape=[jax.ShapeDtypeStruct((t, ni, lanes), F32),
                   jax.ShapeDtypeStruct((nj, ni, lanes), F32)],
        scratch_shapes=[pltpu.VMEM((nj, ni, LANES), F32),
                        pltpu.VMEM((tc, nj, LANES), F32),
                        pltpu.VMEM((tc, nj, LANES), F32),
                        pltpu.VMEM((tc, nj, LANES), F32)] + dup_scr,
        compiler_params=_cparams(("parallel", "arbitrary")),
        name="wkv_seq" if isplit else "wkv_step",
    )(r, w, kraw, al, v, s0, *consts)


def _s5_kernel(*refs, sequential, n_blk, pitch, unroll):
    if sequential:
        (u_ref, bre_ref, bim_ref, cre_ref, cim_ref, d_ref, are_ref, aim_ref, wglu_ref, bglu_ref,
         _yb_all, yb_ref, hre_out, him_out, st_re, st_im, c_re, c_im) = refs
    else:
        (u_ref, h0re_ref, h0im_ref, bre_ref, bim_ref, cre_ref, cim_ref, d_ref, are_ref, aim_ref,
         wglu_ref, bglu_ref, _yb_all, yb_ref, hre_out, him_out, st_re, st_im) = refs

    tm = u_ref.shape[0]
    kin = bre_ref.shape[1]
    kst = bre_ref.shape[2]
    tiles_per_blk = kst // LANES
    n_tiles = n_blk * tiles_per_blk
    u = u_ref[...]
    ub = u.astype(BF16)

    def tile_rows(k):
        return slice(k * pitch, k * pitch + tm)

    for kb in range(n_blk):
        ukb = ub[:, kb * kin:(kb + 1) * kin]
        bu_re = jnp.dot(ukb, bre_ref[kb], preferred_element_type=F32)
        bu_im = jnp.dot(ukb, bim_ref[kb], preferred_element_type=F32)
        for n in range(tiles_per_blk):
            k = kb * tiles_per_blk + n
            cols = slice(n * LANES, (n + 1) * LANES)
            if sequential:
                st_re[tile_rows(k), :] = bu_re[:, cols]
                st_im[tile_rows(k), :] = bu_im[:, cols]
            else:
                kc = slice(k * LANES, (k + 1) * LANES)
                ar, ai = are_ref[k:k + 1, :], aim_ref[k:k + 1, :]
                h0r, h0i = h0re_ref[:, kc], h0im_ref[:, kc]
                nr = bu_re[:, cols] + (ar * h0r - ai * h0i)
                ni = bu_im[:, cols] + (ar * h0i + ai * h0r)
                st_re[tile_rows(k), :] = nr
                st_im[tile_rows(k), :] = ni
                hre_out[:, kc] = nr
                him_out[:, kc] = ni

    if sequential:
        @pl.when(pl.program_id(1) == 0)
        def _():
            c_re[...] = jnp.zeros_like(c_re)
            c_im[...] = jnp.zeros_like(c_im)

        ar = are_ref[...]
        ai = aim_ref[...]
        groups = range(n_tiles // SUBLANES)

        def token_rows(t, m):
            return pl.ds(t + m * SUBLANES * pitch, SUBLANES, stride=pitch)

        def step(t, h):
            hr, hi = h
            bur = jnp.concatenate([st_re[token_rows(t, m), :] for m in groups], axis=0)
            bui = jnp.concatenate([st_im[token_rows(t, m), :] for m in groups], axis=0)
            nr = ar * hr - ai * hi + bur
            ni = ar * hi + ai * hr + bui
            for m in groups:
                st_re[token_rows(t, m), :] = nr[m * SUBLANES:(m + 1) * SUBLANES]
                st_im[token_rows(t, m), :] = ni[m * SUBLANES:(m + 1) * SUBLANES]
            return nr, ni

        hr, hi = lax.fori_loop(0, tm, step, (c_re[...], c_im[...]), unroll=unroll)
        c_re[...] = hr
        c_im[...] = hi
        hre_out[0] = hr
        him_out[0] = hi

    ys = []
    for kb in range(n_blk):
        tiles = range(kb * tiles_per_blk, (kb + 1) * tiles_per_blk)
        h_re = jnp.concatenate([st_re[tile_rows(k), :] for k in tiles], axis=1)
        h_im = jnp.concatenate([st_im[tile_rows(k), :] for k in tiles], axis=1)
        yre = jnp.dot(h_re.astype(BF16), cre_ref[kb], preferred_element_type=F32)
        yim = jnp.dot(h_im.astype(BF16), cim_ref[kb], preferred_element_type=F32)
        ys.append(yre - yim)
    y = jnp.concatenate(ys, axis=1) + d_ref[...] * u
    y = jax.nn.gelu(y)
    gate = jnp.dot(y.astype(BF16), wglu_ref[...], preferred_element_type=F32) + bglu_ref[...]
    yb_ref[...] = (y * jax.nn.sigmoid(gate)).astype(BF16)


def _s5(proj, row_block0, n_rows, tm, seqs, u_col_block, width, consts, yb_all, h0=None):
    sequential = h0 is None
    bre = consts[0]
    n_blk, _, kst = bre.shape
    n_state = n_blk * kst
    n_tiles = n_state // LANES
    assert tm % SUBLANES == 0 and n_tiles % SUBLANES == 0
    pitch = tm
    per_seq = n_rows // seqs // tm
    u_map = lambda b, c: (row_block0 + b * per_seq + c, u_col_block)
    out_map = lambda b, c: (b * per_seq + c, 0)
    row_map = lambda b, c: (row_block0 + b * per_seq + c, 0)
    in_specs = [pl.BlockSpec((tm, width), u_map)]
    args = [proj]
    if not sequential:
        in_specs += [pl.BlockSpec((tm, n_state), out_map)] * 2
        args += list(h0)
    in_specs += [_const_spec(c.shape) for c in consts] + [pl.BlockSpec(memory_space=pl.ANY)]
    args += list(consts) + [yb_all]
    scratch = [pltpu.VMEM((n_tiles * pitch, LANES), F32), pltpu.VMEM((n_tiles * pitch, LANES), F32)]
    if sequential:
        scratch += [pltpu.VMEM((n_tiles, LANES), F32), pltpu.VMEM((n_tiles, LANES), F32)]
        st_spec = pl.BlockSpec((1, n_tiles, LANES), lambda b, c: (b, 0, 0))
        st_shape = jax.ShapeDtypeStruct((seqs, n_tiles, LANES), F32)
    else:
        st_spec = pl.BlockSpec((tm, n_state), out_map)
        st_shape = jax.ShapeDtypeStruct((n_rows, n_state), F32)
    return pl.pallas_call(
        functools.partial(_s5_kernel, sequential=sequential, n_blk=n_blk, pitch=pitch, unroll=2),
        grid=(seqs, per_seq),
        in_specs=in_specs,
        out_specs=[pl.BlockSpec((tm, width), row_map), st_spec, st_spec],
        out_shape=[jax.ShapeDtypeStruct(yb_all.shape, yb_all.dtype), st_shape, st_shape],
        input_output_aliases={len(args) - 1: 0},
        scratch_shapes=scratch,
        compiler_params=_cparams(("parallel", "arbitrary")),
        name="s5_seq" if sequential else "s5_step",
    )(*args)


def _route(logits, n_grp):
    lane = lax.broadcasted_iota(jnp.int32, logits.shape, 1).astype(F32)
    neg = jnp.float32(-1e30)
    big = jnp.float32(1e9)
    is_grp = lane < n_grp
    gl = jnp.where(is_grp, logits, neg)
    gmax = jnp.max(gl, axis=1, keepdims=True)
    gsum = jnp.sum(jnp.where(is_grp, jnp.exp(gl - gmax), 0.0), axis=1, keepdims=True)
    g_p = 1.0 / gsum
    g_idx = jnp.min(jnp.where(is_grp & (gl == gmax), lane, big), axis=1, keepdims=True)
    lo = n_grp + g_idx * EXPERTS_PER_GROUP
    in_grp = (lane >= lo) & (lane < lo + EXPERTS_PER_GROUP)
    el = jnp.where(in_grp, logits, neg)
    v1 = jnp.max(el, axis=1, keepdims=True)
    i1 = jnp.min(jnp.where(in_grp & (el == v1), lane, big), axis=1, keepdims=True)
    rest = in_grp & (lane != i1)
    el2 = jnp.where(rest, logits, neg)
    v2 = jnp.max(el2, axis=1, keepdims=True)
    i2 = jnp.min(jnp.where(rest & (el2 == v2), lane, big), axis=1, keepdims=True)
    e2 = jnp.exp(v2 - v1)
    w1 = g_p / (1.0 + e2)
    w2 = g_p * e2 / (1.0 + e2)
    return (jnp.where(lane == ROUTE_E1, i1 - n_grp, 0.0) + jnp.where(lane == ROUTE_E2, i2 - n_grp, 0.0)
            + jnp.where(lane == ROUTE_W1, w1, 0.0) + jnp.where(lane == ROUTE_W2, w2, 0.0))


def _merge_kernel(ya_ref, g_ref, yb_ref, ga_ref, gb_ref, x_ref, wa_ref, wb_ref, wo_ref,
                  gf_ref, wr_ref, br_ref, h_out, xn_out, comb_out, *, n_grp):
    ya = (ya_ref[...] * g_ref[...]).astype(BF16)
    ma = jnp.dot(ya, wa_ref[...], preferred_element_type=F32)
    mb = jnp.dot(yb_ref[...], wb_ref[...], preferred_element_type=F32)
    merged = jax.nn.sigmoid(ga_ref[...]) * ma + jax.nn.sigmoid(gb_ref[...]) * mb
    h = x_ref[...] + jnp.dot(merged.astype(BF16), wo_ref[...], preferred_element_type=F32)
    h_out[...] = h
    ms = jnp.mean(h * h, axis=-1, keepdims=True)
    xn = h * lax.rsqrt(ms + NORM_EPS) * gf_ref[...]
    xn_out[...] = xn
    x_hi = xn.astype(BF16)
    x_lo = (xn - x_hi.astype(F32)).astype(BF16)
    w_hi, w_lo = wr_ref[0], wr_ref[1]
    logits = (jnp.dot(x_hi, w_hi, preferred_element_type=F32)
              + (jnp.dot(x_hi, w_lo, preferred_element_type=F32)
                 + jnp.dot(x_lo, w_hi, preferred_element_type=F32))) + br_ref[...]
    comb_out[...] = _route(logits, n_grp)


def _merge(ya, g, yb, proj, x_all, w_br_a, w_br_b, w_out, g_ffn, w_router, b_router,
           tm, ga_col_block, n_grp):
    nt, d = x_all.shape
    wdt = ya.shape[1]
    row = lambda i: (i, 0)
    single = dict(pipeline_mode=pl.Buffered(1))
    in_specs = [pl.BlockSpec((tm, wdt), row), pl.BlockSpec((tm, wdt), row),
                pl.BlockSpec((tm, wdt), row),
                pl.BlockSpec((tm, d), lambda i: (i, ga_col_block)),
                pl.BlockSpec((tm, d), lambda i: (i, ga_col_block + 1)),
                pl.BlockSpec((tm, d), row),
                pl.BlockSpec(w_br_a.shape, lambda i: (0, 0), **single),
                pl.BlockSpec(w_br_b.shape, lambda i: (0, 0), **single),
                pl.BlockSpec(w_out.shape, lambda i: (0, 0), **single),
                _const_spec(g_ffn.shape), _const_spec(w_router.shape), _const_spec(b_router.shape)]
    return pl.pallas_call(
        functools.partial(_merge_kernel, n_grp=n_grp),
        grid=(nt // tm,),
        in_specs=in_specs,
        out_specs=[pl.BlockSpec((tm, d), row), pl.BlockSpec((tm, d), row),
                   pl.BlockSpec((tm, LANES), row)],
        out_shape=[jax.ShapeDtypeStruct((nt, d), F32), jax.ShapeDtypeStruct((nt, d), F32),
                   jax.ShapeDtypeStruct((nt, LANES), F32)],
        compiler_params=_cparams(("parallel",)),
        name="merge_route",
    )(ya, g, yb, proj, proj, x_all, w_br_a, w_br_b, w_out, g_ffn, w_router, b_router)


def _moe_kernel(te_ref, first_ref, nvalid_ref, nused_ref, src_cur, src_nxt, dst_cur,
                xn_hbm, wg_ref, wu_ref, wd_ref, y_hbm,
                xbuf, obuf, wg_bf, wu_bf, wd_bf, gsem, ssem, *, tm):
    i = pl.program_id(0)
    last = pl.num_programs(0) - 1
    n_used = nused_ref[0]
    slot = lax.rem(i, 2)

    def gather_start(src, s):
        def body(r, c):
            pltpu.make_async_copy(xn_hbm.at[pl.ds(src[0, 0, r], 1)], xbuf.at[s, pl.ds(r, 1)],
                                  gsem.at[s]).start()
            return c
        lax.fori_loop(0, tm, body, 0, unroll=8)

    def gather_wait(s):
        pltpu.make_async_copy(xn_hbm.at[pl.ds(0, tm)], xbuf.at[s], gsem.at[s]).wait()

    def scatter_start(s, n_rows):
        def body(r, c):
            pltpu.make_async_copy(obuf.at[s, pl.ds(r, 1)], y_hbm.at[pl.ds(dst_cur[0, 0, r], 1)],
                                  ssem.at[s]).start()
            return c
        lax.fori_loop(0, n_rows, body, 0)

    def scatter_wait(s, n_rows):
        n_full = pl.multiple_of(lax.shift_left(lax.shift_right_logical(n_rows, 3), 3), 8)

        @pl.when(n_full > 0)
        def _():
            pltpu.make_async_copy(obuf.at[s, pl.ds(0, n_full)], y_hbm.at[pl.ds(0, n_full)],
                                  ssem.at[s]).wait()

        def body(r, c):
            pltpu.make_async_copy(obuf.at[s, pl.ds(r, 1)], y_hbm.at[pl.ds(r, 1)],
                                  ssem.at[s]).wait()
            return c
        lax.fori_loop(n_full, n_rows, body, 0)

    @pl.when(i == 0)
    def _():
        gather_start(src_cur, 0)

    @pl.when(i < n_used)
    def _():
        gather_wait(slot)

    @pl.when(i + 1 < n_used)
    def _():
        gather_start(src_nxt, 1 - slot)

    @pl.when((i >= 2) & (i - 2 < n_used))
    def _():
        scatter_wait(slot, nvalid_ref[i - 2])

    @pl.when(i < n_used)
    def _():
        @pl.when(first_ref[i] == 1)
        def _():
            wg_bf[...] = wg_ref[0].astype(BF16)
            wu_bf[...] = wu_ref[0].astype(BF16)
            wd_bf[...] = wd_ref[0].astype(BF16)

        x = xbuf[slot].astype(BF16)
        xg = jnp.dot(x, wg_bf[...], preferred_element_type=F32)
        xu = jnp.dot(x, wu_bf[...], preferred_element_type=F32)
        hid = (jax.nn.silu(xg) * xu).astype(BF16)
        obuf[slot] = jnp.dot(hid, wd_bf[...], preferred_element_type=F32)
        scatter_start(slot, nvalid_ref[i])

    @pl.when(i == last)
    def _():
        @pl.when((i >= 1) & (i - 1 < n_used))
        def _():
            scatter_wait(1 - slot, nvalid_ref[i - 1])

        @pl.when(i < n_used)
        def _():
            scatter_wait(slot, nvalid_ref[i])


def _moe(xn, tile_expert, tile_first, tile_valid, n_used, src_rows, dst_rows, wg, wu, wd,
         n_out_rows, tm):
    nt, d = xn.shape
    n_exp, _, de = wg.shape
    n_tiles = src_rows.shape[0]
    smem_cur = pl.BlockSpec((1, 1, tm), lambda i, *_: (i, 0, 0), memory_space=pltpu.SMEM)
    smem_nxt = pl.BlockSpec((1, 1, tm), lambda i, *_: (jnp.minimum(i + 1, n_tiles - 1), 0, 0),
                            memory_space=pltpu.SMEM)
    grid_spec = pltpu.PrefetchScalarGridSpec(
        num_scalar_prefetch=4,
        grid=(n_tiles,),
        in_specs=[smem_cur, smem_nxt, smem_cur,
                  pl.BlockSpec(memory_space=pl.ANY),
                  pl.BlockSpec((1, d, de), lambda i, te, *_: (te[i], 0, 0)),
                  pl.BlockSpec((1, d, de), lambda i, te, *_: (te[i], 0, 0)),
                  pl.BlockSpec((1, de, d), lambda i, te, *_: (te[i], 0, 0))],
        out_specs=pl.BlockSpec(memory_space=pl.ANY),
        scratch_shapes=[pltpu.VMEM((2, tm, d), F32), pltpu.VMEM((2, tm, d), F32),
                        pltpu.VMEM((d, de), BF16), pltpu.VMEM((d, de), BF16),
                        pltpu.VMEM((de, d), BF16),
                        pltpu.SemaphoreType.DMA((2,)), pltpu.SemaphoreType.DMA((2,))])
    return pl.pallas_call(
        functools.partial(_moe_kernel, tm=tm),
        grid_spec=grid_spec,
        out_shape=jax.ShapeDtypeStruct((n_out_rows, d), F32),
        compiler_params=_cparams(("arbitrary",)),
        name="moe_grouped",
    )(tile_expert, tile_first, tile_valid, n_used, src_rows, src_rows, dst_rows, xn, wg, wu, wd)


def _moe_plan(route, n_exp, tm):
    nt = route.shape[0]
    n_pairs = 2 * nt
    n_tiles = n_pairs // tm + n_exp
    eid = jnp.concatenate([route[:, ROUTE_E1], route[:, ROUTE_E2]]).astype(jnp.int32)
    onehot = (eid[:, None] == jnp.arange(n_exp, dtype=jnp.int32)[None, :]).astype(jnp.int32)
    csum = jnp.cumsum(onehot, axis=0)
    rank = jnp.take_along_axis(csum, eid[:, None], axis=1)[:, 0] - 1
    cnt = csum[-1]
    tiles = (cnt + tm - 1) // tm
    tile_end = jnp.cumsum(tiles)
    n_used = tile_end[-1]
    pos = (tile_end - tiles)[eid] * tm + rank
    pair = jnp.arange(n_pairs, dtype=jnp.int32)
    dst_rows = jnp.zeros((n_tiles * tm,), jnp.int32).at[pos].set(pair)
    src_rows = dst_rows % nt
    tile_id = jnp.minimum(jnp.arange(n_tiles, dtype=jnp.int32), n_used - 1)
    tile_expert = jnp.sum((tile_end[None, :] <= tile_id[:, None]).astype(jnp.int32), axis=1)
    tile_first = jnp.concatenate(
        [jnp.ones((1,), jnp.int32), (tile_expert[1:] != tile_expert[:-1]).astype(jnp.int32)])
    tile_valid = jnp.clip(cnt[tile_expert] - (tile_id - (tile_end - tiles)[tile_expert]) * tm, 0, tm)
    return (tile_expert, tile_first, tile_valid.astype(jnp.int32),
            n_used.reshape(1).astype(jnp.int32),
            src_rows.reshape(n_tiles, 1, tm), dst_rows.reshape(n_tiles, 1, tm), n_pairs)


def _final_kernel(h_ref, y1_ref, y2_ref, route_ref, g_ref, o_ref):
    lane = lax.broadcasted_iota(jnp.int32, route_ref.shape, 1)
    route = route_ref[...]
    w1 = jnp.sum(jnp.where(lane == ROUTE_W1, route, 0.0), axis=1, keepdims=True)
    w2 = jnp.sum(jnp.where(lane == ROUTE_W2, route, 0.0), axis=1, keepdims=True)
    h = h_ref[...] + (w1 * y1_ref[...] + w2 * y2_ref[...])
    ms = jnp.mean(h * h, axis=-1, keepdims=True)
    o_ref[...] = h * lax.rsqrt(ms + NORM_EPS) * g_ref[...]


def _final(h, y_moe, route, g_final, row0, seg_stride, seg_rows, n_seg, tm):
    nt, d = h.shape
    per_seg = seg_rows // tm
    assert all(x % SUBLANES == 0 for x in (row0, seg_stride, tm, nt))
    off = lambda s, c: row0 + s * seg_stride + c * tm
    rows = lambda width, base: pl.BlockSpec((pl.Element(tm), pl.Element(width)),
                                            lambda s, c: (pl.multiple_of(base + off(s, c), SUBLANES), 0))
    return pl.pallas_call(
        _final_kernel,
        grid=(n_seg, per_seg),
        in_specs=[rows(d, 0), rows(d, 0), rows(d, nt), rows(LANES, 0),
                  pl.BlockSpec((1, d), lambda s, c: (0, 0))],
        out_specs=pl.BlockSpec((tm, d), lambda s, c: (s * per_seg + c, 0)),
        out_shape=jax.ShapeDtypeStruct((n_seg * seg_rows, d), F32),
        compiler_params=_cparams(("parallel", "parallel")),
        name="final_norm",
    )(h, y_moe, y_moe, route, g_final)


def _pad_cols(w, to):
    return jnp.pad(w, ((0, 0), (0, to - w.shape[1])))


def _pad_rows(w, to):
    return jnp.pad(w, ((0, to - w.shape[0]), (0, 0)))


def _s5_consts(lam_re, lam_im, log_dt, b_re, b_im, c_re, c_im, d, w_glu, b_glu):
    dt = jnp.exp(log_dt)[:, None]
    mag = jnp.exp(lam_re * dt)
    abar_re = mag * jnp.cos(lam_im * dt)
    abar_im = mag * jnp.sin(lam_im * dt)
    den = lam_re * lam_re + lam_im * lam_im
    nr = abar_re - 1.0
    coef_re = (nr * lam_re + abar_im * lam_im) / den
    coef_im = (abar_im * lam_re - nr * lam_im) / den
    bbar_re = coef_re[..., None] * b_re - coef_im[..., None] * b_im
    bbar_im = coef_re[..., None] * b_im + coef_im[..., None] * b_re
    n_g, n_p, n_c = b_re.shape
    gpb = LANES // n_c
    eye = jnp.eye(gpb, dtype=F32)

    def in_blk(bb):
        bb = bb.reshape(n_g // gpb, gpb, n_p, n_c)
        return jnp.einsum('kgpc,gh->kgchp', bb, eye).reshape(
            n_g // gpb, gpb * n_c, gpb * n_p).astype(BF16)

    def out_blk(cc):
        cc = cc.reshape(n_g // gpb, gpb, n_c, n_p)
        return jnp.einsum('kgcp,gh->khpgc', cc, eye).reshape(
            n_g // gpb, gpb * n_p, gpb * n_c).astype(BF16)

    return (in_blk(bbar_re), in_blk(bbar_im), out_blk(c_re), out_blk(c_im),
            d.reshape(1, -1), abar_re.reshape(-1, LANES), abar_im.reshape(-1, LANES),
            w_glu.astype(BF16), b_glu.reshape(1, -1))


def kernel(x_prompt, x_sample, state_shift, state_wkv, state_ssm_re, state_ssm_im, meta_tokens, g_mix, w_in, shift_mu, w0, w_decay_up, a0, w_aaa_up, w_gate_up, k_k, k_a, r_k, lnx_w, lnx_b, ssm_lam_re, ssm_lam_im, ssm_log_dt, ssm_b_re, ssm_b_im, ssm_c_re, ssm_c_im, ssm_d, w_glu, b_glu, w_br_a, w_br_b, w_out, g_ffn, w_router_grp, b_router_grp, w_router_exp, b_router_exp, w_exp_gate, w_exp_up, w_exp_down, g_final):
    depth = g_mix.shape[0]
    assert depth == 1, "single-layer trunk"
    bsz, seq, d = x_prompt.shape
    nb = x_sample.shape[0]
    assert x_sample.shape[1] == 1
    t_p = seq + N_META
    n_p = bsz * t_p
    width = k_k.shape[1]
    heads = width // HEAD
    n_dl, n_al, n_gl = w_decay_up.shape[1], w_aaa_up.shape[1], w_gate_up.shape[1]
    n_grp = w_router_grp.shape[2]
    n_exp = w_router_exp.shape[2]
    assert bsz * heads * 2 == LANES and (nb * heads) % LANES == 0

    s_row0 = -(-n_p // nb) * nb
    n_t = s_row0 + nb
    pieces = []
    for b in range(bsz):
        pieces += [meta_tokens, x_prompt[b]]
    x_all = jnp.concatenate(
        pieces + [jnp.zeros((s_row0 - n_p, d), F32), x_sample.reshape(nb, d)], axis=0)

    c_rkv = 3 * width
    c_xw, c_xa, c_xg = c_rkv, c_rkv + n_dl, c_rkv + n_dl + n_al
    c_u = c_xg + n_gl
    c_ga = c_u + width
    lo_width = 2 * LORA_PAD + n_gl

    def regroup(m):
        return jnp.concatenate(
            [m[:, :c_rkv], m[:, c_u:], _pad_cols(m[:, c_xw:c_xa], LORA_PAD),
             _pad_cols(m[:, c_xa:c_xg], LORA_PAD), m[:, c_xg:c_u]], axis=1)

    def ungroup(m):
        lo = c_rkv + width + 2 * d
        return jnp.concatenate(
            [m[:, :c_rkv], m[:, lo:lo + n_dl], m[:, lo + LORA_PAD:lo + LORA_PAD + n_al],
             m[:, lo + 2 * LORA_PAD:]], axis=1)

    w_in_p = regroup(w_in[0].astype(BF16))
    n_proj = w_in_p.shape[1]
    proj = _inproj(x_all, g_mix, w_in_p, tm=1408, tn=512)

    mu_all = regroup(_pad_cols(shift_mu, c_u + width + 2 * d))
    mu_rkv, mu_lo = mu_all[:, :c_rkv], mu_all[:, n_proj - lo_width:]
    st_all = regroup(_pad_cols(state_shift[0], c_u + width + 2 * d))
    prev_s = (st_all[:, :c_rkv], st_all[:, n_proj - lo_width:])
    lo_col_block = (n_proj - lo_width) // lo_width
    prep_w = (mu_rkv, mu_lo, w0, a0,
              _pad_rows(w_decay_up[0], LORA_PAD).astype(BF16),
              _pad_rows(w_aaa_up[0], LORA_PAD).astype(BF16), w_gate_up[0].astype(BF16))

    tm_seq = t_p // 6
    g_all = jnp.zeros((n_t, width), F32)
    rp, wp, kp, vp, ap, g_all = _rwkv_prep(proj, 0, n_p, tm_seq, bsz, width, lo_col_block,
                                                lo_width, *prep_w, g_all)
    rs, ws, ks, vs, as_, g_all = _rwkv_prep(proj, s_row0 // nb, nb, nb, 1, width,
                                                 lo_col_block, lo_width, *prep_w, g_all,
                                                 prev=prev_s)

    s5c = _s5_consts(ssm_lam_re[0], ssm_lam_im[0], ssm_log_dt[0], ssm_b_re[0], ssm_b_im[0],
                     ssm_c_re[0], ssm_c_im[0], ssm_d[0], w_glu[0], b_glu)
    n_g, n_st = ssm_lam_re.shape[1], ssm_lam_re.shape[2]
    u_col_block = c_rkv // width
    yb_all = jnp.zeros((n_t, width), BF16)
    yb_all, re_p, im_p = _s5(proj, 0, n_p, tm_seq, bsz, u_col_block, width, s5c, yb_all)
    h0 = (state_ssm_re[0].reshape(nb, n_g * n_st), state_ssm_im[0].reshape(nb, n_g * n_st))
    yb_all, re_s, im_s = _s5(proj, s_row0 // nb, nb, nb, 1, u_col_block, width, s5c, yb_all,
                             h0=h0)

    half = HEAD // 2

    def to_lanes_p(z):
        return z.reshape(t_p, bsz * heads, HEAD).transpose(0, 2, 1)

    def to_lanes_s(z):
        return z.reshape(nb, heads, HEAD).transpose(2, 0, 1).reshape(1, HEAD, nb * heads)

    def head_const(c):
        return jnp.tile(c.reshape(heads, HEAD).T, (1, LANES // heads))

    rk_l = head_const(r_k[0])
    kk_l, ka_l = head_const(k_k[0]), head_const(k_a[0])
    lw_p = jnp.broadcast_to(lnx_w[0].reshape(heads, 2, half).transpose(2, 1, 0)[:, :, None, :],
                            (half, 2, bsz, heads)).reshape(half, LANES)
    lb_p = jnp.broadcast_to(lnx_b[0].reshape(heads, 2, half).transpose(2, 1, 0)[:, :, None, :],
                            (half, 2, bsz, heads)).reshape(half, LANES)
    lw_s = jnp.tile(lnx_w[0].reshape(heads, HEAD).T, (1, LANES // heads))
    lb_s = jnp.tile(lnx_b[0].reshape(heads, HEAD).T, (1, LANES // heads))

    s0_p = jnp.zeros((HEAD, half, LANES), F32)
    y_p, sf_p = _wkv(to_lanes_p(rp), to_lanes_p(wp), to_lanes_p(kp), to_lanes_p(ap),
                     to_lanes_p(vp), s0_p, kk_l, ka_l, rk_l, lw_p, lb_p,
                     tc=48, isplit=True, unroll=8)
    s0_s = state_wkv[0].transpose(3, 2, 0, 1).reshape(HEAD, HEAD, nb * heads)
    y_s, sf_s = _wkv(to_lanes_s(rs), to_lanes_s(ws), to_lanes_s(ks), to_lanes_s(as_),
                     to_lanes_s(vs), s0_s, kk_l, ka_l, rk_l, lw_s, lb_s,
                     tc=1, isplit=False, unroll=1)

    ya_p = y_p.reshape(t_p, half, 2, bsz, heads).transpose(3, 0, 4, 2, 1).reshape(n_p, width)
    ya_s = y_s.reshape(HEAD, nb, heads).transpose(1, 2, 0).reshape(nb, width)
    wkv_p = sf_p.reshape(HEAD, half, 2, bsz, heads).transpose(3, 4, 2, 1, 0).reshape(
        1, bsz, heads, HEAD, HEAD)
    wkv_s = sf_s.reshape(HEAD, HEAD, nb, heads).transpose(2, 3, 1, 0)[None]

    def unify(a, b):
        return jnp.concatenate([a, jnp.zeros((s_row0 - n_p, a.shape[1]), a.dtype), b], axis=0)

    w_router = _pad_cols(jnp.concatenate([w_router_grp[0], w_router_exp[0]], axis=1), LANES)
    w_router_hi = w_router.astype(BF16)
    w_router = jnp.stack([w_router_hi, (w_router - w_router_hi.astype(F32)).astype(BF16)])
    b_router = _pad_cols(jnp.concatenate([b_router_grp, b_router_exp], axis=1), LANES)
    assert n_grp + n_exp <= LANES and n_exp == n_grp * EXPERTS_PER_GROUP
    h1, xn2, route = _merge(unify(ya_p, ya_s), g_all, yb_all, proj, x_all,
                           w_br_a[0].astype(BF16), w_br_b[0].astype(BF16), w_out[0].astype(BF16),
                           g_ffn, w_router, b_router, tm=256, ga_col_block=(c_rkv + width) // d,
                           n_grp=n_grp)
    tm_moe = 256
    te, tf, tv, nu, src_rows, dst_rows, n_moe_rows = _moe_plan(route, n_exp, tm_moe)
    y_moe = _moe(xn2, te, tf, tv, nu, src_rows, dst_rows, w_exp_gate[0], w_exp_up[0], w_exp_down[0],
                 n_moe_rows, tm_moe)
    gfin = g_final.reshape(1, d)
    y_prompt = _final(h1, y_moe, route, gfin, N_META, t_p, seq, bsz, tm=512).reshape(bsz, seq, d)
    y_sample = _final(h1, y_moe, route, gfin, s_row0, 0, nb, 1, tm=nb).reshape(nb, 1, d)
    last_p = jnp.concatenate([proj[(b + 1) * t_p - 1:(b + 1) * t_p] for b in range(bsz)], axis=0)
    shift_p = ungroup(last_p)[None]
    shift_s = ungroup(proj[s_row0:])[None]
    return (y_prompt, y_sample, shift_p, wkv_p,
            re_p.reshape(1, bsz, n_g, n_st), im_p.reshape(1, bsz, n_g, n_st),
            shift_s, wkv_s,
            re_s.reshape(1, nb, n_g, n_st), im_s.reshape(1, nb, n_g, n_st))
```

```python
import functools

import jax
import jax.numpy as jnp
from jax import lax
from jax.experimental import pallas as pl
from jax.experimental.pallas import tpu as pltpu

F32 = jnp.float32
BF16 = jnp.bfloat16

NORM_EPS = 1e-6
LNX_EPS = 64e-5
N_META = 16
HEAD = 64
EXPERTS_PER_GROUP = 8
LANES = 128
SUBLANES = 8
LOG2_SUBLANES = SUBLANES.bit_length() - 1
LORA_PAD = 128
ROUTE_E1, ROUTE_E2, ROUTE_W1, ROUTE_W2 = 0, 1, 2, 3
VMEM_LIMIT = 56 * 1024 * 1024


def _cparams(sem):
    return pltpu.CompilerParams(dimension_semantics=sem, vmem_limit_bytes=VMEM_LIMIT)


def _const_spec(shape):
    nd = len(shape)
    return pl.BlockSpec(shape, lambda *_: (0,) * nd)


def _inproj_kernel(x_ref, g_ref, w_ref, o_ref, xn_ref):
    @pl.when(pl.program_id(1) == 0)
    def _():
        x = x_ref[...]
        ms = jnp.mean(x * x, axis=-1, keepdims=True)
        xn_ref[...] = (x * lax.rsqrt(ms + NORM_EPS) * g_ref[...]).astype(BF16)

    o_ref[...] = jnp.dot(xn_ref[...], w_ref[...], preferred_element_type=F32)


def _inproj(x_all, g_mix, w_in_p, tm, tn):
    nt, d = x_all.shape
    n_out = w_in_p.shape[1]
    return pl.pallas_call(
        _inproj_kernel,
        grid=(nt // tm, n_out // tn),
        in_specs=[pl.BlockSpec((tm, d), lambda i, j: (i, 0)),
                  pl.BlockSpec((1, d), lambda i, j: (0, 0)),
                  pl.BlockSpec((d, tn), lambda i, j: (0, j))],
        out_specs=pl.BlockSpec((tm, tn), lambda i, j: (i, j)),
        out_shape=jax.ShapeDtypeStruct((nt, n_out), F32),
        scratch_shapes=[pltpu.VMEM((tm, d), BF16)],
        compiler_params=_cparams(("parallel", "arbitrary")),
        name="inproj",
    )(x_all, g_mix, w_in_p)


def _softplus(z):
    return jnp.maximum(z, 0.0) + jnp.log1p(jnp.exp(-jnp.abs(z)))


def _rwkv_prep_kernel(*refs, width, carry_prev):
    if carry_prev:
        (rkv_ref, lo_ref, mu_rkv_ref, mu_lo_ref, w0_ref, a0_ref,
         wd_ref, wa_ref, wg_ref, _g_all,
         r_out, w_out, k_out, v_out, a_out, g_out, c_rkv, c_lo) = refs

        @pl.when(pl.program_id(1) == 0)
        def _():
            c_rkv[...] = jnp.zeros_like(c_rkv)
            c_lo[...] = jnp.zeros_like(c_lo)
    else:
        (rkv_ref, lo_ref, prev_rkv_ref, prev_lo_ref, mu_rkv_ref, mu_lo_ref, w0_ref, a0_ref,
         wd_ref, wa_ref, wg_ref, _g_all,
         r_out, w_out, k_out, v_out, a_out, g_out) = refs

    tm = rkv_ref.shape[0]
    first_row = lax.broadcasted_iota(jnp.int32, (tm, 1), 0) == 0

    def shifted(p, prev_ref, carry_ref, cols):
        if carry_prev:
            prev = jnp.where(first_row, carry_ref[:, cols], pltpu.roll(p, 1, 0))
        else:
            prev = prev_ref[:, cols]
        return prev

    def lerp(p, prev, mu):
        return p + (prev - p) * mu

    lo_cols = slice(0, lo_ref.shape[1])
    p_lo = lo_ref[...]
    q_lo = lerp(p_lo, shifted(p_lo, None if carry_prev else prev_lo_ref,
                              c_lo if carry_prev else None, lo_cols), mu_lo_ref[...])
    xw = q_lo[:, 0:LORA_PAD]
    xa = q_lo[:, LORA_PAD:2 * LORA_PAD]
    xg = q_lo[:, 2 * LORA_PAD:]
    dw = jnp.dot(jnp.tanh(xw).astype(BF16), wd_ref[...], preferred_element_type=F32)
    wlog = -_softplus(-(w0_ref[...] + dw)) - 0.5
    w_out[...] = jnp.exp(-jnp.exp(wlog))
    a = jax.nn.sigmoid(a0_ref[...] + jnp.dot(xa.astype(BF16), wa_ref[...],
                                             preferred_element_type=F32))
    a_out[...] = a
    g_out[...] = jnp.dot(jax.nn.sigmoid(xg).astype(BF16), wg_ref[...],
                         preferred_element_type=F32)

    def q_of(idx):
        cols = slice(idx * width, (idx + 1) * width)
        p = rkv_ref[:, cols]
        prev = shifted(p, None if carry_prev else prev_rkv_ref,
                       c_rkv if carry_prev else None, cols)
        return lerp(p, prev, mu_rkv_ref[:, cols])

    r_out[...] = q_of(0)
    k_out[...] = q_of(1)
    v_out[...] = q_of(2)

    if carry_prev:
        c_rkv[...] = rkv_ref[tm - 1:tm, :]
        c_lo[...] = lo_ref[tm - 1:tm, :]


def _rwkv_prep(proj, row_block0, n_rows, tm, seqs, width, lo_col_block, lo_width,
               mu_rkv, mu_lo, w0, a0, wd, wa, wg, g_all, prev=None):
    carry_prev = prev is None
    per_seq = n_rows // seqs // tm
    row_map = lambda b, c: (row_block0 + b * per_seq + c, 0)
    lo_map = lambda b, c: (row_block0 + b * per_seq + c, lo_col_block)
    out_map = lambda b, c: (b * per_seq + c, 0)
    in_specs = [pl.BlockSpec((tm, 3 * width), row_map), pl.BlockSpec((tm, lo_width), lo_map)]
    args = [proj, proj]
    if not carry_prev:
        in_specs += [pl.BlockSpec((tm, 3 * width), out_map), pl.BlockSpec((tm, lo_width), out_map)]
        args += list(prev)
    consts = [mu_rkv, mu_lo, w0, a0, wd, wa, wg]
    in_specs += [_const_spec(c.shape) for c in consts] + [pl.BlockSpec(memory_space=pl.ANY)]
    args += consts + [g_all]
    scratch = []
    if carry_prev:
        scratch = [pltpu.VMEM((1, 3 * width), F32), pltpu.VMEM((1, lo_width), F32)]
    out_sd = jax.ShapeDtypeStruct((n_rows // seqs, seqs * width), F32)
    tmaj_map = lambda b, c: (c, b)
    return pl.pallas_call(
        functools.partial(_rwkv_prep_kernel, width=width, carry_prev=carry_prev),
        grid=(seqs, per_seq),
        in_specs=in_specs,
        out_specs=[pl.BlockSpec((tm, width), tmaj_map)] * 5 + [pl.BlockSpec((tm, width), row_map)],
        out_shape=[out_sd] * 5 + [jax.ShapeDtypeStruct(g_all.shape, g_all.dtype)],
        input_output_aliases={len(args) - 1: 5},
        scratch_shapes=scratch,
        compiler_params=_cparams(("parallel", "arbitrary")),
        name="rwkv_prep_seq" if carry_prev else "rwkv_prep_step",
    )(*args)


def _wkv_kernel(r_ref, w_ref, kraw_ref, al_ref, v_ref, s0_ref, kkc_ref, kac_ref, rk_ref, lw_ref,
                lb_ref, y_ref, sf_ref, s_scr, a_scr, b_scr, k_src, *dup_scr,
                ni, nj, tc, isplit, unroll):
    @pl.when(pl.program_id(1) == 0)
    def _():
        s_scr[...] = s0_ref[...]

    def dup(x):
        return jnp.concatenate([x, x], axis=-1) if isplit else x

    if isplit:
        w_src, r_src, v_src = dup_scr
    else:
        w_src, r_src, v_src = w_ref, r_ref, v_ref

    def prep(t, carry):
        kraw = dup(kraw_ref[t])
        al = dup(al_ref[t])
        kk = kraw * kkc_ref[...]
        k_src[t] = kraw * (1.0 + (al - 1.0) * kac_ref[...])
        ss = jnp.sum(kk * kk, axis=0, keepdims=True)
        kkn = kk / jnp.maximum(jnp.sqrt(ss), 1e-12)
        a_scr[t] = -kkn
        b_scr[t] = kkn * al
        if isplit:
            w_src[t] = dup(w_ref[t])
            r_src[t] = dup(r_ref[t])
            v = v_ref[t]
            v_src[t] = jnp.concatenate([v[:ni], v[ni:]], axis=-1)
        return carry

    lax.fori_loop(0, tc, prep, 0, unroll=unroll)

    def row(ref, t, j):
        return ref[t, j:j + 1, :]

    def tree(parts):
        while len(parts) > 1:
            parts = [parts[i] + parts[i + 1] for i in range(0, len(parts), 2)]
        return parts[0]

    n_acc = 4
    sa0 = tree([sum(s_scr[j] * row(a_scr, 0, j) for j in range(q, nj, n_acc))
                for q in range(n_acc)])

    def step(t, sa):
        v = v_src[t]
        tn = jnp.minimum(t + 1, tc - 1)
        y = [None] * n_acc
        san = [None] * n_acc
        for j in range(nj):
            s = s_scr[j] * row(w_src, t, j) + sa * row(b_scr, t, j) + v * row(k_src, t, j)
            s_scr[j] = s
            yj = s * row(r_src, t, j)
            sj = s * row(a_scr, tn, j)
            q = j % n_acc
            y[q] = yj if y[q] is None else y[q] + yj
            san[q] = sj if san[q] is None else san[q] + sj
        y_ref[t] = tree(y)
        return tree(san)

    lax.fori_loop(0, tc, step, sa0)

    def isum(x):
        s = jnp.broadcast_to(jnp.sum(x, axis=0, keepdims=True), (8, LANES))
        if isplit:
            s = s + pltpu.roll(s, LANES // 2, 1)
        return s[0:1]

    def post(t, carry):
        y = y_ref[t]
        v = v_src[t]
        mu = isum(y) * (1.0 / HEAD)
        d = y - mu
        var = isum(d * d) * (1.0 / HEAD)
        yn = d * lax.rsqrt(var + LNX_EPS) * lw_ref[...] + lb_ref[...]
        bonus = jnp.sum(r_src[t] * k_src[t] * rk_ref[...], axis=0, keepdims=True)
        y_ref[t] = yn + bonus * v
        return carry

    lax.fori_loop(0, tc, post, 0, unroll=unroll)

    @pl.when(pl.program_id(1) == pl.num_programs(1) - 1)
    def _():
        sf_ref[...] = s_scr[...]


def _wkv(r, w, kraw, al, v, s0, kkc, kac, rk, lw, lb, tc, isplit, unroll):
    t, nj, jl = r.shape
    ni, lanes = s0.shape[1:]
    assert jl == (LANES // 2 if isplit else lanes) and v.shape == r.shape
    jspec = pl.BlockSpec((tc, nj, min(jl, LANES)), lambda l, c: (c, 0, l))
    dup_scr = []
    if isplit:
        dup_scr = [pltpu.VMEM((tc, nj, LANES), F32)] * 2 + [pltpu.VMEM((tc, ni, LANES), F32)]
    ispec = pl.BlockSpec((tc, ni, LANES), lambda l, c: (c, 0, l))
    sspec = pl.BlockSpec((nj, ni, LANES), lambda l, c: (0, 0, l))
    consts = [kkc, kac, rk, lw, lb]
    return pl.pallas_call(
        functools.partial(_wkv_kernel, ni=ni, nj=nj, tc=tc, isplit=isplit, unroll=unroll),
        grid=(lanes // LANES, t // tc),
        in_specs=[jspec, jspec, jspec, jspec, jspec, sspec] + [_const_spec(c.shape) for c in consts],
        out_specs=[ispec, sspec],
        out_shape=[jax.ShapeDtypeStruct((t, ni, lanes), F32),
                   jax.ShapeDtypeStruct((nj, ni, lanes), F32)],
        scratch_shapes=[pltpu.VMEM((nj, ni, LANES), F32),
                        pltpu.VMEM((tc, nj, LANES), F32),
                        pltpu.VMEM((tc, nj, LANES), F32),
                        pltpu.VMEM((tc, nj, LANES), F32)] + dup_scr,
        compiler_params=_cparams(("parallel", "arbitrary")),
        name="wkv_seq" if isplit else "wkv_step",
    )(r, w, kraw, al, v, s0, *consts)


def _s5_kernel(*refs, sequential, n_blk, pitch, unroll):
    if sequential:
        (u_ref, bre_ref, bim_ref, cre_ref, cim_ref, d_ref, are_ref, aim_ref, wglu_ref, bglu_ref,
         _yb_all, yb_ref, hre_out, him_out, st_re, st_im, c_re, c_im) = refs
    else:
        (u_ref, h0re_ref, h0im_ref, bre_ref, bim_ref, cre_ref, cim_ref, d_ref, are_ref, aim_ref,
         wglu_ref, bglu_ref, _yb_all, yb_ref, hre_out, him_out, st_re, st_im) = refs

    tm = u_ref.shape[0]
    kin = bre_ref.shape[1]
    kst = bre_ref.shape[2]
    tiles_per_blk = kst // LANES
    n_tiles = n_blk * tiles_per_blk
    u = u_ref[...]
    ub = u.astype(BF16)

    def tile_rows(k):
        return slice(k * pitch, k * pitch + tm)

    for kb in range(n_blk):
        ukb = ub[:, kb * kin:(kb + 1) * kin]
        bu_re = jnp.dot(ukb, bre_ref[kb], preferred_element_type=F32)
        bu_im = jnp.dot(ukb, bim_ref[kb], preferred_element_type=F32)
        for n in range(tiles_per_blk):
            k = kb * tiles_per_blk + n
            cols = slice(n * LANES, (n + 1) * LANES)
            if sequential:
                st_re[tile_rows(k), :] = bu_re[:, cols]
                st_im[tile_rows(k), :] = bu_im[:, cols]
            else:
                kc = slice(k * LANES, (k + 1) * LANES)
                ar, ai = are_ref[k:k + 1, :], aim_ref[k:k + 1, :]
                h0r, h0i = h0re_ref[:, kc], h0im_ref[:, kc]
                nr = bu_re[:, cols] + (ar * h0r - ai * h0i)
                ni = bu_im[:, cols] + (ar * h0i + ai * h0r)
                st_re[tile_rows(k), :] = nr
                st_im[tile_rows(k), :] = ni
                hre_out[:, kc] = nr
                him_out[:, kc] = ni

    if sequential:
        @pl.when(pl.program_id(1) == 0)
        def _():
            c_re[...] = jnp.zeros_like(c_re)
            c_im[...] = jnp.zeros_like(c_im)

        ar = are_ref[...]
        ai = aim_ref[...]
        groups = range(n_tiles // SUBLANES)

        def token_rows(t, m):
            return pl.ds(t + m * SUBLANES * pitch, SUBLANES, stride=pitch)

        def step(t, h):
            hr, hi = h
            bur = jnp.concatenate([st_re[token_rows(t, m), :] for m in groups], axis=0)
            bui = jnp.concatenate([st_im[token_rows(t, m), :] for m in groups], axis=0)
            nr = ar * hr - ai * hi + bur
            ni = ar * hi + ai * hr + bui
            for m in groups:
                st_re[token_rows(t, m), :] = nr[m * SUBLANES:(m + 1) * SUBLANES]
                st_im[token_rows(t, m), :] = ni[m * SUBLANES:(m + 1) * SUBLANES]
            return nr, ni

        hr, hi = lax.fori_loop(0, tm, step, (c_re[...], c_im[...]), unroll=unroll)
        c_re[...] = hr
        c_im[...] = hi
        hre_out[0] = hr
        him_out[0] = hi

    ys = []
    for kb in range(n_blk):
        tiles = range(kb * tiles_per_blk, (kb + 1) * tiles_per_blk)
        h_re = jnp.concatenate([st_re[tile_rows(k), :] for k in tiles], axis=1)
        h_im = jnp.concatenate([st_im[tile_rows(k), :] for k in tiles], axis=1)
        yre = jnp.dot(h_re.astype(BF16), cre_ref[kb], preferred_element_type=F32)
        yim = jnp.dot(h_im.astype(BF16), cim_ref[kb], preferred_element_type=F32)
        ys.append(yre - yim)
    y = jnp.concatenate(ys, axis=1) + d_ref[...] * u
    y = jax.nn.gelu(y)
    gate = jnp.dot(y.astype(BF16), wglu_ref[...], preferred_element_type=F32) + bglu_ref[...]
    yb_ref[...] = (y * jax.nn.sigmoid(gate)).astype(BF16)


def _s5(proj, row_block0, n_rows, tm, seqs, u_col_block, width, consts, yb_all, h0=None):
    sequential = h0 is None
    bre = consts[0]
    n_blk, _, kst = bre.shape
    n_state = n_blk * kst
    n_tiles = n_state // LANES
    assert tm % SUBLANES == 0 and n_tiles % SUBLANES == 0
    pitch = tm
    per_seq = n_rows // seqs // tm
    u_map = lambda b, c: (row_block0 + b * per_seq + c, u_col_block)
    out_map = lambda b, c: (b * per_seq + c, 0)
    row_map = lambda b, c: (row_block0 + b * per_seq + c, 0)
    in_specs = [pl.BlockSpec((tm, width), u_map)]
    args = [proj]
    if not sequential:
        in_specs += [pl.BlockSpec((tm, n_state), out_map)] * 2
        args += list(h0)
    in_specs += [_const_spec(c.shape) for c in consts] + [pl.BlockSpec(memory_space=pl.ANY)]
    args += list(consts) + [yb_all]
    scratch = [pltpu.VMEM((n_tiles * pitch, LANES), F32), pltpu.VMEM((n_tiles * pitch, LANES), F32)]
    if sequential:
        scratch += [pltpu.VMEM((n_tiles, LANES), F32), pltpu.VMEM((n_tiles, LANES), F32)]
        st_spec = pl.BlockSpec((1, n_tiles, LANES), lambda b, c: (b, 0, 0))
        st_shape = jax.ShapeDtypeStruct((seqs, n_tiles, LANES), F32)
    else:
        st_spec = pl.BlockSpec((tm, n_state), out_map)
        st_shape = jax.ShapeDtypeStruct((n_rows, n_state), F32)
    return pl.pallas_call(
        functools.partial(_s5_kernel, sequential=sequential, n_blk=n_blk, pitch=pitch, unroll=4),
        grid=(seqs, per_seq),
        in_specs=in_specs,
        out_specs=[pl.BlockSpec((tm, width), row_map), st_spec, st_spec],
        out_shape=[jax.ShapeDtypeStruct(yb_all.shape, yb_all.dtype), st_shape, st_shape],
        input_output_aliases={len(args) - 1: 0},
        scratch_shapes=scratch,
        compiler_params=_cparams(("parallel", "arbitrary")),
        name="s5_seq" if sequential else "s5_step",
    )(*args)


def _route(logits, n_grp):
    lane = lax.broadcasted_iota(jnp.int32, logits.shape, 1).astype(F32)
    neg = jnp.float32(-1e30)
    big = jnp.float32(1e9)
    is_grp = lane < n_grp
    gl = jnp.where(is_grp, logits, neg)
    gmax = jnp.max(gl, axis=1, keepdims=True)
    gsum = jnp.sum(jnp.where(is_grp, jnp.exp(gl - gmax), 0.0), axis=1, keepdims=True)
    g_p = 1.0 / gsum
    g_idx = jnp.min(jnp.where(is_grp & (gl == gmax), lane, big), axis=1, keepdims=True)
    lo = n_grp + g_idx * EXPERTS_PER_GROUP
    in_grp = (lane >= lo) & (lane < lo + EXPERTS_PER_GROUP)
    el = jnp.where(in_grp, logits, neg)
    v1 = jnp.max(el, axis=1, keepdims=True)
    i1 = jnp.min(jnp.where(in_grp & (el == v1), lane, big), axis=1, keepdims=True)
    rest = in_grp & (lane != i1)
    el2 = jnp.where(rest, logits, neg)
    v2 = jnp.max(el2, axis=1, keepdims=True)
    i2 = jnp.min(jnp.where(rest & (el2 == v2), lane, big), axis=1, keepdims=True)
    e2 = jnp.exp(v2 - v1)
    w1 = g_p / (1.0 + e2)
    w2 = g_p * e2 / (1.0 + e2)
    return (jnp.where(lane == ROUTE_E1, i1 - n_grp, 0.0) + jnp.where(lane == ROUTE_E2, i2 - n_grp, 0.0)
            + jnp.where(lane == ROUTE_W1, w1, 0.0) + jnp.where(lane == ROUTE_W2, w2, 0.0))


def _merge_kernel(ya_ref, g_ref, yb_ref, ga_ref, gb_ref, x_ref, wa_ref, wb_ref, wo_ref,
                  gf_ref, wr_ref, br_ref, h_out, xn_out, comb_out, *, n_grp):
    ya = (ya_ref[...] * g_ref[...]).astype(BF16)
    ma = jnp.dot(ya, wa_ref[...], preferred_element_type=F32)
    mb = jnp.dot(yb_ref[...], wb_ref[...], preferred_element_type=F32)
    merged = jax.nn.sigmoid(ga_ref[...]) * ma + jax.nn.sigmoid(gb_ref[...]) * mb
    h = x_ref[...] + jnp.dot(merged.astype(BF16), wo_ref[...], preferred_element_type=F32)
    h_out[...] = h
    ms = jnp.mean(h * h, axis=-1, keepdims=True)
    xn = h * lax.rsqrt(ms + NORM_EPS) * gf_ref[...]
    xn_out[...] = xn
    x_hi = xn.astype(BF16)
    x_lo = (xn - x_hi.astype(F32)).astype(BF16)
    w_hi, w_lo = wr_ref[0], wr_ref[1]
    logits = (jnp.dot(x_hi, w_hi, preferred_element_type=F32)
              + (jnp.dot(x_hi, w_lo, preferred_element_type=F32)
                 + jnp.dot(x_lo, w_hi, preferred_element_type=F32))) + br_ref[...]
    comb_out[...] = _route(logits, n_grp)


def _merge(ya, g, yb, proj, x_all, w_br_a, w_br_b, w_out, g_ffn, w_router, b_router,
           tm, ga_col_block, n_grp):
    nt, d = x_all.shape
    wdt = ya.shape[1]
    row = lambda i: (i, 0)
    single = dict(pipeline_mode=pl.Buffered(1))
    in_specs = [pl.BlockSpec((tm, wdt), row), pl.BlockSpec((tm, wdt), row),
                pl.BlockSpec((tm, wdt), row),
                pl.BlockSpec((tm, d), lambda i: (i, ga_col_block)),
                pl.BlockSpec((tm, d), lambda i: (i, ga_col_block + 1)),
                pl.BlockSpec((tm, d), row),
                pl.BlockSpec(w_br_a.shape, lambda i: (0, 0), **single),
                pl.BlockSpec(w_br_b.shape, lambda i: (0, 0), **single),
                pl.BlockSpec(w_out.shape, lambda i: (0, 0), **single),
                _const_spec(g_ffn.shape), _const_spec(w_router.shape), _const_spec(b_router.shape)]
    return pl.pallas_call(
        functools.partial(_merge_kernel, n_grp=n_grp),
        grid=(nt // tm,),
        in_specs=in_specs,
        out_specs=[pl.BlockSpec((tm, d), row), pl.BlockSpec((tm, d), row),
                   pl.BlockSpec((tm, LANES), row)],
        out_shape=[jax.ShapeDtypeStruct((nt, d), F32), jax.ShapeDtypeStruct((nt, d), F32),
                   jax.ShapeDtypeStruct((nt, LANES), F32)],
        compiler_params=_cparams(("parallel",)),
        name="merge_route",
    )(ya, g, yb, proj, proj, x_all, w_br_a, w_br_b, w_out, g_ffn, w_router, b_router)


def _moe_kernel(te_ref, first_ref, nvalid_ref, nused_ref, src_cur, src_nxt, dst_cur,
                xn_hbm, wg_ref, wu_ref, wd_ref, y_hbm,
                xbuf, obuf, wg_bf, wu_bf, wd_bf, gsem, ssem, *, tm):
    i = pl.program_id(0)
    last = pl.num_programs(0) - 1
    n_used = nused_ref[0]
    slot = lax.rem(i, 2)

    def gather_start(src, s):
        def body(r, c):
            pltpu.make_async_copy(xn_hbm.at[pl.ds(src[0, 0, r], 1)], xbuf.at[s, pl.ds(r, 1)],
                                  gsem.at[s]).start()
            return c
        lax.fori_loop(0, tm, body, 0, unroll=8)

    def gather_wait(s):
        pltpu.make_async_copy(xn_hbm.at[pl.ds(0, tm)], xbuf.at[s], gsem.at[s]).wait()

    def scatter_start(s, n_rows):
        def row(r):
            pltpu.make_async_copy(obuf.at[s, pl.ds(r, 1)], y_hbm.at[pl.ds(dst_cur[0, 0, r], 1)],
                                  ssem.at[s]).start()

        def block(i, c):
            for u in range(SUBLANES):
                row(i * SUBLANES + u)
            return c

        def single(r, c):
            row(r)
            return c

        n_blocks = lax.shift_right_logical(n_rows, LOG2_SUBLANES)
        lax.fori_loop(0, n_blocks, block, 0)
        lax.fori_loop(n_blocks * SUBLANES, n_rows, single, 0)

    def scatter_wait(s, n_rows):
        n_full = pl.multiple_of(
            lax.shift_left(lax.shift_right_logical(n_rows, LOG2_SUBLANES), LOG2_SUBLANES), SUBLANES)

        @pl.when(n_full > 0)
        def _():
            pltpu.make_async_copy(obuf.at[s, pl.ds(0, n_full)], y_hbm.at[pl.ds(0, n_full)],
                                  ssem.at[s]).wait()

        def body(r, c):
            pltpu.make_async_copy(obuf.at[s, pl.ds(r, 1)], y_hbm.at[pl.ds(r, 1)],
                                  ssem.at[s]).wait()
            return c
        lax.fori_loop(n_full, n_rows, body, 0)

    @pl.when(i == 0)
    def _():
        gather_start(src_cur, 0)

    @pl.when(i < n_used)
    def _():
        gather_wait(slot)

    @pl.when(i + 1 < n_used)
    def _():
        gather_start(src_nxt, 1 - slot)

    @pl.when((i >= 2) & (i - 2 < n_used))
    def _():
        scatter_wait(slot, nvalid_ref[i - 2])

    @pl.when(i < n_used)
    def _():
        @pl.when(first_ref[i] == 1)
        def _():
            wg_bf[...] = wg_ref[0].astype(BF16)
            wu_bf[...] = wu_ref[0].astype(BF16)
            wd_bf[...] = wd_ref[0].astype(BF16)

        x = xbuf[slot].astype(BF16)
        xg = jnp.dot(x, wg_bf[...], preferred_element_type=F32)
        xu = jnp.dot(x, wu_bf[...], preferred_element_type=F32)
        hid = (jax.nn.silu(xg) * xu).astype(BF16)
        obuf[slot] = jnp.dot(hid, wd_bf[...], preferred_element_type=F32)
        scatter_start(slot, nvalid_ref[i])

    @pl.when(i == last)
    def _():
        @pl.when((i >= 1) & (i - 1 < n_used))
        def _():
            scatter_wait(1 - slot, nvalid_ref[i - 1])

        @pl.when(i < n_used)
        def _():
            scatter_wait(slot, nvalid_ref[i])


def _moe(xn, tile_expert, tile_first, tile_valid, n_used, src_rows, dst_rows, wg, wu, wd,
         n_out_rows, tm):
    nt, d = xn.shape
    n_exp, _, de = wg.shape
    n_tiles = src_rows.shape[0]
    smem_cur = pl.BlockSpec((1, 1, tm), lambda i, *_: (i, 0, 0), memory_space=pltpu.SMEM)
    smem_nxt = pl.BlockSpec((1, 1, tm), lambda i, *_: (jnp.minimum(i + 1, n_tiles - 1), 0, 0),
                            memory_space=pltpu.SMEM)
    grid_spec = pltpu.PrefetchScalarGridSpec(
        num_scalar_prefetch=4,
        grid=(n_tiles,),
        in_specs=[smem_cur, smem_nxt, smem_cur,
                  pl.BlockSpec(memory_space=pl.ANY),
                  pl.BlockSpec((1, d, de), lambda i, te, *_: (te[i], 0, 0)),
                  pl.BlockSpec((1, d, de), lambda i, te, *_: (te[i], 0, 0)),
                  pl.BlockSpec((1, de, d), lambda i, te, *_: (te[i], 0, 0))],
        out_specs=pl.BlockSpec(memory_space=pl.ANY),
        scratch_shapes=[pltpu.VMEM((2, tm, d), F32), pltpu.VMEM((2, tm, d), F32),
                        pltpu.VMEM((d, de), BF16), pltpu.VMEM((d, de), BF16),
                        pltpu.VMEM((de, d), BF16),
                        pltpu.SemaphoreType.DMA((2,)), pltpu.SemaphoreType.DMA((2,))])
    return pl.pallas_call(
        functools.partial(_moe_kernel, tm=tm),
        grid_spec=grid_spec,
        out_shape=jax.ShapeDtypeStruct((n_out_rows, d), F32),
        compiler_params=_cparams(("arbitrary",)),
        name="moe_grouped",
    )(tile_expert, tile_first, tile_valid, n_used, src_rows, src_rows, dst_rows, xn, wg, wu, wd)


def _moe_plan(route, n_exp, tm):
    nt = route.shape[0]
    n_pairs = 2 * nt
    n_tiles = n_pairs // tm + n_exp
    eid = jnp.concatenate([route[:, ROUTE_E1], route[:, ROUTE_E2]]).astype(jnp.int32)
    onehot = (eid[:, None] == jnp.arange(n_exp, dtype=jnp.int32)[None, :]).astype(jnp.int32)
    csum = jnp.cumsum(onehot, axis=0)
    rank = jnp.take_along_axis(csum, eid[:, None], axis=1)[:, 0] - 1
    cnt = csum[-1]
    tiles = (cnt + tm - 1) // tm
    tile_end = jnp.cumsum(tiles)
    n_used = tile_end[-1]
    pos = (tile_end - tiles)[eid] * tm + rank
    pair = jnp.arange(n_pairs, dtype=jnp.int32)
    dst_rows = jnp.zeros((n_tiles * tm,), jnp.int32).at[pos].set(
        pair, unique_indices=True, mode="promise_in_bounds")
    src_rows = dst_rows % nt
    tile_id = jnp.minimum(jnp.arange(n_tiles, dtype=jnp.int32), n_used - 1)
    tile_expert = jnp.sum((tile_end[None, :] <= tile_id[:, None]).astype(jnp.int32), axis=1)
    tile_first = jnp.concatenate(
        [jnp.ones((1,), jnp.int32), (tile_expert[1:] != tile_expert[:-1]).astype(jnp.int32)])
    tile_valid = jnp.clip(cnt[tile_expert] - (tile_id - (tile_end - tiles)[tile_expert]) * tm, 0, tm)
    return (tile_expert, tile_first, tile_valid.astype(jnp.int32),
            n_used.reshape(1).astype(jnp.int32),
            src_rows.reshape(n_tiles, 1, tm), dst_rows.reshape(n_tiles, 1, tm), n_pairs)


def _final_kernel(h_ref, y1_ref, y2_ref, route_ref, g_ref, o_ref):
    lane = lax.broadcasted_iota(jnp.int32, route_ref.shape, 1)
    route = route_ref[...]
    w1 = jnp.sum(jnp.where(lane == ROUTE_W1, route, 0.0), axis=1, keepdims=True)
    w2 = jnp.sum(jnp.where(lane == ROUTE_W2, route, 0.0), axis=1, keepdims=True)
    h = h_ref[...] + (w1 * y1_ref[...] + w2 * y2_ref[...])
    ms = jnp.mean(h * h, axis=-1, keepdims=True)
    o_ref[...] = h * lax.rsqrt(ms + NORM_EPS) * g_ref[...]


def _final(h, y_moe, route, g_final, row0, seg_stride, seg_rows, n_seg, tm):
    nt, d = h.shape
    per_seg = seg_rows // tm
    assert all(x % SUBLANES == 0 for x in (row0, seg_stride, tm, nt))
    off = lambda s, c: row0 + s * seg_stride + c * tm
    rows = lambda width, base: pl.BlockSpec((pl.Element(tm), pl.Element(width)),
                                            lambda s, c: (pl.multiple_of(base + off(s, c), SUBLANES), 0))
    return pl.pallas_call(
        _final_kernel,
        grid=(n_seg, per_seg),
        in_specs=[rows(d, 0), rows(d, 0), rows(d, nt), rows(LANES, 0),
                  pl.BlockSpec((1, d), lambda s, c: (0, 0))],
        out_specs=pl.BlockSpec((tm, d), lambda s, c: (s * per_seg + c, 0)),
        out_shape=jax.ShapeDtypeStruct((n_seg * seg_rows, d), F32),
        compiler_params=_cparams(("parallel", "parallel")),
        name="final_norm",
    )(h, y_moe, y_moe, route, g_final)


def _pad_cols(w, to):
    return jnp.pad(w, ((0, 0), (0, to - w.shape[1])))


def _pad_rows(w, to):
    return jnp.pad(w, ((0, to - w.shape[0]), (0, 0)))


def _tile_plan(n_t, t_p, seq):
    plan = dict(
        inproj_rows=n_t // 6,
        inproj_cols=4 * LANES,
        seq_rows=t_p // 6,
        wkv_chunk=48, wkv_unroll=8,
        merge_rows=2 * LANES, moe_rows=2 * LANES, final_rows=4 * LANES)
    assert n_t % plan["inproj_rows"] == 0 and plan["inproj_rows"] % SUBLANES == 0
    assert t_p % plan["seq_rows"] == 0 and plan["seq_rows"] % SUBLANES == 0
    assert t_p % plan["wkv_chunk"] == 0 and plan["wkv_chunk"] % plan["wkv_unroll"] == 0
    assert n_t % plan["merge_rows"] == 0 and seq % plan["final_rows"] == 0
    return plan


def _s5_consts(lam_re, lam_im, log_dt, b_re, b_im, c_re, c_im, d, w_glu, b_glu):
    dt = jnp.exp(log_dt)[:, None]
    mag = jnp.exp(lam_re * dt)
    abar_re = mag * jnp.cos(lam_im * dt)
    abar_im = mag * jnp.sin(lam_im * dt)
    den = lam_re * lam_re + lam_im * lam_im
    nr = abar_re - 1.0
    coef_re = (nr * lam_re + abar_im * lam_im) / den
    coef_im = (abar_im * lam_re - nr * lam_im) / den
    bbar_re = coef_re[..., None] * b_re - coef_im[..., None] * b_im
    bbar_im = coef_re[..., None] * b_im + coef_im[..., None] * b_re
    n_g, n_p, n_c = b_re.shape
    gpb = LANES // n_c
    eye = jnp.eye(gpb, dtype=F32)

    def in_blk(bb):
        bb = bb.reshape(n_g // gpb, gpb, n_p, n_c)
        return jnp.einsum('kgpc,gh->kgchp', bb, eye).reshape(
            n_g // gpb, gpb * n_c, gpb * n_p).astype(BF16)

    def out_blk(cc):
        cc = cc.reshape(n_g // gpb, gpb, n_c, n_p)
        return jnp.einsum('kgcp,gh->khpgc', cc, eye).reshape(
            n_g // gpb, gpb * n_p, gpb * n_c).astype(BF16)

    return (in_blk(bbar_re), in_blk(bbar_im), out_blk(c_re), out_blk(c_im),
            d.reshape(1, -1), abar_re.reshape(-1, LANES), abar_im.reshape(-1, LANES),
            w_glu.astype(BF16), b_glu.reshape(1, -1))


def kernel(x_prompt, x_sample, state_shift, state_wkv, state_ssm_re, state_ssm_im, meta_tokens, g_mix, w_in, shift_mu, w0, w_decay_up, a0, w_aaa_up, w_gate_up, k_k, k_a, r_k, lnx_w, lnx_b, ssm_lam_re, ssm_lam_im, ssm_log_dt, ssm_b_re, ssm_b_im, ssm_c_re, ssm_c_im, ssm_d, w_glu, b_glu, w_br_a, w_br_b, w_out, g_ffn, w_router_grp, b_router_grp, w_router_exp, b_router_exp, w_exp_gate, w_exp_up, w_exp_down, g_final):
    depth = g_mix.shape[0]
    assert depth == 1, "single-layer trunk"
    bsz, seq, d = x_prompt.shape
    nb = x_sample.shape[0]
    assert x_sample.shape[1] == 1
    t_p = seq + N_META
    n_p = bsz * t_p
    width = k_k.shape[1]
    heads = width // HEAD
    n_dl, n_al, n_gl = w_decay_up.shape[1], w_aaa_up.shape[1], w_gate_up.shape[1]
    n_grp = w_router_grp.shape[2]
    n_exp = w_router_exp.shape[2]
    assert bsz * heads * 2 == LANES and (nb * heads) % LANES == 0

    s_row0 = -(-n_p // nb) * nb
    n_t = s_row0 + nb
    pieces = []
    for b in range(bsz):
        pieces += [meta_tokens, x_prompt[b]]
    x_all = jnp.concatenate(
        pieces + [jnp.zeros((s_row0 - n_p, d), F32), x_sample.reshape(nb, d)], axis=0)

    c_rkv = 3 * width
    c_xw, c_xa, c_xg = c_rkv, c_rkv + n_dl, c_rkv + n_dl + n_al
    c_u = c_xg + n_gl
    lo_width = 2 * LORA_PAD + n_gl

    def regroup(m):
        return jnp.concatenate(
            [m[:, :c_rkv], m[:, c_u:], _pad_cols(m[:, c_xw:c_xa], LORA_PAD),
             _pad_cols(m[:, c_xa:c_xg], LORA_PAD), m[:, c_xg:c_u]], axis=1)

    def ungroup(m):
        lo = c_rkv + width + 2 * d
        return jnp.concatenate(
            [m[:, :c_rkv], m[:, lo:lo + n_dl], m[:, lo + LORA_PAD:lo + LORA_PAD + n_al],
             m[:, lo + 2 * LORA_PAD:]], axis=1)

    w_in_p = regroup(w_in[0].astype(BF16))
    n_proj = w_in_p.shape[1]
    tiles = _tile_plan(n_t, t_p, seq)
    proj = _inproj(x_all, g_mix, w_in_p, tm=tiles["inproj_rows"], tn=tiles["inproj_cols"])

    mu_all = regroup(_pad_cols(shift_mu, c_u + width + 2 * d))
    mu_rkv, mu_lo = mu_all[:, :c_rkv], mu_all[:, n_proj - lo_width:]
    st_all = regroup(_pad_cols(state_shift[0], c_u + width + 2 * d))
    prev_s = (st_all[:, :c_rkv], st_all[:, n_proj - lo_width:])
    lo_col_block = (n_proj - lo_width) // lo_width
    prep_w = (mu_rkv, mu_lo, w0, a0,
              _pad_rows(w_decay_up[0], LORA_PAD).astype(BF16),
              _pad_rows(w_aaa_up[0], LORA_PAD).astype(BF16), w_gate_up[0].astype(BF16))

    tm_seq = tiles["seq_rows"]
    g_all = jnp.zeros((n_t, width), F32)
    rp, wp, kp, vp, ap, g_all = _rwkv_prep(proj, 0, n_p, tm_seq, bsz, width, lo_col_block,
                                                lo_width, *prep_w, g_all)
    rs, ws, ks, vs, as_, g_all = _rwkv_prep(proj, s_row0 // nb, nb, nb, 1, width,
                                                 lo_col_block, lo_width, *prep_w, g_all,
                                                 prev=prev_s)

    s5c = _s5_consts(ssm_lam_re[0], ssm_lam_im[0], ssm_log_dt[0], ssm_b_re[0], ssm_b_im[0],
                     ssm_c_re[0], ssm_c_im[0], ssm_d[0], w_glu[0], b_glu)
    n_g, n_st = ssm_lam_re.shape[1], ssm_lam_re.shape[2]
    u_col_block = c_rkv // width
    yb_all = jnp.zeros((n_t, width), BF16)
    yb_all, re_p, im_p = _s5(proj, 0, n_p, tm_seq, bsz, u_col_block, width, s5c, yb_all)
    h0 = (state_ssm_re[0].reshape(nb, n_g * n_st), state_ssm_im[0].reshape(nb, n_g * n_st))
    yb_all, re_s, im_s = _s5(proj, s_row0 // nb, nb, nb, 1, u_col_block, width, s5c, yb_all,
                             h0=h0)

    half = HEAD // 2

    def to_lanes_p(z):
        return z.reshape(t_p, bsz * heads, HEAD).transpose(0, 2, 1)

    def to_lanes_s(z):
        return z.reshape(nb, heads, HEAD).transpose(2, 0, 1).reshape(1, HEAD, nb * heads)

    def head_const(c):
        return jnp.tile(c.reshape(heads, HEAD).T, (1, LANES // heads))

    rk_l = head_const(r_k[0])
    kk_l, ka_l = head_const(k_k[0]), head_const(k_a[0])
    lw_p = jnp.broadcast_to(lnx_w[0].reshape(heads, 2, half).transpose(2, 1, 0)[:, :, None, :],
                            (half, 2, bsz, heads)).reshape(half, LANES)
    lb_p = jnp.broadcast_to(lnx_b[0].reshape(heads, 2, half).transpose(2, 1, 0)[:, :, None, :],
                            (half, 2, bsz, heads)).reshape(half, LANES)
    lw_s = jnp.tile(lnx_w[0].reshape(heads, HEAD).T, (1, LANES // heads))
    lb_s = jnp.tile(lnx_b[0].reshape(heads, HEAD).T, (1, LANES // heads))

    s0_p = jnp.zeros((HEAD, half, LANES), F32)
    y_p, sf_p = _wkv(to_lanes_p(rp), to_lanes_p(wp), to_lanes_p(kp), to_lanes_p(ap),
                     to_lanes_p(vp), s0_p, kk_l, ka_l, rk_l, lw_p, lb_p,
                     tc=tiles["wkv_chunk"], isplit=True, unroll=tiles["wkv_unroll"])
    s0_s = state_wkv[0].transpose(3, 2, 0, 1).reshape(HEAD, HEAD, nb * heads)
    y_s, sf_s = _wkv(to_lanes_s(rs), to_lanes_s(ws), to_lanes_s(ks), to_lanes_s(as_),
                     to_lanes_s(vs), s0_s, kk_l, ka_l, rk_l, lw_s, lb_s,
                     tc=1, isplit=False, unroll=1)

    ya_p = y_p.reshape(t_p, half, 2, bsz, heads).transpose(3, 0, 4, 2, 1).reshape(n_p, width)
    ya_s = y_s.reshape(HEAD, nb, heads).transpose(1, 2, 0).reshape(nb, width)
    wkv_p = sf_p.reshape(HEAD, half, 2, bsz, heads).transpose(3, 4, 2, 1, 0).reshape(
        1, bsz, heads, HEAD, HEAD)
    wkv_s = sf_s.reshape(HEAD, HEAD, nb, heads).transpose(2, 3, 1, 0)[None]

    def unify(a, b):
        return jnp.concatenate([a, jnp.zeros((s_row0 - n_p, a.shape[1]), a.dtype), b], axis=0)

    w_router = _pad_cols(jnp.concatenate([w_router_grp[0], w_router_exp[0]], axis=1), LANES)
    w_router_hi = w_router.astype(BF16)
    w_router = jnp.stack([w_router_hi, (w_router - w_router_hi.astype(F32)).astype(BF16)])
    b_router = _pad_cols(jnp.concatenate([b_router_grp, b_router_exp], axis=1), LANES)
    assert n_grp + n_exp <= LANES and n_exp == n_grp * EXPERTS_PER_GROUP
    h1, xn2, route = _merge(unify(ya_p, ya_s), g_all, yb_all, proj, x_all,
                           w_br_a[0].astype(BF16), w_br_b[0].astype(BF16), w_out[0].astype(BF16),
                           g_ffn, w_router, b_router, tm=tiles["merge_rows"],
                           ga_col_block=(c_rkv + width) // d,
                           n_grp=n_grp)
    tm_moe = tiles["moe_rows"]
    te, tf, tv, nu, src_rows, dst_rows, n_moe_rows = _moe_plan(route, n_exp, tm_moe)
    y_moe = _moe(xn2, te, tf, tv, nu, src_rows, dst_rows, w_exp_gate[0], w_exp_up[0], w_exp_down[0],
                 n_moe_rows, tm_moe)
    gfin = g_final.reshape(1, d)
    y_prompt = _final(h1, y_moe, route, gfin, N_META, t_p, seq, bsz,
                      tm=tiles["final_rows"]).reshape(bsz, seq, d)
    y_sample = _final(h1, y_moe, route, gfin, s_row0, 0, nb, 1, tm=nb).reshape(nb, 1, d)
    last_p = jnp.concatenate([proj[(b + 1) * t_p - 1:(b + 1) * t_p] for b in range(bsz)], axis=0)
    shift_p = ungroup(last_p)[None]
    shift_s = ungroup(proj[s_row0:])[None]
    return (y_prompt, y_sample, shift_p, wkv_p,
            re_p.reshape(1, bsz, n_g, n_st), im_p.reshape(1, bsz, n_g, n_st),
            shift_s, wkv_s,
            re_s.reshape(1, nb, n_g, n_st), im_s.reshape(1, nb, n_g, n_st))
```

```python
import functools

import jax
import jax.numpy as jnp
from jax import lax
from jax.experimental import pallas as pl
from jax.experimental.pallas import tpu as pltpu

F32 = jnp.float32
BF16 = jnp.bfloat16

NORM_EPS = 1e-6
LNX_EPS = 64e-5
N_META = 16
HEAD = 64
EXPERTS_PER_GROUP = 8
LANES = 128
SUBLANES = 8
LOG2_SUBLANES = SUBLANES.bit_length() - 1
LORA_PAD = 128
ROUTE_E1, ROUTE_E2, ROUTE_W1, ROUTE_W2 = 0, 1, 2, 3
VMEM_LIMIT = 56 * 1024 * 1024


def _cparams(sem):
    return pltpu.CompilerParams(dimension_semantics=sem, vmem_limit_bytes=VMEM_LIMIT)


def _const_spec(shape):
    nd = len(shape)
    return pl.BlockSpec(shape, lambda *_: (0,) * nd)


def _inproj_kernel(x_ref, g_ref, w_ref, o_ref, xn_ref):
    @pl.when(pl.program_id(1) == 0)
    def _():
        x = x_ref[...]
        ms = jnp.mean(x * x, axis=-1, keepdims=True)
        xn_ref[...] = (x * lax.rsqrt(ms + NORM_EPS) * g_ref[...]).astype(BF16)

    o_ref[...] = jnp.dot(xn_ref[...], w_ref[...], preferred_element_type=F32)


def _inproj(x_all, g_mix, w_in_p, tm, tn):
    nt, d = x_all.shape
    n_out = w_in_p.shape[1]
    return pl.pallas_call(
        _inproj_kernel,
        grid=(nt // tm, n_out // tn),
        in_specs=[pl.BlockSpec((tm, d), lambda i, j: (i, 0)),
                  pl.BlockSpec((1, d), lambda i, j: (0, 0)),
                  pl.BlockSpec((d, tn), lambda i, j: (0, j))],
        out_specs=pl.BlockSpec((tm, tn), lambda i, j: (i, j)),
        out_shape=jax.ShapeDtypeStruct((nt, n_out), F32),
        scratch_shapes=[pltpu.VMEM((tm, d), BF16)],
        compiler_params=_cparams(("parallel", "arbitrary")),
        name="inproj",
    )(x_all, g_mix, w_in_p)


def _softplus(z):
    return jnp.maximum(z, 0.0) + jnp.log1p(jnp.exp(-jnp.abs(z)))


def _rwkv_prep_kernel(*refs, width, carry_prev):
    if carry_prev:
        (rkv_ref, lo_ref, mu_rkv_ref, mu_lo_ref, w0_ref, a0_ref,
         wd_ref, wa_ref, wg_ref, _g_all,
         r_out, w_out, k_out, v_out, a_out, g_out, c_rkv, c_lo) = refs

        @pl.when(pl.program_id(1) == 0)
        def _():
            c_rkv[...] = jnp.zeros_like(c_rkv)
            c_lo[...] = jnp.zeros_like(c_lo)
    else:
        (rkv_ref, lo_ref, prev_rkv_ref, prev_lo_ref, mu_rkv_ref, mu_lo_ref, w0_ref, a0_ref,
         wd_ref, wa_ref, wg_ref, _g_all,
         r_out, w_out, k_out, v_out, a_out, g_out) = refs

    tm = rkv_ref.shape[0]
    first_row = lax.broadcasted_iota(jnp.int32, (tm, 1), 0) == 0

    def shifted(p, prev_ref, carry_ref, cols):
        if carry_prev:
            prev = jnp.where(first_row, carry_ref[:, cols], pltpu.roll(p, 1, 0))
        else:
            prev = prev_ref[:, cols]
        return prev

    def lerp(p, prev, mu):
        return p + (prev - p) * mu

    lo_cols = slice(0, lo_ref.shape[1])
    p_lo = lo_ref[...]
    q_lo = lerp(p_lo, shifted(p_lo, None if carry_prev else prev_lo_ref,
                              c_lo if carry_prev else None, lo_cols), mu_lo_ref[...])
    xw = q_lo[:, 0:LORA_PAD]
    xa = q_lo[:, LORA_PAD:2 * LORA_PAD]
    xg = q_lo[:, 2 * LORA_PAD:]
    dw = jnp.dot(jnp.tanh(xw).astype(BF16), wd_ref[...], preferred_element_type=F32)
    wlog = -_softplus(-(w0_ref[...] + dw)) - 0.5
    w_out[...] = jnp.exp(-jnp.exp(wlog))
    a = jax.nn.sigmoid(a0_ref[...] + jnp.dot(xa.astype(BF16), wa_ref[...],
                                             preferred_element_type=F32))
    a_out[...] = a
    g_out[...] = jnp.dot(jax.nn.sigmoid(xg).astype(BF16), wg_ref[...],
                         preferred_element_type=F32)

    def q_of(idx):
        cols = slice(idx * width, (idx + 1) * width)
        p = rkv_ref[:, cols]
        prev = shifted(p, None if carry_prev else prev_rkv_ref,
                       c_rkv if carry_prev else None, cols)
        return lerp(p, prev, mu_rkv_ref[:, cols])

    r_out[...] = q_of(0)
    k_out[...] = q_of(1)
    v_out[...] = q_of(2)

    if carry_prev:
        c_rkv[...] = rkv_ref[tm - 1:tm, :]
        c_lo[...] = lo_ref[tm - 1:tm, :]


def _rwkv_prep(proj, row_block0, n_rows, tm, seqs, width, lo_col_block, lo_width,
               mu_rkv, mu_lo, w0, a0, wd, wa, wg, g_all, prev=None):
    carry_prev = prev is None
    per_seq = n_rows // seqs // tm
    row_map = lambda b, c: (row_block0 + b * per_seq + c, 0)
    lo_map = lambda b, c: (row_block0 + b * per_seq + c, lo_col_block)
    out_map = lambda b, c: (b * per_seq + c, 0)
    in_specs = [pl.BlockSpec((tm, 3 * width), row_map), pl.BlockSpec((tm, lo_width), lo_map)]
    args = [proj, proj]
    if not carry_prev:
        in_specs += [pl.BlockSpec((tm, 3 * width), out_map), pl.BlockSpec((tm, lo_width), out_map)]
        args += list(prev)
    consts = [mu_rkv, mu_lo, w0, a0, wd, wa, wg]
    in_specs += [_const_spec(c.shape) for c in consts] + [pl.BlockSpec(memory_space=pl.ANY)]
    args += consts + [g_all]
    scratch = []
    if carry_prev:
        scratch = [pltpu.VMEM((1, 3 * width), F32), pltpu.VMEM((1, lo_width), F32)]
    out_sd = jax.ShapeDtypeStruct((n_rows // seqs, seqs * width), F32)
    tmaj_map = lambda b, c: (c, b)
    return pl.pallas_call(
        functools.partial(_rwkv_prep_kernel, width=width, carry_prev=carry_prev),
        grid=(seqs, per_seq),
        in_specs=in_specs,
        out_specs=[pl.BlockSpec((tm, width), tmaj_map)] * 5 + [pl.BlockSpec((tm, width), row_map)],
        out_shape=[out_sd] * 5 + [jax.ShapeDtypeStruct(g_all.shape, g_all.dtype)],
        input_output_aliases={len(args) - 1: 5},
        scratch_shapes=scratch,
        compiler_params=_cparams(("parallel", "arbitrary")),
        name="rwkv_prep_seq" if carry_prev else "rwkv_prep_step",
    )(*args)


def _wkv_kernel(r_ref, w_ref, kraw_ref, al_ref, v_ref, s0_ref, kkc_ref, kac_ref, rk_ref, lw_ref,
                lb_ref, y_ref, sf_ref, s_scr, a_scr, b_scr, k_src, *dup_scr,
                ni, nj, tc, isplit, unroll):
    @pl.when(pl.program_id(1) == 0)
    def _():
        s_scr[...] = s0_ref[...]

    def dup(x):
        return jnp.concatenate([x, x], axis=-1) if isplit else x

    if isplit:
        w_src, r_src, v_src = dup_scr
    else:
        w_src, r_src, v_src = w_ref, r_ref, v_ref

    def prep(t, carry):
        kraw = dup(kraw_ref[t])
        al = dup(al_ref[t])
        kk = kraw * kkc_ref[...]
        k_src[t] = kraw * (1.0 + (al - 1.0) * kac_ref[...])
        ss = jnp.sum(kk * kk, axis=0, keepdims=True)
        kkn = kk / jnp.maximum(jnp.sqrt(ss), 1e-12)
        a_scr[t] = -kkn
        b_scr[t] = kkn * al
        if isplit:
            w_src[t] = dup(w_ref[t])
            r_src[t] = dup(r_ref[t])
            v = v_ref[t]
            v_src[t] = jnp.concatenate([v[:ni], v[ni:]], axis=-1)
        return carry

    lax.fori_loop(0, tc, prep, 0, unroll=unroll)

    def row(ref, t, j):
        return ref[t, j:j + 1, :]

    def tree(parts):
        while len(parts) > 1:
            parts = [parts[i] + parts[i + 1] for i in range(0, len(parts), 2)]
        return parts[0]

    n_acc = 4
    sa0 = tree([sum(s_scr[j] * row(a_scr, 0, j) for j in range(q, nj, n_acc))
                for q in range(n_acc)])

    def step(t, sa):
        v = v_src[t]
        tn = jnp.minimum(t + 1, tc - 1)
        y = [None] * n_acc
        san = [None] * n_acc
        for j in range(nj):
            s = s_scr[j] * row(w_src, t, j) + sa * row(b_scr, t, j) + v * row(k_src, t, j)
            s_scr[j] = s
            yj = s * row(r_src, t, j)
            sj = s * row(a_scr, tn, j)
            q = j % n_acc
            y[q] = yj if y[q] is None else y[q] + yj
            san[q] = sj if san[q] is None else san[q] + sj
        y_ref[t] = tree(y)
        return tree(san)

    lax.fori_loop(0, tc, step, sa0)

    def isum(x):
        s = jnp.broadcast_to(jnp.sum(x, axis=0, keepdims=True), (8, LANES))
        if isplit:
            s = s + pltpu.roll(s, LANES // 2, 1)
        return s[0:1]

    def post(t, carry):
        y = y_ref[t]
        v = v_src[t]
        mu = isum(y) * (1.0 / HEAD)
        d = y - mu
        var = isum(d * d) * (1.0 / HEAD)
        yn = d * lax.rsqrt(var + LNX_EPS) * lw_ref[...] + lb_ref[...]
        bonus = jnp.sum(r_src[t] * k_src[t] * rk_ref[...], axis=0, keepdims=True)
        y_ref[t] = yn + bonus * v
        return carry

    lax.fori_loop(0, tc, post, 0, unroll=unroll)

    @pl.when(pl.program_id(1) == pl.num_programs(1) - 1)
    def _():
        sf_ref[...] = s_scr[...]


def _wkv(r, w, kraw, al, v, s0, kkc, kac, rk, lw, lb, tc, isplit, unroll):
    t, nj, jl = r.shape
    ni, lanes = s0.shape[1:]
    assert jl == (LANES // 2 if isplit else lanes) and v.shape == r.shape
    jspec = pl.BlockSpec((tc, nj, min(jl, LANES)), lambda l, c: (c, 0, l))
    dup_scr = []
    if isplit:
        dup_scr = [pltpu.VMEM((tc, nj, LANES), F32)] * 2 + [pltpu.VMEM((tc, ni, LANES), F32)]
    ispec = pl.BlockSpec((tc, ni, LANES), lambda l, c: (c, 0, l))
    sspec = pl.BlockSpec((nj, ni, LANES), lambda l, c: (0, 0, l))
    consts = [kkc, kac, rk, lw, lb]
    return pl.pallas_call(
        functools.partial(_wkv_kernel, ni=ni, nj=nj, tc=tc, isplit=isplit, unroll=unroll),
        grid=(lanes // LANES, t // tc),
        in_specs=[jspec, jspec, jspec, jspec, jspec, sspec] + [_const_spec(c.shape) for c in consts],
        out_specs=[ispec, sspec],
        out_shape=[jax.ShapeDtypeStruct((t, ni, lanes), F32),
                   jax.ShapeDtypeStruct((nj, ni, lanes), F32)],
        scratch_shapes=[pltpu.VMEM((nj, ni, LANES), F32),
                        pltpu.VMEM((tc, nj, LANES), F32),
                        pltpu.VMEM((tc, nj, LANES), F32),
                        pltpu.VMEM((tc, nj, LANES), F32)] + dup_scr,
        compiler_params=_cparams(("parallel", "arbitrary")),
        name="wkv_seq" if isplit else "wkv_step",
    )(r, w, kraw, al, v, s0, *consts)


def _s5_kernel(*refs, sequential, n_blk, pitch, unroll):
    if sequential:
        (u_ref, bre_ref, bim_ref, cre_ref, cim_ref, d_ref, are_ref, aim_ref, wglu_ref, bglu_ref,
         _yb_all, yb_ref, hre_out, him_out, st_re, st_im, c_re, c_im) = refs
    else:
        (u_ref, h0re_ref, h0im_ref, bre_ref, bim_ref, cre_ref, cim_ref, d_ref, are_ref, aim_ref,
         wglu_ref, bglu_ref, _yb_all, yb_ref, hre_out, him_out, st_re, st_im) = refs

    tm = u_ref.shape[0]
    kin = bre_ref.shape[1]
    kst = bre_ref.shape[2]
    tiles_per_blk = kst // LANES
    n_tiles = n_blk * tiles_per_blk
    u = u_ref[...]
    ub = u.astype(BF16)

    def tile_rows(k):
        return slice(k * pitch, k * pitch + tm)

    for kb in range(n_blk):
        ukb = ub[:, kb * kin:(kb + 1) * kin]
        bu_re = jnp.dot(ukb, bre_ref[kb], preferred_element_type=F32)
        bu_im = jnp.dot(ukb, bim_ref[kb], preferred_element_type=F32)
        for n in range(tiles_per_blk):
            k = kb * tiles_per_blk + n
            cols = slice(n * LANES, (n + 1) * LANES)
            if sequential:
                st_re[tile_rows(k), :] = bu_re[:, cols]
                st_im[tile_rows(k), :] = bu_im[:, cols]
            else:
                kc = slice(k * LANES, (k + 1) * LANES)
                ar, ai = are_ref[k:k + 1, :], aim_ref[k:k + 1, :]
                h0r, h0i = h0re_ref[:, kc], h0im_ref[:, kc]
                nr = bu_re[:, cols] + (ar * h0r - ai * h0i)
                ni = bu_im[:, cols] + (ar * h0i + ai * h0r)
                st_re[tile_rows(k), :] = nr
                st_im[tile_rows(k), :] = ni
                hre_out[:, kc] = nr
                him_out[:, kc] = ni

    if sequential:
        @pl.when(pl.program_id(1) == 0)
        def _():
            c_re[...] = jnp.zeros_like(c_re)
            c_im[...] = jnp.zeros_like(c_im)

        ar = are_ref[...]
        ai = aim_ref[...]
        groups = range(n_tiles // SUBLANES)

        def token_rows(t, m):
            return pl.ds(t + m * SUBLANES * pitch, SUBLANES, stride=pitch)

        def step(t, h):
            hr, hi = h
            bur = jnp.concatenate([st_re[token_rows(t, m), :] for m in groups], axis=0)
            bui = jnp.concatenate([st_im[token_rows(t, m), :] for m in groups], axis=0)
            nr = ar * hr - ai * hi + bur
            ni = ar * hi + ai * hr + bui
            for m in groups:
                st_re[token_rows(t, m), :] = nr[m * SUBLANES:(m + 1) * SUBLANES]
                st_im[token_rows(t, m), :] = ni[m * SUBLANES:(m + 1) * SUBLANES]
            return nr, ni

        hr, hi = lax.fori_loop(0, tm, step, (c_re[...], c_im[...]), unroll=unroll)
        c_re[...] = hr
        c_im[...] = hi
        hre_out[0] = hr
        him_out[0] = hi

    ys = []
    for kb in range(n_blk):
        tiles = range(kb * tiles_per_blk, (kb + 1) * tiles_per_blk)
        h_re = jnp.concatenate([st_re[tile_rows(k), :] for k in tiles], axis=1)
        h_im = jnp.concatenate([st_im[tile_rows(k), :] for k in tiles], axis=1)
        yre = jnp.dot(h_re.astype(BF16), cre_ref[kb], preferred_element_type=F32)
        yim = jnp.dot(h_im.astype(BF16), cim_ref[kb], preferred_element_type=F32)
        ys.append(yre - yim)
    y = jnp.concatenate(ys, axis=1) + d_ref[...] * u
    y = jax.nn.gelu(y)
    gate = jnp.dot(y.astype(BF16), wglu_ref[...], preferred_element_type=F32) + bglu_ref[...]
    yb_ref[...] = (y * jax.nn.sigmoid(gate)).astype(BF16)


def _s5(proj, row_block0, n_rows, tm, seqs, u_col_block, width, consts, yb_all, h0=None):
    sequential = h0 is None
    bre = consts[0]
    n_blk, _, kst = bre.shape
    n_state = n_blk * kst
    n_tiles = n_state // LANES
    assert tm % SUBLANES == 0 and n_tiles % SUBLANES == 0
    pitch = tm
    per_seq = n_rows // seqs // tm
    u_map = lambda b, c: (row_block0 + b * per_seq + c, u_col_block)
    out_map = lambda b, c: (b * per_seq + c, 0)
    row_map = lambda b, c: (row_block0 + b * per_seq + c, 0)
    in_specs = [pl.BlockSpec((tm, width), u_map)]
    args = [proj]
    if not sequential:
        in_specs += [pl.BlockSpec((tm, n_state), out_map)] * 2
        args += list(h0)
    in_specs += [_const_spec(c.shape) for c in consts] + [pl.BlockSpec(memory_space=pl.ANY)]
    args += list(consts) + [yb_all]
    scratch = [pltpu.VMEM((n_tiles * pitch, LANES), F32), pltpu.VMEM((n_tiles * pitch, LANES), F32)]
    if sequential:
        scratch += [pltpu.VMEM((n_tiles, LANES), F32), pltpu.VMEM((n_tiles, LANES), F32)]
        st_spec = pl.BlockSpec((1, n_tiles, LANES), lambda b, c: (b, 0, 0))
        st_shape = jax.ShapeDtypeStruct((seqs, n_tiles, LANES), F32)
    else:
        st_spec = pl.BlockSpec((tm, n_state), out_map)
        st_shape = jax.ShapeDtypeStruct((n_rows, n_state), F32)
    return pl.pallas_call(
        functools.partial(_s5_kernel, sequential=sequential, n_blk=n_blk, pitch=pitch, unroll=4),
        grid=(seqs, per_seq),
        in_specs=in_specs,
        out_specs=[pl.BlockSpec((tm, width), row_map), st_spec, st_spec],
        out_shape=[jax.ShapeDtypeStruct(yb_all.shape, yb_all.dtype), st_shape, st_shape],
        input_output_aliases={len(args) - 1: 0},
        scratch_shapes=scratch,
        compiler_params=_cparams(("parallel", "arbitrary")),
        name="s5_seq" if sequential else "s5_step",
    )(*args)


def _route(logits, n_grp):
    lane = lax.broadcasted_iota(jnp.int32, logits.shape, 1).astype(F32)
    neg = jnp.float32(-1e30)
    big = jnp.float32(1e9)
    is_grp = lane < n_grp
    gl = jnp.where(is_grp, logits, neg)
    gmax = jnp.max(gl, axis=1, keepdims=True)
    gsum = jnp.sum(jnp.where(is_grp, jnp.exp(gl - gmax), 0.0), axis=1, keepdims=True)
    g_p = 1.0 / gsum
    g_idx = jnp.min(jnp.where(is_grp & (gl == gmax), lane, big), axis=1, keepdims=True)
    lo = n_grp + g_idx * EXPERTS_PER_GROUP
    in_grp = (lane >= lo) & (lane < lo + EXPERTS_PER_GROUP)
    el = jnp.where(in_grp, logits, neg)
    v1 = jnp.max(el, axis=1, keepdims=True)
    i1 = jnp.min(jnp.where(in_grp & (el == v1), lane, big), axis=1, keepdims=True)
    rest = in_grp & (lane != i1)
    el2 = jnp.where(rest, logits, neg)
    v2 = jnp.max(el2, axis=1, keepdims=True)
    i2 = jnp.min(jnp.where(rest & (el2 == v2), lane, big), axis=1, keepdims=True)
    e2 = jnp.exp(v2 - v1)
    w1 = g_p / (1.0 + e2)
    w2 = g_p * e2 / (1.0 + e2)
    return (jnp.where(lane == ROUTE_E1, i1 - n_grp, 0.0) + jnp.where(lane == ROUTE_E2, i2 - n_grp, 0.0)
            + jnp.where(lane == ROUTE_W1, w1, 0.0) + jnp.where(lane == ROUTE_W2, w2, 0.0))


def _merge_kernel(ya_ref, g_ref, yb_ref, ga_ref, gb_ref, x_ref, wa_ref, wb_ref, wo_ref,
                  gf_ref, wr_ref, br_ref, h_out, xn_out, comb_out, *, n_grp):
    ya = (ya_ref[...] * g_ref[...]).astype(BF16)
    ma = jnp.dot(ya, wa_ref[...], preferred_element_type=F32)
    mb = jnp.dot(yb_ref[...], wb_ref[...], preferred_element_type=F32)
    merged = jax.nn.sigmoid(ga_ref[...]) * ma + jax.nn.sigmoid(gb_ref[...]) * mb
    h = x_ref[...] + jnp.dot(merged.astype(BF16), wo_ref[...], preferred_element_type=F32)
    h_out[...] = h
    ms = jnp.mean(h * h, axis=-1, keepdims=True)
    xn = h * lax.rsqrt(ms + NORM_EPS) * gf_ref[...]
    xn_out[...] = xn
    x_hi = xn.astype(BF16)
    x_lo = (xn - x_hi.astype(F32)).astype(BF16)
    w_hi, w_lo = wr_ref[0], wr_ref[1]
    logits = (jnp.dot(x_hi, w_hi, preferred_element_type=F32)
              + (jnp.dot(x_hi, w_lo, preferred_element_type=F32)
                 + jnp.dot(x_lo, w_hi, preferred_element_type=F32))) + br_ref[...]
    comb_out[...] = _route(logits, n_grp)


def _merge(ya, g, yb, proj, x_all, w_br_a, w_br_b, w_out, g_ffn, w_router, b_router,
           tm, ga_col_block, n_grp):
    nt, d = x_all.shape
    wdt = ya.shape[1]
    row = lambda i: (i, 0)
    single = dict(pipeline_mode=pl.Buffered(1))
    in_specs = [pl.BlockSpec((tm, wdt), row), pl.BlockSpec((tm, wdt), row),
                pl.BlockSpec((tm, wdt), row),
                pl.BlockSpec((tm, d), lambda i: (i, ga_col_block)),
                pl.BlockSpec((tm, d), lambda i: (i, ga_col_block + 1)),
                pl.BlockSpec((tm, d), row),
                pl.BlockSpec(w_br_a.shape, lambda i: (0, 0), **single),
                pl.BlockSpec(w_br_b.shape, lambda i: (0, 0), **single),
                pl.BlockSpec(w_out.shape, lambda i: (0, 0), **single),
                _const_spec(g_ffn.shape), _const_spec(w_router.shape), _const_spec(b_router.shape)]
    return pl.pallas_call(
        functools.partial(_merge_kernel, n_grp=n_grp),
        grid=(nt // tm,),
        in_specs=in_specs,
        out_specs=[pl.BlockSpec((tm, d), row), pl.BlockSpec((tm, d), row),
                   pl.BlockSpec((tm, LANES), row)],
        out_shape=[jax.ShapeDtypeStruct((nt, d), F32), jax.ShapeDtypeStruct((nt, d), F32),
                   jax.ShapeDtypeStruct((nt, LANES), F32)],
        compiler_params=_cparams(("parallel",)),
        name="merge_route",
    )(ya, g, yb, proj, proj, x_all, w_br_a, w_br_b, w_out, g_ffn, w_router, b_router)


def _moe_kernel(te_ref, first_ref, nvalid_ref, nused_ref, src_cur, src_nxt, dst_cur,
                xn_hbm, wg_ref, wu_ref, wd_ref, y_hbm,
                xbuf, obuf, wg_bf, wu_bf, wd_bf, gsem, ssem, *, tm):
    i = pl.program_id(0)
    last = pl.num_programs(0) - 1
    n_used = nused_ref[0]
    slot = lax.rem(i, 2)

    def gather_start(src, s):
        def block(i, c):
            for u in range(SUBLANES):
                r = i * SUBLANES + u
                pltpu.make_async_copy(xn_hbm.at[pl.ds(src[0, 0, r], 1)], xbuf.at[s, pl.ds(r, 1)],
                                      gsem.at[s]).start(priority=u % 2)
            return c
        lax.fori_loop(0, tm // SUBLANES, block, 0)

    def gather_wait(s):
        pltpu.make_async_copy(xn_hbm.at[pl.ds(0, tm)], xbuf.at[s], gsem.at[s]).wait()

    def scatter_start(s, n_rows):
        def row(r, priority):
            pltpu.make_async_copy(obuf.at[s, pl.ds(r, 1)], y_hbm.at[pl.ds(dst_cur[0, 0, r], 1)],
                                  ssem.at[s]).start(priority=priority)

        def block(i, c):
            for u in range(SUBLANES):
                row(i * SUBLANES + u, u % 2)
            return c

        def single(r, c):
            row(r, 0)
            return c

        n_blocks = lax.shift_right_logical(n_rows, LOG2_SUBLANES)
        lax.fori_loop(0, n_blocks, block, 0)
        lax.fori_loop(n_blocks * SUBLANES, n_rows, single, 0)

    def scatter_wait(s, n_rows):
        n_full = pl.multiple_of(
            lax.shift_left(lax.shift_right_logical(n_rows, LOG2_SUBLANES), LOG2_SUBLANES), SUBLANES)

        @pl.when(n_full > 0)
        def _():
            pltpu.make_async_copy(obuf.at[s, pl.ds(0, n_full)], y_hbm.at[pl.ds(0, n_full)],
                                  ssem.at[s]).wait()

        def body(r, c):
            pltpu.make_async_copy(obuf.at[s, pl.ds(r, 1)], y_hbm.at[pl.ds(r, 1)],
                                  ssem.at[s]).wait()
            return c
        lax.fori_loop(n_full, n_rows, body, 0)

    @pl.when(i == 0)
    def _():
        gather_start(src_cur, 0)

    @pl.when(i < n_used)
    def _():
        gather_wait(slot)

    @pl.when(i + 1 < n_used)
    def _():
        gather_start(src_nxt, 1 - slot)

    @pl.when((i >= 2) & (i - 2 < n_used))
    def _():
        scatter_wait(slot, nvalid_ref[i - 2])

    @pl.when(i < n_used)
    def _():
        @pl.when(first_ref[i] == 1)
        def _():
            wg_bf[...] = wg_ref[0].astype(BF16)
            wu_bf[...] = wu_ref[0].astype(BF16)
            wd_bf[...] = wd_ref[0].astype(BF16)

        x = xbuf[slot].astype(BF16)
        xg = jnp.dot(x, wg_bf[...], preferred_element_type=F32)
        xu = jnp.dot(x, wu_bf[...], preferred_element_type=F32)
        hid = (jax.nn.silu(xg) * xu).astype(BF16)
        obuf[slot] = jnp.dot(hid, wd_bf[...], preferred_element_type=F32)
        scatter_start(slot, nvalid_ref[i])

    @pl.when(i == last)
    def _():
        @pl.when((i >= 1) & (i - 1 < n_used))
        def _():
            scatter_wait(1 - slot, nvalid_ref[i - 1])

        @pl.when(i < n_used)
        def _():
            scatter_wait(slot, nvalid_ref[i])


def _moe(xn, tile_expert, tile_first, tile_valid, n_used, src_rows, dst_rows, wg, wu, wd,
         n_out_rows, tm):
    nt, d = xn.shape
    n_exp, _, de = wg.shape
    n_tiles = src_rows.shape[0]
    smem_cur = pl.BlockSpec((1, 1, tm), lambda i, *_: (i, 0, 0), memory_space=pltpu.SMEM)
    smem_nxt = pl.BlockSpec((1, 1, tm), lambda i, *_: (jnp.minimum(i + 1, n_tiles - 1), 0, 0),
                            memory_space=pltpu.SMEM)
    grid_spec = pltpu.PrefetchScalarGridSpec(
        num_scalar_prefetch=4,
        grid=(n_tiles,),
        in_specs=[smem_cur, smem_nxt, smem_cur,
                  pl.BlockSpec(memory_space=pl.ANY),
                  pl.BlockSpec((1, d, de), lambda i, te, *_: (te[i], 0, 0)),
                  pl.BlockSpec((1, d, de), lambda i, te, *_: (te[i], 0, 0)),
                  pl.BlockSpec((1, de, d), lambda i, te, *_: (te[i], 0, 0))],
        out_specs=pl.BlockSpec(memory_space=pl.ANY),
        scratch_shapes=[pltpu.VMEM((2, tm, d), F32), pltpu.VMEM((2, tm, d), F32),
                        pltpu.VMEM((d, de), BF16), pltpu.VMEM((d, de), BF16),
                        pltpu.VMEM((de, d), BF16),
                        pltpu.SemaphoreType.DMA((2,)), pltpu.SemaphoreType.DMA((2,))])
    return pl.pallas_call(
        functools.partial(_moe_kernel, tm=tm),
        grid_spec=grid_spec,
        out_shape=jax.ShapeDtypeStruct((n_out_rows, d), F32),
        compiler_params=_cparams(("arbitrary",)),
        name="moe_grouped",
    )(tile_expert, tile_first, tile_valid, n_used, src_rows, src_rows, dst_rows, xn, wg, wu, wd)


def _moe_plan(route, n_exp, tm):
    nt = route.shape[0]
    n_pairs = 2 * nt
    n_tiles = n_pairs // tm + n_exp
    eid = jnp.concatenate([route[:, ROUTE_E1], route[:, ROUTE_E2]]).astype(jnp.int32)
    onehot = (eid[:, None] == jnp.arange(n_exp, dtype=jnp.int32)[None, :]).astype(jnp.int32)
    csum = jnp.cumsum(onehot, axis=0)
    rank = jnp.take_along_axis(csum, eid[:, None], axis=1)[:, 0] - 1
    cnt = csum[-1]
    tiles = (cnt + tm - 1) // tm
    tile_end = jnp.cumsum(tiles)
    n_used = tile_end[-1]
    pos = (tile_end - tiles)[eid] * tm + rank
    pair = jnp.arange(n_pairs, dtype=jnp.int32)
    dst_rows = jnp.zeros((n_tiles * tm,), jnp.int32).at[pos].set(
        pair, unique_indices=True, mode="promise_in_bounds")
    src_rows = dst_rows % nt
    tile_id = jnp.minimum(jnp.arange(n_tiles, dtype=jnp.int32), n_used - 1)
    tile_expert = jnp.sum((tile_end[None, :] <= tile_id[:, None]).astype(jnp.int32), axis=1)
    tile_first = jnp.concatenate(
        [jnp.ones((1,), jnp.int32), (tile_expert[1:] != tile_expert[:-1]).astype(jnp.int32)])
    tile_valid = jnp.clip(cnt[tile_expert] - (tile_id - (tile_end - tiles)[tile_expert]) * tm, 0, tm)
    return (tile_expert, tile_first, tile_valid.astype(jnp.int32),
            n_used.reshape(1).astype(jnp.int32),
            src_rows.reshape(n_tiles, 1, tm), dst_rows.reshape(n_tiles, 1, tm), n_pairs)


def _final_kernel(h_ref, y1_ref, y2_ref, route_ref, g_ref, o_ref):
    lane = lax.broadcasted_iota(jnp.int32, route_ref.shape, 1)
    route = route_ref[...]
    w1 = jnp.sum(jnp.where(lane == ROUTE_W1, route, 0.0), axis=1, keepdims=True)
    w2 = jnp.sum(jnp.where(lane == ROUTE_W2, route, 0.0), axis=1, keepdims=True)
    h = h_ref[...] + (w1 * y1_ref[...] + w2 * y2_ref[...])
    ms = jnp.mean(h * h, axis=-1, keepdims=True)
    o_ref[...] = h * lax.rsqrt(ms + NORM_EPS) * g_ref[...]


def _final(h, y_moe, route, g_final, row0, seg_stride, seg_rows, n_seg, tm):
    nt, d = h.shape
    per_seg = seg_rows // tm
    assert all(x % SUBLANES == 0 for x in (row0, seg_stride, tm, nt))
    off = lambda s, c: row0 + s * seg_stride + c * tm
    rows = lambda width, base: pl.BlockSpec((pl.Element(tm), pl.Element(width)),
                                            lambda s, c: (pl.multiple_of(base + off(s, c), SUBLANES), 0))
    return pl.pallas_call(
        _final_kernel,
        grid=(n_seg, per_seg),
        in_specs=[rows(d, 0), rows(d, 0), rows(d, nt), rows(LANES, 0),
                  pl.BlockSpec((1, d), lambda s, c: (0, 0))],
        out_specs=pl.BlockSpec((tm, d), lambda s, c: (s * per_seg + c, 0)),
        out_shape=jax.ShapeDtypeStruct((n_seg * seg_rows, d), F32),
        compiler_params=_cparams(("parallel", "parallel")),
        name="final_norm",
    )(h, y_moe, y_moe, route, g_final)


def _pad_cols(w, to):
    return jnp.pad(w, ((0, 0), (0, to - w.shape[1])))


def _pad_rows(w, to):
    return jnp.pad(w, ((0, to - w.shape[0]), (0, 0)))


def _tile_plan(n_t, t_p, seq):
    plan = dict(
        inproj_rows=n_t // 6,
        inproj_cols=4 * LANES,
        seq_rows=t_p // 6,
        wkv_chunk=48, wkv_unroll=8,
        merge_rows=2 * LANES, moe_rows=2 * LANES, final_rows=4 * LANES)
    assert n_t % plan["inproj_rows"] == 0 and plan["inproj_rows"] % SUBLANES == 0
    assert t_p % plan["seq_rows"] == 0 and plan["seq_rows"] % SUBLANES == 0
    assert t_p % plan["wkv_chunk"] == 0 and plan["wkv_chunk"] % plan["wkv_unroll"] == 0
    assert n_t % plan["merge_rows"] == 0 and seq % plan["final_rows"] == 0
    return plan


def _s5_consts(lam_re, lam_im, log_dt, b_re, b_im, c_re, c_im, d, w_glu, b_glu):
    dt = jnp.exp(log_dt)[:, None]
    mag = jnp.exp(lam_re * dt)
    abar_re = mag * jnp.cos(lam_im * dt)
    abar_im = mag * jnp.sin(lam_im * dt)
    den = lam_re * lam_re + lam_im * lam_im
    nr = abar_re - 1.0
    coef_re = (nr * lam_re + abar_im * lam_im) / den
    coef_im = (abar_im * lam_re - nr * lam_im) / den
    bbar_re = coef_re[..., None] * b_re - coef_im[..., None] * b_im
    bbar_im = coef_re[..., None] * b_im + coef_im[..., None] * b_re
    n_g, n_p, n_c = b_re.shape
    gpb = LANES // n_c
    eye = jnp.eye(gpb, dtype=F32)

    def in_blk(bb):
        bb = bb.reshape(n_g // gpb, gpb, n_p, n_c)
        return jnp.einsum('kgpc,gh->kgchp', bb, eye).reshape(
            n_g // gpb, gpb * n_c, gpb * n_p).astype(BF16)

    def out_blk(cc):
        cc = cc.reshape(n_g // gpb, gpb, n_c, n_p)
        return jnp.einsum('kgcp,gh->khpgc', cc, eye).reshape(
            n_g // gpb, gpb * n_p, gpb * n_c).astype(BF16)

    return (in_blk(bbar_re), in_blk(bbar_im), out_blk(c_re), out_blk(c_im),
            d.reshape(1, -1), abar_re.reshape(-1, LANES), abar_im.reshape(-1, LANES),
            w_glu.astype(BF16), b_glu.reshape(1, -1))


def kernel(x_prompt, x_sample, state_shift, state_wkv, state_ssm_re, state_ssm_im, meta_tokens, g_mix, w_in, shift_mu, w0, w_decay_up, a0, w_aaa_up, w_gate_up, k_k, k_a, r_k, lnx_w, lnx_b, ssm_lam_re, ssm_lam_im, ssm_log_dt, ssm_b_re, ssm_b_im, ssm_c_re, ssm_c_im, ssm_d, w_glu, b_glu, w_br_a, w_br_b, w_out, g_ffn, w_router_grp, b_router_grp, w_router_exp, b_router_exp, w_exp_gate, w_exp_up, w_exp_down, g_final):
    depth = g_mix.shape[0]
    assert depth == 1, "single-layer trunk"
    bsz, seq, d = x_prompt.shape
    nb = x_sample.shape[0]
    assert x_sample.shape[1] == 1
    t_p = seq + N_META
    n_p = bsz * t_p
    width = k_k.shape[1]
    heads = width // HEAD
    n_dl, n_al, n_gl = w_decay_up.shape[1], w_aaa_up.shape[1], w_gate_up.shape[1]
    n_grp = w_router_grp.shape[2]
    n_exp = w_router_exp.shape[2]
    assert bsz * heads * 2 == LANES and (nb * heads) % LANES == 0

    s_row0 = -(-n_p // nb) * nb
    n_t = s_row0 + nb
    pieces = []
    for b in range(bsz):
        pieces += [meta_tokens, x_prompt[b]]
    x_all = jnp.concatenate(
        pieces + [jnp.zeros((s_row0 - n_p, d), F32), x_sample.reshape(nb, d)], axis=0)

    c_rkv = 3 * width
    c_xw, c_xa, c_xg = c_rkv, c_rkv + n_dl, c_rkv + n_dl + n_al
    c_u = c_xg + n_gl
    lo_width = 2 * LORA_PAD + n_gl

    def regroup(m):
        return jnp.concatenate(
            [m[:, :c_rkv], m[:, c_u:], _pad_cols(m[:, c_xw:c_xa], LORA_PAD),
             _pad_cols(m[:, c_xa:c_xg], LORA_PAD), m[:, c_xg:c_u]], axis=1)

    def ungroup(m):
        lo = c_rkv + width + 2 * d
        return jnp.concatenate(
            [m[:, :c_rkv], m[:, lo:lo + n_dl], m[:, lo + LORA_PAD:lo + LORA_PAD + n_al],
             m[:, lo + 2 * LORA_PAD:]], axis=1)

    w_in_p = regroup(w_in[0].astype(BF16))
    n_proj = w_in_p.shape[1]
    tiles = _tile_plan(n_t, t_p, seq)
    proj = _inproj(x_all, g_mix, w_in_p, tm=tiles["inproj_rows"], tn=tiles["inproj_cols"])

    mu_all = regroup(_pad_cols(shift_mu, c_u + width + 2 * d))
    mu_rkv, mu_lo = mu_all[:, :c_rkv], mu_all[:, n_proj - lo_width:]
    st_all = regroup(_pad_cols(state_shift[0], c_u + width + 2 * d))
    prev_s = (st_all[:, :c_rkv], st_all[:, n_proj - lo_width:])
    lo_col_block = (n_proj - lo_width) // lo_width
    prep_w = (mu_rkv, mu_lo, w0, a0,
              _pad_rows(w_decay_up[0], LORA_PAD).astype(BF16),
              _pad_rows(w_aaa_up[0], LORA_PAD).astype(BF16), w_gate_up[0].astype(BF16))

    tm_seq = tiles["seq_rows"]
    g_all = jnp.zeros((n_t, width), F32)
    rp, wp, kp, vp, ap, g_all = _rwkv_prep(proj, 0, n_p, tm_seq, bsz, width, lo_col_block,
                                                lo_width, *prep_w, g_all)
    rs, ws, ks, vs, as_, g_all = _rwkv_prep(proj, s_row0 // nb, nb, nb, 1, width,
                                                 lo_col_block, lo_width, *prep_w, g_all,
                                                 prev=prev_s)

    s5c = _s5_consts(ssm_lam_re[0], ssm_lam_im[0], ssm_log_dt[0], ssm_b_re[0], ssm_b_im[0],
                     ssm_c_re[0], ssm_c_im[0], ssm_d[0], w_glu[0], b_glu)
    n_g, n_st = ssm_lam_re.shape[1], ssm_lam_re.shape[2]
    u_col_block = c_rkv // width
    yb_all = jnp.zeros((n_t, width), BF16)
    yb_all, re_p, im_p = _s5(proj, 0, n_p, tm_seq, bsz, u_col_block, width, s5c, yb_all)
    h0 = (state_ssm_re[0].reshape(nb, n_g * n_st), state_ssm_im[0].reshape(nb, n_g * n_st))
    yb_all, re_s, im_s = _s5(proj, s_row0 // nb, nb, nb, 1, u_col_block, width, s5c, yb_all,
                             h0=h0)

    half = HEAD // 2

    def to_lanes_p(z):
        return z.reshape(t_p, bsz * heads, HEAD).transpose(0, 2, 1)

    def to_lanes_s(z):
        return z.reshape(nb, heads, HEAD).transpose(2, 0, 1).reshape(1, HEAD, nb * heads)

    def head_const(c):
        return jnp.tile(c.reshape(heads, HEAD).T, (1, LANES // heads))

    rk_l = head_const(r_k[0])
    kk_l, ka_l = head_const(k_k[0]), head_const(k_a[0])
    lw_p = jnp.broadcast_to(lnx_w[0].reshape(heads, 2, half).transpose(2, 1, 0)[:, :, None, :],
                            (half, 2, bsz, heads)).reshape(half, LANES)
    lb_p = jnp.broadcast_to(lnx_b[0].reshape(heads, 2, half).transpose(2, 1, 0)[:, :, None, :],
                            (half, 2, bsz, heads)).reshape(half, LANES)
    lw_s = jnp.tile(lnx_w[0].reshape(heads, HEAD).T, (1, LANES // heads))
    lb_s = jnp.tile(lnx_b[0].reshape(heads, HEAD).T, (1, LANES // heads))

    s0_p = jnp.zeros((HEAD, half, LANES), F32)
    y_p, sf_p = _wkv(to_lanes_p(rp), to_lanes_p(wp), to_lanes_p(kp), to_lanes_p(ap),
                     to_lanes_p(vp), s0_p, kk_l, ka_l, rk_l, lw_p, lb_p,
                     tc=tiles["wkv_chunk"], isplit=True, unroll=tiles["wkv_unroll"])
    s0_s = state_wkv[0].transpose(3, 2, 0, 1).reshape(HEAD, HEAD, nb * heads)
    y_s, sf_s = _wkv(to_lanes_s(rs), to_lanes_s(ws), to_lanes_s(ks), to_lanes_s(as_),
                     to_lanes_s(vs), s0_s, kk_l, ka_l, rk_l, lw_s, lb_s,
                     tc=1, isplit=False, unroll=1)

    ya_p = y_p.reshape(t_p, half, 2, bsz, heads).transpose(3, 0, 4, 2, 1).reshape(n_p, width)
    ya_s = y_s.reshape(HEAD, nb, heads).transpose(1, 2, 0).reshape(nb, width)
    wkv_p = sf_p.reshape(HEAD, half, 2, bsz, heads).transpose(3, 4, 2, 1, 0).reshape(
        1, bsz, heads, HEAD, HEAD)
    wkv_s = sf_s.reshape(HEAD, HEAD, nb, heads).transpose(2, 3, 1, 0)[None]

    def unify(a, b):
        return jnp.concatenate([a, jnp.zeros((s_row0 - n_p, a.shape[1]), a.dtype), b], axis=0)

    w_router = _pad_cols(jnp.concatenate([w_router_grp[0], w_router_exp[0]], axis=1), LANES)
    w_router_hi = w_router.astype(BF16)
    w_router = jnp.stack([w_router_hi, (w_router - w_router_hi.astype(F32)).astype(BF16)])
    b_router = _pad_cols(jnp.concatenate([b_router_grp, b_router_exp], axis=1), LANES)
    assert n_grp + n_exp <= LANES and n_exp == n_grp * EXPERTS_PER_GROUP
    h1, xn2, route = _merge(unify(ya_p, ya_s), g_all, yb_all, proj, x_all,
                           w_br_a[0].astype(BF16), w_br_b[0].astype(BF16), w_out[0].astype(BF16),
                           g_ffn, w_router, b_router, tm=tiles["merge_rows"],
                           ga_col_block=(c_rkv + width) // d,
                           n_grp=n_grp)
    tm_moe = tiles["moe_rows"]
    te, tf, tv, nu, src_rows, dst_rows, n_moe_rows = _moe_plan(route, n_exp, tm_moe)
    y_moe = _moe(xn2, te, tf, tv, nu, src_rows, dst_rows, w_exp_gate[0], w_exp_up[0], w_exp_down[0],
                 n_moe_rows, tm_moe)
    gfin = g_final.reshape(1, d)
    y_prompt = _final(h1, y_moe, route, gfin, N_META, t_p, seq, bsz,
                      tm=tiles["final_rows"]).reshape(bsz, seq, d)
    y_sample = _final(h1, y_moe, route, gfin, s_row0, 0, nb, 1, tm=nb).reshape(nb, 1, d)
    last_p = jnp.concatenate([proj[(b + 1) * t_p - 1:(b + 1) * t_p] for b in range(bsz)], axis=0)
    shift_p = ungroup(last_p)[None]
    shift_s = ungroup(proj[s_row0:])[None]
    return (y_prompt, y_sample, shift_p, wkv_p,
            re_p.reshape(1, bsz, n_g, n_st), im_p.reshape(1, bsz, n_g, n_st),
            shift_s, wkv_s,
            re_s.reshape(1, nb, n_g, n_st), im_s.reshape(1, nb, n_g, n_st))
```

```python
import functools

import jax
import jax.numpy as jnp
from jax import lax
from jax.experimental import pallas as pl
from jax.experimental.pallas import tpu as pltpu

F32 = jnp.float32
BF16 = jnp.bfloat16

NORM_EPS = 1e-6
LNX_EPS = 64e-5
N_META = 16
HEAD = 64
EXPERTS_PER_GROUP = 8
LANES = 128
SUBLANES = 8
LOG2_SUBLANES = SUBLANES.bit_length() - 1
LORA_PAD = 128
ROUTE_E1, ROUTE_E2, ROUTE_W1, ROUTE_W2 = 0, 1, 2, 3
VMEM_LIMIT = 56 * 1024 * 1024


def _cparams(sem):
    return pltpu.CompilerParams(dimension_semantics=sem, vmem_limit_bytes=VMEM_LIMIT)


def _const_spec(shape):
    nd = len(shape)
    return pl.BlockSpec(shape, lambda *_: (0,) * nd)


def _inproj_kernel(x_ref, g_ref, w_ref, o_ref, xn_ref):
    @pl.when(pl.program_id(1) == 0)
    def _():
        x = x_ref[...]
        ms = jnp.mean(x * x, axis=-1, keepdims=True)
        xn_ref[...] = (x * lax.rsqrt(ms + NORM_EPS) * g_ref[...]).astype(BF16)

    o_ref[...] = jnp.dot(xn_ref[...], w_ref[...], preferred_element_type=F32)


def _inproj(x_all, g_mix, w_in_p, tm, tn):
    nt, d = x_all.shape
    n_out = w_in_p.shape[1]
    return pl.pallas_call(
        _inproj_kernel,
        grid=(nt // tm, n_out // tn),
        in_specs=[pl.BlockSpec((tm, d), lambda i, j: (i, 0)),
                  pl.BlockSpec((1, d), lambda i, j: (0, 0)),
                  pl.BlockSpec((d, tn), lambda i, j: (0, j))],
        out_specs=pl.BlockSpec((tm, tn), lambda i, j: (i, j)),
        out_shape=jax.ShapeDtypeStruct((nt, n_out), F32),
        scratch_shapes=[pltpu.VMEM((tm, d), BF16)],
        compiler_params=_cparams(("parallel", "arbitrary")),
        name="inproj",
    )(x_all, g_mix, w_in_p)


def _softplus(z):
    return jnp.maximum(z, 0.0) + jnp.log1p(jnp.exp(-jnp.abs(z)))


def _rwkv_prep_kernel(*refs, width, carry_prev):
    if carry_prev:
        (rkv_ref, lo_ref, mu_rkv_ref, mu_lo_ref, w0_ref, a0_ref,
         wd_ref, wa_ref, wg_ref, _g_all,
         r_out, w_out, k_out, v_out, a_out, g_out, c_rkv, c_lo) = refs

        @pl.when(pl.program_id(1) == 0)
        def _():
            c_rkv[...] = jnp.zeros_like(c_rkv)
            c_lo[...] = jnp.zeros_like(c_lo)
    else:
        (rkv_ref, lo_ref, prev_rkv_ref, prev_lo_ref, mu_rkv_ref, mu_lo_ref, w0_ref, a0_ref,
         wd_ref, wa_ref, wg_ref, _g_all,
         r_out, w_out, k_out, v_out, a_out, g_out) = refs

    tm = rkv_ref.shape[0]
    first_row = lax.broadcasted_iota(jnp.int32, (tm, 1), 0) == 0

    def shifted(p, prev_ref, carry_ref, cols):
        if carry_prev:
            prev = jnp.where(first_row, carry_ref[:, cols], pltpu.roll(p, 1, 0))
        else:
            prev = prev_ref[:, cols]
        return prev

    def lerp(p, prev, mu):
        return p + (prev - p) * mu

    lo_cols = slice(0, lo_ref.shape[1])
    p_lo = lo_ref[...]
    q_lo = lerp(p_lo, shifted(p_lo, None if carry_prev else prev_lo_ref,
                              c_lo if carry_prev else None, lo_cols), mu_lo_ref[...])
    xw = q_lo[:, 0:LORA_PAD]
    xa = q_lo[:, LORA_PAD:2 * LORA_PAD]
    xg = q_lo[:, 2 * LORA_PAD:]
    dw = jnp.dot(jnp.tanh(xw).astype(BF16), wd_ref[...], preferred_element_type=F32)
    wlog = -_softplus(-(w0_ref[...] + dw)) - 0.5
    w_out[...] = jnp.exp(-jnp.exp(wlog))
    a = jax.nn.sigmoid(a0_ref[...] + jnp.dot(xa.astype(BF16), wa_ref[...],
                                             preferred_element_type=F32))
    a_out[...] = a
    g_out[...] = jnp.dot(jax.nn.sigmoid(xg).astype(BF16), wg_ref[...],
                         preferred_element_type=F32)

    def q_of(idx):
        cols = slice(idx * width, (idx + 1) * width)
        p = rkv_ref[:, cols]
        prev = shifted(p, None if carry_prev else prev_rkv_ref,
                       c_rkv if carry_prev else None, cols)
        return lerp(p, prev, mu_rkv_ref[:, cols])

    r_out[...] = q_of(0)
    k_out[...] = q_of(1)
    v_out[...] = q_of(2)

    if carry_prev:
        c_rkv[...] = rkv_ref[tm - 1:tm, :]
        c_lo[...] = lo_ref[tm - 1:tm, :]


def _rwkv_prep(proj, row_block0, n_rows, tm, seqs, width, lo_col_block, lo_width,
               mu_rkv, mu_lo, w0, a0, wd, wa, wg, g_all, prev=None):
    carry_prev = prev is None
    per_seq = n_rows // seqs // tm
    row_map = lambda b, c: (row_block0 + b * per_seq + c, 0)
    lo_map = lambda b, c: (row_block0 + b * per_seq + c, lo_col_block)
    out_map = lambda b, c: (b * per_seq + c, 0)
    in_specs = [pl.BlockSpec((tm, 3 * width), row_map), pl.BlockSpec((tm, lo_width), lo_map)]
    args = [proj, proj]
    if not carry_prev:
        in_specs += [pl.BlockSpec((tm, 3 * width), out_map), pl.BlockSpec((tm, lo_width), out_map)]
        args += list(prev)
    consts = [mu_rkv, mu_lo, w0, a0, wd, wa, wg]
    in_specs += [_const_spec(c.shape) for c in consts] + [pl.BlockSpec(memory_space=pl.ANY)]
    args += consts + [g_all]
    scratch = []
    if carry_prev:
        scratch = [pltpu.VMEM((1, 3 * width), F32), pltpu.VMEM((1, lo_width), F32)]
    out_sd = jax.ShapeDtypeStruct((n_rows // seqs, seqs * width), F32)
    tmaj_map = lambda b, c: (c, b)
    return pl.pallas_call(
        functools.partial(_rwkv_prep_kernel, width=width, carry_prev=carry_prev),
        grid=(seqs, per_seq),
        in_specs=in_specs,
        out_specs=[pl.BlockSpec((tm, width), tmaj_map)] * 5 + [pl.BlockSpec((tm, width), row_map)],
        out_shape=[out_sd] * 5 + [jax.ShapeDtypeStruct(g_all.shape, g_all.dtype)],
        input_output_aliases={len(args) - 1: 5},
        scratch_shapes=scratch,
        compiler_params=_cparams(("parallel", "arbitrary")),
        name="rwkv_prep_seq" if carry_prev else "rwkv_prep_step",
    )(*args)


def _wkv_kernel(r_ref, w_ref, kraw_ref, al_ref, v_ref, s0_ref, kkc_ref, kac_ref, rk_ref, lw_ref,
                lb_ref, y_ref, sf_ref, s_scr, a_scr, b_scr, k_src, *dup_scr,
                ni, nj, tc, isplit, unroll):
    @pl.when(pl.program_id(1) == 0)
    def _():
        s_scr[...] = s0_ref[...]

    def dup(x):
        return jnp.concatenate([x, x], axis=-1) if isplit else x

    if isplit:
        w_src, r_src, v_src = dup_scr
    else:
        w_src, r_src, v_src = w_ref, r_ref, v_ref

    def prep(t, carry):
        kraw = dup(kraw_ref[t])
        al = dup(al_ref[t])
        kk = kraw * kkc_ref[...]
        k_src[t] = kraw * (1.0 + (al - 1.0) * kac_ref[...])
        ss = jnp.sum(kk * kk, axis=0, keepdims=True)
        kkn = kk / jnp.maximum(jnp.sqrt(ss), 1e-12)
        a_scr[t] = -kkn
        b_scr[t] = kkn * al
        if isplit:
            w_src[t] = dup(w_ref[t])
            r_src[t] = dup(r_ref[t])
            v = v_ref[t]
            v_src[t] = jnp.concatenate([v[:ni], v[ni:]], axis=-1)
        return carry

    lax.fori_loop(0, tc, prep, 0, unroll=unroll)

    def row(ref, t, j):
        return ref[t, j:j + 1, :]

    def tree(parts):
        while len(parts) > 1:
            parts = [parts[i] + parts[i + 1] for i in range(0, len(parts), 2)]
        return parts[0]

    n_acc = 4
    sa0 = tree([sum(s_scr[j] * row(a_scr, 0, j) for j in range(q, nj, n_acc))
                for q in range(n_acc)])

    def step(t, sa):
        v = v_src[t]
        tn = jnp.minimum(t + 1, tc - 1)
        y = [None] * n_acc
        san = [None] * n_acc
        for j in range(nj):
            s = s_scr[j] * row(w_src, t, j) + sa * row(b_scr, t, j) + v * row(k_src, t, j)
            s_scr[j] = s
            yj = s * row(r_src, t, j)
            sj = s * row(a_scr, tn, j)
            q = j % n_acc
            y[q] = yj if y[q] is None else y[q] + yj
            san[q] = sj if san[q] is None else san[q] + sj
        y_ref[t] = tree(y)
        return tree(san)

    lax.fori_loop(0, tc, step, sa0)

    def isum(x):
        s = jnp.broadcast_to(jnp.sum(x, axis=0, keepdims=True), (8, LANES))
        if isplit:
            s = s + pltpu.roll(s, LANES // 2, 1)
        return s[0:1]

    def post(t, carry):
        y = y_ref[t]
        v = v_src[t]
        mu = isum(y) * (1.0 / HEAD)
        d = y - mu
        var = isum(d * d) * (1.0 / HEAD)
        yn = d * lax.rsqrt(var + LNX_EPS) * lw_ref[...] + lb_ref[...]
        bonus = jnp.sum(r_src[t] * k_src[t] * rk_ref[...], axis=0, keepdims=True)
        y_ref[t] = yn + bonus * v
        return carry

    lax.fori_loop(0, tc, post, 0, unroll=unroll)

    @pl.when(pl.program_id(1) == pl.num_programs(1) - 1)
    def _():
        sf_ref[...] = s_scr[...]


def _wkv(r, w, kraw, al, v, s0, kkc, kac, rk, lw, lb, tc, isplit, unroll):
    t, nj, jl = r.shape
    ni, lanes = s0.shape[1:]
    assert jl == (LANES // 2 if isplit else lanes) and v.shape == r.shape
    jspec = pl.BlockSpec((tc, nj, min(jl, LANES)), lambda l, c: (c, 0, l))
    dup_scr = []
    if isplit:
        dup_scr = [pltpu.VMEM((tc, nj, LANES), F32)] * 2 + [pltpu.VMEM((tc, ni, LANES), F32)]
    ispec = pl.BlockSpec((tc, ni, LANES), lambda l, c: (c, 0, l))
    sspec = pl.BlockSpec((nj, ni, LANES), lambda l, c: (0, 0, l))
    consts = [kkc, kac, rk, lw, lb]
    return pl.pallas_call(
        functools.partial(_wkv_kernel, ni=ni, nj=nj, tc=tc, isplit=isplit, unroll=unroll),
        grid=(lanes // LANES, t // tc),
        in_specs=[jspec, jspec, jspec, jspec, jspec, sspec] + [_const_spec(c.shape) for c in consts],
        out_specs=[ispec, sspec],
        out_shape=[jax.ShapeDtypeStruct((t, ni, lanes), F32),
                   jax.ShapeDtypeStruct((nj, ni, lanes), F32)],
        scratch_shapes=[pltpu.VMEM((nj, ni, LANES), F32),
                        pltpu.VMEM((tc, nj, LANES), F32),
                        pltpu.VMEM((tc, nj, LANES), F32),
                        pltpu.VMEM((tc, nj, LANES), F32)] + dup_scr,
        compiler_params=_cparams(("parallel", "arbitrary")),
        name="wkv_seq" if isplit else "wkv_step",
    )(r, w, kraw, al, v, s0, *consts)


def _s5_kernel(*refs, sequential, n_blk, pitch, unroll):
    if sequential:
        (u_ref, bre_ref, bim_ref, cre_ref, cim_ref, d_ref, are_ref, aim_ref, wglu_ref, bglu_ref,
         _yb_all, yb_ref, hre_out, him_out, st_re, st_im, c_re, c_im) = refs
    else:
        (u_ref, h0re_ref, h0im_ref, bre_ref, bim_ref, cre_ref, cim_ref, d_ref, are_ref, aim_ref,
         wglu_ref, bglu_ref, _yb_all, yb_ref, hre_out, him_out, st_re, st_im) = refs

    tm = u_ref.shape[0]
    kin = bre_ref.shape[1]
    kst = bre_ref.shape[2]
    tiles_per_blk = kst // LANES
    n_tiles = n_blk * tiles_per_blk
    u = u_ref[...]
    ub = u.astype(BF16)

    def tile_rows(k):
        return slice(k * pitch, k * pitch + tm)

    for kb in range(n_blk):
        ukb = ub[:, kb * kin:(kb + 1) * kin]
        bu_re = jnp.dot(ukb, bre_ref[kb], preferred_element_type=F32)
        bu_im = jnp.dot(ukb, bim_ref[kb], preferred_element_type=F32)
        for n in range(tiles_per_blk):
            k = kb * tiles_per_blk + n
            cols = slice(n * LANES, (n + 1) * LANES)
            if sequential:
                st_re[tile_rows(k), :] = bu_re[:, cols]
                st_im[tile_rows(k), :] = bu_im[:, cols]
            else:
                kc = slice(k * LANES, (k + 1) * LANES)
                ar, ai = are_ref[k:k + 1, :], aim_ref[k:k + 1, :]
                h0r, h0i = h0re_ref[:, kc], h0im_ref[:, kc]
                nr = bu_re[:, cols] + (ar * h0r - ai * h0i)
                ni = bu_im[:, cols] + (ar * h0i + ai * h0r)
                st_re[tile_rows(k), :] = nr
                st_im[tile_rows(k), :] = ni
                hre_out[:, kc] = nr
                him_out[:, kc] = ni

    if sequential:
        @pl.when(pl.program_id(1) == 0)
        def _():
            c_re[...] = jnp.zeros_like(c_re)
            c_im[...] = jnp.zeros_like(c_im)

        ar = are_ref[...]
        ai = aim_ref[...]
        groups = range(n_tiles // SUBLANES)

        def token_rows(t, m):
            return pl.ds(t + m * SUBLANES * pitch, SUBLANES, stride=pitch)

        def step(t, h):
            hr, hi = h
            bur = jnp.concatenate([st_re[token_rows(t, m), :] for m in groups], axis=0)
            bui = jnp.concatenate([st_im[token_rows(t, m), :] for m in groups], axis=0)
            nr = ar * hr - ai * hi + bur
            ni = ar * hi + ai * hr + bui
            for m in groups:
                st_re[token_rows(t, m), :] = nr[m * SUBLANES:(m + 1) * SUBLANES]
                st_im[token_rows(t, m), :] = ni[m * SUBLANES:(m + 1) * SUBLANES]
            return nr, ni

        hr, hi = lax.fori_loop(0, tm, step, (c_re[...], c_im[...]), unroll=unroll)
        c_re[...] = hr
        c_im[...] = hi
        hre_out[0] = hr
        him_out[0] = hi

    ys = []
    for kb in range(n_blk):
        tiles = range(kb * tiles_per_blk, (kb + 1) * tiles_per_blk)
        h_re = jnp.concatenate([st_re[tile_rows(k), :] for k in tiles], axis=1)
        h_im = jnp.concatenate([st_im[tile_rows(k), :] for k in tiles], axis=1)
        yre = jnp.dot(h_re.astype(BF16), cre_ref[kb], preferred_element_type=F32)
        yim = jnp.dot(h_im.astype(BF16), cim_ref[kb], preferred_element_type=F32)
        ys.append(yre - yim)
    y = jnp.concatenate(ys, axis=1) + d_ref[...] * u
    y = jax.nn.gelu(y)
    gate = jnp.dot(y.astype(BF16), wglu_ref[...], preferred_element_type=F32) + bglu_ref[...]
    yb_ref[...] = (y * jax.nn.sigmoid(gate)).astype(BF16)


def _s5(proj, row_block0, n_rows, tm, seqs, u_col_block, width, consts, yb_all, h0=None):
    sequential = h0 is None
    bre = consts[0]
    n_blk, _, kst = bre.shape
    n_state = n_blk * kst
    n_tiles = n_state // LANES
    assert tm % SUBLANES == 0 and n_tiles % SUBLANES == 0
    pitch = tm
    per_seq = n_rows // seqs // tm
    u_map = lambda b, c: (row_block0 + b * per_seq + c, u_col_block)
    out_map = lambda b, c: (b * per_seq + c, 0)
    row_map = lambda b, c: (row_block0 + b * per_seq + c, 0)
    in_specs = [pl.BlockSpec((tm, width), u_map)]
    args = [proj]
    if not sequential:
        in_specs += [pl.BlockSpec((tm, n_state), out_map)] * 2
        args += list(h0)
    in_specs += [_const_spec(c.shape) for c in consts] + [pl.BlockSpec(memory_space=pl.ANY)]
    args += list(consts) + [yb_all]
    scratch = [pltpu.VMEM((n_tiles * pitch, LANES), F32), pltpu.VMEM((n_tiles * pitch, LANES), F32)]
    if sequential:
        scratch += [pltpu.VMEM((n_tiles, LANES), F32), pltpu.VMEM((n_tiles, LANES), F32)]
        st_spec = pl.BlockSpec((1, n_tiles, LANES), lambda b, c: (b, 0, 0))
        st_shape = jax.ShapeDtypeStruct((seqs, n_tiles, LANES), F32)
    else:
        st_spec = pl.BlockSpec((tm, n_state), out_map)
        st_shape = jax.ShapeDtypeStruct((n_rows, n_state), F32)
    return pl.pallas_call(
        functools.partial(_s5_kernel, sequential=sequential, n_blk=n_blk, pitch=pitch, unroll=4),
        grid=(seqs, per_seq),
        in_specs=in_specs,
        out_specs=[pl.BlockSpec((tm, width), row_map), st_spec, st_spec],
        out_shape=[jax.ShapeDtypeStruct(yb_all.shape, yb_all.dtype), st_shape, st_shape],
        input_output_aliases={len(args) - 1: 0},
        scratch_shapes=scratch,
        compiler_params=_cparams(("parallel", "arbitrary")),
        name="s5_seq" if sequential else "s5_step",
    )(*args)


def _route(logits, n_grp):
    lane = lax.broadcasted_iota(jnp.int32, logits.shape, 1).astype(F32)
    neg = jnp.float32(-1e30)
    big = jnp.float32(1e9)
    is_grp = lane < n_grp
    gl = jnp.where(is_grp, logits, neg)
    gmax = jnp.max(gl, axis=1, keepdims=True)
    gsum = jnp.sum(jnp.where(is_grp, jnp.exp(gl - gmax), 0.0), axis=1, keepdims=True)
    g_p = 1.0 / gsum
    g_idx = jnp.min(jnp.where(is_grp & (gl == gmax), lane, big), axis=1, keepdims=True)
    lo = n_grp + g_idx * EXPERTS_PER_GROUP
    in_grp = (lane >= lo) & (lane < lo + EXPERTS_PER_GROUP)
    el = jnp.where(in_grp, logits, neg)
    v1 = jnp.max(el, axis=1, keepdims=True)
    i1 = jnp.min(jnp.where(in_grp & (el == v1), lane, big), axis=1, keepdims=True)
    rest = in_grp & (lane != i1)
    el2 = jnp.where(rest, logits, neg)
    v2 = jnp.max(el2, axis=1, keepdims=True)
    i2 = jnp.min(jnp.where(rest & (el2 == v2), lane, big), axis=1, keepdims=True)
    e2 = jnp.exp(v2 - v1)
    w1 = g_p / (1.0 + e2)
    w2 = g_p * e2 / (1.0 + e2)
    return (jnp.where(lane == ROUTE_E1, i1 - n_grp, 0.0) + jnp.where(lane == ROUTE_E2, i2 - n_grp, 0.0)
            + jnp.where(lane == ROUTE_W1, w1, 0.0) + jnp.where(lane == ROUTE_W2, w2, 0.0))


def _merge_kernel(ya_ref, g_ref, yb_ref, ga_ref, gb_ref, x_ref, wa_ref, wb_ref, wo_ref,
                  gf_ref, wr_ref, br_ref, h_out, xn_out, comb_out, *, n_grp):
    ya = (ya_ref[...] * g_ref[...]).astype(BF16)
    ma = jnp.dot(ya, wa_ref[...], preferred_element_type=F32)
    mb = jnp.dot(yb_ref[...], wb_ref[...], preferred_element_type=F32)
    merged = jax.nn.sigmoid(ga_ref[...]) * ma + jax.nn.sigmoid(gb_ref[...]) * mb
    h = x_ref[...] + jnp.dot(merged.astype(BF16), wo_ref[...], preferred_element_type=F32)
    h_out[...] = h
    ms = jnp.mean(h * h, axis=-1, keepdims=True)
    xn = h * lax.rsqrt(ms + NORM_EPS) * gf_ref[...]
    xn_out[...] = xn
    x_hi = xn.astype(BF16)
    x_lo = (xn - x_hi.astype(F32)).astype(BF16)
    w_hi, w_lo = wr_ref[0], wr_ref[1]
    logits = (jnp.dot(x_hi, w_hi, preferred_element_type=F32)
              + (jnp.dot(x_hi, w_lo, preferred_element_type=F32)
                 + jnp.dot(x_lo, w_hi, preferred_element_type=F32))) + br_ref[...]
    comb_out[...] = _route(logits, n_grp)


def _merge(ya, g, yb, proj, x_all, w_br_a, w_br_b, w_out, g_ffn, w_router, b_router,
           tm, ga_col_block, n_grp):
    nt, d = x_all.shape
    wdt = ya.shape[1]
    row = lambda i: (i, 0)
    single = dict(pipeline_mode=pl.Buffered(1))
    in_specs = [pl.BlockSpec((tm, wdt), row), pl.BlockSpec((tm, wdt), row),
                pl.BlockSpec((tm, wdt), row),
                pl.BlockSpec((tm, d), lambda i: (i, ga_col_block)),
                pl.BlockSpec((tm, d), lambda i: (i, ga_col_block + 1)),
                pl.BlockSpec((tm, d), row),
                pl.BlockSpec(w_br_a.shape, lambda i: (0, 0), **single),
                pl.BlockSpec(w_br_b.shape, lambda i: (0, 0), **single),
                pl.BlockSpec(w_out.shape, lambda i: (0, 0), **single),
                _const_spec(g_ffn.shape), _const_spec(w_router.shape), _const_spec(b_router.shape)]
    return pl.pallas_call(
        functools.partial(_merge_kernel, n_grp=n_grp),
        grid=(nt // tm,),
        in_specs=in_specs,
        out_specs=[pl.BlockSpec((tm, d), row), pl.BlockSpec((tm, d), row),
                   pl.BlockSpec((tm, LANES), row)],
        out_shape=[jax.ShapeDtypeStruct((nt, d), F32), jax.ShapeDtypeStruct((nt, d), F32),
                   jax.ShapeDtypeStruct((nt, LANES), F32)],
        compiler_params=_cparams(("parallel",)),
        name="merge_route",
    )(ya, g, yb, proj, proj, x_all, w_br_a, w_br_b, w_out, g_ffn, w_router, b_router)


def _moe_kernel(te_ref, first_ref, nvalid_ref, nused_ref, src_cur, src_nxt, dst_cur,
                xn_hbm, wg_ref, wu_ref, wd_ref, y_hbm,
                xbuf, obuf, wg_bf, wu_bf, wd_bf, gsem, ssem, *, tm):
    i = pl.program_id(0)
    last = pl.num_programs(0) - 1
    n_used = nused_ref[0]
    slot = lax.rem(i, 2)

    def prefix_loops(n_rows, row):
        def block(b, c):
            for u in range(SUBLANES):
                row(b * SUBLANES + u)
            return c

        def single(r, c):
            row(r)
            return c

        n_blocks = lax.shift_right_logical(n_rows, LOG2_SUBLANES)
        lax.fori_loop(0, n_blocks, block, 0)
        lax.fori_loop(n_blocks * SUBLANES, n_rows, single, 0)

    def prefix_wait(n_rows, copy_of):
        n_full = pl.multiple_of(
            lax.shift_left(lax.shift_right_logical(n_rows, LOG2_SUBLANES), LOG2_SUBLANES), SUBLANES)

        @pl.when(n_full > 0)
        def _():
            copy_of(0, n_full).wait()

        def single(r, c):
            copy_of(r, 1).wait()
            return c
        lax.fori_loop(n_full, n_rows, single, 0)

    def gather_start(src, s, n_rows):
        prefix_loops(n_rows, lambda r: pltpu.make_async_copy(
            xn_hbm.at[pl.ds(src[0, 0, r], 1)], xbuf.at[s, pl.ds(r, 1)], gsem.at[s]).start())

    def gather_wait(s, n_rows):
        prefix_wait(n_rows, lambda r0, n: pltpu.make_async_copy(
            xn_hbm.at[pl.ds(0, n)], xbuf.at[s, pl.ds(r0, n)], gsem.at[s]))

    def scatter_start(s, n_rows):
        prefix_loops(n_rows, lambda r: pltpu.make_async_copy(
            obuf.at[s, pl.ds(r, 1)], y_hbm.at[pl.ds(dst_cur[0, 0, r], 1)], ssem.at[s]).start())

    def scatter_wait(s, n_rows):
        prefix_wait(n_rows, lambda r0, n: pltpu.make_async_copy(
            obuf.at[s, pl.ds(r0, n)], y_hbm.at[pl.ds(0, n)], ssem.at[s]))

    @pl.when(i == 0)
    def _():
        xbuf[...] = jnp.zeros_like(xbuf)
        gather_start(src_cur, 0, nvalid_ref[0])

    @pl.when(i < n_used)
    def _():
        gather_wait(slot, nvalid_ref[i])

    @pl.when(i + 1 < n_used)
    def _():
        gather_start(src_nxt, 1 - slot, nvalid_ref[i + 1])

    @pl.when((i >= 2) & (i - 2 < n_used))
    def _():
        scatter_wait(slot, nvalid_ref[i - 2])

    @pl.when(i < n_used)
    def _():
        @pl.when(first_ref[i] == 1)
        def _():
            wg_bf[...] = wg_ref[0].astype(BF16)
            wu_bf[...] = wu_ref[0].astype(BF16)
            wd_bf[...] = wd_ref[0].astype(BF16)

        x = xbuf[slot].astype(BF16)
        xg = jnp.dot(x, wg_bf[...], preferred_element_type=F32)
        xu = jnp.dot(x, wu_bf[...], preferred_element_type=F32)
        hid = (jax.nn.silu(xg) * xu).astype(BF16)
        obuf[slot] = jnp.dot(hid, wd_bf[...], preferred_element_type=F32)
        scatter_start(slot, nvalid_ref[i])

    @pl.when(i == last)
    def _():
        @pl.when((i >= 1) & (i - 1 < n_used))
        def _():
            scatter_wait(1 - slot, nvalid_ref[i - 1])

        @pl.when(i < n_used)
        def _():
            scatter_wait(slot, nvalid_ref[i])


def _moe(xn, tile_expert, tile_first, tile_valid, n_used, src_rows, dst_rows, wg, wu, wd,
         n_out_rows, tm):
    nt, d = xn.shape
    n_exp, _, de = wg.shape
    n_tiles = src_rows.shape[0]
    smem_cur = pl.BlockSpec((1, 1, tm), lambda i, *_: (i, 0, 0), memory_space=pltpu.SMEM)
    smem_nxt = pl.BlockSpec((1, 1, tm), lambda i, *_: (jnp.minimum(i + 1, n_tiles - 1), 0, 0),
                            memory_space=pltpu.SMEM)
    grid_spec = pltpu.PrefetchScalarGridSpec(
        num_scalar_prefetch=4,
        grid=(n_tiles,),
        in_specs=[smem_cur, smem_nxt, smem_cur,
                  pl.BlockSpec(memory_space=pl.ANY),
                  pl.BlockSpec((1, d, de), lambda i, te, *_: (te[i], 0, 0)),
                  pl.BlockSpec((1, d, de), lambda i, te, *_: (te[i], 0, 0)),
                  pl.BlockSpec((1, de, d), lambda i, te, *_: (te[i], 0, 0))],
        out_specs=pl.BlockSpec(memory_space=pl.ANY),
        scratch_shapes=[pltpu.VMEM((2, tm, d), F32), pltpu.VMEM((2, tm, d), F32),
                        pltpu.VMEM((d, de), BF16), pltpu.VMEM((d, de), BF16),
                        pltpu.VMEM((de, d), BF16),
                        pltpu.SemaphoreType.DMA((2,)), pltpu.SemaphoreType.DMA((2,))])
    return pl.pallas_call(
        functools.partial(_moe_kernel, tm=tm),
        grid_spec=grid_spec,
        out_shape=jax.ShapeDtypeStruct((n_out_rows, d), F32),
        compiler_params=_cparams(("arbitrary",)),
        name="moe_grouped",
    )(tile_expert, tile_first, tile_valid, n_used, src_rows, src_rows, dst_rows, xn, wg, wu, wd)


def _moe_plan(route, n_exp, tm):
    nt = route.shape[0]
    n_pairs = 2 * nt
    n_tiles = n_pairs // tm + n_exp
    eid = jnp.concatenate([route[:, ROUTE_E1], route[:, ROUTE_E2]]).astype(jnp.int32)
    onehot = (eid[:, None] == jnp.arange(n_exp, dtype=jnp.int32)[None, :]).astype(jnp.int32)
    csum = jnp.cumsum(onehot, axis=0)
    rank = jnp.take_along_axis(csum, eid[:, None], axis=1)[:, 0] - 1
    cnt = csum[-1]
    tiles = (cnt + tm - 1) // tm
    tile_end = jnp.cumsum(tiles)
    n_used = tile_end[-1]
    pos = (tile_end - tiles)[eid] * tm + rank
    pair = jnp.arange(n_pairs, dtype=jnp.int32)
    dst_rows = jnp.zeros((n_tiles * tm,), jnp.int32).at[pos].set(
        pair, unique_indices=True, mode="promise_in_bounds")
    src_rows = dst_rows % nt
    tile_id = jnp.minimum(jnp.arange(n_tiles, dtype=jnp.int32), n_used - 1)
    tile_expert = jnp.sum((tile_end[None, :] <= tile_id[:, None]).astype(jnp.int32), axis=1)
    tile_first = jnp.concatenate(
        [jnp.ones((1,), jnp.int32), (tile_expert[1:] != tile_expert[:-1]).astype(jnp.int32)])
    tile_valid = jnp.clip(cnt[tile_expert] - (tile_id - (tile_end - tiles)[tile_expert]) * tm, 0, tm)
    return (tile_expert, tile_first, tile_valid.astype(jnp.int32),
            n_used.reshape(1).astype(jnp.int32),
            src_rows.reshape(n_tiles, 1, tm), dst_rows.reshape(n_tiles, 1, tm), n_pairs)


def _final_kernel(h_ref, y1_ref, y2_ref, route_ref, g_ref, o_ref):
    lane = lax.broadcasted_iota(jnp.int32, route_ref.shape, 1)
    route = route_ref[...]
    w1 = jnp.sum(jnp.where(lane == ROUTE_W1, route, 0.0), axis=1, keepdims=True)
    w2 = jnp.sum(jnp.where(lane == ROUTE_W2, route, 0.0), axis=1, keepdims=True)
    h = h_ref[...] + (w1 * y1_ref[...] + w2 * y2_ref[...])
    ms = jnp.mean(h * h, axis=-1, keepdims=True)
    o_ref[...] = h * lax.rsqrt(ms + NORM_EPS) * g_ref[...]


def _final(h, y_moe, route, g_final, row0, seg_stride, seg_rows, n_seg, tm):
    nt, d = h.shape
    per_seg = seg_rows // tm
    assert all(x % SUBLANES == 0 for x in (row0, seg_stride, tm, nt))
    off = lambda s, c: row0 + s * seg_stride + c * tm
    rows = lambda width, base: pl.BlockSpec((pl.Element(tm), pl.Element(width)),
                                            lambda s, c: (pl.multiple_of(base + off(s, c), SUBLANES), 0))
    return pl.pallas_call(
        _final_kernel,
        grid=(n_seg, per_seg),
        in_specs=[rows(d, 0), rows(d, 0), rows(d, nt), rows(LANES, 0),
                  pl.BlockSpec((1, d), lambda s, c: (0, 0))],
        out_specs=pl.BlockSpec((tm, d), lambda s, c: (s * per_seg + c, 0)),
        out_shape=jax.ShapeDtypeStruct((n_seg * seg_rows, d), F32),
        compiler_params=_cparams(("parallel", "parallel")),
        name="final_norm",
    )(h, y_moe, y_moe, route, g_final)


def _pad_cols(w, to):
    return jnp.pad(w, ((0, 0), (0, to - w.shape[1])))


def _pad_rows(w, to):
    return jnp.pad(w, ((0, to - w.shape[0]), (0, 0)))


def _tile_plan(n_t, t_p, seq):
    plan = dict(
        inproj_rows=n_t // 6,
        inproj_cols=4 * LANES,
        seq_rows=t_p // 6,
        wkv_chunk=48, wkv_unroll=8,
        merge_rows=2 * LANES, moe_rows=2 * LANES, final_rows=4 * LANES)
    assert n_t % plan["inproj_rows"] == 0 and plan["inproj_rows"] % SUBLANES == 0
    assert t_p % plan["seq_rows"] == 0 and plan["seq_rows"] % SUBLANES == 0
    assert t_p % plan["wkv_chunk"] == 0 and plan["wkv_chunk"] % plan["wkv_unroll"] == 0
    assert n_t % plan["merge_rows"] == 0 and seq % plan["final_rows"] == 0
    return plan


def _s5_consts(lam_re, lam_im, log_dt, b_re, b_im, c_re, c_im, d, w_glu, b_glu):
    dt = jnp.exp(log_dt)[:, None]
    mag = jnp.exp(lam_re * dt)
    abar_re = mag * jnp.cos(lam_im * dt)
    abar_im = mag * jnp.sin(lam_im * dt)
    den = lam_re * lam_re + lam_im * lam_im
    nr = abar_re - 1.0
    coef_re = (nr * lam_re + abar_im * lam_im) / den
    coef_im = (abar_im * lam_re - nr * lam_im) / den
    bbar_re = coef_re[..., None] * b_re - coef_im[..., None] * b_im
    bbar_im = coef_re[..., None] * b_im + coef_im[..., None] * b_re
    n_g, n_p, n_c = b_re.shape
    gpb = LANES // n_c
    eye = jnp.eye(gpb, dtype=F32)

    def in_blk(bb):
        bb = bb.reshape(n_g // gpb, gpb, n_p, n_c)
        return jnp.einsum('kgpc,gh->kgchp', bb, eye).reshape(
            n_g // gpb, gpb * n_c, gpb * n_p).astype(BF16)

    def out_blk(cc):
        cc = cc.reshape(n_g // gpb, gpb, n_c, n_p)
        return jnp.einsum('kgcp,gh->khpgc', cc, eye).reshape(
            n_g // gpb, gpb * n_p, gpb * n_c).astype(BF16)

    return (in_blk(bbar_re), in_blk(bbar_im), out_blk(c_re), out_blk(c_im),
            d.reshape(1, -1), abar_re.reshape(-1, LANES), abar_im.reshape(-1, LANES),
            w_glu.astype(BF16), b_glu.reshape(1, -1))


def kernel(x_prompt, x_sample, state_shift, state_wkv, state_ssm_re, state_ssm_im, meta_tokens, g_mix, w_in, shift_mu, w0, w_decay_up, a0, w_aaa_up, w_gate_up, k_k, k_a, r_k, lnx_w, lnx_b, ssm_lam_re, ssm_lam_im, ssm_log_dt, ssm_b_re, ssm_b_im, ssm_c_re, ssm_c_im, ssm_d, w_glu, b_glu, w_br_a, w_br_b, w_out, g_ffn, w_router_grp, b_router_grp, w_router_exp, b_router_exp, w_exp_gate, w_exp_up, w_exp_down, g_final):
    depth = g_mix.shape[0]
    assert depth == 1, "single-layer trunk"
    bsz, seq, d = x_prompt.shape
    nb = x_sample.shape[0]
    assert x_sample.shape[1] == 1
    t_p = seq + N_META
    n_p = bsz * t_p
    width = k_k.shape[1]
    heads = width // HEAD
    n_dl, n_al, n_gl = w_decay_up.shape[1], w_aaa_up.shape[1], w_gate_up.shape[1]
    n_grp = w_router_grp.shape[2]
    n_exp = w_router_exp.shape[2]
    assert bsz * heads * 2 == LANES and (nb * heads) % LANES == 0

    s_row0 = -(-n_p // nb) * nb
    n_t = s_row0 + nb
    pieces = []
    for b in range(bsz):
        pieces += [meta_tokens, x_prompt[b]]
    x_all = jnp.concatenate(
        pieces + [jnp.zeros((s_row0 - n_p, d), F32), x_sample.reshape(nb, d)], axis=0)

    c_rkv = 3 * width
    c_xw, c_xa, c_xg = c_rkv, c_rkv + n_dl, c_rkv + n_dl + n_al
    c_u = c_xg + n_gl
    lo_width = 2 * LORA_PAD + n_gl

    def regroup(m):
        return jnp.concatenate(
            [m[:, :c_rkv], m[:, c_u:], _pad_cols(m[:, c_xw:c_xa], LORA_PAD),
             _pad_cols(m[:, c_xa:c_xg], LORA_PAD), m[:, c_xg:c_u]], axis=1)

    def ungroup(m):
        lo = c_rkv + width + 2 * d
        return jnp.concatenate(
            [m[:, :c_rkv], m[:, lo:lo + n_dl], m[:, lo + LORA_PAD:lo + LORA_PAD + n_al],
             m[:, lo + 2 * LORA_PAD:]], axis=1)

    w_in_p = regroup(w_in[0].astype(BF16))
    n_proj = w_in_p.shape[1]
    tiles = _tile_plan(n_t, t_p, seq)
    proj = _inproj(x_all, g_mix, w_in_p, tm=tiles["inproj_rows"], tn=tiles["inproj_cols"])

    mu_all = regroup(_pad_cols(shift_mu, c_u + width + 2 * d))
    mu_rkv, mu_lo = mu_all[:, :c_rkv], mu_all[:, n_proj - lo_width:]
    st_all = regroup(_pad_cols(state_shift[0], c_u + width + 2 * d))
    prev_s = (st_all[:, :c_rkv], st_all[:, n_proj - lo_width:])
    lo_col_block = (n_proj - lo_width) // lo_width
    prep_w = (mu_rkv, mu_lo, w0, a0,
              _pad_rows(w_decay_up[0], LORA_PAD).astype(BF16),
              _pad_rows(w_aaa_up[0], LORA_PAD).astype(BF16), w_gate_up[0].astype(BF16))

    tm_seq = tiles["seq_rows"]
    g_all = jnp.zeros((n_t, width), F32)
    rp, wp, kp, vp, ap, g_all = _rwkv_prep(proj, 0, n_p, tm_seq, bsz, width, lo_col_block,
                                                lo_width, *prep_w, g_all)
    rs, ws, ks, vs, as_, g_all = _rwkv_prep(proj, s_row0 // nb, nb, nb, 1, width,
                                                 lo_col_block, lo_width, *prep_w, g_all,
                                                 prev=prev_s)

    s5c = _s5_consts(ssm_lam_re[0], ssm_lam_im[0], ssm_log_dt[0], ssm_b_re[0], ssm_b_im[0],
                     ssm_c_re[0], ssm_c_im[0], ssm_d[0], w_glu[0], b_glu)
    n_g, n_st = ssm_lam_re.shape[1], ssm_lam_re.shape[2]
    u_col_block = c_rkv // width
    yb_all = jnp.zeros((n_t, width), BF16)
    yb_all, re_p, im_p = _s5(proj, 0, n_p, tm_seq, bsz, u_col_block, width, s5c, yb_all)
    h0 = (state_ssm_re[0].reshape(nb, n_g * n_st), state_ssm_im[0].reshape(nb, n_g * n_st))
    yb_all, re_s, im_s = _s5(proj, s_row0 // nb, nb, nb, 1, u_col_block, width, s5c, yb_all,
                             h0=h0)

    half = HEAD // 2

    def to_lanes_p(z):
        return z.reshape(t_p, bsz * heads, HEAD).transpose(0, 2, 1)

    def to_lanes_s(z):
        return z.reshape(nb, heads, HEAD).transpose(2, 0, 1).reshape(1, HEAD, nb * heads)

    def head_const(c):
        return jnp.tile(c.reshape(heads, HEAD).T, (1, LANES // heads))

    rk_l = head_const(r_k[0])
    kk_l, ka_l = head_const(k_k[0]), head_const(k_a[0])
    lw_p = jnp.broadcast_to(lnx_w[0].reshape(heads, 2, half).transpose(2, 1, 0)[:, :, None, :],
                            (half, 2, bsz, heads)).reshape(half, LANES)
    lb_p = jnp.broadcast_to(lnx_b[0].reshape(heads, 2, half).transpose(2, 1, 0)[:, :, None, :],
                            (half, 2, bsz, heads)).reshape(half, LANES)
    lw_s = jnp.tile(lnx_w[0].reshape(heads, HEAD).T, (1, LANES // heads))
    lb_s = jnp.tile(lnx_b[0].reshape(heads, HEAD).T, (1, LANES // heads))

    s0_p = jnp.zeros((HEAD, half, LANES), F32)
    y_p, sf_p = _wkv(to_lanes_p(rp), to_lanes_p(wp), to_lanes_p(kp), to_lanes_p(ap),
                     to_lanes_p(vp), s0_p, kk_l, ka_l, rk_l, lw_p, lb_p,
                     tc=tiles["wkv_chunk"], isplit=True, unroll=tiles["wkv_unroll"])
    s0_s = state_wkv[0].transpose(3, 2, 0, 1).reshape(HEAD, HEAD, nb * heads)
    y_s, sf_s = _wkv(to_lanes_s(rs), to_lanes_s(ws), to_lanes_s(ks), to_lanes_s(as_),
                     to_lanes_s(vs), s0_s, kk_l, ka_l, rk_l, lw_s, lb_s,
                     tc=1, isplit=False, unroll=1)

    ya_p = y_p.reshape(t_p, half, 2, bsz, heads).transpose(3, 0, 4, 2, 1).reshape(n_p, width)
    ya_s = y_s.reshape(HEAD, nb, heads).transpose(1, 2, 0).reshape(nb, width)
    wkv_p = sf_p.reshape(HEAD, half, 2, bsz, heads).transpose(3, 4, 2, 1, 0).reshape(
        1, bsz, heads, HEAD, HEAD)
    wkv_s = sf_s.reshape(HEAD, HEAD, nb, heads).transpose(2, 3, 1, 0)[None]

    def unify(a, b):
        return jnp.concatenate([a, jnp.zeros((s_row0 - n_p, a.shape[1]), a.dtype), b], axis=0)

    w_router = _pad_cols(jnp.concatenate([w_router_grp[0], w_router_exp[0]], axis=1), LANES)
    w_router_hi = w_router.astype(BF16)
    w_router = jnp.stack([w_router_hi, (w_router - w_router_hi.astype(F32)).astype(BF16)])
    b_router = _pad_cols(jnp.concatenate([b_router_grp, b_router_exp], axis=1), LANES)
    assert n_grp + n_exp <= LANES and n_exp == n_grp * EXPERTS_PER_GROUP
    h1, xn2, route = _merge(unify(ya_p, ya_s), g_all, yb_all, proj, x_all,
                           w_br_a[0].astype(BF16), w_br_b[0].astype(BF16), w_out[0].astype(BF16),
                           g_ffn, w_router, b_router, tm=tiles["merge_rows"],
                           ga_col_block=(c_rkv + width) // d,
                           n_grp=n_grp)
    tm_moe = tiles["moe_rows"]
    te, tf, tv, nu, src_rows, dst_rows, n_moe_rows = _moe_plan(route, n_exp, tm_moe)
    y_moe = _moe(xn2, te, tf, tv, nu, src_rows, dst_rows, w_exp_gate[0], w_exp_up[0], w_exp_down[0],
                 n_moe_rows, tm_moe)
    gfin = g_final.reshape(1, d)
    y_prompt = _final(h1, y_moe, route, gfin, N_META, t_p, seq, bsz,
                      tm=tiles["final_rows"]).reshape(bsz, seq, d)
    y_sample = _final(h1, y_moe, route, gfin, s_row0, 0, nb, 1, tm=nb).reshape(nb, 1, d)
    last_p = jnp.concatenate([proj[(b + 1) * t_p - 1:(b + 1) * t_p] for b in range(bsz)], axis=0)
    shift_p = ungroup(last_p)[None]
    shift_s = ungroup(proj[s_row0:])[None]
    return (y_prompt, y_sample, shift_p, wkv_p,
            re_p.reshape(1, bsz, n_g, n_st), im_p.reshape(1, bsz, n_g, n_st),
            shift_s, wkv_s,
            re_s.reshape(1, nb, n_g, n_st), im_s.reshape(1, nb, n_g, n_st))
```

```python
import functools

import jax
import jax.numpy as jnp
from jax import lax
from jax.experimental import pallas as pl
from jax.experimental.pallas import tpu as pltpu

F32 = jnp.float32
BF16 = jnp.bfloat16

NORM_EPS = 1e-6
LNX_EPS = 64e-5
N_META = 16
HEAD = 64
EXPERTS_PER_GROUP = 8
LANES = 128
SUBLANES = 8
LOG2_SUBLANES = SUBLANES.bit_length() - 1
LORA_PAD = 128
ROUTE_E1, ROUTE_E2, ROUTE_W1, ROUTE_W2 = 0, 1, 2, 3
VMEM_LIMIT = 56 * 1024 * 1024


def _cparams(sem):
    return pltpu.CompilerParams(dimension_semantics=sem, vmem_limit_bytes=VMEM_LIMIT)


def _const_spec(shape):
    nd = len(shape)
    return pl.BlockSpec(shape, lambda *_: (0,) * nd)


def _inproj_kernel(x_ref, g_ref, w_ref, o_ref, xn_ref):
    @pl.when(pl.program_id(1) == 0)
    def _():
        x = x_ref[...]
        ms = jnp.mean(x * x, axis=-1, keepdims=True)
        xn_ref[...] = (x * lax.rsqrt(ms + NORM_EPS) * g_ref[...]).astype(BF16)

    o_ref[...] = jnp.dot(xn_ref[...], w_ref[...], preferred_element_type=F32)


def _inproj(x_all, g_mix, w_in_p, tm, tn):
    nt, d = x_all.shape
    n_out = w_in_p.shape[1]
    return pl.pallas_call(
        _inproj_kernel,
        grid=(nt // tm, n_out // tn),
        in_specs=[pl.BlockSpec((tm, d), lambda i, j: (i, 0)),
                  pl.BlockSpec((1, d), lambda i, j: (0, 0)),
                  pl.BlockSpec((d, tn), lambda i, j: (0, j))],
        out_specs=pl.BlockSpec((tm, tn), lambda i, j: (i, j)),
        out_shape=jax.ShapeDtypeStruct((nt, n_out), F32),
        scratch_shapes=[pltpu.VMEM((tm, d), BF16)],
        compiler_params=_cparams(("parallel", "arbitrary")),
        name="inproj",
    )(x_all, g_mix, w_in_p)


def _softplus(z):
    return jnp.maximum(z, 0.0) + jnp.log1p(jnp.exp(-jnp.abs(z)))


def _rwkv_prep_kernel(*refs, width, carry_prev):
    if carry_prev:
        (rkv_ref, lo_ref, mu_rkv_ref, mu_lo_ref, w0_ref, a0_ref,
         wd_ref, wa_ref, wg_ref, _g_all,
         r_out, w_out, k_out, v_out, a_out, g_out, c_rkv, c_lo) = refs

        @pl.when(pl.program_id(1) == 0)
        def _():
            c_rkv[...] = jnp.zeros_like(c_rkv)
            c_lo[...] = jnp.zeros_like(c_lo)
    else:
        (rkv_ref, lo_ref, prev_rkv_ref, prev_lo_ref, mu_rkv_ref, mu_lo_ref, w0_ref, a0_ref,
         wd_ref, wa_ref, wg_ref, _g_all,
         r_out, w_out, k_out, v_out, a_out, g_out) = refs

    tm = rkv_ref.shape[0]
    first_row = lax.broadcasted_iota(jnp.int32, (tm, 1), 0) == 0

    def shifted(p, prev_ref, carry_ref, cols):
        if carry_prev:
            prev = jnp.where(first_row, carry_ref[:, cols], pltpu.roll(p, 1, 0))
        else:
            prev = prev_ref[:, cols]
        return prev

    def lerp(p, prev, mu):
        return p + (prev - p) * mu

    lo_cols = slice(0, lo_ref.shape[1])
    p_lo = lo_ref[...]
    q_lo = lerp(p_lo, shifted(p_lo, None if carry_prev else prev_lo_ref,
                              c_lo if carry_prev else None, lo_cols), mu_lo_ref[...])
    xw = q_lo[:, 0:LORA_PAD]
    xa = q_lo[:, LORA_PAD:2 * LORA_PAD]
    xg = q_lo[:, 2 * LORA_PAD:]
    dw = jnp.dot(jnp.tanh(xw).astype(BF16), wd_ref[...], preferred_element_type=F32)
    wlog = -_softplus(-(w0_ref[...] + dw)) - 0.5
    w_out[...] = jnp.exp(-jnp.exp(wlog))
    a = jax.nn.sigmoid(a0_ref[...] + jnp.dot(xa.astype(BF16), wa_ref[...],
                                             preferred_element_type=F32))
    a_out[...] = a
    g_out[...] = jnp.dot(jax.nn.sigmoid(xg).astype(BF16), wg_ref[...],
                         preferred_element_type=F32)

    def q_of(idx):
        cols = slice(idx * width, (idx + 1) * width)
        p = rkv_ref[:, cols]
        prev = shifted(p, None if carry_prev else prev_rkv_ref,
                       c_rkv if carry_prev else None, cols)
        return lerp(p, prev, mu_rkv_ref[:, cols])

    r_out[...] = q_of(0)
    k_out[...] = q_of(1)
    v_out[...] = q_of(2)

    if carry_prev:
        c_rkv[...] = rkv_ref[tm - 1:tm, :]
        c_lo[...] = lo_ref[tm - 1:tm, :]


def _rwkv_prep(proj, row_block0, n_rows, tm, seqs, width, lo_col_block, lo_width,
               mu_rkv, mu_lo, w0, a0, wd, wa, wg, g_all, prev=None):
    carry_prev = prev is None
    per_seq = n_rows // seqs // tm
    row_map = lambda b, c: (row_block0 + b * per_seq + c, 0)
    lo_map = lambda b, c: (row_block0 + b * per_seq + c, lo_col_block)
    out_map = lambda b, c: (b * per_seq + c, 0)
    in_specs = [pl.BlockSpec((tm, 3 * width), row_map), pl.BlockSpec((tm, lo_width), lo_map)]
    args = [proj, proj]
    if not carry_prev:
        in_specs += [pl.BlockSpec((tm, 3 * width), out_map), pl.BlockSpec((tm, lo_width), out_map)]
        args += list(prev)
    consts = [mu_rkv, mu_lo, w0, a0, wd, wa, wg]
    in_specs += [_const_spec(c.shape) for c in consts] + [pl.BlockSpec(memory_space=pl.ANY)]
    args += consts + [g_all]
    scratch = []
    if carry_prev:
        scratch = [pltpu.VMEM((1, 3 * width), F32), pltpu.VMEM((1, lo_width), F32)]
    out_sd = jax.ShapeDtypeStruct((n_rows // seqs, seqs * width), F32)
    tmaj_map = lambda b, c: (c, b)
    return pl.pallas_call(
        functools.partial(_rwkv_prep_kernel, width=width, carry_prev=carry_prev),
        grid=(seqs, per_seq),
        in_specs=in_specs,
        out_specs=[pl.BlockSpec((tm, width), tmaj_map)] * 5 + [pl.BlockSpec((tm, width), row_map)],
        out_shape=[out_sd] * 5 + [jax.ShapeDtypeStruct(g_all.shape, g_all.dtype)],
        input_output_aliases={len(args) - 1: 5},
        scratch_shapes=scratch,
        compiler_params=_cparams(("parallel", "arbitrary")),
        name="rwkv_prep_seq" if carry_prev else "rwkv_prep_step",
    )(*args)


def _wkv_kernel(r_ref, w_ref, kraw_ref, al_ref, v_ref, s0_ref, kkc_ref, kac_ref, rk_ref, lw_ref,
                lb_ref, y_ref, sf_ref, s_scr, a_scr, b_scr, k_src, *dup_scr,
                ni, nj, tc, isplit, unroll):
    @pl.when(pl.program_id(1) == 0)
    def _():
        s_scr[...] = s0_ref[...]

    def dup(x):
        return jnp.concatenate([x, x], axis=-1) if isplit else x

    if isplit:
        w_src, r_src, v_src = dup_scr
    else:
        w_src, r_src, v_src = w_ref, r_ref, v_ref

    def prep(t, carry):
        kraw = dup(kraw_ref[t])
        al = dup(al_ref[t])
        kk = kraw * kkc_ref[...]
        k_src[t] = kraw * (1.0 + (al - 1.0) * kac_ref[...])
        ss = jnp.sum(kk * kk, axis=0, keepdims=True)
        kkn = kk / jnp.maximum(jnp.sqrt(ss), 1e-12)
        a_scr[t] = -kkn
        b_scr[t] = kkn * al
        if isplit:
            w_src[t] = dup(w_ref[t])
            r_src[t] = dup(r_ref[t])
            v = v_ref[t]
            v_src[t] = jnp.concatenate([v[:ni], v[ni:]], axis=-1)
        return carry

    lax.fori_loop(0, tc, prep, 0, unroll=unroll)

    def row(ref, t, j):
        return ref[t, j:j + 1, :]

    def tree(parts):
        while len(parts) > 1:
            parts = [parts[i] + parts[i + 1] for i in range(0, len(parts), 2)]
        return parts[0]

    n_acc = 4
    sa0 = tree([sum(s_scr[j] * row(a_scr, 0, j) for j in range(q, nj, n_acc))
                for q in range(n_acc)])

    def step(t, sa):
        v = v_src[t]
        tn = jnp.minimum(t + 1, tc - 1)
        y = [None] * n_acc
        san = [None] * n_acc
        for j in range(nj):
            s = s_scr[j] * row(w_src, t, j) + sa * row(b_scr, t, j) + v * row(k_src, t, j)
            s_scr[j] = s
            yj = s * row(r_src, t, j)
            sj = s * row(a_scr, tn, j)
            q = j % n_acc
            y[q] = yj if y[q] is None else y[q] + yj
            san[q] = sj if san[q] is None else san[q] + sj
        y_ref[t] = tree(y)
        return tree(san)

    lax.fori_loop(0, tc, step, sa0)

    def isum(x):
        s = jnp.broadcast_to(jnp.sum(x, axis=0, keepdims=True), (8, LANES))
        if isplit:
            s = s + pltpu.roll(s, LANES // 2, 1)
        return s[0:1]

    def post(t, carry):
        y = y_ref[t]
        v = v_src[t]
        mu = isum(y) * (1.0 / HEAD)
        d = y - mu
        var = isum(d * d) * (1.0 / HEAD)
        yn = d * lax.rsqrt(var + LNX_EPS) * lw_ref[...] + lb_ref[...]
        bonus = jnp.sum(r_src[t] * k_src[t] * rk_ref[...], axis=0, keepdims=True)
        y_ref[t] = yn + bonus * v
        return carry

    lax.fori_loop(0, tc, post, 0, unroll=unroll)

    @pl.when(pl.program_id(1) == pl.num_programs(1) - 1)
    def _():
        sf_ref[...] = s_scr[...]


def _wkv(r, w, kraw, al, v, s0, kkc, kac, rk, lw, lb, tc, isplit, unroll):
    t, nj, jl = r.shape
    ni, lanes = s0.shape[1:]
    assert jl == (LANES // 2 if isplit else lanes) and v.shape == r.shape
    jspec = pl.BlockSpec((tc, nj, min(jl, LANES)), lambda l, c: (c, 0, l))
    dup_scr = []
    if isplit:
        dup_scr = [pltpu.VMEM((tc, nj, LANES), F32)] * 2 + [pltpu.VMEM((tc, ni, LANES), F32)]
    ispec = pl.BlockSpec((tc, ni, LANES), lambda l, c: (c, 0, l))
    sspec = pl.BlockSpec((nj, ni, LANES), lambda l, c: (0, 0, l))
    consts = [kkc, kac, rk, lw, lb]
    return pl.pallas_call(
        functools.partial(_wkv_kernel, ni=ni, nj=nj, tc=tc, isplit=isplit, unroll=unroll),
        grid=(lanes // LANES, t // tc),
        in_specs=[jspec, jspec, jspec, jspec, jspec, sspec] + [_const_spec(c.shape) for c in consts],
        out_specs=[ispec, sspec],
        out_shape=[jax.ShapeDtypeStruct((t, ni, lanes), F32),
                   jax.ShapeDtypeStruct((nj, ni, lanes), F32)],
        scratch_shapes=[pltpu.VMEM((nj, ni, LANES), F32),
                        pltpu.VMEM((tc, nj, LANES), F32),
                        pltpu.VMEM((tc, nj, LANES), F32),
                        pltpu.VMEM((tc, nj, LANES), F32)] + dup_scr,
        compiler_params=_cparams(("parallel", "arbitrary")),
        name="wkv_seq" if isplit else "wkv_step",
    )(r, w, kraw, al, v, s0, *consts)


def _s5_kernel(*refs, sequential, n_blk, pitch, unroll):
    if sequential:
        (u_ref, bre_ref, bim_ref, cre_ref, cim_ref, d_ref, are_ref, aim_ref, wglu_ref, bglu_ref,
         _yb_all, yb_ref, hre_out, him_out, st_re, st_im, c_re, c_im) = refs
    else:
        (u_ref, h0re_ref, h0im_ref, bre_ref, bim_ref, cre_ref, cim_ref, d_ref, are_ref, aim_ref,
         wglu_ref, bglu_ref, _yb_all, yb_ref, hre_out, him_out, st_re, st_im) = refs

    tm = u_ref.shape[0]
    kin = bre_ref.shape[1]
    kst = bre_ref.shape[2]
    tiles_per_blk = kst // LANES
    n_tiles = n_blk * tiles_per_blk
    u = u_ref[...]
    ub = u.astype(BF16)

    def tile_rows(k):
        return slice(k * pitch, k * pitch + tm)

    for kb in range(n_blk):
        ukb = ub[:, kb * kin:(kb + 1) * kin]
        bu_re = jnp.dot(ukb, bre_ref[kb], preferred_element_type=F32)
        bu_im = jnp.dot(ukb, bim_ref[kb], preferred_element_type=F32)
        for n in range(tiles_per_blk):
            k = kb * tiles_per_blk + n
            cols = slice(n * LANES, (n + 1) * LANES)
            if sequential:
                st_re[tile_rows(k), :] = bu_re[:, cols]
                st_im[tile_rows(k), :] = bu_im[:, cols]
            else:
                kc = slice(k * LANES, (k + 1) * LANES)
                ar, ai = are_ref[k:k + 1, :], aim_ref[k:k + 1, :]
                h0r, h0i = h0re_ref[:, kc], h0im_ref[:, kc]
                nr = bu_re[:, cols] + (ar * h0r - ai * h0i)
                ni = bu_im[:, cols] + (ar * h0i + ai * h0r)
                st_re[tile_rows(k), :] = nr
                st_im[tile_rows(k), :] = ni
                hre_out[:, kc] = nr
                him_out[:, kc] = ni

    if sequential:
        @pl.when(pl.program_id(1) == 0)
        def _():
            c_re[...] = jnp.zeros_like(c_re)
            c_im[...] = jnp.zeros_like(c_im)

        ar = are_ref[...]
        ai = aim_ref[...]
        groups = range(n_tiles // SUBLANES)

        def token_rows(t, m):
            return pl.ds(t + m * SUBLANES * pitch, SUBLANES, stride=pitch)

        def step(t, h):
            hr, hi = h
            bur = jnp.concatenate([st_re[token_rows(t, m), :] for m in groups], axis=0)
            bui = jnp.concatenate([st_im[token_rows(t, m), :] for m in groups], axis=0)
            nr = ar * hr - ai * hi + bur
            ni = ar * hi + ai * hr + bui
            for m in groups:
                st_re[token_rows(t, m), :] = nr[m * SUBLANES:(m + 1) * SUBLANES]
                st_im[token_rows(t, m), :] = ni[m * SUBLANES:(m + 1) * SUBLANES]
            return nr, ni

        hr, hi = lax.fori_loop(0, tm, step, (c_re[...], c_im[...]), unroll=unroll)
        c_re[...] = hr
        c_im[...] = hi
        hre_out[0] = hr
        him_out[0] = hi

    ys = []
    for kb in range(n_blk):
        tiles = range(kb * tiles_per_blk, (kb + 1) * tiles_per_blk)
        h_re = jnp.concatenate([st_re[tile_rows(k), :] for k in tiles], axis=1)
        h_im = jnp.concatenate([st_im[tile_rows(k), :] for k in tiles], axis=1)
        yre = jnp.dot(h_re.astype(BF16), cre_ref[kb], preferred_element_type=F32)
        yim = jnp.dot(h_im.astype(BF16), cim_ref[kb], preferred_element_type=F32)
        ys.append(yre - yim)
    y = jnp.concatenate(ys, axis=1) + d_ref[...] * u
    y = jax.nn.gelu(y)
    gate = jnp.dot(y.astype(BF16), wglu_ref[...], preferred_element_type=F32) + bglu_ref[...]
    yb_ref[...] = (y * jax.nn.sigmoid(gate)).astype(BF16)


def _s5(proj, row_block0, n_rows, tm, seqs, u_col_block, width, consts, yb_all, h0=None):
    sequential = h0 is None
    bre = consts[0]
    n_blk, _, kst = bre.shape
    n_state = n_blk * kst
    n_tiles = n_state // LANES
    assert tm % SUBLANES == 0 and n_tiles % SUBLANES == 0
    pitch = tm
    per_seq = n_rows // seqs // tm
    u_map = lambda b, c: (row_block0 + b * per_seq + c, u_col_block)
    out_map = lambda b, c: (b * per_seq + c, 0)
    row_map = lambda b, c: (row_block0 + b * per_seq + c, 0)
    in_specs = [pl.BlockSpec((tm, width), u_map)]
    args = [proj]
    if not sequential:
        in_specs += [pl.BlockSpec((tm, n_state), out_map)] * 2
        args += list(h0)
    in_specs += [_const_spec(c.shape) for c in consts] + [pl.BlockSpec(memory_space=pl.ANY)]
    args += list(consts) + [yb_all]
    scratch = [pltpu.VMEM((n_tiles * pitch, LANES), F32), pltpu.VMEM((n_tiles * pitch, LANES), F32)]
    if sequential:
        scratch += [pltpu.VMEM((n_tiles, LANES), F32), pltpu.VMEM((n_tiles, LANES), F32)]
        st_spec = pl.BlockSpec((1, n_tiles, LANES), lambda b, c: (b, 0, 0))
        st_shape = jax.ShapeDtypeStruct((seqs, n_tiles, LANES), F32)
    else:
        st_spec = pl.BlockSpec((tm, n_state), out_map)
        st_shape = jax.ShapeDtypeStruct((n_rows, n_state), F32)
    return pl.pallas_call(
        functools.partial(_s5_kernel, sequential=sequential, n_blk=n_blk, pitch=pitch, unroll=4),
        grid=(seqs, per_seq),
        in_specs=in_specs,
        out_specs=[pl.BlockSpec((tm, width), row_map), st_spec, st_spec],
        out_shape=[jax.ShapeDtypeStruct(yb_all.shape, yb_all.dtype), st_shape, st_shape],
        input_output_aliases={len(args) - 1: 0},
        scratch_shapes=scratch,
        compiler_params=_cparams(("parallel", "arbitrary")),
        name="s5_seq" if sequential else "s5_step",
    )(*args)


def _route(logits, n_grp):
    lane = lax.broadcasted_iota(jnp.int32, logits.shape, 1).astype(F32)
    neg = jnp.float32(-1e30)
    big = jnp.float32(1e9)
    is_grp = lane < n_grp
    gl = jnp.where(is_grp, logits, neg)
    gmax = jnp.max(gl, axis=1, keepdims=True)
    gsum = jnp.sum(jnp.where(is_grp, jnp.exp(gl - gmax), 0.0), axis=1, keepdims=True)
    g_p = 1.0 / gsum
    g_idx = jnp.min(jnp.where(is_grp & (gl == gmax), lane, big), axis=1, keepdims=True)
    lo = n_grp + g_idx * EXPERTS_PER_GROUP
    in_grp = (lane >= lo) & (lane < lo + EXPERTS_PER_GROUP)
    el = jnp.where(in_grp, logits, neg)
    v1 = jnp.max(el, axis=1, keepdims=True)
    i1 = jnp.min(jnp.where(in_grp & (el == v1), lane, big), axis=1, keepdims=True)
    rest = in_grp & (lane != i1)
    el2 = jnp.where(rest, logits, neg)
    v2 = jnp.max(el2, axis=1, keepdims=True)
    i2 = jnp.min(jnp.where(rest & (el2 == v2), lane, big), axis=1, keepdims=True)
    e2 = jnp.exp(v2 - v1)
    w1 = g_p / (1.0 + e2)
    w2 = g_p * e2 / (1.0 + e2)
    return (jnp.where(lane == ROUTE_E1, i1 - n_grp, 0.0) + jnp.where(lane == ROUTE_E2, i2 - n_grp, 0.0)
            + jnp.where(lane == ROUTE_W1, w1, 0.0) + jnp.where(lane == ROUTE_W2, w2, 0.0))


def _merge_kernel(ya_ref, g_ref, yb_ref, ga_ref, gb_ref, x_ref, wa_ref, wb_ref, wo_ref,
                  gf_ref, wr_ref, br_ref, h_out, xn_out, comb_out, *, n_grp):
    ya = (ya_ref[...] * g_ref[...]).astype(BF16)
    ma = jnp.dot(ya, wa_ref[...], preferred_element_type=F32)
    mb = jnp.dot(yb_ref[...], wb_ref[...], preferred_element_type=F32)
    merged = jax.nn.sigmoid(ga_ref[...]) * ma + jax.nn.sigmoid(gb_ref[...]) * mb
    h = x_ref[...] + jnp.dot(merged.astype(BF16), wo_ref[...], preferred_element_type=F32)
    h_out[...] = h
    ms = jnp.mean(h * h, axis=-1, keepdims=True)
    xn = h * lax.rsqrt(ms + NORM_EPS) * gf_ref[...]
    xn_out[...] = xn
    x_hi = xn.astype(BF16)
    x_lo = (xn - x_hi.astype(F32)).astype(BF16)
    w_hi, w_lo = wr_ref[0], wr_ref[1]
    logits = (jnp.dot(x_hi, w_hi, preferred_element_type=F32)
              + (jnp.dot(x_hi, w_lo, preferred_element_type=F32)
                 + jnp.dot(x_lo, w_hi, preferred_element_type=F32))) + br_ref[...]
    comb_out[...] = _route(logits, n_grp)


def _merge(ya, g, yb, proj, x_all, w_br_a, w_br_b, w_out, g_ffn, w_router, b_router,
           tm, ga_col_block, n_grp):
    nt, d = x_all.shape
    wdt = ya.shape[1]
    row = lambda i: (i, 0)
    single = dict(pipeline_mode=pl.Buffered(1))
    in_specs = [pl.BlockSpec((tm, wdt), row), pl.BlockSpec((tm, wdt), row),
                pl.BlockSpec((tm, wdt), row),
                pl.BlockSpec((tm, d), lambda i: (i, ga_col_block)),
                pl.BlockSpec((tm, d), lambda i: (i, ga_col_block + 1)),
                pl.BlockSpec((tm, d), row),
                pl.BlockSpec(w_br_a.shape, lambda i: (0, 0), **single),
                pl.BlockSpec(w_br_b.shape, lambda i: (0, 0), **single),
                pl.BlockSpec(w_out.shape, lambda i: (0, 0), **single),
                _const_spec(g_ffn.shape), _const_spec(w_router.shape), _const_spec(b_router.shape)]
    return pl.pallas_call(
        functools.partial(_merge_kernel, n_grp=n_grp),
        grid=(nt // tm,),
        in_specs=in_specs,
        out_specs=[pl.BlockSpec((tm, d), row), pl.BlockSpec((tm, d), row),
                   pl.BlockSpec((tm, LANES), row)],
        out_shape=[jax.ShapeDtypeStruct((nt, d), F32), jax.ShapeDtypeStruct((nt, d), F32),
                   jax.ShapeDtypeStruct((nt, LANES), F32)],
        compiler_params=_cparams(("parallel",)),
        name="merge_route",
    )(ya, g, yb, proj, proj, x_all, w_br_a, w_br_b, w_out, g_ffn, w_router, b_router)


def _prefix_loops(n_rows, row):
    def block(b, c):
        for u in range(SUBLANES):
            row(b * SUBLANES + u)
        return c

    def single(r, c):
        row(r)
        return c

    n_blocks = lax.shift_right_logical(n_rows, LOG2_SUBLANES)
    lax.fori_loop(0, n_blocks, block, 0)
    lax.fori_loop(n_blocks * SUBLANES, n_rows, single, 0)


def _moe_kernel(te_ref, first_ref, nvalid_ref, nused_ref, src_cur, src_nxt,
                xn_hbm, wg_ref, wu_ref, wd_ref, o_ref,
                xbuf, wg_bf, wu_bf, wd_bf, gsem, *, tm):
    i = pl.program_id(0)
    n_used = nused_ref[0]
    slot = lax.rem(i, 2)

    def prefix_wait(n_rows, copy_of):
        n_full = pl.multiple_of(
            lax.shift_left(lax.shift_right_logical(n_rows, LOG2_SUBLANES), LOG2_SUBLANES), SUBLANES)

        @pl.when(n_full > 0)
        def _():
            copy_of(0, n_full).wait()

        def single(r, c):
            copy_of(r, 1).wait()
            return c
        lax.fori_loop(n_full, n_rows, single, 0)

    def gather_start(src, s, n_rows):
        _prefix_loops(n_rows, lambda r: pltpu.make_async_copy(
            xn_hbm.at[pl.ds(src[0, 0, r], 1)], xbuf.at[s, pl.ds(r, 1)], gsem.at[s]).start())

    def gather_wait(s, n_rows):
        prefix_wait(n_rows, lambda r0, n: pltpu.make_async_copy(
            xn_hbm.at[pl.ds(0, n)], xbuf.at[s, pl.ds(r0, n)], gsem.at[s]))

    @pl.when(i == 0)
    def _():
        xbuf[...] = jnp.zeros_like(xbuf)
        gather_start(src_cur, 0, nvalid_ref[0])

    @pl.when(i < n_used)
    def _():
        gather_wait(slot, nvalid_ref[i])

    @pl.when(i + 1 < n_used)
    def _():
        gather_start(src_nxt, 1 - slot, nvalid_ref[i + 1])

    @pl.when(i < n_used)
    def _():
        @pl.when(first_ref[i] == 1)
        def _():
            wg_bf[...] = wg_ref[0].astype(BF16)
            wu_bf[...] = wu_ref[0].astype(BF16)
            wd_bf[...] = wd_ref[0].astype(BF16)

        x = xbuf[slot].astype(BF16)
        xg = jnp.dot(x, wg_bf[...], preferred_element_type=F32)
        xu = jnp.dot(x, wu_bf[...], preferred_element_type=F32)
        hid = (jax.nn.silu(xg) * xu).astype(BF16)
        o_ref[...] = jnp.dot(hid, wd_bf[...], preferred_element_type=F32)

    @pl.when(i >= n_used)
    def _():
        o_ref[...] = jnp.zeros_like(o_ref)


def _moe(xn, tile_expert, tile_first, tile_valid, n_used, src_rows, wg, wu, wd, tm):
    nt, d = xn.shape
    n_exp, _, de = wg.shape
    n_tiles = src_rows.shape[0]
    smem_cur = pl.BlockSpec((1, 1, tm), lambda i, *_: (i, 0, 0), memory_space=pltpu.SMEM)
    smem_nxt = pl.BlockSpec((1, 1, tm), lambda i, *_: (jnp.minimum(i + 1, n_tiles - 1), 0, 0),
                            memory_space=pltpu.SMEM)
    grid_spec = pltpu.PrefetchScalarGridSpec(
        num_scalar_prefetch=4,
        grid=(n_tiles,),
        in_specs=[smem_cur, smem_nxt,
                  pl.BlockSpec(memory_space=pl.ANY),
                  pl.BlockSpec((1, d, de), lambda i, te, *_: (te[i], 0, 0)),
                  pl.BlockSpec((1, d, de), lambda i, te, *_: (te[i], 0, 0)),
                  pl.BlockSpec((1, de, d), lambda i, te, *_: (te[i], 0, 0))],
        out_specs=pl.BlockSpec((tm, d), lambda i, *_: (i, 0)),
        scratch_shapes=[pltpu.VMEM((2, tm, d), F32),
                        pltpu.VMEM((d, de), BF16), pltpu.VMEM((d, de), BF16),
                        pltpu.VMEM((de, d), BF16), pltpu.SemaphoreType.DMA((2,))])
    return pl.pallas_call(
        functools.partial(_moe_kernel, tm=tm),
        grid_spec=grid_spec,
        out_shape=jax.ShapeDtypeStruct((n_tiles * tm, d), F32),
        compiler_params=_cparams(("arbitrary",)),
        name="moe_grouped",
    )(tile_expert, tile_first, tile_valid, n_used, src_rows, src_rows, xn, wg, wu, wd)


def _moe_plan(route, n_exp, tm):
    nt = route.shape[0]
    n_pairs = 2 * nt
    n_tiles = n_pairs // tm + n_exp
    eid = jnp.concatenate([route[:, ROUTE_E1], route[:, ROUTE_E2]]).astype(jnp.int32)
    onehot = (eid[:, None] == jnp.arange(n_exp, dtype=jnp.int32)[None, :]).astype(jnp.int32)
    csum = jnp.cumsum(onehot, axis=0)
    rank = jnp.take_along_axis(csum, eid[:, None], axis=1)[:, 0] - 1
    cnt = csum[-1]
    tiles = (cnt + tm - 1) // tm
    tile_end = jnp.cumsum(tiles)
    n_used = tile_end[-1]
    pos = (tile_end - tiles)[eid] * tm + rank
    pair = jnp.arange(n_pairs, dtype=jnp.int32)
    src_rows = jnp.zeros((n_tiles * tm,), jnp.int32).at[pos].set(
        pair, unique_indices=True, mode="promise_in_bounds") % nt
    tile_id = jnp.minimum(jnp.arange(n_tiles, dtype=jnp.int32), n_used - 1)
    tile_expert = jnp.sum((tile_end[None, :] <= tile_id[:, None]).astype(jnp.int32), axis=1)
    tile_first = jnp.concatenate(
        [jnp.ones((1,), jnp.int32), (tile_expert[1:] != tile_expert[:-1]).astype(jnp.int32)])
    tile_valid = jnp.clip(cnt[tile_expert] - (tile_id - (tile_end - tiles)[tile_expert]) * tm, 0, tm)
    return (tile_expert, tile_first, tile_valid.astype(jnp.int32),
            n_used.reshape(1).astype(jnp.int32), src_rows.reshape(n_tiles, 1, tm),
            pos[:nt], pos[nt:])


def _final_kernel(p1_cur, p1_nxt, p2_cur, p2_nxt, h_ref, route_ref, g_ref, y_hbm, o_ref,
                  ybuf, sem, *, tm):
    i = pl.program_id(0)
    slot = lax.rem(i, 2)

    def gather_start(p1, p2, s):
        def row(r):
            pltpu.make_async_copy(y_hbm.at[pl.ds(p1[0, 0, r], 1)], ybuf.at[s, 0, pl.ds(r, 1)],
                                  sem.at[s]).start()
            pltpu.make_async_copy(y_hbm.at[pl.ds(p2[0, 0, r], 1)], ybuf.at[s, 1, pl.ds(r, 1)],
                                  sem.at[s]).start()
        _prefix_loops(tm, row)

    def gather_wait(s):
        for e in range(2):
            pltpu.make_async_copy(y_hbm.at[pl.ds(0, tm)], ybuf.at[s, e], sem.at[s]).wait()

    @pl.when(i == 0)
    def _():
        gather_start(p1_cur, p2_cur, 0)

    gather_wait(slot)

    @pl.when(i + 1 < pl.num_programs(0))
    def _():
        gather_start(p1_nxt, p2_nxt, 1 - slot)

    lane = lax.broadcasted_iota(jnp.int32, route_ref.shape, 1)
    route = route_ref[...]
    w1 = jnp.sum(jnp.where(lane == ROUTE_W1, route, 0.0), axis=1, keepdims=True)
    w2 = jnp.sum(jnp.where(lane == ROUTE_W2, route, 0.0), axis=1, keepdims=True)
    h = h_ref[...] + (w1 * ybuf[slot, 0] + w2 * ybuf[slot, 1])
    ms = jnp.mean(h * h, axis=-1, keepdims=True)
    o_ref[...] = h * lax.rsqrt(ms + NORM_EPS) * g_ref[...]


def _final(h, y_sorted, pos1, pos2, route, g_final, row0, seg_stride, seg_rows, n_seg, tm):
    nt, d = h.shape
    per_seg = seg_rows // tm
    n_steps = n_seg * per_seg
    assert all(x % SUBLANES == 0 for x in (row0, seg_stride, tm, nt))

    def off(i):
        return pl.multiple_of(row0 + (i // per_seg) * seg_stride + (i % per_seg) * tm, SUBLANES)

    rows = lambda width: pl.BlockSpec((pl.Element(tm), pl.Element(width)), lambda i: (off(i), 0))
    smem = lambda nxt: pl.BlockSpec(
        (1, 1, tm), lambda i: (jnp.minimum(i + nxt, n_steps - 1), 0, 0), memory_space=pltpu.SMEM)
    sel = lambda p: jnp.concatenate(
        [p[row0 + s * seg_stride:row0 + s * seg_stride + seg_rows] for s in range(n_seg)]
    ).reshape(n_steps, 1, tm)
    p1, p2 = sel(pos1), sel(pos2)
    return pl.pallas_call(
        functools.partial(_final_kernel, tm=tm),
        grid=(n_steps,),
        in_specs=[smem(0), smem(1), smem(0), smem(1), rows(d), rows(LANES),
                  pl.BlockSpec((1, d), lambda i: (0, 0)), pl.BlockSpec(memory_space=pl.ANY)],
        out_specs=pl.BlockSpec((tm, d), lambda i: (i, 0)),
        out_shape=jax.ShapeDtypeStruct((n_seg * seg_rows, d), F32),
        scratch_shapes=[pltpu.VMEM((2, 2, tm, d), F32), pltpu.SemaphoreType.DMA((2,))],
        compiler_params=_cparams(("arbitrary",)),
        name="final_norm",
    )(p1, p1, p2, p2, h, route, g_final, y_sorted)


def _pad_cols(w, to):
    return jnp.pad(w, ((0, 0), (0, to - w.shape[1])))


def _pad_rows(w, to):
    return jnp.pad(w, ((0, to - w.shape[0]), (0, 0)))


def _tile_plan(n_t, t_p, seq):
    plan = dict(
        inproj_rows=n_t // 6,
        inproj_cols=4 * LANES,
        seq_rows=t_p // 6,
        wkv_chunk=48, wkv_unroll=8,
        merge_rows=2 * LANES, moe_rows=2 * LANES, final_rows=4 * LANES)
    assert n_t % plan["inproj_rows"] == 0 and plan["inproj_rows"] % SUBLANES == 0
    assert t_p % plan["seq_rows"] == 0 and plan["seq_rows"] % SUBLANES == 0
    assert t_p % plan["wkv_chunk"] == 0 and plan["wkv_chunk"] % plan["wkv_unroll"] == 0
    assert n_t % plan["merge_rows"] == 0 and seq % plan["final_rows"] == 0
    return plan


def _s5_consts(lam_re, lam_im, log_dt, b_re, b_im, c_re, c_im, d, w_glu, b_glu):
    dt = jnp.exp(log_dt)[:, None]
    mag = jnp.exp(lam_re * dt)
    abar_re = mag * jnp.cos(lam_im * dt)
    abar_im = mag * jnp.sin(lam_im * dt)
    den = lam_re * lam_re + lam_im * lam_im
    nr = abar_re - 1.0
    coef_re = (nr * lam_re + abar_im * lam_im) / den
    coef_im = (abar_im * lam_re - nr * lam_im) / den
    bbar_re = coef_re[..., None] * b_re - coef_im[..., None] * b_im
    bbar_im = coef_re[..., None] * b_im + coef_im[..., None] * b_re
    n_g, n_p, n_c = b_re.shape
    gpb = LANES // n_c
    eye = jnp.eye(gpb, dtype=F32)

    def in_blk(bb):
        bb = bb.reshape(n_g // gpb, gpb, n_p, n_c)
        return jnp.einsum('kgpc,gh->kgchp', bb, eye).reshape(
            n_g // gpb, gpb * n_c, gpb * n_p).astype(BF16)

    def out_blk(cc):
        cc = cc.reshape(n_g // gpb, gpb, n_c, n_p)
        return jnp.einsum('kgcp,gh->khpgc', cc, eye).reshape(
            n_g // gpb, gpb * n_p, gpb * n_c).astype(BF16)

    return (in_blk(bbar_re), in_blk(bbar_im), out_blk(c_re), out_blk(c_im),
            d.reshape(1, -1), abar_re.reshape(-1, LANES), abar_im.reshape(-1, LANES),
            w_glu.astype(BF16), b_glu.reshape(1, -1))


def kernel(x_prompt, x_sample, state_shift, state_wkv, state_ssm_re, state_ssm_im, meta_tokens, g_mix, w_in, shift_mu, w0, w_decay_up, a0, w_aaa_up, w_gate_up, k_k, k_a, r_k, lnx_w, lnx_b, ssm_lam_re, ssm_lam_im, ssm_log_dt, ssm_b_re, ssm_b_im, ssm_c_re, ssm_c_im, ssm_d, w_glu, b_glu, w_br_a, w_br_b, w_out, g_ffn, w_router_grp, b_router_grp, w_router_exp, b_router_exp, w_exp_gate, w_exp_up, w_exp_down, g_final):
    depth = g_mix.shape[0]
    assert depth == 1, "single-layer trunk"
    bsz, seq, d = x_prompt.shape
    nb = x_sample.shape[0]
    assert x_sample.shape[1] == 1
    t_p = seq + N_META
    n_p = bsz * t_p
    width = k_k.shape[1]
    heads = width // HEAD
    n_dl, n_al, n_gl = w_decay_up.shape[1], w_aaa_up.shape[1], w_gate_up.shape[1]
    n_grp = w_router_grp.shape[2]
    n_exp = w_router_exp.shape[2]
    assert bsz * heads * 2 == LANES and (nb * heads) % LANES == 0

    s_row0 = -(-n_p // nb) * nb
    n_t = s_row0 + nb
    pieces = []
    for b in range(bsz):
        pieces += [meta_tokens, x_prompt[b]]
    x_all = jnp.concatenate(
        pieces + [jnp.zeros((s_row0 - n_p, d), F32), x_sample.reshape(nb, d)], axis=0)

    c_rkv = 3 * width
    c_xw, c_xa, c_xg = c_rkv, c_rkv + n_dl, c_rkv + n_dl + n_al
    c_u = c_xg + n_gl
    lo_width = 2 * LORA_PAD + n_gl

    def regroup(m):
        return jnp.concatenate(
            [m[:, :c_rkv], m[:, c_u:], _pad_cols(m[:, c_xw:c_xa], LORA_PAD),
             _pad_cols(m[:, c_xa:c_xg], LORA_PAD), m[:, c_xg:c_u]], axis=1)

    def ungroup(m):
        lo = c_rkv + width + 2 * d
        return jnp.concatenate(
            [m[:, :c_rkv], m[:, lo:lo + n_dl], m[:, lo + LORA_PAD:lo + LORA_PAD + n_al],
             m[:, lo + 2 * LORA_PAD:]], axis=1)

    w_in_p = regroup(w_in[0].astype(BF16))
    n_proj = w_in_p.shape[1]
    tiles = _tile_plan(n_t, t_p, seq)
    proj = _inproj(x_all, g_mix, w_in_p, tm=tiles["inproj_rows"], tn=tiles["inproj_cols"])

    mu_all = regroup(_pad_cols(shift_mu, c_u + width + 2 * d))
    mu_rkv, mu_lo = mu_all[:, :c_rkv], mu_all[:, n_proj - lo_width:]
    st_all = regroup(_pad_cols(state_shift[0], c_u + width + 2 * d))
    prev_s = (st_all[:, :c_rkv], st_all[:, n_proj - lo_width:])
    lo_col_block = (n_proj - lo_width) // lo_width
    prep_w = (mu_rkv, mu_lo, w0, a0,
              _pad_rows(w_decay_up[0], LORA_PAD).astype(BF16),
              _pad_rows(w_aaa_up[0], LORA_PAD).astype(BF16), w_gate_up[0].astype(BF16))

    tm_seq = tiles["seq_rows"]
    g_all = jnp.zeros((n_t, width), F32)
    rp, wp, kp, vp, ap, g_all = _rwkv_prep(proj, 0, n_p, tm_seq, bsz, width, lo_col_block,
                                                lo_width, *prep_w, g_all)
    rs, ws, ks, vs, as_, g_all = _rwkv_prep(proj, s_row0 // nb, nb, nb, 1, width,
                                                 lo_col_block, lo_width, *prep_w, g_all,
                                                 prev=prev_s)

    s5c = _s5_consts(ssm_lam_re[0], ssm_lam_im[0], ssm_log_dt[0], ssm_b_re[0], ssm_b_im[0],
                     ssm_c_re[0], ssm_c_im[0], ssm_d[0], w_glu[0], b_glu)
    n_g, n_st = ssm_lam_re.shape[1], ssm_lam_re.shape[2]
    u_col_block = c_rkv // width
    yb_all = jnp.zeros((n_t, width), BF16)
    yb_all, re_p, im_p = _s5(proj, 0, n_p, tm_seq, bsz, u_col_block, width, s5c, yb_all)
    h0 = (state_ssm_re[0].reshape(nb, n_g * n_st), state_ssm_im[0].reshape(nb, n_g * n_st))
    yb_all, re_s, im_s = _s5(proj, s_row0 // nb, nb, nb, 1, u_col_block, width, s5c, yb_all,
                             h0=h0)

    half = HEAD // 2

    def to_lanes_p(z):
        return z.reshape(t_p, bsz * heads, HEAD).transpose(0, 2, 1)

    def to_lanes_s(z):
        return z.reshape(nb, heads, HEAD).transpose(2, 0, 1).reshape(1, HEAD, nb * heads)

    def head_const(c):
        return jnp.tile(c.reshape(heads, HEAD).T, (1, LANES // heads))

    rk_l = head_const(r_k[0])
    kk_l, ka_l = head_const(k_k[0]), head_const(k_a[0])
    lw_p = jnp.broadcast_to(lnx_w[0].reshape(heads, 2, half).transpose(2, 1, 0)[:, :, None, :],
                            (half, 2, bsz, heads)).reshape(half, LANES)
    lb_p = jnp.broadcast_to(lnx_b[0].reshape(heads, 2, half).transpose(2, 1, 0)[:, :, None, :],
                            (half, 2, bsz, heads)).reshape(half, LANES)
    lw_s = jnp.tile(lnx_w[0].reshape(heads, HEAD).T, (1, LANES // heads))
    lb_s = jnp.tile(lnx_b[0].reshape(heads, HEAD).T, (1, LANES // heads))

    s0_p = jnp.zeros((HEAD, half, LANES), F32)
    y_p, sf_p = _wkv(to_lanes_p(rp), to_lanes_p(wp), to_lanes_p(kp), to_lanes_p(ap),
                     to_lanes_p(vp), s0_p, kk_l, ka_l, rk_l, lw_p, lb_p,
                     tc=tiles["wkv_chunk"], isplit=True, unroll=tiles["wkv_unroll"])
    s0_s = state_wkv[0].transpose(3, 2, 0, 1).reshape(HEAD, HEAD, nb * heads)
    y_s, sf_s = _wkv(to_lanes_s(rs), to_lanes_s(ws), to_lanes_s(ks), to_lanes_s(as_),
                     to_lanes_s(vs), s0_s, kk_l, ka_l, rk_l, lw_s, lb_s,
                     tc=1, isplit=False, unroll=1)

    ya_p = y_p.reshape(t_p, half, 2, bsz, heads).transpose(3, 0, 4, 2, 1).reshape(n_p, width)
    ya_s = y_s.reshape(HEAD, nb, heads).transpose(1, 2, 0).reshape(nb, width)
    wkv_p = sf_p.reshape(HEAD, half, 2, bsz, heads).transpose(3, 4, 2, 1, 0).reshape(
        1, bsz, heads, HEAD, HEAD)
    wkv_s = sf_s.reshape(HEAD, HEAD, nb, heads).transpose(2, 3, 1, 0)[None]

    def unify(a, b):
        return jnp.concatenate([a, jnp.zeros((s_row0 - n_p, a.shape[1]), a.dtype), b], axis=0)

    w_router = _pad_cols(jnp.concatenate([w_router_grp[0], w_router_exp[0]], axis=1), LANES)
    w_router_hi = w_router.astype(BF16)
    w_router = jnp.stack([w_router_hi, (w_router - w_router_hi.astype(F32)).astype(BF16)])
    b_router = _pad_cols(jnp.concatenate([b_router_grp, b_router_exp], axis=1), LANES)
    assert n_grp + n_exp <= LANES and n_exp == n_grp * EXPERTS_PER_GROUP
    h1, xn2, route = _merge(unify(ya_p, ya_s), g_all, yb_all, proj, x_all,
                           w_br_a[0].astype(BF16), w_br_b[0].astype(BF16), w_out[0].astype(BF16),
                           g_ffn, w_router, b_router, tm=tiles["merge_rows"],
                           ga_col_block=(c_rkv + width) // d,
                           n_grp=n_grp)
    tm_moe = tiles["moe_rows"]
    te, tf, tv, nu, src_rows, pos1, pos2 = _moe_plan(route, n_exp, tm_moe)
    y_moe = _moe(xn2, te, tf, tv, nu, src_rows, w_exp_gate[0], w_exp_up[0], w_exp_down[0], tm_moe)
    gfin = g_final.reshape(1, d)
    y_prompt = _final(h1, y_moe, pos1, pos2, route, gfin, N_META, t_p, seq, bsz,
                      tm=tiles["final_rows"]).reshape(bsz, seq, d)
    y_sample = _final(h1, y_moe, pos1, pos2, route, gfin, s_row0, 0, nb, 1,
                      tm=nb).reshape(nb, 1, d)
    last_p = jnp.concatenate([proj[(b + 1) * t_p - 1:(b + 1) * t_p] for b in range(bsz)], axis=0)
    shift_p = ungroup(last_p)[None]
    shift_s = ungroup(proj[s_row0:])[None]
    return (y_prompt, y_sample, shift_p, wkv_p,
            re_p.reshape(1, bsz, n_g, n_st), im_p.reshape(1, bsz, n_g, n_st),
            shift_s, wkv_s,
            re_s.reshape(1, nb, n_g, n_st), im_s.reshape(1, nb, n_g, n_st))
```

```python
import functools

import jax
import jax.numpy as jnp
from jax import lax
from jax.experimental import pallas as pl
from jax.experimental.pallas import tpu as pltpu

F32 = jnp.float32
BF16 = jnp.bfloat16

NORM_EPS = 1e-6
LNX_EPS = 64e-5
N_META = 16
HEAD = 64
EXPERTS_PER_GROUP = 8
LANES = 128
SUBLANES = 8
LOG2_SUBLANES = SUBLANES.bit_length() - 1
LORA_PAD = 128
ROUTE_E1, ROUTE_E2, ROUTE_W1, ROUTE_W2 = 0, 1, 2, 3
VMEM_LIMIT = 56 * 1024 * 1024


def _cparams(sem):
    return pltpu.CompilerParams(dimension_semantics=sem, vmem_limit_bytes=VMEM_LIMIT)


def _const_spec(shape):
    nd = len(shape)
    return pl.BlockSpec(shape, lambda *_: (0,) * nd)


def _inproj_kernel(x_ref, g_ref, w_ref, o_ref, xn_ref):
    @pl.when(pl.program_id(1) == 0)
    def _():
        x = x_ref[...]
        ms = jnp.mean(x * x, axis=-1, keepdims=True)
        xn_ref[...] = (x * lax.rsqrt(ms + NORM_EPS) * g_ref[...]).astype(BF16)

    o_ref[...] = jnp.dot(xn_ref[...], w_ref[...], preferred_element_type=F32)


def _inproj(x_all, g_mix, w_in_p, tm, tn):
    nt, d = x_all.shape
    n_out = w_in_p.shape[1]
    return pl.pallas_call(
        _inproj_kernel,
        grid=(nt // tm, n_out // tn),
        in_specs=[pl.BlockSpec((tm, d), lambda i, j: (i, 0)),
                  pl.BlockSpec((1, d), lambda i, j: (0, 0)),
                  pl.BlockSpec((d, tn), lambda i, j: (0, j))],
        out_specs=pl.BlockSpec((tm, tn), lambda i, j: (i, j)),
        out_shape=jax.ShapeDtypeStruct((nt, n_out), F32),
        scratch_shapes=[pltpu.VMEM((tm, d), BF16)],
        compiler_params=_cparams(("parallel", "arbitrary")),
        name="inproj",
    )(x_all, g_mix, w_in_p)


def _softplus(z):
    return jnp.maximum(z, 0.0) + jnp.log1p(jnp.exp(-jnp.abs(z)))


def _rwkv_prep_kernel(*refs, width, carry_prev):
    if carry_prev:
        (rkv_ref, lo_ref, mu_rkv_ref, mu_lo_ref, w0_ref, a0_ref,
         wd_ref, wa_ref, wg_ref, _g_all,
         r_out, w_out, k_out, v_out, a_out, g_out, c_rkv, c_lo) = refs

        @pl.when(pl.program_id(1) == 0)
        def _():
            c_rkv[...] = jnp.zeros_like(c_rkv)
            c_lo[...] = jnp.zeros_like(c_lo)
    else:
        (rkv_ref, lo_ref, prev_rkv_ref, prev_lo_ref, mu_rkv_ref, mu_lo_ref, w0_ref, a0_ref,
         wd_ref, wa_ref, wg_ref, _g_all,
         r_out, w_out, k_out, v_out, a_out, g_out) = refs

    tm = rkv_ref.shape[0]
    first_row = lax.broadcasted_iota(jnp.int32, (tm, 1), 0) == 0

    def shifted(p, prev_ref, carry_ref, cols):
        if carry_prev:
            prev = jnp.where(first_row, carry_ref[:, cols], pltpu.roll(p, 1, 0))
        else:
            prev = prev_ref[:, cols]
        return prev

    def lerp(p, prev, mu):
        return p + (prev - p) * mu

    lo_cols = slice(0, lo_ref.shape[1])
    p_lo = lo_ref[...]
    q_lo = lerp(p_lo, shifted(p_lo, None if carry_prev else prev_lo_ref,
                              c_lo if carry_prev else None, lo_cols), mu_lo_ref[...])
    xw = q_lo[:, 0:LORA_PAD]
    xa = q_lo[:, LORA_PAD:2 * LORA_PAD]
    xg = q_lo[:, 2 * LORA_PAD:]
    dw = jnp.dot(jnp.tanh(xw).astype(BF16), wd_ref[...], preferred_element_type=F32)
    wlog = -_softplus(-(w0_ref[...] + dw)) - 0.5
    w_out[...] = jnp.exp(-jnp.exp(wlog))
    a = jax.nn.sigmoid(a0_ref[...] + jnp.dot(xa.astype(BF16), wa_ref[...],
                                             preferred_element_type=F32))
    a_out[...] = a
    g_out[...] = jnp.dot(jax.nn.sigmoid(xg).astype(BF16), wg_ref[...],
                         preferred_element_type=F32)

    def q_of(idx):
        cols = slice(idx * width, (idx + 1) * width)
        p = rkv_ref[:, cols]
        prev = shifted(p, None if carry_prev else prev_rkv_ref,
                       c_rkv if carry_prev else None, cols)
        return lerp(p, prev, mu_rkv_ref[:, cols])

    r_out[...] = q_of(0)
    k_out[...] = q_of(1)
    v_out[...] = q_of(2)

    if carry_prev:
        c_rkv[...] = rkv_ref[tm - 1:tm, :]
        c_lo[...] = lo_ref[tm - 1:tm, :]


def _rwkv_prep(proj, row_block0, n_rows, tm, seqs, width, lo_col_block, lo_width,
               mu_rkv, mu_lo, w0, a0, wd, wa, wg, g_all, prev=None):
    carry_prev = prev is None
    per_seq = n_rows // seqs // tm
    row_map = lambda b, c: (row_block0 + b * per_seq + c, 0)
    lo_map = lambda b, c: (row_block0 + b * per_seq + c, lo_col_block)
    out_map = lambda b, c: (b * per_seq + c, 0)
    in_specs = [pl.BlockSpec((tm, 3 * width), row_map), pl.BlockSpec((tm, lo_width), lo_map)]
    args = [proj, proj]
    if not carry_prev:
        in_specs += [pl.BlockSpec((tm, 3 * width), out_map), pl.BlockSpec((tm, lo_width), out_map)]
        args += list(prev)
    consts = [mu_rkv, mu_lo, w0, a0, wd, wa, wg]
    in_specs += [_const_spec(c.shape) for c in consts] + [pl.BlockSpec(memory_space=pl.ANY)]
    args += consts + [g_all]
    scratch = []
    if carry_prev:
        scratch = [pltpu.VMEM((1, 3 * width), F32), pltpu.VMEM((1, lo_width), F32)]
    out_sd = jax.ShapeDtypeStruct((n_rows // seqs, seqs * width), F32)
    tmaj_map = lambda b, c: (c, b)
    return pl.pallas_call(
        functools.partial(_rwkv_prep_kernel, width=width, carry_prev=carry_prev),
        grid=(seqs, per_seq),
        in_specs=in_specs,
        out_specs=[pl.BlockSpec((tm, width), tmaj_map)] * 5 + [pl.BlockSpec((tm, width), row_map)],
        out_shape=[out_sd] * 5 + [jax.ShapeDtypeStruct(g_all.shape, g_all.dtype)],
        input_output_aliases={len(args) - 1: 5},
        scratch_shapes=scratch,
        compiler_params=_cparams(("parallel", "arbitrary")),
        name="rwkv_prep_seq" if carry_prev else "rwkv_prep_step",
    )(*args)


def _wkv_kernel(r_ref, w_ref, kraw_ref, al_ref, v_ref, s0_ref, kkc_ref, kac_ref, rk_ref, lw_ref,
                lb_ref, y_ref, sf_ref, s_scr, a_scr, b_scr, k_src, *dup_scr,
                ni, nj, tc, isplit, unroll):
    @pl.when(pl.program_id(1) == 0)
    def _():
        s_scr[...] = s0_ref[...]

    def dup(x):
        return jnp.concatenate([x, x], axis=-1) if isplit else x

    if isplit:
        w_src, r_src, v_src = dup_scr
    else:
        w_src, r_src, v_src = w_ref, r_ref, v_ref

    def prep(t, carry):
        kraw = dup(kraw_ref[t])
        al = dup(al_ref[t])
        kk = kraw * kkc_ref[...]
        k_src[t] = kraw * (1.0 + (al - 1.0) * kac_ref[...])
        ss = jnp.sum(kk * kk, axis=0, keepdims=True)
        kkn = kk / jnp.maximum(jnp.sqrt(ss), 1e-12)
        a_scr[t] = -kkn
        b_scr[t] = kkn * al
        if isplit:
            w_src[t] = dup(w_ref[t])
            r_src[t] = dup(r_ref[t])
            v = v_ref[t]
            v_src[t] = jnp.concatenate([v[:ni], v[ni:]], axis=-1)
        return carry

    lax.fori_loop(0, tc, prep, 0, unroll=unroll)

    def row(ref, t, j):
        return ref[t, j:j + 1, :]

    def tree(parts):
        while len(parts) > 1:
            parts = [parts[i] + parts[i + 1] for i in range(0, len(parts), 2)]
        return parts[0]

    n_acc = 4
    sa0 = tree([sum(s_scr[j] * row(a_scr, 0, j) for j in range(q, nj, n_acc))
                for q in range(n_acc)])

    def step(t, sa):
        v = v_src[t]
        tn = jnp.minimum(t + 1, tc - 1)
        y = [None] * n_acc
        san = [None] * n_acc
        for j in range(nj):
            s = s_scr[j] * row(w_src, t, j) + sa * row(b_scr, t, j) + v * row(k_src, t, j)
            s_scr[j] = s
            yj = s * row(r_src, t, j)
            sj = s * row(a_scr, tn, j)
            q = j % n_acc
            y[q] = yj if y[q] is None else y[q] + yj
            san[q] = sj if san[q] is None else san[q] + sj
        y_ref[t] = tree(y)
        return tree(san)

    lax.fori_loop(0, tc, step, sa0)

    def isum(x):
        s = jnp.broadcast_to(jnp.sum(x, axis=0, keepdims=True), (8, LANES))
        if isplit:
            s = s + pltpu.roll(s, LANES // 2, 1)
        return s[0:1]

    def post(t, carry):
        y = y_ref[t]
        v = v_src[t]
        mu = isum(y) * (1.0 / HEAD)
        d = y - mu
        var = isum(d * d) * (1.0 / HEAD)
        yn = d * lax.rsqrt(var + LNX_EPS) * lw_ref[...] + lb_ref[...]
        bonus = jnp.sum(r_src[t] * k_src[t] * rk_ref[...], axis=0, keepdims=True)
        y_ref[t] = yn + bonus * v
        return carry

    lax.fori_loop(0, tc, post, 0, unroll=unroll)

    @pl.when(pl.program_id(1) == pl.num_programs(1) - 1)
    def _():
        sf_ref[...] = s_scr[...]


def _wkv(r, w, kraw, al, v, s0, kkc, kac, rk, lw, lb, tc, isplit, unroll):
    t, nj, jl = r.shape
    ni, lanes = s0.shape[1:]
    assert jl == (LANES // 2 if isplit else lanes) and v.shape == r.shape
    jspec = pl.BlockSpec((tc, nj, min(jl, LANES)), lambda l, c: (c, 0, l))
    dup_scr = []
    if isplit:
        dup_scr = [pltpu.VMEM((tc, nj, LANES), F32)] * 2 + [pltpu.VMEM((tc, ni, LANES), F32)]
    ispec = pl.BlockSpec((tc, ni, LANES), lambda l, c: (c, 0, l))
    sspec = pl.BlockSpec((nj, ni, LANES), lambda l, c: (0, 0, l))
    consts = [kkc, kac, rk, lw, lb]
    return pl.pallas_call(
        functools.partial(_wkv_kernel, ni=ni, nj=nj, tc=tc, isplit=isplit, unroll=unroll),
        grid=(lanes // LANES, t // tc),
        in_specs=[jspec, jspec, jspec, jspec, jspec, sspec] + [_const_spec(c.shape) for c in consts],
        out_specs=[ispec, sspec],
        out_shape=[jax.ShapeDtypeStruct((t, ni, lanes), F32),
                   jax.ShapeDtypeStruct((nj, ni, lanes), F32)],
        scratch_shapes=[pltpu.VMEM((nj, ni, LANES), F32),
                        pltpu.VMEM((tc, nj, LANES), F32),
                        pltpu.VMEM((tc, nj, LANES), F32),
                        pltpu.VMEM((tc, nj, LANES), F32)] + dup_scr,
        compiler_params=_cparams(("parallel", "arbitrary")),
        name="wkv_seq" if isplit else "wkv_step",
    )(r, w, kraw, al, v, s0, *consts)


def _s5_kernel(*refs, sequential, n_blk, pitch, unroll):
    if sequential:
        (u_ref, bre_ref, bim_ref, cre_ref, cim_ref, d_ref, are_ref, aim_ref, wglu_ref, bglu_ref,
         _yb_all, yb_ref, hre_out, him_out, st_re, st_im, c_re, c_im) = refs
    else:
        (u_ref, h0re_ref, h0im_ref, bre_ref, bim_ref, cre_ref, cim_ref, d_ref, are_ref, aim_ref,
         wglu_ref, bglu_ref, _yb_all, yb_ref, hre_out, him_out, st_re, st_im) = refs

    tm = u_ref.shape[0]
    kin = bre_ref.shape[1]
    kst = bre_ref.shape[2]
    tiles_per_blk = kst // LANES
    n_tiles = n_blk * tiles_per_blk
    u = u_ref[...]
    ub = u.astype(BF16)

    def tile_rows(k):
        return slice(k * pitch, k * pitch + tm)

    for kb in range(n_blk):
        ukb = ub[:, kb * kin:(kb + 1) * kin]
        bu_re = jnp.dot(ukb, bre_ref[kb], preferred_element_type=F32)
        bu_im = jnp.dot(ukb, bim_ref[kb], preferred_element_type=F32)
        for n in range(tiles_per_blk):
            k = kb * tiles_per_blk + n
            cols = slice(n * LANES, (n + 1) * LANES)
            if sequential:
                st_re[tile_rows(k), :] = bu_re[:, cols]
                st_im[tile_rows(k), :] = bu_im[:, cols]
            else:
                kc = slice(k * LANES, (k + 1) * LANES)
                ar, ai = are_ref[k:k + 1, :], aim_ref[k:k + 1, :]
                h0r, h0i = h0re_ref[:, kc], h0im_ref[:, kc]
                nr = bu_re[:, cols] + (ar * h0r - ai * h0i)
                ni = bu_im[:, cols] + (ar * h0i + ai * h0r)
                st_re[tile_rows(k), :] = nr
                st_im[tile_rows(k), :] = ni
                hre_out[:, kc] = nr
                him_out[:, kc] = ni

    if sequential:
        @pl.when(pl.program_id(1) == 0)
        def _():
            c_re[...] = jnp.zeros_like(c_re)
            c_im[...] = jnp.zeros_like(c_im)

        ar = are_ref[...]
        ai = aim_ref[...]
        groups = range(n_tiles // SUBLANES)

        def token_rows(t, m):
            return pl.ds(t + m * SUBLANES * pitch, SUBLANES, stride=pitch)

        def step(t, h):
            hr, hi = h
            bur = jnp.concatenate([st_re[token_rows(t, m), :] for m in groups], axis=0)
            bui = jnp.concatenate([st_im[token_rows(t, m), :] for m in groups], axis=0)
            nr = ar * hr - ai * hi + bur
            ni = ar * hi + ai * hr + bui
            for m in groups:
                st_re[token_rows(t, m), :] = nr[m * SUBLANES:(m + 1) * SUBLANES]
                st_im[token_rows(t, m), :] = ni[m * SUBLANES:(m + 1) * SUBLANES]
            return nr, ni

        hr, hi = lax.fori_loop(0, tm, step, (c_re[...], c_im[...]), unroll=unroll)
        c_re[...] = hr
        c_im[...] = hi
        hre_out[0] = hr
        him_out[0] = hi

    ys = []
    for kb in range(n_blk):
        tiles = range(kb * tiles_per_blk, (kb + 1) * tiles_per_blk)
        h_re = jnp.concatenate([st_re[tile_rows(k), :] for k in tiles], axis=1)
        h_im = jnp.concatenate([st_im[tile_rows(k), :] for k in tiles], axis=1)
        yre = jnp.dot(h_re.astype(BF16), cre_ref[kb], preferred_element_type=F32)
        yim = jnp.dot(h_im.astype(BF16), cim_ref[kb], preferred_element_type=F32)
        ys.append(yre - yim)
    y = jnp.concatenate(ys, axis=1) + d_ref[...] * u
    y = jax.nn.gelu(y)
    gate = jnp.dot(y.astype(BF16), wglu_ref[...], preferred_element_type=F32) + bglu_ref[...]
    yb_ref[...] = (y * jax.nn.sigmoid(gate)).astype(BF16)


def _s5(proj, row_block0, n_rows, tm, seqs, u_col_block, width, consts, yb_all, h0=None):
    sequential = h0 is None
    bre = consts[0]
    n_blk, _, kst = bre.shape
    n_state = n_blk * kst
    n_tiles = n_state // LANES
    assert tm % SUBLANES == 0 and n_tiles % SUBLANES == 0
    pitch = tm
    per_seq = n_rows // seqs // tm
    u_map = lambda b, c: (row_block0 + b * per_seq + c, u_col_block)
    out_map = lambda b, c: (b * per_seq + c, 0)
    row_map = lambda b, c: (row_block0 + b * per_seq + c, 0)
    in_specs = [pl.BlockSpec((tm, width), u_map)]
    args = [proj]
    if not sequential:
        in_specs += [pl.BlockSpec((tm, n_state), out_map)] * 2
        args += list(h0)
    in_specs += [_const_spec(c.shape) for c in consts] + [pl.BlockSpec(memory_space=pl.ANY)]
    args += list(consts) + [yb_all]
    scratch = [pltpu.VMEM((n_tiles * pitch, LANES), F32), pltpu.VMEM((n_tiles * pitch, LANES), F32)]
    if sequential:
        scratch += [pltpu.VMEM((n_tiles, LANES), F32), pltpu.VMEM((n_tiles, LANES), F32)]
        st_spec = pl.BlockSpec((1, n_tiles, LANES), lambda b, c: (b, 0, 0))
        st_shape = jax.ShapeDtypeStruct((seqs, n_tiles, LANES), F32)
    else:
        st_spec = pl.BlockSpec((tm, n_state), out_map)
        st_shape = jax.ShapeDtypeStruct((n_rows, n_state), F32)
    return pl.pallas_call(
        functools.partial(_s5_kernel, sequential=sequential, n_blk=n_blk, pitch=pitch, unroll=4),
        grid=(seqs, per_seq),
        in_specs=in_specs,
        out_specs=[pl.BlockSpec((tm, width), row_map), st_spec, st_spec],
        out_shape=[jax.ShapeDtypeStruct(yb_all.shape, yb_all.dtype), st_shape, st_shape],
        input_output_aliases={len(args) - 1: 0},
        scratch_shapes=scratch,
        compiler_params=_cparams(("parallel", "arbitrary")),
        name="s5_seq" if sequential else "s5_step",
    )(*args)


def _route(logits, n_grp):
    lane = lax.broadcasted_iota(jnp.int32, logits.shape, 1).astype(F32)
    neg = jnp.float32(-1e30)
    big = jnp.float32(1e9)
    is_grp = lane < n_grp
    gl = jnp.where(is_grp, logits, neg)
    gmax = jnp.max(gl, axis=1, keepdims=True)
    gsum = jnp.sum(jnp.where(is_grp, jnp.exp(gl - gmax), 0.0), axis=1, keepdims=True)
    g_p = 1.0 / gsum
    g_idx = jnp.min(jnp.where(is_grp & (gl == gmax), lane, big), axis=1, keepdims=True)
    lo = n_grp + g_idx * EXPERTS_PER_GROUP
    in_grp = (lane >= lo) & (lane < lo + EXPERTS_PER_GROUP)
    el = jnp.where(in_grp, logits, neg)
    v1 = jnp.max(el, axis=1, keepdims=True)
    i1 = jnp.min(jnp.where(in_grp & (el == v1), lane, big), axis=1, keepdims=True)
    rest = in_grp & (lane != i1)
    el2 = jnp.where(rest, logits, neg)
    v2 = jnp.max(el2, axis=1, keepdims=True)
    i2 = jnp.min(jnp.where(rest & (el2 == v2), lane, big), axis=1, keepdims=True)
    e2 = jnp.exp(v2 - v1)
    w1 = g_p / (1.0 + e2)
    w2 = g_p * e2 / (1.0 + e2)
    return (jnp.where(lane == ROUTE_E1, i1 - n_grp, 0.0) + jnp.where(lane == ROUTE_E2, i2 - n_grp, 0.0)
            + jnp.where(lane == ROUTE_W1, w1, 0.0) + jnp.where(lane == ROUTE_W2, w2, 0.0))


def _merge_kernel(ya_ref, g_ref, yb_ref, ga_ref, gb_ref, x_ref, wa_ref, wb_ref, wo_ref,
                  gf_ref, wr_ref, br_ref, h_out, xn_out, comb_out, *, n_grp):
    ya = (ya_ref[...] * g_ref[...]).astype(BF16)
    ma = jnp.dot(ya, wa_ref[...], preferred_element_type=F32)
    mb = jnp.dot(yb_ref[...], wb_ref[...], preferred_element_type=F32)
    merged = jax.nn.sigmoid(ga_ref[...]) * ma + jax.nn.sigmoid(gb_ref[...]) * mb
    h = x_ref[...] + jnp.dot(merged.astype(BF16), wo_ref[...], preferred_element_type=F32)
    h_out[...] = h
    ms = jnp.mean(h * h, axis=-1, keepdims=True)
    xn = h * lax.rsqrt(ms + NORM_EPS) * gf_ref[...]
    xn_out[...] = xn
    x_hi = xn.astype(BF16)
    x_lo = (xn - x_hi.astype(F32)).astype(BF16)
    w_hi, w_lo = wr_ref[0], wr_ref[1]
    logits = (jnp.dot(x_hi, w_hi, preferred_element_type=F32)
              + (jnp.dot(x_hi, w_lo, preferred_element_type=F32)
                 + jnp.dot(x_lo, w_hi, preferred_element_type=F32))) + br_ref[...]
    comb_out[...] = _route(logits, n_grp)


def _merge(ya, g, yb, proj, x_all, w_br_a, w_br_b, w_out, g_ffn, w_router, b_router,
           tm, ga_col_block, n_grp):
    nt, d = x_all.shape
    wdt = ya.shape[1]
    row = lambda i: (i, 0)
    single = dict(pipeline_mode=pl.Buffered(1))
    in_specs = [pl.BlockSpec((tm, wdt), row), pl.BlockSpec((tm, wdt), row),
                pl.BlockSpec((tm, wdt), row),
                pl.BlockSpec((tm, d), lambda i: (i, ga_col_block)),
                pl.BlockSpec((tm, d), lambda i: (i, ga_col_block + 1)),
                pl.BlockSpec((tm, d), row),
                pl.BlockSpec(w_br_a.shape, lambda i: (0, 0), **single),
                pl.BlockSpec(w_br_b.shape, lambda i: (0, 0), **single),
                pl.BlockSpec(w_out.shape, lambda i: (0, 0), **single),
                _const_spec(g_ffn.shape), _const_spec(w_router.shape), _const_spec(b_router.shape)]
    return pl.pallas_call(
        functools.partial(_merge_kernel, n_grp=n_grp),
        grid=(nt // tm,),
        in_specs=in_specs,
        out_specs=[pl.BlockSpec((tm, d), row), pl.BlockSpec((tm, d), row),
                   pl.BlockSpec((tm, LANES), row)],
        out_shape=[jax.ShapeDtypeStruct((nt, d), F32), jax.ShapeDtypeStruct((nt, d), F32),
                   jax.ShapeDtypeStruct((nt, LANES), F32)],
        compiler_params=_cparams(("parallel",)),
        name="merge_route",
    )(ya, g, yb, proj, proj, x_all, w_br_a, w_br_b, w_out, g_ffn, w_router, b_router)


def _prefix_loops(n_rows, row):
    def block(b, c):
        for u in range(SUBLANES):
            row(b * SUBLANES + u)
        return c

    def single(r, c):
        row(r)
        return c

    n_blocks = lax.shift_right_logical(n_rows, LOG2_SUBLANES)
    lax.fori_loop(0, n_blocks, block, 0)
    lax.fori_loop(n_blocks * SUBLANES, n_rows, single, 0)


def _moe_kernel(te_ref, first_ref, nvalid_ref, wslot_ref, nexte_ref, nused_ref, src_cur, src_nxt,
                xn_hbm, wg_hbm, wu_hbm, wd_hbm, o_ref,
                xbuf, wg_f, wu_f, wd_f, wg_bf, wu_bf, wd_bf, gsem, wsem, *, tm):
    i = pl.program_id(0)
    n_used = nused_ref[0]
    slot = lax.rem(i, 2)

    def prefix_wait(n_rows, copy_of):
        n_full = pl.multiple_of(
            lax.shift_left(lax.shift_right_logical(n_rows, LOG2_SUBLANES), LOG2_SUBLANES), SUBLANES)

        @pl.when(n_full > 0)
        def _():
            copy_of(0, n_full).wait()

        def single(r, c):
            copy_of(r, 1).wait()
            return c
        lax.fori_loop(n_full, n_rows, single, 0)

    def gather_start(src, s, n_rows):
        _prefix_loops(n_rows, lambda r: pltpu.make_async_copy(
            xn_hbm.at[pl.ds(src[0, 0, r], 1)], xbuf.at[s, pl.ds(r, 1)], gsem.at[s]).start())

    def gather_wait(s, n_rows):
        prefix_wait(n_rows, lambda r0, n: pltpu.make_async_copy(
            xn_hbm.at[pl.ds(0, n)], xbuf.at[s, pl.ds(r0, n)], gsem.at[s]))

    def weight_copies(e, s):
        return [pltpu.make_async_copy(hbm.at[e], buf.at[s], wsem.at[s])
                for hbm, buf in ((wg_hbm, wg_f), (wu_hbm, wu_f), (wd_hbm, wd_f))]

    @pl.when(i == 0)
    def _():
        for c in weight_copies(te_ref[0], 0):
            c.start()
        xbuf[...] = jnp.zeros_like(xbuf)
        gather_start(src_cur, 0, nvalid_ref[0])

    @pl.when(i < n_used)
    def _():
        gather_wait(slot, nvalid_ref[i])

    @pl.when(i + 1 < n_used)
    def _():
        gather_start(src_nxt, 1 - slot, nvalid_ref[i + 1])

    @pl.when(i < n_used)
    def _():
        @pl.when(first_ref[i] == 1)
        def _():
            ws = wslot_ref[i]
            for c in weight_copies(te_ref[i], ws):
                c.wait()

            @pl.when(nexte_ref[i] >= 0)
            def _():
                for c in weight_copies(nexte_ref[i], 1 - ws):
                    c.start()

            wg_bf[...] = wg_f[ws].astype(BF16)
            wu_bf[...] = wu_f[ws].astype(BF16)
            wd_bf[...] = wd_f[ws].astype(BF16)

        x = xbuf[slot].astype(BF16)
        xg = jnp.dot(x, wg_bf[...], preferred_element_type=F32)
        xu = jnp.dot(x, wu_bf[...], preferred_element_type=F32)
        hid = (jax.nn.silu(xg) * xu).astype(BF16)
        o_ref[...] = jnp.dot(hid, wd_bf[...], preferred_element_type=F32)

    @pl.when(i >= n_used)
    def _():
        o_ref[...] = jnp.zeros_like(o_ref)


def _moe(xn, tile_expert, tile_first, tile_valid, tile_wslot, tile_nexte, n_used, src_rows,
         wg, wu, wd, tm):
    nt, d = xn.shape
    n_exp, _, de = wg.shape
    n_tiles = src_rows.shape[0]
    smem_cur = pl.BlockSpec((1, 1, tm), lambda i, *_: (i, 0, 0), memory_space=pltpu.SMEM)
    smem_nxt = pl.BlockSpec((1, 1, tm), lambda i, *_: (jnp.minimum(i + 1, n_tiles - 1), 0, 0),
                            memory_space=pltpu.SMEM)
    grid_spec = pltpu.PrefetchScalarGridSpec(
        num_scalar_prefetch=6,
        grid=(n_tiles,),
        in_specs=[smem_cur, smem_nxt] + [pl.BlockSpec(memory_space=pl.ANY)] * 4,
        out_specs=pl.BlockSpec((tm, d), lambda i, *_: (i, 0)),
        scratch_shapes=[pltpu.VMEM((2, tm, d), F32),
                        pltpu.VMEM((2, d, de), F32), pltpu.VMEM((2, d, de), F32),
                        pltpu.VMEM((2, de, d), F32),
                        pltpu.VMEM((d, de), BF16), pltpu.VMEM((d, de), BF16),
                        pltpu.VMEM((de, d), BF16),
                        pltpu.SemaphoreType.DMA((2,)), pltpu.SemaphoreType.DMA((2,))])
    return pl.pallas_call(
        functools.partial(_moe_kernel, tm=tm),
        grid_spec=grid_spec,
        out_shape=jax.ShapeDtypeStruct((n_tiles * tm, d), F32),
        compiler_params=_cparams(("arbitrary",)),
        name="moe_grouped",
    )(tile_expert, tile_first, tile_valid, tile_wslot, tile_nexte, n_used, src_rows, src_rows,
      xn, wg, wu, wd)


def _moe_plan(route, n_exp, tm):
    nt = route.shape[0]
    n_pairs = 2 * nt
    n_tiles = n_pairs // tm + n_exp
    eid = jnp.concatenate([route[:, ROUTE_E1], route[:, ROUTE_E2]]).astype(jnp.int32)
    onehot = (eid[:, None] == jnp.arange(n_exp, dtype=jnp.int32)[None, :]).astype(jnp.int32)
    csum = jnp.cumsum(onehot, axis=0)
    rank = jnp.take_along_axis(csum, eid[:, None], axis=1)[:, 0] - 1
    cnt = csum[-1]
    tiles = (cnt + tm - 1) // tm
    tile_end = jnp.cumsum(tiles)
    n_used = tile_end[-1]
    pos = (tile_end - tiles)[eid] * tm + rank
    pair = jnp.arange(n_pairs, dtype=jnp.int32)
    src_rows = jnp.zeros((n_tiles * tm,), jnp.int32).at[pos].set(
        pair, unique_indices=True, mode="promise_in_bounds") % nt
    tile_id = jnp.minimum(jnp.arange(n_tiles, dtype=jnp.int32), n_used - 1)
    tile_expert = jnp.sum((tile_end[None, :] <= tile_id[:, None]).astype(jnp.int32), axis=1)
    tile_first = jnp.concatenate(
        [jnp.ones((1,), jnp.int32), (tile_expert[1:] != tile_expert[:-1]).astype(jnp.int32)])
    tile_valid = jnp.clip(cnt[tile_expert] - (tile_id - (tile_end - tiles)[tile_expert]) * tm, 0, tm)
    tile_wslot = (jnp.cumsum(tile_first) - 1) % 2
    nxt_tile = tile_end[tile_expert]
    nxt_expert = jnp.sum((tile_end[None, :] <= nxt_tile[:, None]).astype(jnp.int32), axis=1)
    tile_nexte = jnp.where(nxt_tile < n_used, nxt_expert, -1)
    return (tile_expert, tile_first, tile_valid.astype(jnp.int32), tile_wslot.astype(jnp.int32),
            tile_nexte.astype(jnp.int32), n_used.reshape(1).astype(jnp.int32),
            src_rows.reshape(n_tiles, 1, tm), pos[:nt], pos[nt:])


def _final_kernel(p1_cur, p1_nxt, p2_cur, p2_nxt, h_ref, route_ref, g_ref, y_hbm, o_ref,
                  ybuf, sem, *, tm):
    i = pl.program_id(0)
    slot = lax.rem(i, 2)

    def gather_start(p1, p2, s):
        def row(r):
            pltpu.make_async_copy(y_hbm.at[pl.ds(p1[0, 0, r], 1)], ybuf.at[s, 0, pl.ds(r, 1)],
                                  sem.at[s]).start()
            pltpu.make_async_copy(y_hbm.at[pl.ds(p2[0, 0, r], 1)], ybuf.at[s, 1, pl.ds(r, 1)],
                                  sem.at[s]).start()
        _prefix_loops(tm, row)

    def gather_wait(s):
        for e in range(2):
            pltpu.make_async_copy(y_hbm.at[pl.ds(0, tm)], ybuf.at[s, e], sem.at[s]).wait()

    @pl.when(i == 0)
    def _():
        gather_start(p1_cur, p2_cur, 0)

    gather_wait(slot)

    @pl.when(i + 1 < pl.num_programs(0))
    def _():
        gather_start(p1_nxt, p2_nxt, 1 - slot)

    lane = lax.broadcasted_iota(jnp.int32, route_ref.shape, 1)
    route = route_ref[...]
    w1 = jnp.sum(jnp.where(lane == ROUTE_W1, route, 0.0), axis=1, keepdims=True)
    w2 = jnp.sum(jnp.where(lane == ROUTE_W2, route, 0.0), axis=1, keepdims=True)
    h = h_ref[...] + (w1 * ybuf[slot, 0] + w2 * ybuf[slot, 1])
    ms = jnp.mean(h * h, axis=-1, keepdims=True)
    o_ref[...] = h * lax.rsqrt(ms + NORM_EPS) * g_ref[...]


def _final(h, y_sorted, pos1, pos2, route, g_final, row0, seg_stride, seg_rows, n_seg, tm):
    nt, d = h.shape
    per_seg = seg_rows // tm
    n_steps = n_seg * per_seg
    assert all(x % SUBLANES == 0 for x in (row0, seg_stride, tm, nt))

    def off(i):
        return pl.multiple_of(row0 + (i // per_seg) * seg_stride + (i % per_seg) * tm, SUBLANES)

    rows = lambda width: pl.BlockSpec((pl.Element(tm), pl.Element(width)), lambda i: (off(i), 0))
    smem = lambda nxt: pl.BlockSpec(
        (1, 1, tm), lambda i: (jnp.minimum(i + nxt, n_steps - 1), 0, 0), memory_space=pltpu.SMEM)
    sel = lambda p: jnp.concatenate(
        [p[row0 + s * seg_stride:row0 + s * seg_stride + seg_rows] for s in range(n_seg)]
    ).reshape(n_steps, 1, tm)
    p1, p2 = sel(pos1), sel(pos2)
    return pl.pallas_call(
        functools.partial(_final_kernel, tm=tm),
        grid=(n_steps,),
        in_specs=[smem(0), smem(1), smem(0), smem(1), rows(d), rows(LANES),
                  pl.BlockSpec((1, d), lambda i: (0, 0)), pl.BlockSpec(memory_space=pl.ANY)],
        out_specs=pl.BlockSpec((tm, d), lambda i: (i, 0)),
        out_shape=jax.ShapeDtypeStruct((n_seg * seg_rows, d), F32),
        scratch_shapes=[pltpu.VMEM((2, 2, tm, d), F32), pltpu.SemaphoreType.DMA((2,))],
        compiler_params=_cparams(("arbitrary",)),
        name="final_norm",
    )(p1, p1, p2, p2, h, route, g_final, y_sorted)


def _pad_cols(w, to):
    return jnp.pad(w, ((0, 0), (0, to - w.shape[1])))


def _pad_rows(w, to):
    return jnp.pad(w, ((0, to - w.shape[0]), (0, 0)))


def _tile_plan(n_t, t_p, seq):
    plan = dict(
        inproj_rows=n_t // 6,
        inproj_cols=4 * LANES,
        seq_rows=t_p // 6,
        wkv_chunk=48, wkv_unroll=8,
        merge_rows=2 * LANES, moe_rows=2 * LANES, final_rows=4 * LANES)
    assert n_t % plan["inproj_rows"] == 0 and plan["inproj_rows"] % SUBLANES == 0
    assert t_p % plan["seq_rows"] == 0 and plan["seq_rows"] % SUBLANES == 0
    assert t_p % plan["wkv_chunk"] == 0 and plan["wkv_chunk"] % plan["wkv_unroll"] == 0
    assert n_t % plan["merge_rows"] == 0 and seq % plan["final_rows"] == 0
    return plan


def _s5_consts(lam_re, lam_im, log_dt, b_re, b_im, c_re, c_im, d, w_glu, b_glu):
    dt = jnp.exp(log_dt)[:, None]
    mag = jnp.exp(lam_re * dt)
    abar_re = mag * jnp.cos(lam_im * dt)
    abar_im = mag * jnp.sin(lam_im * dt)
    den = lam_re * lam_re + lam_im * lam_im
    nr = abar_re - 1.0
    coef_re = (nr * lam_re + abar_im * lam_im) / den
    coef_im = (abar_im * lam_re - nr * lam_im) / den
    bbar_re = coef_re[..., None] * b_re - coef_im[..., None] * b_im
    bbar_im = coef_re[..., None] * b_im + coef_im[..., None] * b_re
    n_g, n_p, n_c = b_re.shape
    gpb = LANES // n_c
    eye = jnp.eye(gpb, dtype=F32)

    def in_blk(bb):
        bb = bb.reshape(n_g // gpb, gpb, n_p, n_c)
        return jnp.einsum('kgpc,gh->kgchp', bb, eye).reshape(
            n_g // gpb, gpb * n_c, gpb * n_p).astype(BF16)

    def out_blk(cc):
        cc = cc.reshape(n_g // gpb, gpb, n_c, n_p)
        return jnp.einsum('kgcp,gh->khpgc', cc, eye).reshape(
            n_g // gpb, gpb * n_p, gpb * n_c).astype(BF16)

    return (in_blk(bbar_re), in_blk(bbar_im), out_blk(c_re), out_blk(c_im),
            d.reshape(1, -1), abar_re.reshape(-1, LANES), abar_im.reshape(-1, LANES),
            w_glu.astype(BF16), b_glu.reshape(1, -1))


def kernel(x_prompt, x_sample, state_shift, state_wkv, state_ssm_re, state_ssm_im, meta_tokens, g_mix, w_in, shift_mu, w0, w_decay_up, a0, w_aaa_up, w_gate_up, k_k, k_a, r_k, lnx_w, lnx_b, ssm_lam_re, ssm_lam_im, ssm_log_dt, ssm_b_re, ssm_b_im, ssm_c_re, ssm_c_im, ssm_d, w_glu, b_glu, w_br_a, w_br_b, w_out, g_ffn, w_router_grp, b_router_grp, w_router_exp, b_router_exp, w_exp_gate, w_exp_up, w_exp_down, g_final):
    depth = g_mix.shape[0]
    assert depth == 1, "single-layer trunk"
    bsz, seq, d = x_prompt.shape
    nb = x_sample.shape[0]
    assert x_sample.shape[1] == 1
    t_p = seq + N_META
    n_p = bsz * t_p
    width = k_k.shape[1]
    heads = width // HEAD
    n_dl, n_al, n_gl = w_decay_up.shape[1], w_aaa_up.shape[1], w_gate_up.shape[1]
    n_grp = w_router_grp.shape[2]
    n_exp = w_router_exp.shape[2]
    assert bsz * heads * 2 == LANES and (nb * heads) % LANES == 0

    s_row0 = -(-n_p // nb) * nb
    n_t = s_row0 + nb
    pieces = []
    for b in range(bsz):
        pieces += [meta_tokens, x_prompt[b]]
    x_all = jnp.concatenate(
        pieces + [jnp.zeros((s_row0 - n_p, d), F32), x_sample.reshape(nb, d)], axis=0)

    c_rkv = 3 * width
    c_xw, c_xa, c_xg = c_rkv, c_rkv + n_dl, c_rkv + n_dl + n_al
    c_u = c_xg + n_gl
    lo_width = 2 * LORA_PAD + n_gl

    def regroup(m):
        return jnp.concatenate(
            [m[:, :c_rkv], m[:, c_u:], _pad_cols(m[:, c_xw:c_xa], LORA_PAD),
             _pad_cols(m[:, c_xa:c_xg], LORA_PAD), m[:, c_xg:c_u]], axis=1)

    def ungroup(m):
        lo = c_rkv + width + 2 * d
        return jnp.concatenate(
            [m[:, :c_rkv], m[:, lo:lo + n_dl], m[:, lo + LORA_PAD:lo + LORA_PAD + n_al],
             m[:, lo + 2 * LORA_PAD:]], axis=1)

    w_in_p = regroup(w_in[0].astype(BF16))
    n_proj = w_in_p.shape[1]
    tiles = _tile_plan(n_t, t_p, seq)
    proj = _inproj(x_all, g_mix, w_in_p, tm=tiles["inproj_rows"], tn=tiles["inproj_cols"])

    mu_all = regroup(_pad_cols(shift_mu, c_u + width + 2 * d))
    mu_rkv, mu_lo = mu_all[:, :c_rkv], mu_all[:, n_proj - lo_width:]
    st_all = regroup(_pad_cols(state_shift[0], c_u + width + 2 * d))
    prev_s = (st_all[:, :c_rkv], st_all[:, n_proj - lo_width:])
    lo_col_block = (n_proj - lo_width) // lo_width
    prep_w = (mu_rkv, mu_lo, w0, a0,
              _pad_rows(w_decay_up[0], LORA_PAD).astype(BF16),
              _pad_rows(w_aaa_up[0], LORA_PAD).astype(BF16), w_gate_up[0].astype(BF16))

    tm_seq = tiles["seq_rows"]
    g_all = jnp.zeros((n_t, width), F32)
    rp, wp, kp, vp, ap, g_all = _rwkv_prep(proj, 0, n_p, tm_seq, bsz, width, lo_col_block,
                                                lo_width, *prep_w, g_all)
    rs, ws, ks, vs, as_, g_all = _rwkv_prep(proj, s_row0 // nb, nb, nb, 1, width,
                                                 lo_col_block, lo_width, *prep_w, g_all,
                                                 prev=prev_s)

    s5c = _s5_consts(ssm_lam_re[0], ssm_lam_im[0], ssm_log_dt[0], ssm_b_re[0], ssm_b_im[0],
                     ssm_c_re[0], ssm_c_im[0], ssm_d[0], w_glu[0], b_glu)
    n_g, n_st = ssm_lam_re.shape[1], ssm_lam_re.shape[2]
    u_col_block = c_rkv // width
    yb_all = jnp.zeros((n_t, width), BF16)
    yb_all, re_p, im_p = _s5(proj, 0, n_p, tm_seq, bsz, u_col_block, width, s5c, yb_all)
    h0 = (state_ssm_re[0].reshape(nb, n_g * n_st), state_ssm_im[0].reshape(nb, n_g * n_st))
    yb_all, re_s, im_s = _s5(proj, s_row0 // nb, nb, nb, 1, u_col_block, width, s5c, yb_all,
                             h0=h0)

    half = HEAD // 2

    def to_lanes_p(z):
        return z.reshape(t_p, bsz * heads, HEAD).transpose(0, 2, 1)

    def to_lanes_s(z):
        return z.reshape(nb, heads, HEAD).transpose(2, 0, 1).reshape(1, HEAD, nb * heads)

    def head_const(c):
        return jnp.tile(c.reshape(heads, HEAD).T, (1, LANES // heads))

    rk_l = head_const(r_k[0])
    kk_l, ka_l = head_const(k_k[0]), head_const(k_a[0])
    lw_p = jnp.broadcast_to(lnx_w[0].reshape(heads, 2, half).transpose(2, 1, 0)[:, :, None, :],
                            (half, 2, bsz, heads)).reshape(half, LANES)
    lb_p = jnp.broadcast_to(lnx_b[0].reshape(heads, 2, half).transpose(2, 1, 0)[:, :, None, :],
                            (half, 2, bsz, heads)).reshape(half, LANES)
    lw_s = jnp.tile(lnx_w[0].reshape(heads, HEAD).T, (1, LANES // heads))
    lb_s = jnp.tile(lnx_b[0].reshape(heads, HEAD).T, (1, LANES // heads))

    s0_p = jnp.zeros((HEAD, half, LANES), F32)
    y_p, sf_p = _wkv(to_lanes_p(rp), to_lanes_p(wp), to_lanes_p(kp), to_lanes_p(ap),
                     to_lanes_p(vp), s0_p, kk_l, ka_l, rk_l, lw_p, lb_p,
                     tc=tiles["wkv_chunk"], isplit=True, unroll=tiles["wkv_unroll"])
    s0_s = state_wkv[0].transpose(3, 2, 0, 1).reshape(HEAD, HEAD, nb * heads)
    y_s, sf_s = _wkv(to_lanes_s(rs), to_lanes_s(ws), to_lanes_s(ks), to_lanes_s(as_),
                     to_lanes_s(vs), s0_s, kk_l, ka_l, rk_l, lw_s, lb_s,
                     tc=1, isplit=False, unroll=1)

    ya_p = y_p.reshape(t_p, half, 2, bsz, heads).transpose(3, 0, 4, 2, 1).reshape(n_p, width)
    ya_s = y_s.reshape(HEAD, nb, heads).transpose(1, 2, 0).reshape(nb, width)
    wkv_p = sf_p.reshape(HEAD, half, 2, bsz, heads).transpose(3, 4, 2, 1, 0).reshape(
        1, bsz, heads, HEAD, HEAD)
    wkv_s = sf_s.reshape(HEAD, HEAD, nb, heads).transpose(2, 3, 1, 0)[None]

    def unify(a, b):
        return jnp.concatenate([a, jnp.zeros((s_row0 - n_p, a.shape[1]), a.dtype), b], axis=0)

    w_router = _pad_cols(jnp.concatenate([w_router_grp[0], w_router_exp[0]], axis=1), LANES)
    w_router_hi = w_router.astype(BF16)
    w_router = jnp.stack([w_router_hi, (w_router - w_router_hi.astype(F32)).astype(BF16)])
    b_router = _pad_cols(jnp.concatenate([b_router_grp, b_router_exp], axis=1), LANES)
    assert n_grp + n_exp <= LANES and n_exp == n_grp * EXPERTS_PER_GROUP
    h1, xn2, route = _merge(unify(ya_p, ya_s), g_all, yb_all, proj, x_all,
                           w_br_a[0].astype(BF16), w_br_b[0].astype(BF16), w_out[0].astype(BF16),
                           g_ffn, w_router, b_router, tm=tiles["merge_rows"],
                           ga_col_block=(c_rkv + width) // d,
                           n_grp=n_grp)
    tm_moe = tiles["moe_rows"]
    te, tf, tv, tws, tne, nu, src_rows, pos1, pos2 = _moe_plan(route, n_exp, tm_moe)
    y_moe = _moe(xn2, te, tf, tv, tws, tne, nu, src_rows, w_exp_gate[0], w_exp_up[0],
                 w_exp_down[0], tm_moe)
    gfin = g_final.reshape(1, d)
    y_prompt = _final(h1, y_moe, pos1, pos2, route, gfin, N_META, t_p, seq, bsz,
                      tm=tiles["final_rows"]).reshape(bsz, seq, d)
    y_sample = _final(h1, y_moe, pos1, pos2, route, gfin, s_row0, 0, nb, 1,
                      tm=nb).reshape(nb, 1, d)
    last_p = jnp.concatenate([proj[(b + 1) * t_p - 1:(b + 1) * t_p] for b in range(bsz)], axis=0)
    shift_p = ungroup(last_p)[None]
    shift_s = ungroup(proj[s_row0:])[None]
    return (y_prompt, y_sample, shift_p, wkv_p,
            re_p.reshape(1, bsz, n_g, n_st), im_p.reshape(1, bsz, n_g, n_st),
            shift_s, wkv_s,
            re_s.reshape(1, nb, n_g, n_st), im_s.reshape(1, nb, n_g, n_st))
```

```python
import functools

import jax
import jax.numpy as jnp
from jax import lax
from jax.experimental import pallas as pl
from jax.experimental.pallas import tpu as pltpu

F32 = jnp.float32
BF16 = jnp.bfloat16

NORM_EPS = 1e-6
LNX_EPS = 64e-5
N_META = 16
HEAD = 64
EXPERTS_PER_GROUP = 8
LANES = 128
SUBLANES = 8
LOG2_SUBLANES = SUBLANES.bit_length() - 1
LORA_PAD = 128
ROUTE_E1, ROUTE_E2, ROUTE_W1, ROUTE_W2 = 0, 1, 2, 3
VMEM_LIMIT = 56 * 1024 * 1024


def _cparams(sem):
    return pltpu.CompilerParams(dimension_semantics=sem, vmem_limit_bytes=VMEM_LIMIT)


def _const_spec(shape):
    nd = len(shape)
    return pl.BlockSpec(shape, lambda *_: (0,) * nd)


def _inproj_kernel(x_ref, g_ref, w_ref, o_ref, xn_ref):
    @pl.when(pl.program_id(1) == 0)
    def _():
        x = x_ref[...]
        ms = jnp.mean(x * x, axis=-1, keepdims=True)
        xn_ref[...] = (x * lax.rsqrt(ms + NORM_EPS) * g_ref[...]).astype(BF16)

    o_ref[...] = jnp.dot(xn_ref[...], w_ref[...], preferred_element_type=F32)


def _inproj(x_all, g_mix, w_in_p, tm, tn):
    nt, d = x_all.shape
    n_out = w_in_p.shape[1]
    return pl.pallas_call(
        _inproj_kernel,
        grid=(nt // tm, n_out // tn),
        in_specs=[pl.BlockSpec((tm, d), lambda i, j: (i, 0)),
                  pl.BlockSpec((1, d), lambda i, j: (0, 0)),
                  pl.BlockSpec((d, tn), lambda i, j: (0, j))],
        out_specs=pl.BlockSpec((tm, tn), lambda i, j: (i, j)),
        out_shape=jax.ShapeDtypeStruct((nt, n_out), F32),
        scratch_shapes=[pltpu.VMEM((tm, d), BF16)],
        compiler_params=_cparams(("parallel", "arbitrary")),
        name="inproj",
    )(x_all, g_mix, w_in_p)


def _softplus(z):
    return jnp.maximum(z, 0.0) + jnp.log1p(jnp.exp(-jnp.abs(z)))


def _rwkv_prep_kernel(*refs, width, carry_prev):
    if carry_prev:
        (rkv_ref, lo_ref, mu_rkv_ref, mu_lo_ref, w0_ref, a0_ref,
         wd_ref, wa_ref, wg_ref, _g_all,
         r_out, w_out, k_out, v_out, a_out, g_out, c_rkv, c_lo) = refs

        @pl.when(pl.program_id(1) == 0)
        def _():
            c_rkv[...] = jnp.zeros_like(c_rkv)
            c_lo[...] = jnp.zeros_like(c_lo)
    else:
        (rkv_ref, lo_ref, prev_rkv_ref, prev_lo_ref, mu_rkv_ref, mu_lo_ref, w0_ref, a0_ref,
         wd_ref, wa_ref, wg_ref, _g_all,
         r_out, w_out, k_out, v_out, a_out, g_out) = refs

    tm = rkv_ref.shape[0]
    first_row = lax.broadcasted_iota(jnp.int32, (tm, 1), 0) == 0

    def shifted(p, prev_ref, carry_ref, cols):
        if carry_prev:
            prev = jnp.where(first_row, carry_ref[:, cols], pltpu.roll(p, 1, 0))
        else:
            prev = prev_ref[:, cols]
        return prev

    def lerp(p, prev, mu):
        return p + (prev - p) * mu

    lo_cols = slice(0, lo_ref.shape[1])
    p_lo = lo_ref[...]
    q_lo = lerp(p_lo, shifted(p_lo, None if carry_prev else prev_lo_ref,
                              c_lo if carry_prev else None, lo_cols), mu_lo_ref[...])
    xw = q_lo[:, 0:LORA_PAD]
    xa = q_lo[:, LORA_PAD:2 * LORA_PAD]
    xg = q_lo[:, 2 * LORA_PAD:]
    dw = jnp.dot(jnp.tanh(xw).astype(BF16), wd_ref[...], preferred_element_type=F32)
    wlog = -_softplus(-(w0_ref[...] + dw)) - 0.5
    w_out[...] = jnp.exp(-jnp.exp(wlog))
    a = jax.nn.sigmoid(a0_ref[...] + jnp.dot(xa.astype(BF16), wa_ref[...],
                                             preferred_element_type=F32))
    a_out[...] = a
    g_out[...] = jnp.dot(jax.nn.sigmoid(xg).astype(BF16), wg_ref[...],
                         preferred_element_type=F32)

    def q_of(idx):
        cols = slice(idx * width, (idx + 1) * width)
        p = rkv_ref[:, cols]
        prev = shifted(p, None if carry_prev else prev_rkv_ref,
                       c_rkv if carry_prev else None, cols)
        return lerp(p, prev, mu_rkv_ref[:, cols])

    r_out[...] = q_of(0)
    k_out[...] = q_of(1)
    v_out[...] = q_of(2)

    if carry_prev:
        c_rkv[...] = rkv_ref[tm - 1:tm, :]
        c_lo[...] = lo_ref[tm - 1:tm, :]


def _rwkv_prep(proj, row_block0, n_rows, tm, seqs, width, lo_col_block, lo_width,
               mu_rkv, mu_lo, w0, a0, wd, wa, wg, g_all, prev=None):
    carry_prev = prev is None
    per_seq = n_rows // seqs // tm
    row_map = lambda b, c: (row_block0 + b * per_seq + c, 0)
    lo_map = lambda b, c: (row_block0 + b * per_seq + c, lo_col_block)
    out_map = lambda b, c: (b * per_seq + c, 0)
    in_specs = [pl.BlockSpec((tm, 3 * width), row_map), pl.BlockSpec((tm, lo_width), lo_map)]
    args = [proj, proj]
    if not carry_prev:
        in_specs += [pl.BlockSpec((tm, 3 * width), out_map), pl.BlockSpec((tm, lo_width), out_map)]
        args += list(prev)
    consts = [mu_rkv, mu_lo, w0, a0, wd, wa, wg]
    in_specs += [_const_spec(c.shape) for c in consts] + [pl.BlockSpec(memory_space=pl.ANY)]
    args += consts + [g_all]
    scratch = []
    if carry_prev:
        scratch = [pltpu.VMEM((1, 3 * width), F32), pltpu.VMEM((1, lo_width), F32)]
    out_sd = jax.ShapeDtypeStruct((n_rows // seqs, seqs * width), F32)
    tmaj_map = lambda b, c: (c, b)
    return pl.pallas_call(
        functools.partial(_rwkv_prep_kernel, width=width, carry_prev=carry_prev),
        grid=(seqs, per_seq),
        in_specs=in_specs,
        out_specs=[pl.BlockSpec((tm, width), tmaj_map)] * 5 + [pl.BlockSpec((tm, width), row_map)],
        out_shape=[out_sd] * 5 + [jax.ShapeDtypeStruct(g_all.shape, g_all.dtype)],
        input_output_aliases={len(args) - 1: 5},
        scratch_shapes=scratch,
        compiler_params=_cparams(("parallel", "arbitrary")),
        name="rwkv_prep_seq" if carry_prev else "rwkv_prep_step",
    )(*args)


def _wkv_kernel(r_ref, w_ref, kraw_ref, al_ref, v_ref, s0_ref, kkc_ref, kac_ref, rk_ref, lw_ref,
                lb_ref, y_ref, sf_ref, s_scr, a_scr, b_scr, k_src, *dup_scr,
                ni, nj, tc, isplit, unroll):
    @pl.when(pl.program_id(1) == 0)
    def _():
        s_scr[...] = s0_ref[...]

    def dup(x):
        if not isplit:
            return x
        swapped = pltpu.roll(x, LANES // 2, 1)
        low = lax.broadcasted_iota(jnp.int32, x.shape, 1) < LANES // 2
        return jnp.concatenate([jnp.where(low, x, swapped), jnp.where(low, swapped, x)], axis=0)

    if isplit:
        w_src, r_src = dup_scr
    else:
        w_src, r_src = w_ref, r_ref
    v_src = v_ref

    def prep(t, carry):
        kraw = dup(kraw_ref[t])
        al = dup(al_ref[t])
        kk = kraw * kkc_ref[...]
        k_src[t] = kraw * (1.0 + (al - 1.0) * kac_ref[...])
        ss = jnp.sum(kk * kk, axis=0, keepdims=True)
        kkn = kk / jnp.maximum(jnp.sqrt(ss), 1e-12)
        a_scr[t] = -kkn
        b_scr[t] = kkn * al
        if isplit:
            w_src[t] = dup(w_ref[t])
            r_src[t] = dup(r_ref[t])
        return carry

    lax.fori_loop(0, tc, prep, 0, unroll=unroll)

    def row(ref, t, j):
        return ref[t, j:j + 1, :]

    def tree(parts):
        while len(parts) > 1:
            parts = [parts[i] + parts[i + 1] for i in range(0, len(parts), 2)]
        return parts[0]

    n_acc = 4
    sa0 = tree([sum(s_scr[j] * row(a_scr, 0, j) for j in range(q, nj, n_acc))
                for q in range(n_acc)])

    def step(t, sa):
        v = v_src[t]
        tn = jnp.minimum(t + 1, tc - 1)
        y = [None] * n_acc
        san = [None] * n_acc
        for j in range(nj):
            s = s_scr[j] * row(w_src, t, j) + sa * row(b_scr, t, j) + v * row(k_src, t, j)
            s_scr[j] = s
            yj = s * row(r_src, t, j)
            sj = s * row(a_scr, tn, j)
            q = j % n_acc
            y[q] = yj if y[q] is None else y[q] + yj
            san[q] = sj if san[q] is None else san[q] + sj
        y_ref[t] = tree(y)
        return tree(san)

    lax.fori_loop(0, tc, step, sa0)

    def isum(x):
        s = jnp.broadcast_to(jnp.sum(x, axis=0, keepdims=True), (8, LANES))
        if isplit:
            s = s + pltpu.roll(s, LANES // 2, 1)
        return s[0:1]

    def post(t, carry):
        y = y_ref[t]
        v = v_src[t]
        mu = isum(y) * (1.0 / HEAD)
        d = y - mu
        var = isum(d * d) * (1.0 / HEAD)
        yn = d * lax.rsqrt(var + LNX_EPS) * lw_ref[...] + lb_ref[...]
        bonus = jnp.sum(r_src[t] * k_src[t] * rk_ref[...], axis=0, keepdims=True)
        y_ref[t] = yn + bonus * v
        return carry

    lax.fori_loop(0, tc, post, 0, unroll=unroll)

    @pl.when(pl.program_id(1) == pl.num_programs(1) - 1)
    def _():
        sf_ref[...] = s_scr[...]


def _wkv(r, w, kraw, al, v, s0, kkc, kac, rk, lw, lb, tc, isplit, unroll):
    t, rows_in, _ = r.shape
    nj, ni, lanes = s0.shape
    assert rows_in == (nj // 2 if isplit else nj) and v.shape == r.shape
    jspec = pl.BlockSpec((tc, rows_in, LANES), lambda l, c: (c, 0, l))
    dup_scr = [pltpu.VMEM((tc, nj, LANES), F32)] * 2 if isplit else []
    ispec = pl.BlockSpec((tc, ni, LANES), lambda l, c: (c, 0, l))
    sspec = pl.BlockSpec((nj, ni, LANES), lambda l, c: (0, 0, l))
    consts = [kkc, kac, rk, lw, lb]
    return pl.pallas_call(
        functools.partial(_wkv_kernel, ni=ni, nj=nj, tc=tc, isplit=isplit, unroll=unroll),
        grid=(lanes // LANES, t // tc),
        in_specs=[jspec, jspec, jspec, jspec, jspec, sspec] + [_const_spec(c.shape) for c in consts],
        out_specs=[ispec, sspec],
        out_shape=[jax.ShapeDtypeStruct((t, ni, lanes), F32),
                   jax.ShapeDtypeStruct((nj, ni, lanes), F32)],
        scratch_shapes=[pltpu.VMEM((nj, ni, LANES), F32),
                        pltpu.VMEM((tc, nj, LANES), F32),
                        pltpu.VMEM((tc, nj, LANES), F32),
                        pltpu.VMEM((tc, nj, LANES), F32)] + dup_scr,
        compiler_params=_cparams(("parallel", "arbitrary")),
        name="wkv_seq" if isplit else "wkv_step",
    )(r, w, kraw, al, v, s0, *consts)


def _s5_kernel(*refs, sequential, n_blk, pitch, unroll):
    if sequential:
        (u_ref, bre_ref, bim_ref, cre_ref, cim_ref, d_ref, are_ref, aim_ref, wglu_ref, bglu_ref,
         _yb_all, yb_ref, hre_out, him_out, st_re, st_im, c_re, c_im) = refs
    else:
        (u_ref, h0re_ref, h0im_ref, bre_ref, bim_ref, cre_ref, cim_ref, d_ref, are_ref, aim_ref,
         wglu_ref, bglu_ref, _yb_all, yb_ref, hre_out, him_out, st_re, st_im) = refs

    tm = u_ref.shape[0]
    kin = bre_ref.shape[1]
    kst = bre_ref.shape[2]
    tiles_per_blk = kst // LANES
    n_tiles = n_blk * tiles_per_blk
    u = u_ref[...]
    ub = u.astype(BF16)

    def tile_rows(k):
        return slice(k * pitch, k * pitch + tm)

    for kb in range(n_blk):
        ukb = ub[:, kb * kin:(kb + 1) * kin]
        bu_re = jnp.dot(ukb, bre_ref[kb], preferred_element_type=F32)
        bu_im = jnp.dot(ukb, bim_ref[kb], preferred_element_type=F32)
        for n in range(tiles_per_blk):
            k = kb * tiles_per_blk + n
            cols = slice(n * LANES, (n + 1) * LANES)
            if sequential:
                st_re[tile_rows(k), :] = bu_re[:, cols]
                st_im[tile_rows(k), :] = bu_im[:, cols]
            else:
                kc = slice(k * LANES, (k + 1) * LANES)
                ar, ai = are_ref[k:k + 1, :], aim_ref[k:k + 1, :]
                h0r, h0i = h0re_ref[:, kc], h0im_ref[:, kc]
                nr = bu_re[:, cols] + (ar * h0r - ai * h0i)
                ni = bu_im[:, cols] + (ar * h0i + ai * h0r)
                st_re[tile_rows(k), :] = nr
                st_im[tile_rows(k), :] = ni
                hre_out[:, kc] = nr
                him_out[:, kc] = ni

    if sequential:
        @pl.when(pl.program_id(1) == 0)
        def _():
            c_re[...] = jnp.zeros_like(c_re)
            c_im[...] = jnp.zeros_like(c_im)

        ar = are_ref[...]
        ai = aim_ref[...]
        groups = range(n_tiles // SUBLANES)

        def token_rows(t, m):
            return pl.ds(t + m * SUBLANES * pitch, SUBLANES, stride=pitch)

        def step(t, h):
            hr, hi = h
            bur = jnp.concatenate([st_re[token_rows(t, m), :] for m in groups], axis=0)
            bui = jnp.concatenate([st_im[token_rows(t, m), :] for m in groups], axis=0)
            nr = ar * hr - ai * hi + bur
            ni = ar * hi + ai * hr + bui
            for m in groups:
                st_re[token_rows(t, m), :] = nr[m * SUBLANES:(m + 1) * SUBLANES]
                st_im[token_rows(t, m), :] = ni[m * SUBLANES:(m + 1) * SUBLANES]
            return nr, ni

        hr, hi = lax.fori_loop(0, tm, step, (c_re[...], c_im[...]), unroll=unroll)
        c_re[...] = hr
        c_im[...] = hi
        hre_out[0] = hr
        him_out[0] = hi

    ys = []
    for kb in range(n_blk):
        tiles = range(kb * tiles_per_blk, (kb + 1) * tiles_per_blk)
        h_re = jnp.concatenate([st_re[tile_rows(k), :] for k in tiles], axis=1)
        h_im = jnp.concatenate([st_im[tile_rows(k), :] for k in tiles], axis=1)
        yre = jnp.dot(h_re.astype(BF16), cre_ref[kb], preferred_element_type=F32)
        yim = jnp.dot(h_im.astype(BF16), cim_ref[kb], preferred_element_type=F32)
        ys.append(yre - yim)
    y = jnp.concatenate(ys, axis=1) + d_ref[...] * u
    y = jax.nn.gelu(y)
    gate = jnp.dot(y.astype(BF16), wglu_ref[...], preferred_element_type=F32) + bglu_ref[...]
    yb_ref[...] = (y * jax.nn.sigmoid(gate)).astype(BF16)


def _s5(proj, row_block0, n_rows, tm, seqs, u_col_block, width, consts, yb_all, h0=None):
    sequential = h0 is None
    bre = consts[0]
    n_blk, _, kst = bre.shape
    n_state = n_blk * kst
    n_tiles = n_state // LANES
    assert tm % SUBLANES == 0 and n_tiles % SUBLANES == 0
    pitch = tm
    per_seq = n_rows // seqs // tm
    u_map = lambda b, c: (row_block0 + b * per_seq + c, u_col_block)
    out_map = lambda b, c: (b * per_seq + c, 0)
    row_map = lambda b, c: (row_block0 + b * per_seq + c, 0)
    in_specs = [pl.BlockSpec((tm, width), u_map)]
    args = [proj]
    if not sequential:
        in_specs += [pl.BlockSpec((tm, n_state), out_map)] * 2
        args += list(h0)
    in_specs += [_const_spec(c.shape) for c in consts] + [pl.BlockSpec(memory_space=pl.ANY)]
    args += list(consts) + [yb_all]
    scratch = [pltpu.VMEM((n_tiles * pitch, LANES), F32), pltpu.VMEM((n_tiles * pitch, LANES), F32)]
    if sequential:
        scratch += [pltpu.VMEM((n_tiles, LANES), F32), pltpu.VMEM((n_tiles, LANES), F32)]
        st_spec = pl.BlockSpec((1, n_tiles, LANES), lambda b, c: (b, 0, 0))
        st_shape = jax.ShapeDtypeStruct((seqs, n_tiles, LANES), F32)
    else:
        st_spec = pl.BlockSpec((tm, n_state), out_map)
        st_shape = jax.ShapeDtypeStruct((n_rows, n_state), F32)
    return pl.pallas_call(
        functools.partial(_s5_kernel, sequential=sequential, n_blk=n_blk, pitch=pitch, unroll=4),
        grid=(seqs, per_seq),
        in_specs=in_specs,
        out_specs=[pl.BlockSpec((tm, width), row_map), st_spec, st_spec],
        out_shape=[jax.ShapeDtypeStruct(yb_all.shape, yb_all.dtype), st_shape, st_shape],
        input_output_aliases={len(args) - 1: 0},
        scratch_shapes=scratch,
        compiler_params=_cparams(("parallel", "arbitrary")),
        name="s5_seq" if sequential else "s5_step",
    )(*args)


def _route(logits, n_grp):
    lane = lax.broadcasted_iota(jnp.int32, logits.shape, 1).astype(F32)
    neg = jnp.float32(-1e30)
    big = jnp.float32(1e9)
    is_grp = lane < n_grp
    gl = jnp.where(is_grp, logits, neg)
    gmax = jnp.max(gl, axis=1, keepdims=True)
    gsum = jnp.sum(jnp.where(is_grp, jnp.exp(gl - gmax), 0.0), axis=1, keepdims=True)
    g_p = 1.0 / gsum
    g_idx = jnp.min(jnp.where(is_grp & (gl == gmax), lane, big), axis=1, keepdims=True)
    lo = n_grp + g_idx * EXPERTS_PER_GROUP
    in_grp = (lane >= lo) & (lane < lo + EXPERTS_PER_GROUP)
    el = jnp.where(in_grp, logits, neg)
    v1 = jnp.max(el, axis=1, keepdims=True)
    i1 = jnp.min(jnp.where(in_grp & (el == v1), lane, big), axis=1, keepdims=True)
    rest = in_grp & (lane != i1)
    el2 = jnp.where(rest, logits, neg)
    v2 = jnp.max(el2, axis=1, keepdims=True)
    i2 = jnp.min(jnp.where(rest & (el2 == v2), lane, big), axis=1, keepdims=True)
    e2 = jnp.exp(v2 - v1)
    w1 = g_p / (1.0 + e2)
    w2 = g_p * e2 / (1.0 + e2)
    return (jnp.where(lane == ROUTE_E1, i1 - n_grp, 0.0) + jnp.where(lane == ROUTE_E2, i2 - n_grp, 0.0)
            + jnp.where(lane == ROUTE_W1, w1, 0.0) + jnp.where(lane == ROUTE_W2, w2, 0.0))


def _merge_kernel(ya_ref, g_ref, yb_ref, ga_ref, gb_ref, x_ref, wa_ref, wb_ref, wo_ref,
                  gf_ref, wr_ref, br_ref, h_out, xn_out, comb_out, *, n_grp):
    ya = (ya_ref[...] * g_ref[...]).astype(BF16)
    ma = jnp.dot(ya, wa_ref[...], preferred_element_type=F32)
    mb = jnp.dot(yb_ref[...], wb_ref[...], preferred_element_type=F32)
    merged = jax.nn.sigmoid(ga_ref[...]) * ma + jax.nn.sigmoid(gb_ref[...]) * mb
    h = x_ref[...] + jnp.dot(merged.astype(BF16), wo_ref[...], preferred_element_type=F32)
    h_out[...] = h
    ms = jnp.mean(h * h, axis=-1, keepdims=True)
    xn = h * lax.rsqrt(ms + NORM_EPS) * gf_ref[...]
    xn_out[...] = xn
    x_hi = xn.astype(BF16)
    x_lo = (xn - x_hi.astype(F32)).astype(BF16)
    w_hi, w_lo = wr_ref[0], wr_ref[1]
    logits = (jnp.dot(x_hi, w_hi, preferred_element_type=F32)
              + (jnp.dot(x_hi, w_lo, preferred_element_type=F32)
                 + jnp.dot(x_lo, w_hi, preferred_element_type=F32))) + br_ref[...]
    comb_out[...] = _route(logits, n_grp)


def _merge(ya, g, yb, proj, x_all, w_br_a, w_br_b, w_out, g_ffn, w_router, b_router,
           tm, ga_col_block, n_grp):
    nt, d = x_all.shape
    wdt = ya.shape[1]
    row = lambda i: (i, 0)
    single = dict(pipeline_mode=pl.Buffered(1))
    in_specs = [pl.BlockSpec((tm, wdt), row), pl.BlockSpec((tm, wdt), row),
                pl.BlockSpec((tm, wdt), row),
                pl.BlockSpec((tm, d), lambda i: (i, ga_col_block)),
                pl.BlockSpec((tm, d), lambda i: (i, ga_col_block + 1)),
                pl.BlockSpec((tm, d), row),
                pl.BlockSpec(w_br_a.shape, lambda i: (0, 0), **single),
                pl.BlockSpec(w_br_b.shape, lambda i: (0, 0), **single),
                pl.BlockSpec(w_out.shape, lambda i: (0, 0), **single),
                _const_spec(g_ffn.shape), _const_spec(w_router.shape), _const_spec(b_router.shape)]
    return pl.pallas_call(
        functools.partial(_merge_kernel, n_grp=n_grp),
        grid=(nt // tm,),
        in_specs=in_specs,
        out_specs=[pl.BlockSpec((tm, d), row), pl.BlockSpec((tm, d), row),
                   pl.BlockSpec((tm, LANES), row)],
        out_shape=[jax.ShapeDtypeStruct((nt, d), F32), jax.ShapeDtypeStruct((nt, d), F32),
                   jax.ShapeDtypeStruct((nt, LANES), F32)],
        compiler_params=_cparams(("parallel",)),
        name="merge_route",
    )(ya, g, yb, proj, proj, x_all, w_br_a, w_br_b, w_out, g_ffn, w_router, b_router)


def _prefix_loops(n_rows, row):
    def block(b, c):
        for u in range(SUBLANES):
            row(b * SUBLANES + u)
        return c

    def single(r, c):
        row(r)
        return c

    n_blocks = lax.shift_right_logical(n_rows, LOG2_SUBLANES)
    lax.fori_loop(0, n_blocks, block, 0)
    lax.fori_loop(n_blocks * SUBLANES, n_rows, single, 0)


def _moe_kernel(te_ref, first_ref, nvalid_ref, wslot_ref, nexte_ref, nused_ref, src_cur, src_nxt,
                xn_hbm, wg_hbm, wu_hbm, wd_hbm, o_ref,
                xbuf, wg_f, wu_f, wd_f, wg_bf, wu_bf, wd_bf, gsem, wsem, *, tm):
    i = pl.program_id(0)
    n_used = nused_ref[0]
    slot = lax.rem(i, 2)

    def prefix_wait(n_rows, copy_of):
        n_full = pl.multiple_of(
            lax.shift_left(lax.shift_right_logical(n_rows, LOG2_SUBLANES), LOG2_SUBLANES), SUBLANES)

        @pl.when(n_full > 0)
        def _():
            copy_of(0, n_full).wait()

        def single(r, c):
            copy_of(r, 1).wait()
            return c
        lax.fori_loop(n_full, n_rows, single, 0)

    def gather_start(src, s, n_rows):
        _prefix_loops(n_rows, lambda r: pltpu.make_async_copy(
            xn_hbm.at[pl.ds(src[0, 0, r], 1)], xbuf.at[s, pl.ds(r, 1)], gsem.at[s]).start())

    def gather_wait(s, n_rows):
        prefix_wait(n_rows, lambda r0, n: pltpu.make_async_copy(
            xn_hbm.at[pl.ds(0, n)], xbuf.at[s, pl.ds(r0, n)], gsem.at[s]))

    def weight_copies(e, s):
        return [pltpu.make_async_copy(hbm.at[e], buf.at[s], wsem.at[s])
                for hbm, buf in ((wg_hbm, wg_f), (wu_hbm, wu_f), (wd_hbm, wd_f))]

    @pl.when(i == 0)
    def _():
        for c in weight_copies(te_ref[0], 0):
            c.start()
        xbuf[...] = jnp.zeros_like(xbuf)
        gather_start(src_cur, 0, nvalid_ref[0])

    @pl.when(i < n_used)
    def _():
        gather_wait(slot, nvalid_ref[i])

    @pl.when(i + 1 < n_used)
    def _():
        gather_start(src_nxt, 1 - slot, nvalid_ref[i + 1])

    @pl.when(i < n_used)
    def _():
        @pl.when(first_ref[i] == 1)
        def _():
            ws = wslot_ref[i]
            for c in weight_copies(te_ref[i], ws):
                c.wait()

            @pl.when(nexte_ref[i] >= 0)
            def _():
                for c in weight_copies(nexte_ref[i], 1 - ws):
                    c.start()

            wg_bf[...] = wg_f[ws].astype(BF16)
            wu_bf[...] = wu_f[ws].astype(BF16)
            wd_bf[...] = wd_f[ws].astype(BF16)

        x = xbuf[slot].astype(BF16)
        xg = jnp.dot(x, wg_bf[...], preferred_element_type=F32)
        xu = jnp.dot(x, wu_bf[...], preferred_element_type=F32)
        hid = (jax.nn.silu(xg) * xu).astype(BF16)
        o_ref[...] = jnp.dot(hid, wd_bf[...], preferred_element_type=F32)

    @pl.when(i >= n_used)
    def _():
        o_ref[...] = jnp.zeros_like(o_ref)


def _moe(xn, tile_expert, tile_first, tile_valid, tile_wslot, tile_nexte, n_used, src_rows,
         wg, wu, wd, tm):
    nt, d = xn.shape
    n_exp, _, de = wg.shape
    n_tiles = src_rows.shape[0]
    smem_cur = pl.BlockSpec((1, 1, tm), lambda i, *_: (i, 0, 0), memory_space=pltpu.SMEM)
    smem_nxt = pl.BlockSpec((1, 1, tm), lambda i, *_: (jnp.minimum(i + 1, n_tiles - 1), 0, 0),
                            memory_space=pltpu.SMEM)
    grid_spec = pltpu.PrefetchScalarGridSpec(
        num_scalar_prefetch=6,
        grid=(n_tiles,),
        in_specs=[smem_cur, smem_nxt] + [pl.BlockSpec(memory_space=pl.ANY)] * 4,
        out_specs=pl.BlockSpec((tm, d), lambda i, *_: (i, 0)),
        scratch_shapes=[pltpu.VMEM((2, tm, d), F32),
                        pltpu.VMEM((2, d, de), F32), pltpu.VMEM((2, d, de), F32),
                        pltpu.VMEM((2, de, d), F32),
                        pltpu.VMEM((d, de), BF16), pltpu.VMEM((d, de), BF16),
                        pltpu.VMEM((de, d), BF16),
                        pltpu.SemaphoreType.DMA((2,)), pltpu.SemaphoreType.DMA((2,))])
    return pl.pallas_call(
        functools.partial(_moe_kernel, tm=tm),
        grid_spec=grid_spec,
        out_shape=jax.ShapeDtypeStruct((n_tiles * tm, d), F32),
        compiler_params=_cparams(("arbitrary",)),
        name="moe_grouped",
    )(tile_expert, tile_first, tile_valid, tile_wslot, tile_nexte, n_used, src_rows, src_rows,
      xn, wg, wu, wd)


def _moe_plan(route, n_exp, tm):
    nt = route.shape[0]
    n_pairs = 2 * nt
    n_tiles = n_pairs // tm + n_exp
    eid = jnp.concatenate([route[:, ROUTE_E1], route[:, ROUTE_E2]]).astype(jnp.int32)
    onehot = (eid[:, None] == jnp.arange(n_exp, dtype=jnp.int32)[None, :]).astype(jnp.int32)
    csum = jnp.cumsum(onehot, axis=0)
    rank = jnp.take_along_axis(csum, eid[:, None], axis=1)[:, 0] - 1
    cnt = csum[-1]
    tiles = (cnt + tm - 1) // tm
    tile_end = jnp.cumsum(tiles)
    n_used = tile_end[-1]
    pos = (tile_end - tiles)[eid] * tm + rank
    pair = jnp.arange(n_pairs, dtype=jnp.int32)
    src_rows = jnp.zeros((n_tiles * tm,), jnp.int32).at[pos].set(
        pair, unique_indices=True, mode="promise_in_bounds") % nt
    tile_id = jnp.minimum(jnp.arange(n_tiles, dtype=jnp.int32), n_used - 1)
    tile_expert = jnp.sum((tile_end[None, :] <= tile_id[:, None]).astype(jnp.int32), axis=1)
    tile_first = jnp.concatenate(
        [jnp.ones((1,), jnp.int32), (tile_expert[1:] != tile_expert[:-1]).astype(jnp.int32)])
    tile_valid = jnp.clip(cnt[tile_expert] - (tile_id - (tile_end - tiles)[tile_expert]) * tm, 0, tm)
    tile_wslot = (jnp.cumsum(tile_first) - 1) % 2
    nxt_tile = tile_end[tile_expert]
    nxt_expert = jnp.sum((tile_end[None, :] <= nxt_tile[:, None]).astype(jnp.int32), axis=1)
    tile_nexte = jnp.where(nxt_tile < n_used, nxt_expert, -1)
    return (tile_expert, tile_first, tile_valid.astype(jnp.int32), tile_wslot.astype(jnp.int32),
            tile_nexte.astype(jnp.int32), n_used.reshape(1).astype(jnp.int32),
            src_rows.reshape(n_tiles, 1, tm), pos[:nt], pos[nt:])


def _final_kernel(p1_cur, p1_nxt, p2_cur, p2_nxt, h_ref, route_ref, g_ref, y_hbm, o_ref,
                  ybuf, sem, *, tm):
    i = pl.program_id(0)
    slot = lax.rem(i, 2)

    def gather_start(p1, p2, s):
        def row(r):
            pltpu.make_async_copy(y_hbm.at[pl.ds(p1[0, 0, r], 1)], ybuf.at[s, 0, pl.ds(r, 1)],
                                  sem.at[s]).start()
            pltpu.make_async_copy(y_hbm.at[pl.ds(p2[0, 0, r], 1)], ybuf.at[s, 1, pl.ds(r, 1)],
                                  sem.at[s]).start()
        _prefix_loops(tm, row)

    def gather_wait(s):
        for e in range(2):
            pltpu.make_async_copy(y_hbm.at[pl.ds(0, tm)], ybuf.at[s, e], sem.at[s]).wait()

    @pl.when(i == 0)
    def _():
        gather_start(p1_cur, p2_cur, 0)

    gather_wait(slot)

    @pl.when(i + 1 < pl.num_programs(0))
    def _():
        gather_start(p1_nxt, p2_nxt, 1 - slot)

    lane = lax.broadcasted_iota(jnp.int32, route_ref.shape, 1)
    route = route_ref[...]
    w1 = jnp.sum(jnp.where(lane == ROUTE_W1, route, 0.0), axis=1, keepdims=True)
    w2 = jnp.sum(jnp.where(lane == ROUTE_W2, route, 0.0), axis=1, keepdims=True)
    h = h_ref[...] + (w1 * ybuf[slot, 0] + w2 * ybuf[slot, 1])
    ms = jnp.mean(h * h, axis=-1, keepdims=True)
    o_ref[...] = h * lax.rsqrt(ms + NORM_EPS) * g_ref[...]


def _final(h, y_sorted, pos1, pos2, route, g_final, row0, seg_stride, seg_rows, n_seg, tm):
    nt, d = h.shape
    per_seg = seg_rows // tm
    n_steps = n_seg * per_seg
    assert all(x % SUBLANES == 0 for x in (row0, seg_stride, tm, nt))

    def off(i):
        return pl.multiple_of(row0 + (i // per_seg) * seg_stride + (i % per_seg) * tm, SUBLANES)

    rows = lambda width: pl.BlockSpec((pl.Element(tm), pl.Element(width)), lambda i: (off(i), 0))
    smem = lambda nxt: pl.BlockSpec(
        (1, 1, tm), lambda i: (jnp.minimum(i + nxt, n_steps - 1), 0, 0), memory_space=pltpu.SMEM)
    sel = lambda p: jnp.concatenate(
        [p[row0 + s * seg_stride:row0 + s * seg_stride + seg_rows] for s in range(n_seg)]
    ).reshape(n_steps, 1, tm)
    p1, p2 = sel(pos1), sel(pos2)
    return pl.pallas_call(
        functools.partial(_final_kernel, tm=tm),
        grid=(n_steps,),
        in_specs=[smem(0), smem(1), smem(0), smem(1), rows(d), rows(LANES),
                  pl.BlockSpec((1, d), lambda i: (0, 0)), pl.BlockSpec(memory_space=pl.ANY)],
        out_specs=pl.BlockSpec((tm, d), lambda i: (i, 0)),
        out_shape=jax.ShapeDtypeStruct((n_seg * seg_rows, d), F32),
        scratch_shapes=[pltpu.VMEM((2, 2, tm, d), F32), pltpu.SemaphoreType.DMA((2,))],
        compiler_params=_cparams(("arbitrary",)),
        name="final_norm",
    )(p1, p1, p2, p2, h, route, g_final, y_sorted)


def _pad_cols(w, to):
    return jnp.pad(w, ((0, 0), (0, to - w.shape[1])))


def _pad_rows(w, to):
    return jnp.pad(w, ((0, to - w.shape[0]), (0, 0)))


def _tile_plan(n_t, t_p, seq):
    plan = dict(
        inproj_rows=n_t // 6,
        inproj_cols=4 * LANES,
        seq_rows=t_p // 6,
        wkv_chunk=48, wkv_unroll=8,
        merge_rows=2 * LANES, moe_rows=2 * LANES, final_rows=4 * LANES)
    assert n_t % plan["inproj_rows"] == 0 and plan["inproj_rows"] % SUBLANES == 0
    assert t_p % plan["seq_rows"] == 0 and plan["seq_rows"] % SUBLANES == 0
    assert t_p % plan["wkv_chunk"] == 0 and plan["wkv_chunk"] % plan["wkv_unroll"] == 0
    assert n_t % plan["merge_rows"] == 0 and seq % plan["final_rows"] == 0
    return plan


def _s5_consts(lam_re, lam_im, log_dt, b_re, b_im, c_re, c_im, d, w_glu, b_glu):
    dt = jnp.exp(log_dt)[:, None]
    mag = jnp.exp(lam_re * dt)
    abar_re = mag * jnp.cos(lam_im * dt)
    abar_im = mag * jnp.sin(lam_im * dt)
    den = lam_re * lam_re + lam_im * lam_im
    nr = abar_re - 1.0
    coef_re = (nr * lam_re + abar_im * lam_im) / den
    coef_im = (abar_im * lam_re - nr * lam_im) / den
    bbar_re = coef_re[..., None] * b_re - coef_im[..., None] * b_im
    bbar_im = coef_re[..., None] * b_im + coef_im[..., None] * b_re
    n_g, n_p, n_c = b_re.shape
    gpb = LANES // n_c
    eye = jnp.eye(gpb, dtype=F32)

    def in_blk(bb):
        bb = bb.reshape(n_g // gpb, gpb, n_p, n_c)
        return jnp.einsum('kgpc,gh->kgchp', bb, eye).reshape(
            n_g // gpb, gpb * n_c, gpb * n_p).astype(BF16)

    def out_blk(cc):
        cc = cc.reshape(n_g // gpb, gpb, n_c, n_p)
        return jnp.einsum('kgcp,gh->khpgc', cc, eye).reshape(
            n_g // gpb, gpb * n_p, gpb * n_c).astype(BF16)

    return (in_blk(bbar_re), in_blk(bbar_im), out_blk(c_re), out_blk(c_im),
            d.reshape(1, -1), abar_re.reshape(-1, LANES), abar_im.reshape(-1, LANES),
            w_glu.astype(BF16), b_glu.reshape(1, -1))


def kernel(x_prompt, x_sample, state_shift, state_wkv, state_ssm_re, state_ssm_im, meta_tokens, g_mix, w_in, shift_mu, w0, w_decay_up, a0, w_aaa_up, w_gate_up, k_k, k_a, r_k, lnx_w, lnx_b, ssm_lam_re, ssm_lam_im, ssm_log_dt, ssm_b_re, ssm_b_im, ssm_c_re, ssm_c_im, ssm_d, w_glu, b_glu, w_br_a, w_br_b, w_out, g_ffn, w_router_grp, b_router_grp, w_router_exp, b_router_exp, w_exp_gate, w_exp_up, w_exp_down, g_final):
    depth = g_mix.shape[0]
    assert depth == 1, "single-layer trunk"
    bsz, seq, d = x_prompt.shape
    nb = x_sample.shape[0]
    assert x_sample.shape[1] == 1
    t_p = seq + N_META
    n_p = bsz * t_p
    width = k_k.shape[1]
    heads = width // HEAD
    n_dl, n_al, n_gl = w_decay_up.shape[1], w_aaa_up.shape[1], w_gate_up.shape[1]
    n_grp = w_router_grp.shape[2]
    n_exp = w_router_exp.shape[2]
    assert bsz * heads * 2 == LANES and (nb * heads) % LANES == 0

    s_row0 = -(-n_p // nb) * nb
    n_t = s_row0 + nb
    pieces = []
    for b in range(bsz):
        pieces += [meta_tokens, x_prompt[b]]
    x_all = jnp.concatenate(
        pieces + [jnp.zeros((s_row0 - n_p, d), F32), x_sample.reshape(nb, d)], axis=0)

    c_rkv = 3 * width
    c_xw, c_xa, c_xg = c_rkv, c_rkv + n_dl, c_rkv + n_dl + n_al
    c_u = c_xg + n_gl
    lo_width = 2 * LORA_PAD + n_gl

    def regroup(m):
        return jnp.concatenate(
            [m[:, :c_rkv], m[:, c_u:], _pad_cols(m[:, c_xw:c_xa], LORA_PAD),
             _pad_cols(m[:, c_xa:c_xg], LORA_PAD), m[:, c_xg:c_u]], axis=1)

    def ungroup(m):
        lo = c_rkv + width + 2 * d
        return jnp.concatenate(
            [m[:, :c_rkv], m[:, lo:lo + n_dl], m[:, lo + LORA_PAD:lo + LORA_PAD + n_al],
             m[:, lo + 2 * LORA_PAD:]], axis=1)

    w_in_p = regroup(w_in[0].astype(BF16))
    n_proj = w_in_p.shape[1]
    tiles = _tile_plan(n_t, t_p, seq)
    proj = _inproj(x_all, g_mix, w_in_p, tm=tiles["inproj_rows"], tn=tiles["inproj_cols"])

    mu_all = regroup(_pad_cols(shift_mu, c_u + width + 2 * d))
    mu_rkv, mu_lo = mu_all[:, :c_rkv], mu_all[:, n_proj - lo_width:]
    st_all = regroup(_pad_cols(state_shift[0], c_u + width + 2 * d))
    prev_s = (st_all[:, :c_rkv], st_all[:, n_proj - lo_width:])
    lo_col_block = (n_proj - lo_width) // lo_width
    prep_w = (mu_rkv, mu_lo, w0, a0,
              _pad_rows(w_decay_up[0], LORA_PAD).astype(BF16),
              _pad_rows(w_aaa_up[0], LORA_PAD).astype(BF16), w_gate_up[0].astype(BF16))

    tm_seq = tiles["seq_rows"]
    g_all = jnp.zeros((n_t, width), F32)
    rp, wp, kp, vp, ap, g_all = _rwkv_prep(proj, 0, n_p, tm_seq, bsz, width, lo_col_block,
                                                lo_width, *prep_w, g_all)
    rs, ws, ks, vs, as_, g_all = _rwkv_prep(proj, s_row0 // nb, nb, nb, 1, width,
                                                 lo_col_block, lo_width, *prep_w, g_all,
                                                 prev=prev_s)

    s5c = _s5_consts(ssm_lam_re[0], ssm_lam_im[0], ssm_log_dt[0], ssm_b_re[0], ssm_b_im[0],
                     ssm_c_re[0], ssm_c_im[0], ssm_d[0], w_glu[0], b_glu)
    n_g, n_st = ssm_lam_re.shape[1], ssm_lam_re.shape[2]
    u_col_block = c_rkv // width
    yb_all = jnp.zeros((n_t, width), BF16)
    yb_all, re_p, im_p = _s5(proj, 0, n_p, tm_seq, bsz, u_col_block, width, s5c, yb_all)
    h0 = (state_ssm_re[0].reshape(nb, n_g * n_st), state_ssm_im[0].reshape(nb, n_g * n_st))
    yb_all, re_s, im_s = _s5(proj, s_row0 // nb, nb, nb, 1, u_col_block, width, s5c, yb_all,
                             h0=h0)

    half = HEAD // 2

    def to_lanes_p(z):
        return z.reshape(t_p, bsz * heads, HEAD).transpose(0, 2, 1).reshape(t_p, half, LANES)

    def even_odd(c):
        return jnp.concatenate([c[0::2], c[1::2]], axis=0)

    def to_lanes_s(z):
        return z.reshape(nb, heads, HEAD).transpose(2, 0, 1).reshape(1, HEAD, nb * heads)

    def head_const(c):
        return jnp.tile(c.reshape(heads, HEAD).T, (1, LANES // heads))

    rk_l = head_const(r_k[0])
    kk_l, ka_l = head_const(k_k[0]), head_const(k_a[0])
    lw_p = jnp.broadcast_to(lnx_w[0].reshape(heads, half, 2).transpose(1, 2, 0)[:, :, None, :],
                            (half, 2, bsz, heads)).reshape(half, LANES)
    lb_p = jnp.broadcast_to(lnx_b[0].reshape(heads, half, 2).transpose(1, 2, 0)[:, :, None, :],
                            (half, 2, bsz, heads)).reshape(half, LANES)
    lw_s = jnp.tile(lnx_w[0].reshape(heads, HEAD).T, (1, LANES // heads))
    lb_s = jnp.tile(lnx_b[0].reshape(heads, HEAD).T, (1, LANES // heads))

    s0_p = jnp.zeros((HEAD, half, LANES), F32)
    y_p, sf_p = _wkv(to_lanes_p(rp), to_lanes_p(wp), to_lanes_p(kp), to_lanes_p(ap),
                     to_lanes_p(vp), s0_p, even_odd(kk_l), even_odd(ka_l), even_odd(rk_l),
                     lw_p, lb_p,
                     tc=tiles["wkv_chunk"], isplit=True, unroll=tiles["wkv_unroll"])
    s0_s = state_wkv[0].transpose(3, 2, 0, 1).reshape(HEAD, HEAD, nb * heads)
    y_s, sf_s = _wkv(to_lanes_s(rs), to_lanes_s(ws), to_lanes_s(ks), to_lanes_s(as_),
                     to_lanes_s(vs), s0_s, kk_l, ka_l, rk_l, lw_s, lb_s,
                     tc=1, isplit=False, unroll=1)

    ya_p = y_p.reshape(t_p, half, 2, bsz, heads).transpose(3, 0, 4, 1, 2).reshape(n_p, width)
    ya_s = y_s.reshape(HEAD, nb, heads).transpose(1, 2, 0).reshape(nb, width)
    sf_p = jnp.stack([sf_p[:half], sf_p[half:]], axis=1).reshape(HEAD, half, LANES)
    wkv_p = sf_p.reshape(HEAD, half, 2, bsz, heads).transpose(3, 4, 1, 2, 0).reshape(
        1, bsz, heads, HEAD, HEAD)
    wkv_s = sf_s.reshape(HEAD, HEAD, nb, heads).transpose(2, 3, 1, 0)[None]

    def unify(a, b):
        return jnp.concatenate([a, jnp.zeros((s_row0 - n_p, a.shape[1]), a.dtype), b], axis=0)

    w_router = _pad_cols(jnp.concatenate([w_router_grp[0], w_router_exp[0]], axis=1), LANES)
    w_router_hi = w_router.astype(BF16)
    w_router = jnp.stack([w_router_hi, (w_router - w_router_hi.astype(F32)).astype(BF16)])
    b_router = _pad_cols(jnp.concatenate([b_router_grp, b_router_exp], axis=1), LANES)
    assert n_grp + n_exp <= LANES and n_exp == n_grp * EXPERTS_PER_GROUP
    h1, xn2, route = _merge(unify(ya_p, ya_s), g_all, yb_all, proj, x_all,
                           w_br_a[0].astype(BF16), w_br_b[0].astype(BF16), w_out[0].astype(BF16),
                           g_ffn, w_router, b_router, tm=tiles["merge_rows"],
                           ga_col_block=(c_rkv + width) // d,
                           n_grp=n_grp)
    tm_moe = tiles["moe_rows"]
    te, tf, tv, tws, tne, nu, src_rows, pos1, pos2 = _moe_plan(route, n_exp, tm_moe)
    y_moe = _moe(xn2, te, tf, tv, tws, tne, nu, src_rows, w_exp_gate[0], w_exp_up[0],
                 w_exp_down[0], tm_moe)
    gfin = g_final.reshape(1, d)
    y_prompt = _final(h1, y_moe, pos1, pos2, route, gfin, N_META, t_p, seq, bsz,
                      tm=tiles["final_rows"]).reshape(bsz, seq, d)
    y_sample = _final(h1, y_moe, pos1, pos2, route, gfin, s_row0, 0, nb, 1,
                      tm=nb).reshape(nb, 1, d)
    last_p = jnp.concatenate([proj[(b + 1) * t_p - 1:(b + 1) * t_p] for b in range(bsz)], axis=0)
    shift_p = ungroup(last_p)[None]
    shift_s = ungroup(proj[s_row0:])[None]
    return (y_prompt, y_sample, shift_p, wkv_p,
            re_p.reshape(1, bsz, n_g, n_st), im_p.reshape(1, bsz, n_g, n_st),
            shift_s, wkv_s,
            re_s.reshape(1, nb, n_g, n_st), im_s.reshape(1, nb, n_g, n_st))
```

```python
import functools

import jax
import jax.numpy as jnp
from jax import lax
from jax.experimental import pallas as pl
from jax.experimental.pallas import tpu as pltpu

F32 = jnp.float32
BF16 = jnp.bfloat16

NORM_EPS = 1e-6
LNX_EPS = 64e-5
N_META = 16
HEAD = 64
EXPERTS_PER_GROUP = 8
LANES = 128
SUBLANES = 8
LOG2_SUBLANES = SUBLANES.bit_length() - 1
LORA_PAD = 128
ROUTE_E1, ROUTE_E2, ROUTE_W1, ROUTE_W2 = 0, 1, 2, 3
VMEM_LIMIT = 56 * 1024 * 1024


def _cparams(sem):
    return pltpu.CompilerParams(dimension_semantics=sem, vmem_limit_bytes=VMEM_LIMIT)


def _const_spec(shape):
    nd = len(shape)
    return pl.BlockSpec(shape, lambda *_: (0,) * nd)


def _inproj_kernel(x_ref, g_ref, w_ref, o_ref, xn_ref):
    @pl.when(pl.program_id(1) == 0)
    def _():
        x = x_ref[...]
        ms = jnp.mean(x * x, axis=-1, keepdims=True)
        xn_ref[...] = (x * lax.rsqrt(ms + NORM_EPS) * g_ref[...]).astype(BF16)

    o_ref[...] = jnp.dot(xn_ref[...], w_ref[...], preferred_element_type=F32)


def _inproj(x_all, g_mix, w_in_p, tm, tn):
    nt, d = x_all.shape
    n_out = w_in_p.shape[1]
    return pl.pallas_call(
        _inproj_kernel,
        grid=(nt // tm, n_out // tn),
        in_specs=[pl.BlockSpec((tm, d), lambda i, j: (i, 0)),
                  pl.BlockSpec((1, d), lambda i, j: (0, 0)),
                  pl.BlockSpec((d, tn), lambda i, j: (0, j))],
        out_specs=pl.BlockSpec((tm, tn), lambda i, j: (i, j)),
        out_shape=jax.ShapeDtypeStruct((nt, n_out), F32),
        scratch_shapes=[pltpu.VMEM((tm, d), BF16)],
        compiler_params=_cparams(("parallel", "arbitrary")),
        name="inproj",
    )(x_all, g_mix, w_in_p)


def _softplus(z):
    return jnp.maximum(z, 0.0) + jnp.log1p(jnp.exp(-jnp.abs(z)))


def _rwkv_prep_kernel(*refs, width, carry_prev):
    if carry_prev:
        (rkv_ref, lo_ref, mu_rkv_ref, mu_lo_ref, w0_ref, a0_ref,
         wd_ref, wa_ref, wg_ref, _g_all,
         r_out, w_out, k_out, v_out, a_out, g_out, c_rkv, c_lo) = refs

        @pl.when(pl.program_id(1) == 0)
        def _():
            c_rkv[...] = jnp.zeros_like(c_rkv)
            c_lo[...] = jnp.zeros_like(c_lo)
    else:
        (rkv_ref, lo_ref, prev_rkv_ref, prev_lo_ref, mu_rkv_ref, mu_lo_ref, w0_ref, a0_ref,
         wd_ref, wa_ref, wg_ref, _g_all,
         r_out, w_out, k_out, v_out, a_out, g_out) = refs

    tm = rkv_ref.shape[0]
    first_row = lax.broadcasted_iota(jnp.int32, (tm, 1), 0) == 0

    def shifted(p, prev_ref, carry_ref, cols):
        if carry_prev:
            prev = jnp.where(first_row, carry_ref[:, cols], pltpu.roll(p, 1, 0))
        else:
            prev = prev_ref[:, cols]
        return prev

    def lerp(p, prev, mu):
        return p + (prev - p) * mu

    lo_cols = slice(0, lo_ref.shape[1])
    p_lo = lo_ref[...]
    q_lo = lerp(p_lo, shifted(p_lo, None if carry_prev else prev_lo_ref,
                              c_lo if carry_prev else None, lo_cols), mu_lo_ref[...])
    xw = q_lo[:, 0:LORA_PAD]
    xa = q_lo[:, LORA_PAD:2 * LORA_PAD]
    xg = q_lo[:, 2 * LORA_PAD:]
    dw = jnp.dot(jnp.tanh(xw).astype(BF16), wd_ref[...], preferred_element_type=F32)
    wlog = -_softplus(-(w0_ref[...] + dw)) - 0.5
    w_out[...] = jnp.exp(-jnp.exp(wlog))
    a = jax.nn.sigmoid(a0_ref[...] + jnp.dot(xa.astype(BF16), wa_ref[...],
                                             preferred_element_type=F32))
    a_out[...] = a
    g_out[...] = jnp.dot(jax.nn.sigmoid(xg).astype(BF16), wg_ref[...],
                         preferred_element_type=F32)

    def q_of(idx):
        cols = slice(idx * width, (idx + 1) * width)
        p = rkv_ref[:, cols]
        prev = shifted(p, None if carry_prev else prev_rkv_ref,
                       c_rkv if carry_prev else None, cols)
        return lerp(p, prev, mu_rkv_ref[:, cols])

    r_out[...] = q_of(0)
    k_out[...] = q_of(1)
    v_out[...] = q_of(2)

    if carry_prev:
        c_rkv[...] = rkv_ref[tm - 1:tm, :]
        c_lo[...] = lo_ref[tm - 1:tm, :]


def _rwkv_prep(proj, row_block0, n_rows, tm, seqs, width, lo_col_block, lo_width,
               mu_rkv, mu_lo, w0, a0, wd, wa, wg, g_all, prev=None):
    carry_prev = prev is None
    per_seq = n_rows // seqs // tm
    row_map = lambda b, c: (row_block0 + b * per_seq + c, 0)
    lo_map = lambda b, c: (row_block0 + b * per_seq + c, lo_col_block)
    out_map = lambda b, c: (b * per_seq + c, 0)
    in_specs = [pl.BlockSpec((tm, 3 * width), row_map), pl.BlockSpec((tm, lo_width), lo_map)]
    args = [proj, proj]
    if not carry_prev:
        in_specs += [pl.BlockSpec((tm, 3 * width), out_map), pl.BlockSpec((tm, lo_width), out_map)]
        args += list(prev)
    consts = [mu_rkv, mu_lo, w0, a0, wd, wa, wg]
    in_specs += [_const_spec(c.shape) for c in consts] + [pl.BlockSpec(memory_space=pl.ANY)]
    args += consts + [g_all]
    scratch = []
    if carry_prev:
        scratch = [pltpu.VMEM((1, 3 * width), F32), pltpu.VMEM((1, lo_width), F32)]
    out_sd = jax.ShapeDtypeStruct((n_rows // seqs, seqs * width), F32)
    tmaj_map = lambda b, c: (c, b)
    return pl.pallas_call(
        functools.partial(_rwkv_prep_kernel, width=width, carry_prev=carry_prev),
        grid=(seqs, per_seq),
        in_specs=in_specs,
        out_specs=[pl.BlockSpec((tm, width), tmaj_map)] * 5 + [pl.BlockSpec((tm, width), row_map)],
        out_shape=[out_sd] * 5 + [jax.ShapeDtypeStruct(g_all.shape, g_all.dtype)],
        input_output_aliases={len(args) - 1: 5},
        scratch_shapes=scratch,
        compiler_params=_cparams(("parallel", "arbitrary")),
        name="rwkv_prep_seq" if carry_prev else "rwkv_prep_step",
    )(*args)


def _wkv_kernel(r_ref, w_ref, kraw_ref, al_ref, v_ref, s0_ref, kkc_ref, kac_ref, rk_ref, lw_ref,
                lb_ref, y_ref, sf_ref, s_scr, a_scr, b_scr, k_src, *dup_scr,
                ni, nj, tc, isplit, unroll):
    @pl.when(pl.program_id(1) == 0)
    def _():
        s_scr[...] = s0_ref[...]

    def dup(x):
        return jnp.concatenate([x, x], axis=-1) if isplit else x

    if isplit:
        w_src, r_src, v_src = dup_scr
    else:
        w_src, r_src, v_src = w_ref, r_ref, v_ref

    def prep(t, carry):
        kraw = dup(kraw_ref[t])
        al = dup(al_ref[t])
        kk = kraw * kkc_ref[...]
        k_src[t] = kraw * (1.0 + (al - 1.0) * kac_ref[...])
        ss = jnp.sum(kk * kk, axis=0, keepdims=True)
        kkn = kk / jnp.maximum(jnp.sqrt(ss), 1e-12)
        a_scr[t] = -kkn
        b_scr[t] = kkn * al
        if isplit:
            w_src[t] = dup(w_ref[t])
            r_src[t] = dup(r_ref[t])
            v = v_ref[t]
            v_src[t] = jnp.concatenate([v[:ni], v[ni:]], axis=-1)
        return carry

    lax.fori_loop(0, tc, prep, 0, unroll=unroll)

    def row(ref, t, j):
        return ref[t, j:j + 1, :]

    def tree(parts):
        while len(parts) > 1:
            parts = [parts[i] + parts[i + 1] for i in range(0, len(parts), 2)]
        return parts[0]

    n_acc = 4
    sa0 = tree([sum(s_scr[j] * row(a_scr, 0, j) for j in range(q, nj, n_acc))
                for q in range(n_acc)])

    def step(t, sa):
        v = v_src[t]
        tn = jnp.minimum(t + 1, tc - 1)
        y = [None] * n_acc
        san = [None] * n_acc
        for j in range(nj):
            s = s_scr[j] * row(w_src, t, j) + sa * row(b_scr, t, j) + v * row(k_src, t, j)
            s_scr[j] = s
            yj = s * row(r_src, t, j)
            sj = s * row(a_scr, tn, j)
            q = j % n_acc
            y[q] = yj if y[q] is None else y[q] + yj
            san[q] = sj if san[q] is None else san[q] + sj
        y_ref[t] = tree(y)
        return tree(san)

    lax.fori_loop(0, tc, step, sa0)

    def isum(x):
        s = jnp.broadcast_to(jnp.sum(x, axis=0, keepdims=True), (8, LANES))
        if isplit:
            s = s + pltpu.roll(s, LANES // 2, 1)
        return s[0:1]

    def post(t, carry):
        y = y_ref[t]
        v = v_src[t]
        mu = isum(y) * (1.0 / HEAD)
        d = y - mu
        var = isum(d * d) * (1.0 / HEAD)
        yn = d * lax.rsqrt(var + LNX_EPS) * lw_ref[...] + lb_ref[...]
        bonus = jnp.sum(r_src[t] * k_src[t] * rk_ref[...], axis=0, keepdims=True)
        y_ref[t] = yn + bonus * v
        return carry

    lax.fori_loop(0, tc, post, 0, unroll=unroll)

    @pl.when(pl.program_id(1) == pl.num_programs(1) - 1)
    def _():
        sf_ref[...] = s_scr[...]


def _wkv(r, w, kraw, al, v, s0, kkc, kac, rk, lw, lb, tc, isplit, unroll):
    t, nj, jl = r.shape
    ni, lanes = s0.shape[1:]
    assert jl == (LANES // 2 if isplit else lanes) and v.shape == r.shape
    jspec = pl.BlockSpec((tc, nj, min(jl, LANES)), lambda l, c: (c, 0, l))
    dup_scr = []
    if isplit:
        dup_scr = [pltpu.VMEM((tc, nj, LANES), F32)] * 2 + [pltpu.VMEM((tc, ni, LANES), F32)]
    ispec = pl.BlockSpec((tc, ni, LANES), lambda l, c: (c, 0, l))
    sspec = pl.BlockSpec((nj, ni, LANES), lambda l, c: (0, 0, l))
    consts = [kkc, kac, rk, lw, lb]
    return pl.pallas_call(
        functools.partial(_wkv_kernel, ni=ni, nj=nj, tc=tc, isplit=isplit, unroll=unroll),
        grid=(lanes // LANES, t // tc),
        in_specs=[jspec, jspec, jspec, jspec, jspec, sspec] + [_const_spec(c.shape) for c in consts],
        out_specs=[ispec, sspec],
        out_shape=[jax.ShapeDtypeStruct((t, ni, lanes), F32),
                   jax.ShapeDtypeStruct((nj, ni, lanes), F32)],
        scratch_shapes=[pltpu.VMEM((nj, ni, LANES), F32),
                        pltpu.VMEM((tc, nj, LANES), F32),
                        pltpu.VMEM((tc, nj, LANES), F32),
                        pltpu.VMEM((tc, nj, LANES), F32)] + dup_scr,
        compiler_params=_cparams(("parallel", "arbitrary")),
        name="wkv_seq" if isplit else "wkv_step",
    )(r, w, kraw, al, v, s0, *consts)


def _s5_kernel(*refs, sequential, n_blk, pitch, unroll):
    if sequential:
        (u_ref, bre_ref, bim_ref, cre_ref, cim_ref, d_ref, are_ref, aim_ref, wglu_ref, bglu_ref,
         _yb_all, yb_ref, hre_out, him_out, st_re, st_im, c_re, c_im) = refs
    else:
        (u_ref, h0re_ref, h0im_ref, bre_ref, bim_ref, cre_ref, cim_ref, d_ref, are_ref, aim_ref,
         wglu_ref, bglu_ref, _yb_all, yb_ref, hre_out, him_out, st_re, st_im) = refs

    tm = u_ref.shape[0]
    kin = bre_ref.shape[1]
    kst = bre_ref.shape[2]
    tiles_per_blk = kst // LANES
    n_tiles = n_blk * tiles_per_blk
    u = u_ref[...]
    ub = u.astype(BF16)

    def tile_rows(k):
        return slice(k * pitch, k * pitch + tm)

    for kb in range(n_blk):
        ukb = ub[:, kb * kin:(kb + 1) * kin]
        bu_re = jnp.dot(ukb, bre_ref[kb], preferred_element_type=F32)
        bu_im = jnp.dot(ukb, bim_ref[kb], preferred_element_type=F32)
        for n in range(tiles_per_blk):
            k = kb * tiles_per_blk + n
            cols = slice(n * LANES, (n + 1) * LANES)
            if sequential:
                st_re[tile_rows(k), :] = bu_re[:, cols]
                st_im[tile_rows(k), :] = bu_im[:, cols]
            else:
                kc = slice(k * LANES, (k + 1) * LANES)
                ar, ai = are_ref[k:k + 1, :], aim_ref[k:k + 1, :]
                h0r, h0i = h0re_ref[:, kc], h0im_ref[:, kc]
                nr = bu_re[:, cols] + (ar * h0r - ai * h0i)
                ni = bu_im[:, cols] + (ar * h0i + ai * h0r)
                st_re[tile_rows(k), :] = nr
                st_im[tile_rows(k), :] = ni
                hre_out[:, kc] = nr
                him_out[:, kc] = ni

    if sequential:
        @pl.when(pl.program_id(1) == 0)
        def _():
            c_re[...] = jnp.zeros_like(c_re)
            c_im[...] = jnp.zeros_like(c_im)

        ar = are_ref[...]
        ai = aim_ref[...]
        groups = range(n_tiles // SUBLANES)

        def token_rows(t, m):
            return pl.ds(t + m * SUBLANES * pitch, SUBLANES, stride=pitch)

        def step(t, h):
            hr, hi = h
            bur = jnp.concatenate([st_re[token_rows(t, m), :] for m in groups], axis=0)
            bui = jnp.concatenate([st_im[token_rows(t, m), :] for m in groups], axis=0)
            nr = ar * hr - ai * hi + bur
            ni = ar * hi + ai * hr + bui
            for m in groups:
                st_re[token_rows(t, m), :] = nr[m * SUBLANES:(m + 1) * SUBLANES]
                st_im[token_rows(t, m), :] = ni[m * SUBLANES:(m + 1) * SUBLANES]
            return nr, ni

        hr, hi = lax.fori_loop(0, tm, step, (c_re[...], c_im[...]), unroll=unroll)
        c_re[...] = hr
        c_im[...] = hi
        hre_out[0] = hr
        him_out[0] = hi

    ys = []
    for kb in range(n_blk):
        tiles = range(kb * tiles_per_blk, (kb + 1) * tiles_per_blk)
        h_re = jnp.concatenate([st_re[tile_rows(k), :] for k in tiles], axis=1)
        h_im = jnp.concatenate([st_im[tile_rows(k), :] for k in tiles], axis=1)
        yre = jnp.dot(h_re.astype(BF16), cre_ref[kb], preferred_element_type=F32)
        yim = jnp.dot(h_im.astype(BF16), cim_ref[kb], preferred_element_type=F32)
        ys.append(yre - yim)
    y = jnp.concatenate(ys, axis=1) + d_ref[...] * u
    y = jax.nn.gelu(y)
    gate = jnp.dot(y.astype(BF16), wglu_ref[...], preferred_element_type=F32) + bglu_ref[...]
    yb_ref[...] = (y * jax.nn.sigmoid(gate)).astype(BF16)


def _s5(proj, row_block0, n_rows, tm, seqs, u_col_block, width, consts, yb_all, h0=None):
    sequential = h0 is None
    bre = consts[0]
    n_blk, _, kst = bre.shape
    n_state = n_blk * kst
    n_tiles = n_state // LANES
    assert tm % SUBLANES == 0 and n_tiles % SUBLANES == 0
    pitch = tm
    per_seq = n_rows // seqs // tm
    u_map = lambda b, c: (row_block0 + b * per_seq + c, u_col_block)
    out_map = lambda b, c: (b * per_seq + c, 0)
    row_map = lambda b, c: (row_block0 + b * per_seq + c, 0)
    in_specs = [pl.BlockSpec((tm, width), u_map)]
    args = [proj]
    if not sequential:
        in_specs += [pl.BlockSpec((tm, n_state), out_map)] * 2
        args += list(h0)
    in_specs += [_const_spec(c.shape) for c in consts] + [pl.BlockSpec(memory_space=pl.ANY)]
    args += list(consts) + [yb_all]
    scratch = [pltpu.VMEM((n_tiles * pitch, LANES), F32), pltpu.VMEM((n_tiles * pitch, LANES), F32)]
    if sequential:
        scratch += [pltpu.VMEM((n_tiles, LANES), F32), pltpu.VMEM((n_tiles, LANES), F32)]
        st_spec = pl.BlockSpec((1, n_tiles, LANES), lambda b, c: (b, 0, 0))
        st_shape = jax.ShapeDtypeStruct((seqs, n_tiles, LANES), F32)
    else:
        st_spec = pl.BlockSpec((tm, n_state), out_map)
        st_shape = jax.ShapeDtypeStruct((n_rows, n_state), F32)
    return pl.pallas_call(
        functools.partial(_s5_kernel, sequential=sequential, n_blk=n_blk, pitch=pitch, unroll=4),
        grid=(seqs, per_seq),
        in_specs=in_specs,
        out_specs=[pl.BlockSpec((tm, width), row_map), st_spec, st_spec],
        out_shape=[jax.ShapeDtypeStruct(yb_all.shape, yb_all.dtype), st_shape, st_shape],
        input_output_aliases={len(args) - 1: 0},
        scratch_shapes=scratch,
        compiler_params=_cparams(("parallel", "arbitrary")),
        name="s5_seq" if sequential else "s5_step",
    )(*args)


def _route(logits, n_grp):
    lane = lax.broadcasted_iota(jnp.int32, logits.shape, 1).astype(F32)
    neg = jnp.float32(-1e30)
    big = jnp.float32(1e9)
    is_grp = lane < n_grp
    gl = jnp.where(is_grp, logits, neg)
    gmax = jnp.max(gl, axis=1, keepdims=True)
    gsum = jnp.sum(jnp.where(is_grp, jnp.exp(gl - gmax), 0.0), axis=1, keepdims=True)
    g_p = 1.0 / gsum
    g_idx = jnp.min(jnp.where(is_grp & (gl == gmax), lane, big), axis=1, keepdims=True)
    lo = n_grp + g_idx * EXPERTS_PER_GROUP
    in_grp = (lane >= lo) & (lane < lo + EXPERTS_PER_GROUP)
    el = jnp.where(in_grp, logits, neg)
    v1 = jnp.max(el, axis=1, keepdims=True)
    i1 = jnp.min(jnp.where(in_grp & (el == v1), lane, big), axis=1, keepdims=True)
    rest = in_grp & (lane != i1)
    el2 = jnp.where(rest, logits, neg)
    v2 = jnp.max(el2, axis=1, keepdims=True)
    i2 = jnp.min(jnp.where(rest & (el2 == v2), lane, big), axis=1, keepdims=True)
    e2 = jnp.exp(v2 - v1)
    w1 = g_p / (1.0 + e2)
    w2 = g_p * e2 / (1.0 + e2)
    return (jnp.where(lane == ROUTE_E1, i1 - n_grp, 0.0) + jnp.where(lane == ROUTE_E2, i2 - n_grp, 0.0)
            + jnp.where(lane == ROUTE_W1, w1, 0.0) + jnp.where(lane == ROUTE_W2, w2, 0.0))


def _merge_kernel(ya_ref, g_ref, yb_ref, ga_ref, gb_ref, x_ref, wa_ref, wb_ref, wo_ref,
                  gf_ref, wr_ref, br_ref, h_out, xn_out, comb_out, *, n_grp):
    ya = (ya_ref[...] * g_ref[...]).astype(BF16)
    ma = jnp.dot(ya, wa_ref[...], preferred_element_type=F32)
    mb = jnp.dot(yb_ref[...], wb_ref[...], preferred_element_type=F32)
    merged = jax.nn.sigmoid(ga_ref[...]) * ma + jax.nn.sigmoid(gb_ref[...]) * mb
    h = x_ref[...] + jnp.dot(merged.astype(BF16), wo_ref[...], preferred_element_type=F32)
    h_out[...] = h
    ms = jnp.mean(h * h, axis=-1, keepdims=True)
    xn = h * lax.rsqrt(ms + NORM_EPS) * gf_ref[...]
    xn_out[...] = xn
    x_hi = xn.astype(BF16)
    x_lo = (xn - x_hi.astype(F32)).astype(BF16)
    w_hi, w_lo = wr_ref[0], wr_ref[1]
    logits = (jnp.dot(x_hi, w_hi, preferred_element_type=F32)
              + (jnp.dot(x_hi, w_lo, preferred_element_type=F32)
                 + jnp.dot(x_lo, w_hi, preferred_element_type=F32))) + br_ref[...]
    comb_out[...] = _route(logits, n_grp)


def _merge(ya, g, yb, proj, x_all, w_br_a, w_br_b, w_out, g_ffn, w_router, b_router,
           tm, ga_col_block, n_grp):
    nt, d = x_all.shape
    wdt = ya.shape[1]
    row = lambda i: (i, 0)
    single = dict(pipeline_mode=pl.Buffered(1))
    in_specs = [pl.BlockSpec((tm, wdt), row), pl.BlockSpec((tm, wdt), row),
                pl.BlockSpec((tm, wdt), row),
                pl.BlockSpec((tm, d), lambda i: (i, ga_col_block)),
                pl.BlockSpec((tm, d), lambda i: (i, ga_col_block + 1)),
                pl.BlockSpec((tm, d), row),
                pl.BlockSpec(w_br_a.shape, lambda i: (0, 0), **single),
                pl.BlockSpec(w_br_b.shape, lambda i: (0, 0), **single),
                pl.BlockSpec(w_out.shape, lambda i: (0, 0), **single),
                _const_spec(g_ffn.shape), _const_spec(w_router.shape), _const_spec(b_router.shape)]
    return pl.pallas_call(
        functools.partial(_merge_kernel, n_grp=n_grp),
        grid=(nt // tm,),
        in_specs=in_specs,
        out_specs=[pl.BlockSpec((tm, d), row), pl.BlockSpec((tm, d), row),
                   pl.BlockSpec((tm, LANES), row)],
        out_shape=[jax.ShapeDtypeStruct((nt, d), F32), jax.ShapeDtypeStruct((nt, d), F32),
                   jax.ShapeDtypeStruct((nt, LANES), F32)],
        compiler_params=_cparams(("parallel",)),
        name="merge_route",
    )(ya, g, yb, proj, proj, x_all, w_br_a, w_br_b, w_out, g_ffn, w_router, b_router)


def _prefix_loops(n_rows, row):
    def block(b, c):
        for u in range(SUBLANES):
            row(b * SUBLANES + u)
        return c

    def single(r, c):
        row(r)
        return c

    n_blocks = lax.shift_right_logical(n_rows, LOG2_SUBLANES)
    lax.fori_loop(0, n_blocks, block, 0)
    lax.fori_loop(n_blocks * SUBLANES, n_rows, single, 0)


def _moe_kernel(te_ref, first_ref, nvalid_ref, wslot_ref, nexte_ref, nused_ref, src_cur, src_nxt,
                xn_hbm, wg_hbm, wu_hbm, wd_hbm, o_ref,
                xbuf, wg_f, wu_f, wd_f, wg_bf, wu_bf, wd_bf, gsem, wsem, *, tm):
    i = pl.program_id(0)
    n_used = nused_ref[0]
    slot = lax.rem(i, 2)

    def prefix_wait(n_rows, copy_of):
        n_full = pl.multiple_of(
            lax.shift_left(lax.shift_right_logical(n_rows, LOG2_SUBLANES), LOG2_SUBLANES), SUBLANES)

        @pl.when(n_full > 0)
        def _():
            copy_of(0, n_full).wait()

        def single(r, c):
            copy_of(r, 1).wait()
            return c
        lax.fori_loop(n_full, n_rows, single, 0)

    def gather_start(src, s, n_rows):
        _prefix_loops(n_rows, lambda r: pltpu.make_async_copy(
            xn_hbm.at[pl.ds(src[0, 0, r], 1)], xbuf.at[s, pl.ds(r, 1)], gsem.at[s]).start())

    def gather_wait(s, n_rows):
        prefix_wait(n_rows, lambda r0, n: pltpu.make_async_copy(
            xn_hbm.at[pl.ds(0, n)], xbuf.at[s, pl.ds(r0, n)], gsem.at[s]))

    def weight_copies(e, s):
        return [pltpu.make_async_copy(hbm.at[e], buf.at[s], wsem.at[s])
                for hbm, buf in ((wg_hbm, wg_f), (wu_hbm, wu_f), (wd_hbm, wd_f))]

    @pl.when(i == 0)
    def _():
        for c in weight_copies(te_ref[0], 0):
            c.start()
        xbuf[...] = jnp.zeros_like(xbuf)
        gather_start(src_cur, 0, nvalid_ref[0])

    @pl.when(i < n_used)
    def _():
        gather_wait(slot, nvalid_ref[i])

    @pl.when(i + 1 < n_used)
    def _():
        gather_start(src_nxt, 1 - slot, nvalid_ref[i + 1])

    @pl.when(i < n_used)
    def _():
        @pl.when(first_ref[i] == 1)
        def _():
            ws = wslot_ref[i]
            for c in weight_copies(te_ref[i], ws):
                c.wait()

            @pl.when(nexte_ref[i] >= 0)
            def _():
                for c in weight_copies(nexte_ref[i], 1 - ws):
                    c.start()

            wg_bf[...] = wg_f[ws].astype(BF16)
            wu_bf[...] = wu_f[ws].astype(BF16)
            wd_bf[...] = wd_f[ws].astype(BF16)

        x = xbuf[slot].astype(BF16)
        xg = jnp.dot(x, wg_bf[...], preferred_element_type=F32)
        xu = jnp.dot(x, wu_bf[...], preferred_element_type=F32)
        hid = (jax.nn.silu(xg) * xu).astype(BF16)
        o_ref[...] = jnp.dot(hid, wd_bf[...], preferred_element_type=F32)

    @pl.when(i >= n_used)
    def _():
        o_ref[...] = jnp.zeros_like(o_ref)


def _moe(xn, tile_expert, tile_first, tile_valid, tile_wslot, tile_nexte, n_used, src_rows,
         wg, wu, wd, tm):
    nt, d = xn.shape
    n_exp, _, de = wg.shape
    n_tiles = src_rows.shape[0]
    smem_cur = pl.BlockSpec((1, 1, tm), lambda i, *_: (i, 0, 0), memory_space=pltpu.SMEM)
    smem_nxt = pl.BlockSpec((1, 1, tm), lambda i, *_: (jnp.minimum(i + 1, n_tiles - 1), 0, 0),
                            memory_space=pltpu.SMEM)
    grid_spec = pltpu.PrefetchScalarGridSpec(
        num_scalar_prefetch=6,
        grid=(n_tiles,),
        in_specs=[smem_cur, smem_nxt] + [pl.BlockSpec(memory_space=pl.ANY)] * 4,
        out_specs=pl.BlockSpec((tm, d), lambda i, *_: (i, 0)),
        scratch_shapes=[pltpu.VMEM((2, tm, d), F32),
                        pltpu.VMEM((2, d, de), F32), pltpu.VMEM((2, d, de), F32),
                        pltpu.VMEM((2, de, d), F32),
                        pltpu.VMEM((d, de), BF16), pltpu.VMEM((d, de), BF16),
                        pltpu.VMEM((de, d), BF16),
                        pltpu.SemaphoreType.DMA((2,)), pltpu.SemaphoreType.DMA((2,))])
    return pl.pallas_call(
        functools.partial(_moe_kernel, tm=tm),
        grid_spec=grid_spec,
        out_shape=jax.ShapeDtypeStruct((n_tiles * tm, d), F32),
        compiler_params=_cparams(("arbitrary",)),
        name="moe_grouped",
    )(tile_expert, tile_first, tile_valid, tile_wslot, tile_nexte, n_used, src_rows, src_rows,
      xn, wg, wu, wd)


def _moe_plan(route, n_exp, tm):
    nt = route.shape[0]
    n_pairs = 2 * nt
    n_tiles = n_pairs // tm + n_exp
    eid = jnp.concatenate([route[:, ROUTE_E1], route[:, ROUTE_E2]]).astype(jnp.int32)
    onehot = (eid[:, None] == jnp.arange(n_exp, dtype=jnp.int32)[None, :]).astype(jnp.int32)
    csum = jnp.cumsum(onehot, axis=0)
    rank = jnp.take_along_axis(csum, eid[:, None], axis=1)[:, 0] - 1
    cnt = csum[-1]
    tiles = (cnt + tm - 1) // tm
    tile_end = jnp.cumsum(tiles)
    n_used = tile_end[-1]
    pos = (tile_end - tiles)[eid] * tm + rank
    pair = jnp.arange(n_pairs, dtype=jnp.int32)
    src_rows = jnp.zeros((n_tiles * tm,), jnp.int32).at[pos].set(
        pair, unique_indices=True, mode="promise_in_bounds") % nt
    tile_id = jnp.minimum(jnp.arange(n_tiles, dtype=jnp.int32), n_used - 1)
    tile_expert = jnp.sum((tile_end[None, :] <= tile_id[:, None]).astype(jnp.int32), axis=1)
    tile_first = jnp.concatenate(
        [jnp.ones((1,), jnp.int32), (tile_expert[1:] != tile_expert[:-1]).astype(jnp.int32)])
    tile_valid = jnp.clip(cnt[tile_expert] - (tile_id - (tile_end - tiles)[tile_expert]) * tm, 0, tm)
    tile_wslot = (jnp.cumsum(tile_first) - 1) % 2
    nxt_tile = tile_end[tile_expert]
    nxt_expert = jnp.sum((tile_end[None, :] <= nxt_tile[:, None]).astype(jnp.int32), axis=1)
    tile_nexte = jnp.where(nxt_tile < n_used, nxt_expert, -1)
    return (tile_expert, tile_first, tile_valid.astype(jnp.int32), tile_wslot.astype(jnp.int32),
            tile_nexte.astype(jnp.int32), n_used.reshape(1).astype(jnp.int32),
            src_rows.reshape(n_tiles, 1, tm), pos[:nt], pos[nt:])


def _final_kernel(p1_cur, p1_nxt, p2_cur, p2_nxt, h_ref, route_ref, g_ref, y_hbm, o_ref,
                  ybuf, sem, *, tm):
    i = pl.program_id(0)
    slot = lax.rem(i, 2)

    def gather_start(p1, p2, s):
        def row(r):
            pltpu.make_async_copy(y_hbm.at[pl.ds(p1[0, 0, r], 1)], ybuf.at[s, 0, pl.ds(r, 1)],
                                  sem.at[s]).start(priority=0)
            pltpu.make_async_copy(y_hbm.at[pl.ds(p2[0, 0, r], 1)], ybuf.at[s, 1, pl.ds(r, 1)],
                                  sem.at[s]).start(priority=1)
        _prefix_loops(tm, row)

    def gather_wait(s):
        for e in range(2):
            pltpu.make_async_copy(y_hbm.at[pl.ds(0, tm)], ybuf.at[s, e], sem.at[s]).wait()

    @pl.when(i == 0)
    def _():
        gather_start(p1_cur, p2_cur, 0)

    gather_wait(slot)

    @pl.when(i + 1 < pl.num_programs(0))
    def _():
        gather_start(p1_nxt, p2_nxt, 1 - slot)

    lane = lax.broadcasted_iota(jnp.int32, route_ref.shape, 1)
    route = route_ref[...]
    w1 = jnp.sum(jnp.where(lane == ROUTE_W1, route, 0.0), axis=1, keepdims=True)
    w2 = jnp.sum(jnp.where(lane == ROUTE_W2, route, 0.0), axis=1, keepdims=True)
    h = h_ref[...] + (w1 * ybuf[slot, 0] + w2 * ybuf[slot, 1])
    ms = jnp.mean(h * h, axis=-1, keepdims=True)
    o_ref[...] = h * lax.rsqrt(ms + NORM_EPS) * g_ref[...]


def _final(h, y_sorted, pos1, pos2, route, g_final, row0, seg_stride, seg_rows, n_seg, tm):
    nt, d = h.shape
    per_seg = seg_rows // tm
    n_steps = n_seg * per_seg
    assert all(x % SUBLANES == 0 for x in (row0, seg_stride, tm, nt))

    def off(i):
        return pl.multiple_of(row0 + (i // per_seg) * seg_stride + (i % per_seg) * tm, SUBLANES)

    rows = lambda width: pl.BlockSpec((pl.Element(tm), pl.Element(width)), lambda i: (off(i), 0))
    smem = lambda nxt: pl.BlockSpec(
        (1, 1, tm), lambda i: (jnp.minimum(i + nxt, n_steps - 1), 0, 0), memory_space=pltpu.SMEM)
    sel = lambda p: jnp.concatenate(
        [p[row0 + s * seg_stride:row0 + s * seg_stride + seg_rows] for s in range(n_seg)]
    ).reshape(n_steps, 1, tm)
    p1, p2 = sel(pos1), sel(pos2)
    return pl.pallas_call(
        functools.partial(_final_kernel, tm=tm),
        grid=(n_steps,),
        in_specs=[smem(0), smem(1), smem(0), smem(1), rows(d), rows(LANES),
                  pl.BlockSpec((1, d), lambda i: (0, 0)), pl.BlockSpec(memory_space=pl.ANY)],
        out_specs=pl.BlockSpec((tm, d), lambda i: (i, 0)),
        out_shape=jax.ShapeDtypeStruct((n_seg * seg_rows, d), F32),
        scratch_shapes=[pltpu.VMEM((2, 2, tm, d), F32), pltpu.SemaphoreType.DMA((2,))],
        compiler_params=_cparams(("arbitrary",)),
        name="final_norm",
    )(p1, p1, p2, p2, h, route, g_final, y_sorted)


def _pad_cols(w, to):
    return jnp.pad(w, ((0, 0), (0, to - w.shape[1])))


def _pad_rows(w, to):
    return jnp.pad(w, ((0, to - w.shape[0]), (0, 0)))


def _tile_plan(n_t, t_p, seq):
    plan = dict(
        inproj_rows=n_t // 6,
        inproj_cols=4 * LANES,
        seq_rows=t_p // 6,
        wkv_chunk=48, wkv_unroll=8,
        merge_rows=2 * LANES, moe_rows=2 * LANES, final_rows=4 * LANES)
    assert n_t % plan["inproj_rows"] == 0 and plan["inproj_rows"] % SUBLANES == 0
    assert t_p % plan["seq_rows"] == 0 and plan["seq_rows"] % SUBLANES == 0
    assert t_p % plan["wkv_chunk"] == 0 and plan["wkv_chunk"] % plan["wkv_unroll"] == 0
    assert n_t % plan["merge_rows"] == 0 and seq % plan["final_rows"] == 0
    return plan


def _s5_consts(lam_re, lam_im, log_dt, b_re, b_im, c_re, c_im, d, w_glu, b_glu):
    dt = jnp.exp(log_dt)[:, None]
    mag = jnp.exp(lam_re * dt)
    abar_re = mag * jnp.cos(lam_im * dt)
    abar_im = mag * jnp.sin(lam_im * dt)
    den = lam_re * lam_re + lam_im * lam_im
    nr = abar_re - 1.0
    coef_re = (nr * lam_re + abar_im * lam_im) / den
    coef_im = (abar_im * lam_re - nr * lam_im) / den
    bbar_re = coef_re[..., None] * b_re - coef_im[..., None] * b_im
    bbar_im = coef_re[..., None] * b_im + coef_im[..., None] * b_re
    n_g, n_p, n_c = b_re.shape
    gpb = LANES // n_c
    eye = jnp.eye(gpb, dtype=F32)

    def in_blk(bb):
        bb = bb.reshape(n_g // gpb, gpb, n_p, n_c)
        return jnp.einsum('kgpc,gh->kgchp', bb, eye).reshape(
            n_g // gpb, gpb * n_c, gpb * n_p).astype(BF16)

    def out_blk(cc):
        cc = cc.reshape(n_g // gpb, gpb, n_c, n_p)
        return jnp.einsum('kgcp,gh->khpgc', cc, eye).reshape(
            n_g // gpb, gpb * n_p, gpb * n_c).astype(BF16)

    return (in_blk(bbar_re), in_blk(bbar_im), out_blk(c_re), out_blk(c_im),
            d.reshape(1, -1), abar_re.reshape(-1, LANES), abar_im.reshape(-1, LANES),
            w_glu.astype(BF16), b_glu.reshape(1, -1))


def kernel(x_prompt, x_sample, state_shift, state_wkv, state_ssm_re, state_ssm_im, meta_tokens, g_mix, w_in, shift_mu, w0, w_decay_up, a0, w_aaa_up, w_gate_up, k_k, k_a, r_k, lnx_w, lnx_b, ssm_lam_re, ssm_lam_im, ssm_log_dt, ssm_b_re, ssm_b_im, ssm_c_re, ssm_c_im, ssm_d, w_glu, b_glu, w_br_a, w_br_b, w_out, g_ffn, w_router_grp, b_router_grp, w_router_exp, b_router_exp, w_exp_gate, w_exp_up, w_exp_down, g_final):
    depth = g_mix.shape[0]
    assert depth == 1, "single-layer trunk"
    bsz, seq, d = x_prompt.shape
    nb = x_sample.shape[0]
    assert x_sample.shape[1] == 1
    t_p = seq + N_META
    n_p = bsz * t_p
    width = k_k.shape[1]
    heads = width // HEAD
    n_dl, n_al, n_gl = w_decay_up.shape[1], w_aaa_up.shape[1], w_gate_up.shape[1]
    n_grp = w_router_grp.shape[2]
    n_exp = w_router_exp.shape[2]
    assert bsz * heads * 2 == LANES and (nb * heads) % LANES == 0

    s_row0 = -(-n_p // nb) * nb
    n_t = s_row0 + nb
    pieces = []
    for b in range(bsz):
        pieces += [meta_tokens, x_prompt[b]]
    x_all = jnp.concatenate(
        pieces + [jnp.zeros((s_row0 - n_p, d), F32), x_sample.reshape(nb, d)], axis=0)

    c_rkv = 3 * width
    c_xw, c_xa, c_xg = c_rkv, c_rkv + n_dl, c_rkv + n_dl + n_al
    c_u = c_xg + n_gl
    lo_width = 2 * LORA_PAD + n_gl

    def regroup(m):
        return jnp.concatenate(
            [m[:, :c_rkv], m[:, c_u:], _pad_cols(m[:, c_xw:c_xa], LORA_PAD),
             _pad_cols(m[:, c_xa:c_xg], LORA_PAD), m[:, c_xg:c_u]], axis=1)

    def ungroup(m):
        lo = c_rkv + width + 2 * d
        return jnp.concatenate(
            [m[:, :c_rkv], m[:, lo:lo + n_dl], m[:, lo + LORA_PAD:lo + LORA_PAD + n_al],
             m[:, lo + 2 * LORA_PAD:]], axis=1)

    w_in_p = regroup(w_in[0].astype(BF16))
    n_proj = w_in_p.shape[1]
    tiles = _tile_plan(n_t, t_p, seq)
    proj = _inproj(x_all, g_mix, w_in_p, tm=tiles["inproj_rows"], tn=tiles["inproj_cols"])

    mu_all = regroup(_pad_cols(shift_mu, c_u + width + 2 * d))
    mu_rkv, mu_lo = mu_all[:, :c_rkv], mu_all[:, n_proj - lo_width:]
    st_all = regroup(_pad_cols(state_shift[0], c_u + width + 2 * d))
    prev_s = (st_all[:, :c_rkv], st_all[:, n_proj - lo_width:])
    lo_col_block = (n_proj - lo_width) // lo_width
    prep_w = (mu_rkv, mu_lo, w0, a0,
              _pad_rows(w_decay_up[0], LORA_PAD).astype(BF16),
              _pad_rows(w_aaa_up[0], LORA_PAD).astype(BF16), w_gate_up[0].astype(BF16))

    tm_seq = tiles["seq_rows"]
    g_all = jnp.zeros((n_t, width), F32)
    rp, wp, kp, vp, ap, g_all = _rwkv_prep(proj, 0, n_p, tm_seq, bsz, width, lo_col_block,
                                                lo_width, *prep_w, g_all)
    rs, ws, ks, vs, as_, g_all = _rwkv_prep(proj, s_row0 // nb, nb, nb, 1, width,
                                                 lo_col_block, lo_width, *prep_w, g_all,
                                                 prev=prev_s)

    s5c = _s5_consts(ssm_lam_re[0], ssm_lam_im[0], ssm_log_dt[0], ssm_b_re[0], ssm_b_im[0],
                     ssm_c_re[0], ssm_c_im[0], ssm_d[0], w_glu[0], b_glu)
    n_g, n_st = ssm_lam_re.shape[1], ssm_lam_re.shape[2]
    u_col_block = c_rkv // width
    yb_all = jnp.zeros((n_t, width), BF16)
    yb_all, re_p, im_p = _s5(proj, 0, n_p, tm_seq, bsz, u_col_block, width, s5c, yb_all)
    h0 = (state_ssm_re[0].reshape(nb, n_g * n_st), state_ssm_im[0].reshape(nb, n_g * n_st))
    yb_all, re_s, im_s = _s5(proj, s_row0 // nb, nb, nb, 1, u_col_block, width, s5c, yb_all,
                             h0=h0)

    half = HEAD // 2

    def to_lanes_p(z):
        return z.reshape(t_p, bsz * heads, HEAD).transpose(0, 2, 1)

    def to_lanes_s(z):
        return z.reshape(nb, heads, HEAD).transpose(2, 0, 1).reshape(1, HEAD, nb * heads)

    def head_const(c):
        return jnp.tile(c.reshape(heads, HEAD).T, (1, LANES // heads))

    rk_l = head_const(r_k[0])
    kk_l, ka_l = head_const(k_k[0]), head_const(k_a[0])
    lw_p = jnp.broadcast_to(lnx_w[0].reshape(heads, 2, half).transpose(2, 1, 0)[:, :, None, :],
                            (half, 2, bsz, heads)).reshape(half, LANES)
    lb_p = jnp.broadcast_to(lnx_b[0].reshape(heads, 2, half).transpose(2, 1, 0)[:, :, None, :],
                            (half, 2, bsz, heads)).reshape(half, LANES)
    lw_s = jnp.tile(lnx_w[0].reshape(heads, HEAD).T, (1, LANES // heads))
    lb_s = jnp.tile(lnx_b[0].reshape(heads, HEAD).T, (1, LANES // heads))

    s0_p = jnp.zeros((HEAD, half, LANES), F32)
    y_p, sf_p = _wkv(to_lanes_p(rp), to_lanes_p(wp), to_lanes_p(kp), to_lanes_p(ap),
                     to_lanes_p(vp), s0_p, kk_l, ka_l, rk_l, lw_p, lb_p,
                     tc=tiles["wkv_chunk"], isplit=True, unroll=tiles["wkv_unroll"])
    s0_s = state_wkv[0].transpose(3, 2, 0, 1).reshape(HEAD, HEAD, nb * heads)
    y_s, sf_s = _wkv(to_lanes_s(rs), to_lanes_s(ws), to_lanes_s(ks), to_lanes_s(as_),
                     to_lanes_s(vs), s0_s, kk_l, ka_l, rk_l, lw_s, lb_s,
                     tc=1, isplit=False, unroll=1)

    ya_p = y_p.reshape(t_p, half, 2, bsz, heads).transpose(3, 0, 4, 2, 1).reshape(n_p, width)
    ya_s = y_s.reshape(HEAD, nb, heads).transpose(1, 2, 0).reshape(nb, width)
    wkv_p = sf_p.reshape(HEAD, half, 2, bsz, heads).transpose(3, 4, 2, 1, 0).reshape(
        1, bsz, heads, HEAD, HEAD)
    wkv_s = sf_s.reshape(HEAD, HEAD, nb, heads).transpose(2, 3, 1, 0)[None]

    def unify(a, b):
        return jnp.concatenate([a, jnp.zeros((s_row0 - n_p, a.shape[1]), a.dtype), b], axis=0)

    w_router = _pad_cols(jnp.concatenate([w_router_grp[0], w_router_exp[0]], axis=1), LANES)
    w_router_hi = w_router.astype(BF16)
    w_router = jnp.stack([w_router_hi, (w_router - w_router_hi.astype(F32)).astype(BF16)])
    b_router = _pad_cols(jnp.concatenate([b_router_grp, b_router_exp], axis=1), LANES)
    assert n_grp + n_exp <= LANES and n_exp == n_grp * EXPERTS_PER_GROUP
    h1, xn2, route = _merge(unify(ya_p, ya_s), g_all, yb_all, proj, x_all,
                           w_br_a[0].astype(BF16), w_br_b[0].astype(BF16), w_out[0].astype(BF16),
                           g_ffn, w_router, b_router, tm=tiles["merge_rows"],
                           ga_col_block=(c_rkv + width) // d,
                           n_grp=n_grp)
    tm_moe = tiles["moe_rows"]
    te, tf, tv, tws, tne, nu, src_rows, pos1, pos2 = _moe_plan(route, n_exp, tm_moe)
    y_moe = _moe(xn2, te, tf, tv, tws, tne, nu, src_rows, w_exp_gate[0], w_exp_up[0],
                 w_exp_down[0], tm_moe)
    gfin = g_final.reshape(1, d)
    y_prompt = _final(h1, y_moe, pos1, pos2, route, gfin, N_META, t_p, seq, bsz,
                      tm=tiles["final_rows"]).reshape(bsz, seq, d)
    y_sample = _final(h1, y_moe, pos1, pos2, route, gfin, s_row0, 0, nb, 1,
                      tm=nb).reshape(nb, 1, d)
    last_p = jnp.concatenate([proj[(b + 1) * t_p - 1:(b + 1) * t_p] for b in range(bsz)], axis=0)
    shift_p = ungroup(last_p)[None]
    shift_s = ungroup(proj[s_row0:])[None]
    return (y_prompt, y_sample, shift_p, wkv_p,
            re_p.reshape(1, bsz, n_g, n_st), im_p.reshape(1, bsz, n_g, n_st),
            shift_s, wkv_s,
            re_s.reshape(1, nb, n_g, n_st), im_s.reshape(1, nb, n_g, n_st))
```

```python
import functools

import jax
import jax.numpy as jnp
from jax import lax
from jax.experimental import pallas as pl
from jax.experimental.pallas import tpu as pltpu

F32 = jnp.float32
BF16 = jnp.bfloat16

NORM_EPS = 1e-6
LNX_EPS = 64e-5
N_META = 16
HEAD = 64
EXPERTS_PER_GROUP = 8
LANES = 128
SUBLANES = 8
LOG2_SUBLANES = SUBLANES.bit_length() - 1
LORA_PAD = 128
ROUTE_E1, ROUTE_E2, ROUTE_W1, ROUTE_W2 = 0, 1, 2, 3
VMEM_LIMIT = 56 * 1024 * 1024


def _cparams(sem):
    return pltpu.CompilerParams(dimension_semantics=sem, vmem_limit_bytes=VMEM_LIMIT)


def _const_spec(shape):
    nd = len(shape)
    return pl.BlockSpec(shape, lambda *_: (0,) * nd)


def _inproj_kernel(x_ref, g_ref, w_ref, o_ref, xn_ref):
    @pl.when(pl.program_id(1) == 0)
    def _():
        x = x_ref[...]
        ms = jnp.mean(x * x, axis=-1, keepdims=True)
        xn_ref[...] = (x * lax.rsqrt(ms + NORM_EPS) * g_ref[...]).astype(BF16)

    o_ref[...] = jnp.dot(xn_ref[...], w_ref[...], preferred_element_type=F32)


def _inproj(x_all, g_mix, w_in_p, tm, tn):
    nt, d = x_all.shape
    n_out = w_in_p.shape[1]
    return pl.pallas_call(
        _inproj_kernel,
        grid=(nt // tm, n_out // tn),
        in_specs=[pl.BlockSpec((tm, d), lambda i, j: (i, 0)),
                  pl.BlockSpec((1, d), lambda i, j: (0, 0)),
                  pl.BlockSpec((d, tn), lambda i, j: (0, j))],
        out_specs=pl.BlockSpec((tm, tn), lambda i, j: (i, j)),
        out_shape=jax.ShapeDtypeStruct((nt, n_out), F32),
        scratch_shapes=[pltpu.VMEM((tm, d), BF16)],
        compiler_params=_cparams(("parallel", "arbitrary")),
        name="inproj",
    )(x_all, g_mix, w_in_p)


def _softplus(z):
    return jnp.maximum(z, 0.0) + jnp.log1p(jnp.exp(-jnp.abs(z)))


def _rwkv_prep_kernel(*refs, width, carry_prev):
    if carry_prev:
        (rkv_ref, lo_ref, mu_rkv_ref, mu_lo_ref, w0_ref, a0_ref,
         wd_ref, wa_ref, wg_ref, _g_all,
         r_out, w_out, k_out, v_out, a_out, g_out, c_rkv, c_lo) = refs

        @pl.when(pl.program_id(1) == 0)
        def _():
            c_rkv[...] = jnp.zeros_like(c_rkv)
            c_lo[...] = jnp.zeros_like(c_lo)
    else:
        (rkv_ref, lo_ref, prev_rkv_ref, prev_lo_ref, mu_rkv_ref, mu_lo_ref, w0_ref, a0_ref,
         wd_ref, wa_ref, wg_ref, _g_all,
         r_out, w_out, k_out, v_out, a_out, g_out) = refs

    tm = rkv_ref.shape[0]
    first_row = lax.broadcasted_iota(jnp.int32, (tm, 1), 0) == 0

    def shifted(p, prev_ref, carry_ref, cols):
        if carry_prev:
            prev = jnp.where(first_row, carry_ref[:, cols], pltpu.roll(p, 1, 0))
        else:
            prev = prev_ref[:, cols]
        return prev

    def lerp(p, prev, mu):
        return p + (prev - p) * mu

    lo_cols = slice(0, lo_ref.shape[1])
    p_lo = lo_ref[...]
    q_lo = lerp(p_lo, shifted(p_lo, None if carry_prev else prev_lo_ref,
                              c_lo if carry_prev else None, lo_cols), mu_lo_ref[...])
    xw = q_lo[:, 0:LORA_PAD]
    xa = q_lo[:, LORA_PAD:2 * LORA_PAD]
    xg = q_lo[:, 2 * LORA_PAD:]
    dw = jnp.dot(jnp.tanh(xw).astype(BF16), wd_ref[...], preferred_element_type=F32)
    wlog = -_softplus(-(w0_ref[...] + dw)) - 0.5
    w_out[...] = jnp.exp(-jnp.exp(wlog))
    a = jax.nn.sigmoid(a0_ref[...] + jnp.dot(xa.astype(BF16), wa_ref[...],
                                             preferred_element_type=F32))
    a_out[...] = a
    g_out[...] = jnp.dot(jax.nn.sigmoid(xg).astype(BF16), wg_ref[...],
                         preferred_element_type=F32)

    def q_of(idx):
        cols = slice(idx * width, (idx + 1) * width)
        p = rkv_ref[:, cols]
        prev = shifted(p, None if carry_prev else prev_rkv_ref,
                       c_rkv if carry_prev else None, cols)
        return lerp(p, prev, mu_rkv_ref[:, cols])

    r_out[...] = q_of(0)
    k_out[...] = q_of(1)
    v_out[...] = q_of(2)

    if carry_prev:
        c_rkv[...] = rkv_ref[tm - 1:tm, :]
        c_lo[...] = lo_ref[tm - 1:tm, :]


def _rwkv_prep(proj, row_block0, n_rows, tm, seqs, width, lo_col_block, lo_width,
               mu_rkv, mu_lo, w0, a0, wd, wa, wg, g_all, prev=None):
    carry_prev = prev is None
    per_seq = n_rows // seqs // tm
    row_map = lambda b, c: (row_block0 + b * per_seq + c, 0)
    lo_map = lambda b, c: (row_block0 + b * per_seq + c, lo_col_block)
    out_map = lambda b, c: (b * per_seq + c, 0)
    in_specs = [pl.BlockSpec((tm, 3 * width), row_map), pl.BlockSpec((tm, lo_width), lo_map)]
    args = [proj, proj]
    if not carry_prev:
        in_specs += [pl.BlockSpec((tm, 3 * width), out_map), pl.BlockSpec((tm, lo_width), out_map)]
        args += list(prev)
    consts = [mu_rkv, mu_lo, w0, a0, wd, wa, wg]
    in_specs += [_const_spec(c.shape) for c in consts] + [pl.BlockSpec(memory_space=pl.ANY)]
    args += consts + [g_all]
    scratch = []
    if carry_prev:
        scratch = [pltpu.VMEM((1, 3 * width), F32), pltpu.VMEM((1, lo_width), F32)]
    out_sd = jax.ShapeDtypeStruct((n_rows // seqs, seqs * width), F32)
    tmaj_map = lambda b, c: (c, b)
    return pl.pallas_call(
        functools.partial(_rwkv_prep_kernel, width=width, carry_prev=carry_prev),
        grid=(seqs, per_seq),
        in_specs=in_specs,
        out_specs=[pl.BlockSpec((tm, width), tmaj_map)] * 5 + [pl.BlockSpec((tm, width), row_map)],
        out_shape=[out_sd] * 5 + [jax.ShapeDtypeStruct(g_all.shape, g_all.dtype)],
        input_output_aliases={len(args) - 1: 5},
        scratch_shapes=scratch,
        compiler_params=_cparams(("parallel", "arbitrary")),
        name="rwkv_prep_seq" if carry_prev else "rwkv_prep_step",
    )(*args)


def _wkv_kernel(r_ref, w_ref, kraw_ref, al_ref, v_ref, s0_ref, kkc_ref, kac_ref, rk_ref, lw_ref,
                lb_ref, y_ref, sf_ref, s_scr, a_scr, b_scr, k_src, *dup_scr,
                ni, nj, tc, isplit, unroll):
    @pl.when(pl.program_id(1) == 0)
    def _():
        s_scr[...] = s0_ref[...]

    def dup(x):
        return jnp.concatenate([x, x], axis=-1) if isplit else x

    if isplit:
        w_src, r_src, v_src = dup_scr
    else:
        w_src, r_src, v_src = w_ref, r_ref, v_ref

    def prep(t, carry):
        kraw = dup(kraw_ref[t])
        al = dup(al_ref[t])
        kk = kraw * kkc_ref[...]
        k_src[t] = kraw * (1.0 + (al - 1.0) * kac_ref[...])
        ss = jnp.sum(kk * kk, axis=0, keepdims=True)
        kkn = kk / jnp.maximum(jnp.sqrt(ss), 1e-12)
        a_scr[t] = -kkn
        b_scr[t] = kkn * al
        if isplit:
            w_src[t] = dup(w_ref[t])
            r_src[t] = dup(r_ref[t])
            v = v_ref[t]
            v_src[t] = jnp.concatenate([v[:ni], v[ni:]], axis=-1)
        return carry

    lax.fori_loop(0, tc, prep, 0, unroll=unroll)

    def row(ref, t, j):
        return ref[t, j:j + 1, :]

    def tree(parts):
        while len(parts) > 1:
            parts = [parts[i] + parts[i + 1] for i in range(0, len(parts), 2)]
        return parts[0]

    n_acc = 4
    sa0 = tree([sum(s_scr[j] * row(a_scr, 0, j) for j in range(q, nj, n_acc))
                for q in range(n_acc)])

    def step(t, sa):
        v = v_src[t]
        tn = jnp.minimum(t + 1, tc - 1)
        y = [None] * n_acc
        san = [None] * n_acc
        for j in range(nj):
            s = s_scr[j] * row(w_src, t, j) + sa * row(b_scr, t, j) + v * row(k_src, t, j)
            s_scr[j] = s
            yj = s * row(r_src, t, j)
            sj = s * row(a_scr, tn, j)
            q = j % n_acc
            y[q] = yj if y[q] is None else y[q] + yj
            san[q] = sj if san[q] is None else san[q] + sj
        y_ref[t] = tree(y)
        return tree(san)

    lax.fori_loop(0, tc, step, sa0)

    def isum(x):
        s = jnp.broadcast_to(jnp.sum(x, axis=0, keepdims=True), (8, LANES))
        if isplit:
            s = s + pltpu.roll(s, LANES // 2, 1)
        return s[0:1]

    def post(t, carry):
        y = y_ref[t]
        v = v_src[t]
        mu = isum(y) * (1.0 / HEAD)
        d = y - mu
        var = isum(d * d) * (1.0 / HEAD)
        yn = d * lax.rsqrt(var + LNX_EPS) * lw_ref[...] + lb_ref[...]
        bonus = jnp.sum(r_src[t] * k_src[t] * rk_ref[...], axis=0, keepdims=True)
        y_ref[t] = yn + bonus * v
        return carry

    lax.fori_loop(0, tc, post, 0, unroll=unroll)

    @pl.when(pl.program_id(1) == pl.num_programs(1) - 1)
    def _():
        sf_ref[...] = s_scr[...]


def _wkv(r, w, kraw, al, v, s0, kkc, kac, rk, lw, lb, tc, isplit, unroll):
    t, nj, jl = r.shape
    ni, lanes = s0.shape[1:]
    assert jl == (LANES // 2 if isplit else lanes) and v.shape == r.shape
    jspec = pl.BlockSpec((tc, nj, min(jl, LANES)), lambda l, c: (c, 0, l))
    dup_scr = []
    if isplit:
        dup_scr = [pltpu.VMEM((tc, nj, LANES), F32)] * 2 + [pltpu.VMEM((tc, ni, LANES), F32)]
    ispec = pl.BlockSpec((tc, ni, LANES), lambda l, c: (c, 0, l))
    sspec = pl.BlockSpec((nj, ni, LANES), lambda l, c: (0, 0, l))
    consts = [kkc, kac, rk, lw, lb]
    return pl.pallas_call(
        functools.partial(_wkv_kernel, ni=ni, nj=nj, tc=tc, isplit=isplit, unroll=unroll),
        grid=(lanes // LANES, t // tc),
        in_specs=[jspec, jspec, jspec, jspec, jspec, sspec] + [_const_spec(c.shape) for c in consts],
        out_specs=[ispec, sspec],
        out_shape=[jax.ShapeDtypeStruct((t, ni, lanes), F32),
                   jax.ShapeDtypeStruct((nj, ni, lanes), F32)],
        scratch_shapes=[pltpu.VMEM((nj, ni, LANES), F32),
                        pltpu.VMEM((tc, nj, LANES), F32),
                        pltpu.VMEM((tc, nj, LANES), F32),
                        pltpu.VMEM((tc, nj, LANES), F32)] + dup_scr,
        compiler_params=_cparams(("parallel", "arbitrary")),
        name="wkv_seq" if isplit else "wkv_step",
    )(r, w, kraw, al, v, s0, *consts)


def _s5_kernel(*refs, sequential, n_blk, pitch, unroll):
    if sequential:
        (u_ref, bre_ref, bim_ref, cre_ref, cim_ref, d_ref, are_ref, aim_ref, wglu_ref, bglu_ref,
         _yb_all, yb_ref, hre_out, him_out, st_re, st_im, c_re, c_im) = refs
    else:
        (u_ref, h0re_ref, h0im_ref, bre_ref, bim_ref, cre_ref, cim_ref, d_ref, are_ref, aim_ref,
         wglu_ref, bglu_ref, _yb_all, yb_ref, hre_out, him_out, st_re, st_im) = refs

    tm = u_ref.shape[0]
    kin = bre_ref.shape[1]
    kst = bre_ref.shape[2]
    tiles_per_blk = kst // LANES
    n_tiles = n_blk * tiles_per_blk
    u = u_ref[...]
    ub = u.astype(BF16)

    def tile_rows(k):
        return slice(k * pitch, k * pitch + tm)

    for kb in range(n_blk):
        ukb = ub[:, kb * kin:(kb + 1) * kin]
        bu_re = jnp.dot(ukb, bre_ref[kb], preferred_element_type=F32)
        bu_im = jnp.dot(ukb, bim_ref[kb], preferred_element_type=F32)
        for n in range(tiles_per_blk):
            k = kb * tiles_per_blk + n
            cols = slice(n * LANES, (n + 1) * LANES)
            if sequential:
                st_re[tile_rows(k), :] = bu_re[:, cols]
                st_im[tile_rows(k), :] = bu_im[:, cols]
            else:
                kc = slice(k * LANES, (k + 1) * LANES)
                ar, ai = are_ref[k:k + 1, :], aim_ref[k:k + 1, :]
                h0r, h0i = h0re_ref[:, kc], h0im_ref[:, kc]
                nr = bu_re[:, cols] + (ar * h0r - ai * h0i)
                ni = bu_im[:, cols] + (ar * h0i + ai * h0r)
                st_re[tile_rows(k), :] = nr
                st_im[tile_rows(k), :] = ni
                hre_out[:, kc] = nr
                him_out[:, kc] = ni

    if sequential:
        @pl.when(pl.program_id(1) == 0)
        def _():
            c_re[...] = jnp.zeros_like(c_re)
            c_im[...] = jnp.zeros_like(c_im)

        ar = are_ref[...]
        ai = aim_ref[...]
        groups = range(n_tiles // SUBLANES)

        def token_rows(t, m):
            return pl.ds(t + m * SUBLANES * pitch, SUBLANES, stride=pitch)

        def step(t, h):
            hr, hi = h
            bur = jnp.concatenate([st_re[token_rows(t, m), :] for m in groups], axis=0)
            bui = jnp.concatenate([st_im[token_rows(t, m), :] for m in groups], axis=0)
            nr = ar * hr - ai * hi + bur
            ni = ar * hi + ai * hr + bui
            for m in groups:
                st_re[token_rows(t, m), :] = nr[m * SUBLANES:(m + 1) * SUBLANES]
                st_im[token_rows(t, m), :] = ni[m * SUBLANES:(m + 1) * SUBLANES]
            return nr, ni

        hr, hi = lax.fori_loop(0, tm, step, (c_re[...], c_im[...]), unroll=unroll)
        c_re[...] = hr
        c_im[...] = hi
        hre_out[0] = hr
        him_out[0] = hi

    ys = []
    for kb in range(n_blk):
        tiles = range(kb * tiles_per_blk, (kb + 1) * tiles_per_blk)
        h_re = jnp.concatenate([st_re[tile_rows(k), :] for k in tiles], axis=1)
        h_im = jnp.concatenate([st_im[tile_rows(k), :] for k in tiles], axis=1)
        yre = jnp.dot(h_re.astype(BF16), cre_ref[kb], preferred_element_type=F32)
        yim = jnp.dot(h_im.astype(BF16), cim_ref[kb], preferred_element_type=F32)
        ys.append(yre - yim)
    y = jnp.concatenate(ys, axis=1) + d_ref[...] * u
    y = jax.nn.gelu(y)
    gate = jnp.dot(y.astype(BF16), wglu_ref[...], preferred_element_type=F32) + bglu_ref[...]
    yb_ref[...] = (y * jax.nn.sigmoid(gate)).astype(BF16)


def _s5(proj, row_block0, n_rows, tm, seqs, u_col_block, width, consts, yb_all, h0=None):
    sequential = h0 is None
    bre = consts[0]
    n_blk, _, kst = bre.shape
    n_state = n_blk * kst
    n_tiles = n_state // LANES
    assert tm % SUBLANES == 0 and n_tiles % SUBLANES == 0
    pitch = tm
    per_seq = n_rows // seqs // tm
    u_map = lambda b, c: (row_block0 + b * per_seq + c, u_col_block)
    out_map = lambda b, c: (b * per_seq + c, 0)
    row_map = lambda b, c: (row_block0 + b * per_seq + c, 0)
    in_specs = [pl.BlockSpec((tm, width), u_map)]
    args = [proj]
    if not sequential:
        in_specs += [pl.BlockSpec((tm, n_state), out_map)] * 2
        args += list(h0)
    in_specs += [_const_spec(c.shape) for c in consts] + [pl.BlockSpec(memory_space=pl.ANY)]
    args += list(consts) + [yb_all]
    scratch = [pltpu.VMEM((n_tiles * pitch, LANES), F32), pltpu.VMEM((n_tiles * pitch, LANES), F32)]
    if sequential:
        scratch += [pltpu.VMEM((n_tiles, LANES), F32), pltpu.VMEM((n_tiles, LANES), F32)]
        st_spec = pl.BlockSpec((1, n_tiles, LANES), lambda b, c: (b, 0, 0))
        st_shape = jax.ShapeDtypeStruct((seqs, n_tiles, LANES), F32)
    else:
        st_spec = pl.BlockSpec((tm, n_state), out_map)
        st_shape = jax.ShapeDtypeStruct((n_rows, n_state), F32)
    return pl.pallas_call(
        functools.partial(_s5_kernel, sequential=sequential, n_blk=n_blk, pitch=pitch, unroll=8),
        grid=(seqs, per_seq),
        in_specs=in_specs,
        out_specs=[pl.BlockSpec((tm, width), row_map), st_spec, st_spec],
        out_shape=[jax.ShapeDtypeStruct(yb_all.shape, yb_all.dtype), st_shape, st_shape],
        input_output_aliases={len(args) - 1: 0},
        scratch_shapes=scratch,
        compiler_params=_cparams(("parallel", "arbitrary")),
        name="s5_seq" if sequential else "s5_step",
    )(*args)


def _route(logits, n_grp):
    lane = lax.broadcasted_iota(jnp.int32, logits.shape, 1).astype(F32)
    neg = jnp.float32(-1e30)
    big = jnp.float32(1e9)
    is_grp = lane < n_grp
    gl = jnp.where(is_grp, logits, neg)
    gmax = jnp.max(gl, axis=1, keepdims=True)
    gsum = jnp.sum(jnp.where(is_grp, jnp.exp(gl - gmax), 0.0), axis=1, keepdims=True)
    g_p = 1.0 / gsum
    g_idx = jnp.min(jnp.where(is_grp & (gl == gmax), lane, big), axis=1, keepdims=True)
    lo = n_grp + g_idx * EXPERTS_PER_GROUP
    in_grp = (lane >= lo) & (lane < lo + EXPERTS_PER_GROUP)
    el = jnp.where(in_grp, logits, neg)
    v1 = jnp.max(el, axis=1, keepdims=True)
    i1 = jnp.min(jnp.where(in_grp & (el == v1), lane, big), axis=1, keepdims=True)
    rest = in_grp & (lane != i1)
    el2 = jnp.where(rest, logits, neg)
    v2 = jnp.max(el2, axis=1, keepdims=True)
    i2 = jnp.min(jnp.where(rest & (el2 == v2), lane, big), axis=1, keepdims=True)
    e2 = jnp.exp(v2 - v1)
    w1 = g_p / (1.0 + e2)
    w2 = g_p * e2 / (1.0 + e2)
    return (jnp.where(lane == ROUTE_E1, i1 - n_grp, 0.0) + jnp.where(lane == ROUTE_E2, i2 - n_grp, 0.0)
            + jnp.where(lane == ROUTE_W1, w1, 0.0) + jnp.where(lane == ROUTE_W2, w2, 0.0))


def _merge_kernel(ya_ref, g_ref, yb_ref, ga_ref, gb_ref, x_ref, wa_ref, wb_ref, wo_ref,
                  gf_ref, wr_ref, br_ref, h_out, xn_out, comb_out, *, n_grp):
    ya = (ya_ref[...] * g_ref[...]).astype(BF16)
    ma = jnp.dot(ya, wa_ref[...], preferred_element_type=F32)
    mb = jnp.dot(yb_ref[...], wb_ref[...], preferred_element_type=F32)
    merged = jax.nn.sigmoid(ga_ref[...]) * ma + jax.nn.sigmoid(gb_ref[...]) * mb
    h = x_ref[...] + jnp.dot(merged.astype(BF16), wo_ref[...], preferred_element_type=F32)
    h_out[...] = h
    ms = jnp.mean(h * h, axis=-1, keepdims=True)
    xn = h * lax.rsqrt(ms + NORM_EPS) * gf_ref[...]
    xn_out[...] = xn
    x_hi = xn.astype(BF16)
    x_lo = (xn - x_hi.astype(F32)).astype(BF16)
    w_hi, w_lo = wr_ref[0], wr_ref[1]
    logits = (jnp.dot(x_hi, w_hi, preferred_element_type=F32)
              + (jnp.dot(x_hi, w_lo, preferred_element_type=F32)
                 + jnp.dot(x_lo, w_hi, preferred_element_type=F32))) + br_ref[...]
    comb_out[...] = _route(logits, n_grp)


def _merge(ya, g, yb, proj, x_all, w_br_a, w_br_b, w_out, g_ffn, w_router, b_router,
           tm, ga_col_block, n_grp):
    nt, d = x_all.shape
    wdt = ya.shape[1]
    row = lambda i: (i, 0)
    single = dict(pipeline_mode=pl.Buffered(1))
    in_specs = [pl.BlockSpec((tm, wdt), row), pl.BlockSpec((tm, wdt), row),
                pl.BlockSpec((tm, wdt), row),
                pl.BlockSpec((tm, d), lambda i: (i, ga_col_block)),
                pl.BlockSpec((tm, d), lambda i: (i, ga_col_block + 1)),
                pl.BlockSpec((tm, d), row),
                pl.BlockSpec(w_br_a.shape, lambda i: (0, 0), **single),
                pl.BlockSpec(w_br_b.shape, lambda i: (0, 0), **single),
                pl.BlockSpec(w_out.shape, lambda i: (0, 0), **single),
                _const_spec(g_ffn.shape), _const_spec(w_router.shape), _const_spec(b_router.shape)]
    return pl.pallas_call(
        functools.partial(_merge_kernel, n_grp=n_grp),
        grid=(nt // tm,),
        in_specs=in_specs,
        out_specs=[pl.BlockSpec((tm, d), row), pl.BlockSpec((tm, d), row),
                   pl.BlockSpec((tm, LANES), row)],
        out_shape=[jax.ShapeDtypeStruct((nt, d), F32), jax.ShapeDtypeStruct((nt, d), F32),
                   jax.ShapeDtypeStruct((nt, LANES), F32)],
        compiler_params=_cparams(("parallel",)),
        name="merge_route",
    )(ya, g, yb, proj, proj, x_all, w_br_a, w_br_b, w_out, g_ffn, w_router, b_router)


def _prefix_loops(n_rows, row):
    def block(b, c):
        for u in range(SUBLANES):
            row(b * SUBLANES + u)
        return c

    def single(r, c):
        row(r)
        return c

    n_blocks = lax.shift_right_logical(n_rows, LOG2_SUBLANES)
    lax.fori_loop(0, n_blocks, block, 0)
    lax.fori_loop(n_blocks * SUBLANES, n_rows, single, 0)


def _moe_kernel(te_ref, first_ref, nvalid_ref, wslot_ref, nexte_ref, nused_ref, src_cur, src_nxt,
                xn_hbm, wg_hbm, wu_hbm, wd_hbm, o_ref,
                xbuf, wg_f, wu_f, wd_f, wg_bf, wu_bf, wd_bf, gsem, wsem, *, tm):
    i = pl.program_id(0)
    n_used = nused_ref[0]
    slot = lax.rem(i, 2)

    def prefix_wait(n_rows, copy_of):
        n_full = pl.multiple_of(
            lax.shift_left(lax.shift_right_logical(n_rows, LOG2_SUBLANES), LOG2_SUBLANES), SUBLANES)

        @pl.when(n_full > 0)
        def _():
            copy_of(0, n_full).wait()

        def single(r, c):
            copy_of(r, 1).wait()
            return c
        lax.fori_loop(n_full, n_rows, single, 0)

    def gather_start(src, s, n_rows):
        _prefix_loops(n_rows, lambda r: pltpu.make_async_copy(
            xn_hbm.at[pl.ds(src[0, 0, r], 1)], xbuf.at[s, pl.ds(r, 1)], gsem.at[s]).start())

    def gather_wait(s, n_rows):
        prefix_wait(n_rows, lambda r0, n: pltpu.make_async_copy(
            xn_hbm.at[pl.ds(0, n)], xbuf.at[s, pl.ds(r0, n)], gsem.at[s]))

    def weight_copies(e, s):
        return [pltpu.make_async_copy(hbm.at[e], buf.at[s], wsem.at[s])
                for hbm, buf in ((wg_hbm, wg_f), (wu_hbm, wu_f), (wd_hbm, wd_f))]

    @pl.when(i == 0)
    def _():
        for c in weight_copies(te_ref[0], 0):
            c.start()
        xbuf[...] = jnp.zeros_like(xbuf)
        gather_start(src_cur, 0, nvalid_ref[0])

    @pl.when(i < n_used)
    def _():
        gather_wait(slot, nvalid_ref[i])

    @pl.when(i + 1 < n_used)
    def _():
        gather_start(src_nxt, 1 - slot, nvalid_ref[i + 1])

    @pl.when(i < n_used)
    def _():
        @pl.when(first_ref[i] == 1)
        def _():
            ws = wslot_ref[i]
            for c in weight_copies(te_ref[i], ws):
                c.wait()

            @pl.when(nexte_ref[i] >= 0)
            def _():
                for c in weight_copies(nexte_ref[i], 1 - ws):
                    c.start()

            wg_bf[...] = wg_f[ws].astype(BF16)
            wu_bf[...] = wu_f[ws].astype(BF16)
            wd_bf[...] = wd_f[ws].astype(BF16)

        x = xbuf[slot].astype(BF16)
        xg = jnp.dot(x, wg_bf[...], preferred_element_type=F32)
        xu = jnp.dot(x, wu_bf[...], preferred_element_type=F32)
        hid = (jax.nn.silu(xg) * xu).astype(BF16)
        o_ref[...] = jnp.dot(hid, wd_bf[...], preferred_element_type=F32)

    @pl.when(i >= n_used)
    def _():
        o_ref[...] = jnp.zeros_like(o_ref)


def _moe(xn, tile_expert, tile_first, tile_valid, tile_wslot, tile_nexte, n_used, src_rows,
         wg, wu, wd, tm):
    nt, d = xn.shape
    n_exp, _, de = wg.shape
    n_tiles = src_rows.shape[0]
    smem_cur = pl.BlockSpec((1, 1, tm), lambda i, *_: (i, 0, 0), memory_space=pltpu.SMEM)
    smem_nxt = pl.BlockSpec((1, 1, tm), lambda i, *_: (jnp.minimum(i + 1, n_tiles - 1), 0, 0),
                            memory_space=pltpu.SMEM)
    grid_spec = pltpu.PrefetchScalarGridSpec(
        num_scalar_prefetch=6,
        grid=(n_tiles,),
        in_specs=[smem_cur, smem_nxt] + [pl.BlockSpec(memory_space=pl.ANY)] * 4,
        out_specs=pl.BlockSpec((tm, d), lambda i, *_: (i, 0)),
        scratch_shapes=[pltpu.VMEM((2, tm, d), F32),
                        pltpu.VMEM((2, d, de), F32), pltpu.VMEM((2, d, de), F32),
                        pltpu.VMEM((2, de, d), F32),
                        pltpu.VMEM((d, de), BF16), pltpu.VMEM((d, de), BF16),
                        pltpu.VMEM((de, d), BF16),
                        pltpu.SemaphoreType.DMA((2,)), pltpu.SemaphoreType.DMA((2,))])
    return pl.pallas_call(
        functools.partial(_moe_kernel, tm=tm),
        grid_spec=grid_spec,
        out_shape=jax.ShapeDtypeStruct((n_tiles * tm, d), F32),
        compiler_params=_cparams(("arbitrary",)),
        name="moe_grouped",
    )(tile_expert, tile_first, tile_valid, tile_wslot, tile_nexte, n_used, src_rows, src_rows,
      xn, wg, wu, wd)


def _moe_plan(route, n_exp, tm):
    nt = route.shape[0]
    n_pairs = 2 * nt
    n_tiles = n_pairs // tm + n_exp
    eid = jnp.concatenate([route[:, ROUTE_E1], route[:, ROUTE_E2]]).astype(jnp.int32)
    onehot = (eid[:, None] == jnp.arange(n_exp, dtype=jnp.int32)[None, :]).astype(jnp.int32)
    csum = jnp.cumsum(onehot, axis=0)
    rank = jnp.take_along_axis(csum, eid[:, None], axis=1)[:, 0] - 1
    cnt = csum[-1]
    tiles = (cnt + tm - 1) // tm
    tile_end = jnp.cumsum(tiles)
    n_used = tile_end[-1]
    pos = (tile_end - tiles)[eid] * tm + rank
    pair = jnp.arange(n_pairs, dtype=jnp.int32)
    src_rows = jnp.zeros((n_tiles * tm,), jnp.int32).at[pos].set(
        pair, unique_indices=True, mode="promise_in_bounds") % nt
    tile_id = jnp.minimum(jnp.arange(n_tiles, dtype=jnp.int32), n_used - 1)
    tile_expert = jnp.sum((tile_end[None, :] <= tile_id[:, None]).astype(jnp.int32), axis=1)
    tile_first = jnp.concatenate(
        [jnp.ones((1,), jnp.int32), (tile_expert[1:] != tile_expert[:-1]).astype(jnp.int32)])
    tile_valid = jnp.clip(cnt[tile_expert] - (tile_id - (tile_end - tiles)[tile_expert]) * tm, 0, tm)
    tile_wslot = (jnp.cumsum(tile_first) - 1) % 2
    nxt_tile = tile_end[tile_expert]
    nxt_expert = jnp.sum((tile_end[None, :] <= nxt_tile[:, None]).astype(jnp.int32), axis=1)
    tile_nexte = jnp.where(nxt_tile < n_used, nxt_expert, -1)
    return (tile_expert, tile_first, tile_valid.astype(jnp.int32), tile_wslot.astype(jnp.int32),
            tile_nexte.astype(jnp.int32), n_used.reshape(1).astype(jnp.int32),
            src_rows.reshape(n_tiles, 1, tm), pos[:nt], pos[nt:])


def _final_kernel(p1_cur, p1_nxt, p2_cur, p2_nxt, h_ref, route_ref, g_ref, y_hbm, o_ref,
                  ybuf, sem, *, tm):
    i = pl.program_id(0)
    slot = lax.rem(i, 2)

    def gather_start(p1, p2, s):
        def row(r):
            pltpu.make_async_copy(y_hbm.at[pl.ds(p1[0, 0, r], 1)], ybuf.at[s, 0, pl.ds(r, 1)],
                                  sem.at[s]).start()
            pltpu.make_async_copy(y_hbm.at[pl.ds(p2[0, 0, r], 1)], ybuf.at[s, 1, pl.ds(r, 1)],
                                  sem.at[s]).start()
        _prefix_loops(tm, row)

    def gather_wait(s):
        for e in range(2):
            pltpu.make_async_copy(y_hbm.at[pl.ds(0, tm)], ybuf.at[s, e], sem.at[s]).wait()

    @pl.when(i == 0)
    def _():
        gather_start(p1_cur, p2_cur, 0)

    gather_wait(slot)

    @pl.when(i + 1 < pl.num_programs(0))
    def _():
        gather_start(p1_nxt, p2_nxt, 1 - slot)

    lane = lax.broadcasted_iota(jnp.int32, route_ref.shape, 1)
    route = route_ref[...]
    w1 = jnp.sum(jnp.where(lane == ROUTE_W1, route, 0.0), axis=1, keepdims=True)
    w2 = jnp.sum(jnp.where(lane == ROUTE_W2, route, 0.0), axis=1, keepdims=True)
    h = h_ref[...] + (w1 * ybuf[slot, 0] + w2 * ybuf[slot, 1])
    ms = jnp.mean(h * h, axis=-1, keepdims=True)
    o_ref[...] = h * lax.rsqrt(ms + NORM_EPS) * g_ref[...]


def _final(h, y_sorted, pos1, pos2, route, g_final, row0, seg_stride, seg_rows, n_seg, tm):
    nt, d = h.shape
    per_seg = seg_rows // tm
    n_steps = n_seg * per_seg
    assert all(x % SUBLANES == 0 for x in (row0, seg_stride, tm, nt))

    def off(i):
        return pl.multiple_of(row0 + (i // per_seg) * seg_stride + (i % per_seg) * tm, SUBLANES)

    rows = lambda width: pl.BlockSpec((pl.Element(tm), pl.Element(width)), lambda i: (off(i), 0))
    smem = lambda nxt: pl.BlockSpec(
        (1, 1, tm), lambda i: (jnp.minimum(i + nxt, n_steps - 1), 0, 0), memory_space=pltpu.SMEM)
    sel = lambda p: jnp.concatenate(
        [p[row0 + s * seg_stride:row0 + s * seg_stride + seg_rows] for s in range(n_seg)]
    ).reshape(n_steps, 1, tm)
    p1, p2 = sel(pos1), sel(pos2)
    return pl.pallas_call(
        functools.partial(_final_kernel, tm=tm),
        grid=(n_steps,),
        in_specs=[smem(0), smem(1), smem(0), smem(1), rows(d), rows(LANES),
                  pl.BlockSpec((1, d), lambda i: (0, 0)), pl.BlockSpec(memory_space=pl.ANY)],
        out_specs=pl.BlockSpec((tm, d), lambda i: (i, 0)),
        out_shape=jax.ShapeDtypeStruct((n_seg * seg_rows, d), F32),
        scratch_shapes=[pltpu.VMEM((2, 2, tm, d), F32), pltpu.SemaphoreType.DMA((2,))],
        compiler_params=_cparams(("arbitrary",)),
        name="final_norm",
    )(p1, p1, p2, p2, h, route, g_final, y_sorted)


def _pad_cols(w, to):
    return jnp.pad(w, ((0, 0), (0, to - w.shape[1])))


def _pad_rows(w, to):
    return jnp.pad(w, ((0, to - w.shape[0]), (0, 0)))


def _tile_plan(n_t, t_p, seq):
    plan = dict(
        inproj_rows=n_t // 6,
        inproj_cols=4 * LANES,
        seq_rows=t_p // 6,
        wkv_chunk=48, wkv_unroll=8,
        merge_rows=2 * LANES, moe_rows=2 * LANES, final_rows=4 * LANES)
    assert n_t % plan["inproj_rows"] == 0 and plan["inproj_rows"] % SUBLANES == 0
    assert t_p % plan["seq_rows"] == 0 and plan["seq_rows"] % SUBLANES == 0
    assert t_p % plan["wkv_chunk"] == 0 and plan["wkv_chunk"] % plan["wkv_unroll"] == 0
    assert n_t % plan["merge_rows"] == 0 and seq % plan["final_rows"] == 0
    return plan


def _s5_consts(lam_re, lam_im, log_dt, b_re, b_im, c_re, c_im, d, w_glu, b_glu):
    dt = jnp.exp(log_dt)[:, None]
    mag = jnp.exp(lam_re * dt)
    abar_re = mag * jnp.cos(lam_im * dt)
    abar_im = mag * jnp.sin(lam_im * dt)
    den = lam_re * lam_re + lam_im * lam_im
    nr = abar_re - 1.0
    coef_re = (nr * lam_re + abar_im * lam_im) / den
    coef_im = (abar_im * lam_re - nr * lam_im) / den
    bbar_re = coef_re[..., None] * b_re - coef_im[..., None] * b_im
    bbar_im = coef_re[..., None] * b_im + coef_im[..., None] * b_re
    n_g, n_p, n_c = b_re.shape
    gpb = LANES // n_c
    eye = jnp.eye(gpb, dtype=F32)

    def in_blk(bb):
        bb = bb.reshape(n_g // gpb, gpb, n_p, n_c)
        return jnp.einsum('kgpc,gh->kgchp', bb, eye).reshape(
            n_g // gpb, gpb * n_c, gpb * n_p).astype(BF16)

    def out_blk(cc):
        cc = cc.reshape(n_g // gpb, gpb, n_c, n_p)
        return jnp.einsum('kgcp,gh->khpgc', cc, eye).reshape(
            n_g // gpb, gpb * n_p, gpb * n_c).astype(BF16)

    return (in_blk(bbar_re), in_blk(bbar_im), out_blk(c_re), out_blk(c_im),
            d.reshape(1, -1), abar_re.reshape(-1, LANES), abar_im.reshape(-1, LANES),
            w_glu.astype(BF16), b_glu.reshape(1, -1))


def kernel(x_prompt, x_sample, state_shift, state_wkv, state_ssm_re, state_ssm_im, meta_tokens, g_mix, w_in, shift_mu, w0, w_decay_up, a0, w_aaa_up, w_gate_up, k_k, k_a, r_k, lnx_w, lnx_b, ssm_lam_re, ssm_lam_im, ssm_log_dt, ssm_b_re, ssm_b_im, ssm_c_re, ssm_c_im, ssm_d, w_glu, b_glu, w_br_a, w_br_b, w_out, g_ffn, w_router_grp, b_router_grp, w_router_exp, b_router_exp, w_exp_gate, w_exp_up, w_exp_down, g_final):
    depth = g_mix.shape[0]
    assert depth == 1, "single-layer trunk"
    bsz, seq, d = x_prompt.shape
    nb = x_sample.shape[0]
    assert x_sample.shape[1] == 1
    t_p = seq + N_META
    n_p = bsz * t_p
    width = k_k.shape[1]
    heads = width // HEAD
    n_dl, n_al, n_gl = w_decay_up.shape[1], w_aaa_up.shape[1], w_gate_up.shape[1]
    n_grp = w_router_grp.shape[2]
    n_exp = w_router_exp.shape[2]
    assert bsz * heads * 2 == LANES and (nb * heads) % LANES == 0

    s_row0 = -(-n_p // nb) * nb
    n_t = s_row0 + nb
    pieces = []
    for b in range(bsz):
        pieces += [meta_tokens, x_prompt[b]]
    x_all = jnp.concatenate(
        pieces + [jnp.zeros((s_row0 - n_p, d), F32), x_sample.reshape(nb, d)], axis=0)

    c_rkv = 3 * width
    c_xw, c_xa, c_xg = c_rkv, c_rkv + n_dl, c_rkv + n_dl + n_al
    c_u = c_xg + n_gl
    lo_width = 2 * LORA_PAD + n_gl

    def regroup(m):
        return jnp.concatenate(
            [m[:, :c_rkv], m[:, c_u:], _pad_cols(m[:, c_xw:c_xa], LORA_PAD),
             _pad_cols(m[:, c_xa:c_xg], LORA_PAD), m[:, c_xg:c_u]], axis=1)

    def ungroup(m):
        lo = c_rkv + width + 2 * d
        return jnp.concatenate(
            [m[:, :c_rkv], m[:, lo:lo + n_dl], m[:, lo + LORA_PAD:lo + LORA_PAD + n_al],
             m[:, lo + 2 * LORA_PAD:]], axis=1)

    w_in_p = regroup(w_in[0].astype(BF16))
    n_proj = w_in_p.shape[1]
    tiles = _tile_plan(n_t, t_p, seq)
    proj = _inproj(x_all, g_mix, w_in_p, tm=tiles["inproj_rows"], tn=tiles["inproj_cols"])

    mu_all = regroup(_pad_cols(shift_mu, c_u + width + 2 * d))
    mu_rkv, mu_lo = mu_all[:, :c_rkv], mu_all[:, n_proj - lo_width:]
    st_all = regroup(_pad_cols(state_shift[0], c_u + width + 2 * d))
    prev_s = (st_all[:, :c_rkv], st_all[:, n_proj - lo_width:])
    lo_col_block = (n_proj - lo_width) // lo_width
    prep_w = (mu_rkv, mu_lo, w0, a0,
              _pad_rows(w_decay_up[0], LORA_PAD).astype(BF16),
              _pad_rows(w_aaa_up[0], LORA_PAD).astype(BF16), w_gate_up[0].astype(BF16))

    tm_seq = tiles["seq_rows"]
    g_all = jnp.zeros((n_t, width), F32)
    rp, wp, kp, vp, ap, g_all = _rwkv_prep(proj, 0, n_p, tm_seq, bsz, width, lo_col_block,
                                                lo_width, *prep_w, g_all)
    rs, ws, ks, vs, as_, g_all = _rwkv_prep(proj, s_row0 // nb, nb, nb, 1, width,
                                                 lo_col_block, lo_width, *prep_w, g_all,
                                                 prev=prev_s)

    s5c = _s5_consts(ssm_lam_re[0], ssm_lam_im[0], ssm_log_dt[0], ssm_b_re[0], ssm_b_im[0],
                     ssm_c_re[0], ssm_c_im[0], ssm_d[0], w_glu[0], b_glu)
    n_g, n_st = ssm_lam_re.shape[1], ssm_lam_re.shape[2]
    u_col_block = c_rkv // width
    yb_all = jnp.zeros((n_t, width), BF16)
    yb_all, re_p, im_p = _s5(proj, 0, n_p, tm_seq, bsz, u_col_block, width, s5c, yb_all)
    h0 = (state_ssm_re[0].reshape(nb, n_g * n_st), state_ssm_im[0].reshape(nb, n_g * n_st))
    yb_all, re_s, im_s = _s5(proj, s_row0 // nb, nb, nb, 1, u_col_block, width, s5c, yb_all,
                             h0=h0)

    half = HEAD // 2

    def to_lanes_p(z):
        return z.reshape(t_p, bsz * heads, HEAD).transpose(0, 2, 1)

    def to_lanes_s(z):
        return z.reshape(nb, heads, HEAD).transpose(2, 0, 1).reshape(1, HEAD, nb * heads)

    def head_const(c):
        return jnp.tile(c.reshape(heads, HEAD).T, (1, LANES // heads))

    rk_l = head_const(r_k[0])
    kk_l, ka_l = head_const(k_k[0]), head_const(k_a[0])
    lw_p = jnp.broadcast_to(lnx_w[0].reshape(heads, 2, half).transpose(2, 1, 0)[:, :, None, :],
                            (half, 2, bsz, heads)).reshape(half, LANES)
    lb_p = jnp.broadcast_to(lnx_b[0].reshape(heads, 2, half).transpose(2, 1, 0)[:, :, None, :],
                            (half, 2, bsz, heads)).reshape(half, LANES)
    lw_s = jnp.tile(lnx_w[0].reshape(heads, HEAD).T, (1, LANES // heads))
    lb_s = jnp.tile(lnx_b[0].reshape(heads, HEAD).T, (1, LANES // heads))

    s0_p = jnp.zeros((HEAD, half, LANES), F32)
    y_p, sf_p = _wkv(to_lanes_p(rp), to_lanes_p(wp), to_lanes_p(kp), to_lanes_p(ap),
                     to_lanes_p(vp), s0_p, kk_l, ka_l, rk_l, lw_p, lb_p,
                     tc=tiles["wkv_chunk"], isplit=True, unroll=tiles["wkv_unroll"])
    s0_s = state_wkv[0].transpose(3, 2, 0, 1).reshape(HEAD, HEAD, nb * heads)
    y_s, sf_s = _wkv(to_lanes_s(rs), to_lanes_s(ws), to_lanes_s(ks), to_lanes_s(as_),
                     to_lanes_s(vs), s0_s, kk_l, ka_l, rk_l, lw_s, lb_s,
                     tc=1, isplit=False, unroll=1)

    ya_p = y_p.reshape(t_p, half, 2, bsz, heads).transpose(3, 0, 4, 2, 1).reshape(n_p, width)
    ya_s = y_s.reshape(HEAD, nb, heads).transpose(1, 2, 0).reshape(nb, width)
    wkv_p = sf_p.reshape(HEAD, half, 2, bsz, heads).transpose(3, 4, 2, 1, 0).reshape(
        1, bsz, heads, HEAD, HEAD)
    wkv_s = sf_s.reshape(HEAD, HEAD, nb, heads).transpose(2, 3, 1, 0)[None]

    def unify(a, b):
        return jnp.concatenate([a, jnp.zeros((s_row0 - n_p, a.shape[1]), a.dtype), b], axis=0)

    w_router = _pad_cols(jnp.concatenate([w_router_grp[0], w_router_exp[0]], axis=1), LANES)
    w_router_hi = w_router.astype(BF16)
    w_router = jnp.stack([w_router_hi, (w_router - w_router_hi.astype(F32)).astype(BF16)])
    b_router = _pad_cols(jnp.concatenate([b_router_grp, b_router_exp], axis=1), LANES)
    assert n_grp + n_exp <= LANES and n_exp == n_grp * EXPERTS_PER_GROUP
    h1, xn2, route = _merge(unify(ya_p, ya_s), g_all, yb_all, proj, x_all,
                           w_br_a[0].astype(BF16), w_br_b[0].astype(BF16), w_out[0].astype(BF16),
                           g_ffn, w_router, b_router, tm=tiles["merge_rows"],
                           ga_col_block=(c_rkv + width) // d,
                           n_grp=n_grp)
    tm_moe = tiles["moe_rows"]
    te, tf, tv, tws, tne, nu, src_rows, pos1, pos2 = _moe_plan(route, n_exp, tm_moe)
    y_moe = _moe(xn2, te, tf, tv, tws, tne, nu, src_rows, w_exp_gate[0], w_exp_up[0],
                 w_exp_down[0], tm_moe)
    gfin = g_final.reshape(1, d)
    y_prompt = _final(h1, y_moe, pos1, pos2, route, gfin, N_META, t_p, seq, bsz,
                      tm=tiles["final_rows"]).reshape(bsz, seq, d)
    y_sample = _final(h1, y_moe, pos1, pos2, route, gfin, s_row0, 0, nb, 1,
                      tm=nb).reshape(nb, 1, d)
    last_p = jnp.concatenate([proj[(b + 1) * t_p - 1:(b + 1) * t_p] for b in range(bsz)], axis=0)
    shift_p = ungroup(last_p)[None]
    shift_s = ungroup(proj[s_row0:])[None]
    return (y_prompt, y_sample, shift_p, wkv_p,
            re_p.reshape(1, bsz, n_g, n_st), im_p.reshape(1, bsz, n_g, n_st),
            shift_s, wkv_s,
            re_s.reshape(1, nb, n_g, n_st), im_s.reshape(1, nb, n_g, n_st))
```
